```python
import math
import jax, jax.numpy as jnp
from jax import lax
import numpy as np

D_MODEL = 1024
BATCH = 2
SEQ = 8192
DEPTH = 2

HEAD_DIM = 64
SB_HEADS = 8
SB_WIDTH = SB_HEADS * HEAD_DIM
SB_BLOCK = 128
LRU_WIDTH = 512
LRU_BLOCKS = 8
LRU_BLOCK_DIM = LRU_WIDTH // LRU_BLOCKS
LRU_C = 8.0
CONV_WIDTH = 4
AB_IN_WIDTH = 3 * SB_WIDTH + 2 * LRU_WIDTH
AB_MIX_WIDTH = SB_WIDTH + LRU_WIDTH
SWA_HEADS = 16
SWA_KV_HEADS = 4
SWA_GROUP = SWA_HEADS // SWA_KV_HEADS
SWA_WINDOW = 128
C_IN_WIDTH = (SWA_HEADS + 2 * SWA_KV_HEADS) * HEAD_DIM
C_MIX_WIDTH = SWA_HEADS * HEAD_DIM
ROPE_THETA = 10000.0
N_EXPERTS = 16
N_GROUPS = 4
GROUP_SIZE = N_EXPERTS // N_GROUPS
TOP_K = 2
D_FF = 512
PLE_DIM = 256
N_EVEN = (DEPTH + 1) // 2
N_ODD = DEPTH // 2
DEEPNORM_ALPHA = (2 * DEPTH) ** 0.25
DEEPNORM_BETA = (8 * DEPTH) ** -0.25
LN_EPS = 1e-5

kernel_name = "hybrid_sb_lru_swa_grouped_moe_deepnorm"


def layer_norm(x, g, b):
    xf = x.astype(jnp.float32)
    mu = jnp.mean(xf, axis=-1, keepdims=True)
    var = jnp.mean(jnp.square(xf - mu), axis=-1, keepdims=True)
    y = (xf - mu) * lax.rsqrt(var + LN_EPS) * g.astype(jnp.float32) + b.astype(jnp.float32)
    return y.astype(x.dtype)


def stick_breaking_attention(q, k, v):
    b, s = q.shape[0], q.shape[1]
    n_blk = s // SB_BLOCK
    qb = (q * (HEAD_DIM ** -0.5)).reshape(b, n_blk, SB_BLOCK, SB_HEADS, HEAD_DIM).transpose(1, 0, 3, 2, 4)
    kh = k.transpose(0, 2, 1, 3)
    vh = v.transpose(0, 2, 1, 3)
    key_idx = jnp.arange(s)

    def one_block(args):
        q_blk, blk = args
        z = jnp.einsum('bhqd,bhkd->bhqk', q_blk, kh, preferred_element_type=jnp.float32)
        q_idx = blk * SB_BLOCK + jnp.arange(SB_BLOCK)
        causal = key_idx[None, :] < q_idx[:, None]
        log_keep = jnp.where(causal, -jax.nn.softplus(z), 0.0)
        after = lax.cumsum(log_keep, axis=3, reverse=True) - log_keep
        w = jnp.where(causal, jnp.exp(jax.nn.log_sigmoid(z) + after), 0.0)
        return jnp.einsum('bhqk,bhkd->bqhd', w.astype(vh.dtype), vh)

    out = lax.map(one_block, (qb, jnp.arange(n_blk)))
    return out.transpose(1, 0, 2, 3, 4).reshape(b, s, SB_WIDTH)


def causal_depthwise_conv(x, w, bias):
    y = lax.conv_general_dilated(
        x, w[:, None, :].astype(x.dtype), window_strides=(1,),
        padding=[(CONV_WIDTH - 1, 0)], dimension_numbers=('NWC', 'WIO', 'NWC'),
        feature_group_count=x.shape[-1])
    return y + bias.astype(x.dtype)


def rg_lru(x, w_r, b_r, w_i, b_i, lam):
    b, s, _ = x.shape
    xb = x.reshape(b, s, LRU_BLOCKS, LRU_BLOCK_DIM)
    gate_r = jnp.einsum('bsnc,ncd->bsnd', xb, w_r).reshape(b, s, LRU_WIDTH) + b_r
    gate_i = jnp.einsum('bsnc,ncd->bsnd', xb, w_i).reshape(b, s, LRU_WIDTH) + b_i
    r = jax.nn.sigmoid(gate_r.astype(jnp.float32))
    i = jax.nn.sigmoid(gate_i.astype(jnp.float32))
    log_a = -LRU_C * r * jax.nn.softplus(-lam.astype(jnp.float32))
    a = jnp.exp(log_a)
    u = jnp.sqrt(-jnp.expm1(2.0 * log_a)) * (i * x.astype(jnp.float32))

    def combine(left, right):
        a_l, u_l = left
        a_r, u_r = right
        return a_l * a_r, a_r * u_l + u_r

    _, h = lax.associative_scan(combine, (a, u), axis=1)
    return h.astype(x.dtype)


def mixer_sb_lru(x, w_in, w_out, conv_w, conv_b, w_r, b_r, w_i, b_i, lam):
    b, s, _ = x.shape
    proj = x @ w_in
    q, k, v, xr, gr = jnp.split(
        proj, [SB_WIDTH, 2 * SB_WIDTH, 3 * SB_WIDTH, 3 * SB_WIDTH + LRU_WIDTH], axis=-1)
    heads = lambda t: t.reshape(b, s, SB_HEADS, HEAD_DIM)
    y_sb = stick_breaking_attention(heads(q), heads(k), heads(v))
    h = rg_lru(causal_depthwise_conv(xr, conv_w, conv_b), w_r, b_r, w_i, b_i, lam)
    y_lru = jax.nn.gelu(gr) * h
    return jnp.concatenate([y_sb, y_lru], axis=-1) @ w_out


def rope(x, positions):
    half = HEAD_DIM // 2
    inv_freq = ROPE_THETA ** (-jnp.arange(half, dtype=jnp.float32) / half)
    ang = positions.astype(jnp.float32)[..., None] * inv_freq
    cos = jnp.cos(ang)[:, :, None, :]
    sin = jnp.sin(ang)[:, :, None, :]
    x1 = x[..., :half].astype(jnp.float32)
    x2 = x[..., half:].astype(jnp.float32)
    return jnp.concatenate([x1 * cos - x2 * sin, x2 * cos + x1 * sin], axis=-1).astype(x.dtype)


def sliding_window_attention_with_sinks(q, k, v, sinks):
    b, s = q.shape[0], q.shape[1]
    n_blk = s // SWA_WINDOW
    qb = q.reshape(b, n_blk, SWA_WINDOW, SWA_KV_HEADS, SWA_GROUP, HEAD_DIM)

    def with_prev(t):
        tb = t.reshape(b, n_blk, SWA_WINDOW, SWA_KV_HEADS, HEAD_DIM)
        prev = jnp.pad(tb[:, :-1], ((0, 0), (1, 0), (0, 0), (0, 0), (0, 0)))
        return jnp.concatenate([prev, tb], axis=2)

    kk, vv = with_prev(k), with_prev(v)
    scores = jnp.einsum('bnqkgd,bnskd->bnkgqs', qb, kk,
                        preferred_element_type=jnp.float32) * (HEAD_DIM ** -0.5)
    qi = jnp.arange(SWA_WINDOW)[:, None] + SWA_WINDOW
    si = jnp.arange(2 * SWA_WINDOW)[None, :]
    dist = qi - si
    band = (dist >= 0) & (dist < SWA_WINDOW)
    not_pad = (jnp.arange(n_blk)[:, None, None] > 0) | (si[None] >= SWA_WINDOW)
    mask = band[None] & not_pad
    scores = jnp.where(mask[None, :, None, None], scores, -jnp.inf)
    sink = jnp.broadcast_to(
        sinks.astype(jnp.float32).reshape(SWA_KV_HEADS, SWA_GROUP)[None, None, :, :, None, None],
        scores.shape[:-1] + (1,))
    probs = jax.nn.softmax(jnp.concatenate([scores, sink], axis=-1), axis=-1)[..., :-1]
    out = jnp.einsum('bnkgqs,bnskd->bnqkgd', probs.astype(vv.dtype), vv)
    return out.reshape(b, s, C_MIX_WIDTH)


def mixer_swa(x, positions, w_qkv, w_out, sinks):
    b, s, _ = x.shape
    proj = x @ w_qkv
    q, k, v = jnp.split(proj, [SWA_HEADS * HEAD_DIM, (SWA_HEADS + SWA_KV_HEADS) * HEAD_DIM], axis=-1)
    q = rope(q.reshape(b, s, SWA_HEADS, HEAD_DIM), positions)
    k = rope(k.reshape(b, s, SWA_KV_HEADS, HEAD_DIM), positions)
    v = v.reshape(b, s, SWA_KV_HEADS, HEAD_DIM)
    return sliding_window_attention_with_sinks(q, k, v, sinks) @ w_out


def grouped_moe(x, w_router, b_router, w_gate, w_up, w_down):
    logits = jnp.einsum('bsd,de->bse', x, w_router,
                        preferred_element_type=jnp.float32) + b_router.astype(jnp.float32)
    probs = jax.nn.softmax(logits, axis=-1)
    grouped = probs.reshape(probs.shape[:-1] + (N_GROUPS, GROUP_SIZE))
    group_score = jnp.sum(lax.top_k(grouped, TOP_K)[0], axis=-1)
    g_sel = jnp.argmax(group_score, axis=-1)
    in_group = jnp.take_along_axis(grouped, g_sel[..., None, None], axis=-2)[..., 0, :]
    top_w, top_i = lax.top_k(in_group, TOP_K)
    top_w = top_w / jnp.sum(top_w, axis=-1, keepdims=True)
    expert_idx = g_sel[..., None] * GROUP_SIZE + top_i
    gates = jnp.sum(jax.nn.one_hot(expert_idx, N_EXPERTS, dtype=jnp.float32) * top_w[..., None], axis=-2)
    hidden = jax.nn.silu(jnp.einsum('bsd,edf->bsef', x, w_gate)) * jnp.einsum('bsd,edf->bsef', x, w_up)
    hidden = hidden * gates[..., None].astype(hidden.dtype)
    return jnp.einsum('bsef,efd->bsd', hidden, w_down).astype(x.dtype)


def per_layer_embedding(x, p_i, w_proj, w_gate, b_gate):
    gate = jax.nn.sigmoid(x @ w_gate + b_gate)
    return x + gate * (p_i @ w_proj)


def setup_inputs(seed: int = 0) -> dict:
    key = jax.random.key(seed)
    ks = jax.random.split(key, 28)

    def nrm(k, shape, scale):
        return jax.random.normal(k, shape, jnp.float32) * scale

    x = nrm(ks[0], (BATCH, SEQ, D_MODEL), 1.0)
    p = nrm(ks[1], (DEPTH, BATCH, SEQ, PLE_DIM), 1.0)
    positions = (jnp.arange(SEQ, dtype=jnp.int32)[None, :]
                 + jax.random.randint(ks[2], (BATCH, 1), 0, 1024, dtype=jnp.int32))
    w_in_ab = nrm(ks[3], (N_EVEN, D_MODEL, AB_IN_WIDTH), D_MODEL ** -0.5)
    w_out_ab = nrm(ks[4], (N_EVEN, AB_MIX_WIDTH, D_MODEL), DEEPNORM_BETA * AB_MIX_WIDTH ** -0.5)
    conv_w = nrm(ks[5], (N_EVEN, CONV_WIDTH, LRU_WIDTH), CONV_WIDTH ** -0.5)
    conv_b = nrm(ks[6], (N_EVEN, LRU_WIDTH), 0.01)
    lru_w_r = nrm(ks[7], (N_EVEN, LRU_BLOCKS, LRU_BLOCK_DIM, LRU_BLOCK_DIM), LRU_BLOCK_DIM ** -0.5)
    lru_b_r = nrm(ks[8], (N_EVEN, LRU_WIDTH), 0.01)
    lru_w_i = nrm(ks[9], (N_EVEN, LRU_BLOCKS, LRU_BLOCK_DIM, LRU_BLOCK_DIM), LRU_BLOCK_DIM ** -0.5)
    lru_b_i = nrm(ks[10], (N_EVEN, LRU_WIDTH), 0.01)
    a_c = jax.random.uniform(ks[11], (N_EVEN, LRU_WIDTH), jnp.float32, 0.9, 0.999)
    a0 = a_c ** (1.0 / LRU_C)
    lru_lambda = jnp.log(a0) - jnp.log1p(-a0)
    w_qkv_c = nrm(ks[12], (N_ODD, D_MODEL, C_IN_WIDTH), D_MODEL ** -0.5)
    w_out_c = nrm(ks[13], (N_ODD, C_MIX_WIDTH, D_MODEL), DEEPNORM_BETA * C_MIX_WIDTH ** -0.5)
    sinks_c = nrm(ks[14], (N_ODD, SWA_HEADS), 0.5)
    ln_mix_g = 1.0 + nrm(ks[15], (DEPTH, D_MODEL), 0.01)
    ln_mix_b = nrm(ks[16], (DEPTH, D_MODEL), 0.01)
    ln_ffn_g = 1.0 + nrm(ks[17], (DEPTH, D_MODEL), 0.01)
    ln_ffn_b = nrm(ks[18], (DEPTH, D_MODEL), 0.01)
    w_router = nrm(ks[19], (D_MODEL, N_EXPERTS), D_MODEL ** -0.5)
    b_router = nrm(ks[20], (N_EXPERTS,), 0.01)
    exp_w_gate = nrm(ks[21], (DEPTH, N_EXPERTS, D_MODEL, D_FF), D_MODEL ** -0.5)
    exp_w_up = nrm(ks[22], (DEPTH, N_EXPERTS, D_MODEL, D_FF), D_MODEL ** -0.5)
    exp_w_down = nrm(ks[23], (DEPTH, N_EXPERTS, D_FF, D_MODEL), DEEPNORM_BETA * D_FF ** -0.5)
    ple_w_proj = nrm(ks[24], (DEPTH, PLE_DIM, D_MODEL), PLE_DIM ** -0.5)
    ple_w_gate = nrm(ks[25], (DEPTH, D_MODEL, D_MODEL), D_MODEL ** -0.5)
    ple_b_gate = nrm(ks[26], (DEPTH, D_MODEL), 0.01)
    return {
        'x': x, 'p': p, 'positions': positions,
        'w_in_ab': w_in_ab, 'w_out_ab': w_out_ab, 'conv_w': conv_w, 'conv_b': conv_b,
        'lru_w_r': lru_w_r, 'lru_b_r': lru_b_r, 'lru_w_i': lru_w_i, 'lru_b_i': lru_b_i,
        'lru_lambda': lru_lambda,
        'w_qkv_c': w_qkv_c, 'w_out_c': w_out_c, 'sinks_c': sinks_c,
        'ln_mix_g': ln_mix_g, 'ln_mix_b': ln_mix_b, 'ln_ffn_g': ln_ffn_g, 'ln_ffn_b': ln_ffn_b,
        'w_router': w_router, 'b_router': b_router,
        'exp_w_gate': exp_w_gate, 'exp_w_up': exp_w_up, 'exp_w_down': exp_w_down,
        'ple_w_proj': ple_w_proj, 'ple_w_gate': ple_w_gate, 'ple_b_gate': ple_b_gate,
    }


def reference(x, p, positions, w_in_ab, w_out_ab, conv_w, conv_b, lru_w_r, lru_b_r, lru_w_i,
              lru_b_i, lru_lambda, w_qkv_c, w_out_c, sinks_c, ln_mix_g, ln_mix_b, ln_ffn_g,
              ln_ffn_b, w_router, b_router, exp_w_gate, exp_w_up, exp_w_down, ple_w_proj,
              ple_w_gate, ple_b_gate):
    for i in range(DEPTH):
        j = i // 2
        if i % 2 == 0:
            h = mixer_sb_lru(x, w_in_ab[j], w_out_ab[j], conv_w[j], conv_b[j], lru_w_r[j],
                             lru_b_r[j], lru_w_i[j], lru_b_i[j], lru_lambda[j])
        else:
            h = mixer_swa(x, positions, w_qkv_c[j], w_out_c[j], sinks_c[j])
        x = layer_norm(DEEPNORM_ALPHA * x + h, ln_mix_g[i], ln_mix_b[i])
        m = grouped_moe(x, w_router, b_router, exp_w_gate[i], exp_w_up[i], exp_w_down[i])
        x = layer_norm(DEEPNORM_ALPHA * x + m, ln_ffn_g[i], ln_ffn_b[i])
        x = per_layer_embedding(x, p[i], ple_w_proj[i], ple_w_gate[i], ple_b_gate[i])
    return x
```

```python
import functools
import math

import jax
import jax.numpy as jnp
from jax import lax
from jax.experimental import pallas as pl
from jax.experimental.pallas import tpu as pltpu

HEAD_DIM = 64
SB_HEADS = 8
SB_WIDTH = SB_HEADS * HEAD_DIM
LRU_WIDTH = 512
LRU_BLOCKS = 8
LRU_C = 8.0
CONV_WIDTH = 4
SWA_HEADS = 16
SWA_KV_HEADS = 4
SWA_GROUP = SWA_HEADS // SWA_KV_HEADS
SWA_WINDOW = 128
ROPE_THETA = 10000.0
N_EXPERTS = 16
N_GROUPS = 4
GROUP_SIZE = N_EXPERTS // N_GROUPS
LN_EPS = 1e-5
Q_SCALE = HEAD_DIM ** -0.5

LANES = 128
SUBLANES = 8
VMEM_LIMIT_BYTES = 48 * 1024 * 1024

NEG_BIG = -1e30

BF16 = jnp.bfloat16
F32 = jnp.float32


def _params(*semantics):
    return pltpu.CompilerParams(dimension_semantics=semantics, vmem_limit_bytes=VMEM_LIMIT_BYTES)


def _softplus(z):
    return jnp.maximum(z, 0.0) + jnp.log(1.0 + jnp.exp(-jnp.abs(z)))


def _sigmoid(z):
    return 1.0 / (1.0 + jnp.exp(-z))


def _layer_norm(y, g, b):
    mu = jnp.mean(y, axis=-1, keepdims=True)
    d = y - mu
    var = jnp.mean(d * d, axis=-1, keepdims=True)
    return d * lax.rsqrt(var + LN_EPS) * g + b


def _proj_ab_kernel(x_ref, w_ref, q_ref, k_ref, v_ref, xr_ref, gr_ref):
    xb = x_ref[...].astype(BF16)

    def chunk(c):
        return jnp.dot(xb, w_ref[:, c * SB_WIDTH:(c + 1) * SB_WIDTH], preferred_element_type=F32)

    for c, (ref, scale) in enumerate(((q_ref, Q_SCALE), (k_ref, None), (v_ref, None))):
        r = chunk(c)
        if scale is not None:
            r = r * scale
        for h in range(SB_HEADS):
            ref[h] = r[:, h * HEAD_DIM:(h + 1) * HEAD_DIM].astype(BF16)
    xr_ref[...] = chunk(3)
    gr_ref[...] = chunk(4)


def _proj_ab(x, w_bf16, tm):
    b, s, d = x.shape
    n = w_bf16.shape[1]
    heads = jax.ShapeDtypeStruct((b, SB_HEADS, s, HEAD_DIM), BF16)
    flat = jax.ShapeDtypeStruct((b, s, LRU_WIDTH), F32)
    head_spec = pl.BlockSpec((None, SB_HEADS, tm, HEAD_DIM), lambda bi, i: (bi, 0, i, 0))
    flat_spec = pl.BlockSpec((None, tm, LRU_WIDTH), lambda bi, i: (bi, i, 0))
    return pl.pallas_call(
        _proj_ab_kernel,
        grid=(b, s // tm),
        in_specs=[pl.BlockSpec((None, tm, d), lambda bi, i: (bi, i, 0)),
                  pl.BlockSpec((d, n), lambda bi, i: (0, 0))],
        out_specs=[head_spec, head_spec, head_spec, flat_spec, flat_spec],
        out_shape=[heads, heads, heads, flat, flat],
        compiler_params=_params("parallel", "parallel"),
        name="proj_ab",
    )(x, w_bf16)


def _sb_attn_kernel(q_ref, k_ref, v_ref, o_ref, *, tq):
    i = pl.program_id(2)
    q = q_ref[...]
    row = lax.broadcasted_iota(jnp.int32, (tq, tq), 0)
    col = lax.broadcasted_iota(jnp.int32, (tq, tq), 1)
    later = (row > col).astype(BF16)
    causal = col < row

    def block(jb, carry, acc, masked):
        start = pl.multiple_of(jb * tq, tq)
        kj = k_ref[pl.ds(start, tq), :]
        vj = v_ref[pl.ds(start, tq), :]
        z = lax.dot_general(q, kj, (((1,), (1,)), ((), ())), preferred_element_type=F32)
        sp = _softplus(z)
        log_keep = -sp
        if masked:
            log_keep = jnp.where(causal, log_keep, 0.0)
        after = jnp.dot(log_keep.astype(BF16), later, preferred_element_type=F32)
        w = jnp.exp((z - sp) + after + carry)
        if masked:
            w = jnp.where(causal, w, 0.0)
        acc = acc + jnp.dot(w.astype(BF16), vj, preferred_element_type=F32)
        carry = carry + jnp.sum(log_keep, axis=1, keepdims=True)
        return carry, acc

    carry, acc = block(i, jnp.zeros((tq, 1), F32), jnp.zeros((tq, HEAD_DIM), F32), True)

    def body(step, state):
        return block(i - 1 - step, state[0], state[1], False)

    carry, acc = lax.fori_loop(0, i, body, (carry, acc))
    o_ref[...] = acc.astype(o_ref.dtype)


def _sb_attention(q, k, v, tq):
    b, h, s, dh = q.shape
    return pl.pallas_call(
        functools.partial(_sb_attn_kernel, tq=tq),
        grid=(b, h, s // tq),
        in_specs=[pl.BlockSpec((None, None, tq, dh), lambda bi, hi, i: (bi, hi, i, 0)),
                  pl.BlockSpec((None, None, s, dh), lambda bi, hi, i: (bi, hi, 0, 0)),
                  pl.BlockSpec((None, None, s, dh), lambda bi, hi, i: (bi, hi, 0, 0))],
        out_specs=pl.BlockSpec((None, None, tq, dh), lambda bi, hi, i: (bi, hi, i, 0)),
        out_shape=jax.ShapeDtypeStruct((b, h, s, dh), BF16),
        compiler_params=_params("parallel", "parallel", "parallel"),
        name="sb_attention",
    )(q, k, v)


def _gelu_tanh(x):
    return 0.5 * x * (1.0 + jnp.tanh(math.sqrt(2.0 / math.pi) * (x + 0.044715 * (x * x * x))))


def _lru_kernel(xr_ref, gr_ref, cw_ref, cb_ref, wr_ref, br_ref, wi_ref, bi_ref, lam_ref, y_ref,
                xbuf, hprev, *, ts):
    @pl.when(pl.program_id(1) == 0)
    def _():
        xbuf[0:SUBLANES, :] = jnp.zeros((SUBLANES, LRU_WIDTH), F32)
        hprev[...] = jnp.zeros_like(hprev)

    xbuf[SUBLANES:SUBLANES + ts, :] = xr_ref[...]
    xc = cb_ref[...] + cw_ref[CONV_WIDTH - 1:CONV_WIDTH, :] * xbuf[SUBLANES:SUBLANES + ts, :]
    for kk in range(CONV_WIDTH - 1):
        off = SUBLANES - (CONV_WIDTH - 1) + kk
        xc = xc + cw_ref[kk:kk + 1, :] * xbuf[off:off + ts, :]
    xbuf[0:SUBLANES, :] = xbuf[ts:ts + SUBLANES, :]

    xcb = xc.astype(BF16)
    r = _sigmoid(jnp.dot(xcb, wr_ref[...], preferred_element_type=F32) + br_ref[...])
    gi = _sigmoid(jnp.dot(xcb, wi_ref[...], preferred_element_type=F32) + bi_ref[...])
    log_a = (-LRU_C) * r * _softplus(-lam_ref[...])
    a = jnp.exp(log_a)
    u = jnp.sqrt(1.0 - a * a) * (gi * xc)

    row = lax.broadcasted_iota(jnp.int32, (ts, LRU_WIDTH), 0)
    d = 1
    while d < ts:
        keep = row >= d
        a_sh = jnp.where(keep, pltpu.roll(a, d, axis=0), 1.0)
        u_sh = jnp.where(keep, pltpu.roll(u, d, axis=0), 0.0)
        u = a * u_sh + u
        a = a * a_sh
        d *= 2
    h = a * hprev[0:1, :] + u
    hprev[...] = jnp.broadcast_to(h[ts - 1:ts, :], hprev.shape)
    y_ref[...] = (_gelu_tanh(gr_ref[...]) * h).astype(y_ref.dtype)


def _lru(xr, gr, conv_w, conv_b, wr_bd, b_r, wi_bd, b_i, lam, ts):
    b, s, w = xr.shape
    seq_spec = pl.BlockSpec((None, ts, w), lambda bi, i: (bi, i, 0))

    def full(shape):
        return pl.BlockSpec(shape, lambda bi, i: (0,) * len(shape))

    return pl.pallas_call(
        functools.partial(_lru_kernel, ts=ts),
        grid=(b, s // ts),
        in_specs=[seq_spec, seq_spec, full((CONV_WIDTH, w)), full((1, w)), full((w, w)), full((1, w)),
                  full((w, w)), full((1, w)), full((1, w))],
        out_specs=seq_spec,
        out_shape=jax.ShapeDtypeStruct((b, s, w), BF16),
        scratch_shapes=[pltpu.VMEM((ts + 2 * SUBLANES, w), F32), pltpu.VMEM((SUBLANES, w), F32)],
        compiler_params=_params("parallel", "arbitrary"),
        name="rg_lru",
    )(xr, gr, conv_w, conv_b.reshape(1, w), wr_bd, b_r.reshape(1, w), wi_bd, b_i.reshape(1, w),
      lam.reshape(1, w))


def _block_diag(w):
    n, c, d = w.shape
    eye = jnp.eye(n, dtype=w.dtype)
    return (eye[:, None, :, None] * w[:, :, None, :]).reshape(n * c, n * d)


def _proj_rope_kernel(x_ref, pos_ref, freq_ref, w_ref, q_ref, k_ref, v_ref):
    tm = x_ref.shape[0]
    xb = x_ref[...].astype(BF16)
    ang = pos_ref[...].astype(F32) * freq_ref[...]
    cos = jnp.cos(ang)
    sin = jnp.sin(ang)
    lane = lax.broadcasted_iota(jnp.int32, (tm, LANES), 1)
    first_half = (lane % HEAD_DIM) < (HEAD_DIM // 2)
    heads_per_slab = LANES // HEAD_DIM

    def rope(r):
        upper = pltpu.roll(r, LANES - HEAD_DIM // 2, axis=1)
        lower = pltpu.roll(r, HEAD_DIM // 2, axis=1)
        return r * cos + jnp.where(first_half, -upper, lower) * sin

    def emit(ref, n_heads, col0, rotary, scale):
        for slab in range(n_heads // heads_per_slab):
            c0 = col0 + slab * LANES
            r = jnp.dot(xb, w_ref[:, c0:c0 + LANES], preferred_element_type=F32)
            if rotary:
                r = rope(r)
            if scale is not None:
                r = r * scale
            for j in range(heads_per_slab):
                ref[slab * heads_per_slab + j] = r[:, j * HEAD_DIM:(j + 1) * HEAD_DIM].astype(BF16)

    emit(q_ref, SWA_HEADS, 0, True, Q_SCALE)
    emit(k_ref, SWA_KV_HEADS, SWA_HEADS * HEAD_DIM, True, None)
    emit(v_ref, SWA_KV_HEADS, (SWA_HEADS + SWA_KV_HEADS) * HEAD_DIM, False, None)


def _proj_rope(x, positions, w_bf16, tm):
    b, s, d = x.shape
    n = w_bf16.shape[1]
    half = HEAD_DIM // 2
    inv_freq = ROPE_THETA ** (-jnp.arange(half, dtype=F32) / half)
    freq_row = jnp.tile(inv_freq, LANES // half).reshape(1, LANES)

    def heads(nh):
        return (jax.ShapeDtypeStruct((b, nh, s, HEAD_DIM), BF16),
                pl.BlockSpec((None, nh, tm, HEAD_DIM), lambda bi, i: (bi, 0, i, 0)))

    (qs, qspec), (ks, kspec), (vs, vspec) = heads(SWA_HEADS), heads(SWA_KV_HEADS), heads(SWA_KV_HEADS)
    return pl.pallas_call(
        _proj_rope_kernel,
        grid=(b, s // tm),
        in_specs=[pl.BlockSpec((None, tm, d), lambda bi, i: (bi, i, 0)),
                  pl.BlockSpec((None, tm, 1), lambda bi, i: (bi, i, 0)),
                  pl.BlockSpec((1, LANES), lambda bi, i: (0, 0)),
                  pl.BlockSpec((d, n), lambda bi, i: (0, 0))],
        out_specs=[qspec, kspec, vspec],
        out_shape=[qs, ks, vs],
        compiler_params=_params("parallel", "parallel"),
        name="proj_rope",
    )(x, positions.reshape(b, s, 1), freq_row, w_bf16)


def _swa_kernel(q_ref, kp_ref, kc_ref, vp_ref, vc_ref, sink_ref, o_ref):
    i = pl.program_id(1)
    w = SWA_WINDOW
    row = lax.broadcasted_iota(jnp.int32, (w, 2 * w), 0)
    col = lax.broadcasted_iota(jnp.int32, (w, 2 * w), 1)
    dist = row + w - col
    visible = (dist >= 0) & (dist < w) & ((col >= w) | (i > 0))
    outs = []
    for kv in range(SWA_KV_HEADS):
        kk = jnp.concatenate([kp_ref[kv], kc_ref[kv]], axis=0)
        vv = jnp.concatenate([vp_ref[kv], vc_ref[kv]], axis=0)
        for g in range(SWA_GROUP):
            hd = kv * SWA_GROUP + g
            s = lax.dot_general(q_ref[hd], kk, (((1,), (1,)), ((), ())), preferred_element_type=F32)
            s = jnp.where(visible, s, NEG_BIG)
            sink = sink_ref[hd:hd + 1, 0:1]
            m = jnp.maximum(jnp.max(s, axis=1, keepdims=True), sink)
            p = jnp.exp(s - m)
            denom = jnp.sum(p, axis=1, keepdims=True) + jnp.exp(sink - m)
            o = jnp.dot(p.astype(BF16), vv, preferred_element_type=F32)
            outs.append(o / denom)
    o_ref[...] = jnp.concatenate(outs, axis=-1).astype(o_ref.dtype)


def _swa(q, k, v, sinks):
    b, nh, s, dh = q.shape
    nkv = k.shape[1]
    w = SWA_WINDOW
    cur = pl.BlockSpec((None, nkv, w, dh), lambda bi, i: (bi, 0, i, 0))
    prev = pl.BlockSpec((None, nkv, w, dh), lambda bi, i: (bi, 0, jnp.maximum(i - 1, 0), 0))
    sink_tile = jnp.broadcast_to(sinks.astype(F32)[:, None], (nh, LANES))
    return pl.pallas_call(
        _swa_kernel,
        grid=(b, s // w),
        in_specs=[pl.BlockSpec((None, nh, w, dh), lambda bi, i: (bi, 0, i, 0)),
                  prev, cur, prev, cur,
                  pl.BlockSpec((nh, LANES), lambda bi, i: (0, 0))],
        out_specs=pl.BlockSpec((None, w, nh * dh), lambda bi, i: (bi, i, 0)),
        out_shape=jax.ShapeDtypeStruct((b, s, nh * dh), BF16),
        compiler_params=_params("parallel", "parallel"),
        name="swa",
    )(q, k, k, v, v, sink_tile)


def _route(logits_t):
    rows = [logits_t[e:e + 1, :] for e in range(N_EXPERTS)]
    mx = functools.reduce(jnp.maximum, rows)
    ex = [jnp.exp(r - mx) for r in rows]
    total = functools.reduce(lambda p, q: p + q, ex)
    probs = [e / total for e in ex]

    group_score = []
    for g in range(N_GROUPS):
        a, b, c, d = probs[g * GROUP_SIZE:(g + 1) * GROUP_SIZE]
        hi1, lo1 = jnp.maximum(a, b), jnp.minimum(a, b)
        hi2, lo2 = jnp.maximum(c, d), jnp.minimum(c, d)
        top1 = jnp.maximum(hi1, hi2)
        top2 = jnp.maximum(jnp.minimum(hi1, hi2), jnp.maximum(lo1, lo2))
        group_score.append(top1 + top2)
    best = functools.reduce(jnp.maximum, group_score)
    g_sel = jnp.full(best.shape, N_GROUPS - 1, jnp.int32)
    for g in range(N_GROUPS - 2, -1, -1):
        g_sel = jnp.where(group_score[g] == best, g, g_sel)

    in_group = []
    for j in range(GROUP_SIZE):
        val = probs[(N_GROUPS - 1) * GROUP_SIZE + j]
        for g in range(N_GROUPS - 2, -1, -1):
            val = jnp.where(g_sel == g, probs[g * GROUP_SIZE + j], val)
        in_group.append(val)

    def first_argmax(vals):
        m = functools.reduce(jnp.maximum, vals)
        idx = jnp.full(m.shape, GROUP_SIZE - 1, jnp.int32)
        for j in range(GROUP_SIZE - 2, -1, -1):
            idx = jnp.where(vals[j] == m, j, idx)
        return m, idx

    w1, i1 = first_argmax(in_group)
    rest = [jnp.where(i1 == j, -1.0, in_group[j]) for j in range(GROUP_SIZE)]
    w2, i2 = first_argmax(rest)
    norm = w1 + w2
    e1 = g_sel * GROUP_SIZE + i1
    e2 = g_sel * GROUP_SIZE + i2
    gates = [jnp.where(e1 == e, w1 / norm, 0.0) + jnp.where(e2 == e, w2 / norm, 0.0)
             for e in range(N_EXPERTS)]
    return jnp.concatenate(gates, axis=0)


def _mix_out_kernel(*refs, n_head_major, alpha):
    x_ref = refs[0]
    y_refs = refs[1:1 + len(n_head_major)]
    w_refs = refs[1 + len(n_head_major):1 + 2 * len(n_head_major)]
    g_ref, b_ref, wrt_ref, brt_ref, x1_ref, x1b_ref, gates_ref = refs[1 + 2 * len(n_head_major):]
    h = alpha * x_ref[...]
    for y_ref, w_ref, nh in zip(y_refs, w_refs, n_head_major):
        if nh:
            y = jnp.concatenate([y_ref[j] for j in range(nh)], axis=-1)
        else:
            y = y_ref[...]
        h = h + jnp.dot(y, w_ref[...], preferred_element_type=F32)
    x1 = _layer_norm(h, g_ref[...], b_ref[...])
    x1_ref[...] = x1
    x1b_ref[...] = x1.astype(BF16)
    logits_t = lax.dot_general(wrt_ref[...], x1, (((1,), (1,)), ((), ())),
                               precision=lax.Precision.HIGHEST,
                               preferred_element_type=F32) + brt_ref[...]
    gates_ref[...] = _route(logits_t)


def _mix_out(x, ys, ws, ln_g, ln_b, w_router, b_router, alpha, tm):
    b, s, d = x.shape
    n_head_major = tuple(y.shape[1] if hm else 0 for y, hm in ys)
    y_specs = []
    for (y, hm) in ys:
        if hm:
            y_specs.append(pl.BlockSpec((None, y.shape[1], tm, y.shape[3]), lambda bi, i: (bi, 0, i, 0)))
        else:
            y_specs.append(pl.BlockSpec((None, tm, y.shape[2]), lambda bi, i: (bi, i, 0)))
    w_specs = [pl.BlockSpec(w.shape, lambda bi, i: (0, 0)) for w in ws]
    row = pl.BlockSpec((1, d), lambda bi, i: (0, 0))
    tok = pl.BlockSpec((None, tm, d), lambda bi, i: (bi, i, 0))
    return pl.pallas_call(
        functools.partial(_mix_out_kernel, n_head_major=n_head_major, alpha=alpha),
        grid=(b, s // tm),
        in_specs=[tok] + y_specs + w_specs + [row, row,
                  pl.BlockSpec((N_EXPERTS, d), lambda bi, i: (0, 0)),
                  pl.BlockSpec((N_EXPERTS, 1), lambda bi, i: (0, 0))],
        out_specs=[tok, tok, pl.BlockSpec((None, N_EXPERTS, tm), lambda bi, i: (bi, 0, i))],
        out_shape=[jax.ShapeDtypeStruct((b, s, d), F32), jax.ShapeDtypeStruct((b, s, d), BF16),
                   jax.ShapeDtypeStruct((b, N_EXPERTS, s), F32)],
        compiler_params=_params("parallel", "parallel"),
        name="mix_out_ln_router",
    )(x, *[y for y, _ in ys], *ws, ln_g.reshape(1, d), ln_b.reshape(1, d),
      w_router.T.astype(F32), b_router.astype(F32).reshape(N_EXPERTS, 1))


def _moe_kernel(x_ref, gates_ref, wg_ref, wu_ref, wd_ref, o_ref):
    e = pl.program_id(1)

    @pl.when(e == 0)
    def _():
        o_ref[...] = jnp.zeros_like(o_ref)

    x = x_ref[...]
    gates = gates_ref[...]
    lane = lax.broadcasted_iota(jnp.int32, gates.shape, 1)
    gate = jnp.sum(jnp.where(lane == e, gates, 0.0), axis=1, keepdims=True)
    hg = jnp.dot(x, wg_ref[...], preferred_element_type=F32)
    hu = jnp.dot(x, wu_ref[...], preferred_element_type=F32)
    hidden = (hg * _sigmoid(hg)) * hu * gate
    o_ref[...] += jnp.dot(hidden.astype(BF16), wd_ref[...], preferred_element_type=F32)


def _moe(x1b, gates, wg, wu, wd, tm):
    t, d = x1b.shape
    ne, _, f = wg.shape
    return pl.pallas_call(
        _moe_kernel,
        grid=(t // tm, ne),
        in_specs=[pl.BlockSpec((tm, d), lambda i, e: (i, 0)),
                  pl.BlockSpec((tm, ne), lambda i, e: (i, 0)),
                  pl.BlockSpec((None, d, f), lambda i, e: (e, 0, 0)),
                  pl.BlockSpec((None, d, f), lambda i, e: (e, 0, 0)),
                  pl.BlockSpec((None, f, d), lambda i, e: (e, 0, 0))],
        out_specs=pl.BlockSpec((tm, d), lambda i, e: (i, 0)),
        out_shape=jax.ShapeDtypeStruct((t, d), F32),
        compiler_params=_params("parallel", "arbitrary"),
        name="moe",
    )(x1b, gates, wg, wu, wd)


def _ln_ple_kernel(x1_ref, m_ref, p_ref, g_ref, b_ref, wg_ref, bg_ref, wp_ref, o_ref, *, alpha):
    x2 = _layer_norm(alpha * x1_ref[...] + m_ref[...], g_ref[...], b_ref[...])
    gate = _sigmoid(jnp.dot(x2.astype(BF16), wg_ref[...], preferred_element_type=F32) + bg_ref[...])
    emb = jnp.dot(p_ref[...].astype(BF16), wp_ref[...], preferred_element_type=F32)
    o_ref[...] = x2 + gate * emb


def _ln_ple(x1, m, p, ln_g, ln_b, wg, bg, wp, alpha, tm):
    t, d = x1.shape
    pd = p.shape[1]
    tok = pl.BlockSpec((tm, d), lambda i: (i, 0))
    row = pl.BlockSpec((1, d), lambda i: (0, 0))
    return pl.pallas_call(
        functools.partial(_ln_ple_kernel, alpha=alpha),
        grid=(t // tm,),
        in_specs=[tok, tok, pl.BlockSpec((tm, pd), lambda i: (i, 0)), row, row,
                  pl.BlockSpec((d, d), lambda i: (0, 0)), row, pl.BlockSpec((pd, d), lambda i: (0, 0))],
        out_specs=tok,
        out_shape=jax.ShapeDtypeStruct((t, d), F32),
        compiler_params=_params("parallel"),
        name="ln_ple",
    )(x1, m, p, ln_g.reshape(1, d), ln_b.reshape(1, d), wg, bg.reshape(1, d), wp)


def _pick_tile(n, target):
    t = min(n, target)
    while n % t:
        t //= 2
    return t


def kernel(x, p, positions, w_in_ab, w_out_ab, conv_w, conv_b, lru_w_r, lru_b_r, lru_w_i, lru_b_i, lru_lambda, w_qkv_c, w_out_c, sinks_c, ln_mix_g, ln_mix_b, ln_ffn_g, ln_ffn_b, w_router, b_router, exp_w_gate, exp_w_up, exp_w_down, ple_w_proj, ple_w_gate, ple_b_gate):
    b, s, d = x.shape
    depth = p.shape[0]
    t = b * s
    alpha = (2 * depth) ** 0.25
    tm = _pick_tile(s, 512)
    for i in range(depth):
        j = i // 2
        if i % 2 == 0:
            q, k, v, xr, gr = _proj_ab(x, w_in_ab[j].astype(BF16), tm)
            y_sb = _sb_attention(q, k, v, _pick_tile(s, 256))
            y_lru = _lru(xr, gr, conv_w[j], conv_b[j], _block_diag(lru_w_r[j]).astype(BF16), lru_b_r[j],
                         _block_diag(lru_w_i[j]).astype(BF16), lru_b_i[j], lru_lambda[j],
                         _pick_tile(s, 256))
            w_out = w_out_ab[j].astype(BF16)
            ys = [(y_sb, True), (y_lru, False)]
            ws = [w_out[:SB_WIDTH], w_out[SB_WIDTH:]]
        else:
            q, k, v = _proj_rope(x, positions, w_qkv_c[j].astype(BF16), tm)
            y = _swa(q, k, v, sinks_c[j])
            ys = [(y, False)]
            ws = [w_out_c[j].astype(BF16)]
        x1, x1b, gates_t = _mix_out(x, ys, ws, ln_mix_g[i], ln_mix_b[i], w_router, b_router, alpha, tm)
        gates = gates_t.transpose(0, 2, 1).reshape(t, N_EXPERTS)
        m = _moe(x1b.reshape(t, d), gates, exp_w_gate[i].astype(BF16), exp_w_up[i].astype(BF16),
                 exp_w_down[i].astype(BF16), _pick_tile(t, 1024))
        x = _ln_ple(x1.reshape(t, d), m, p[i].reshape(t, -1), ln_ffn_g[i], ln_ffn_b[i],
                    ple_w_gate[i].astype(BF16), ple_b_gate[i], ple_w_proj[i].astype(BF16), alpha,
                    _pick_tile(t, 512)).reshape(b, s, d)
    return x
```

```python
import functools
import math

import jax
import jax.numpy as jnp
from jax import lax
from jax.experimental import pallas as pl
from jax.experimental.pallas import tpu as pltpu

HEAD_DIM = 64
SB_HEADS = 8
SB_WIDTH = SB_HEADS * HEAD_DIM
LRU_WIDTH = 512
LRU_BLOCKS = 8
LRU_C = 8.0
CONV_WIDTH = 4
SWA_HEADS = 16
SWA_KV_HEADS = 4
SWA_GROUP = SWA_HEADS // SWA_KV_HEADS
SWA_WINDOW = 128
ROPE_THETA = 10000.0
N_EXPERTS = 16
N_GROUPS = 4
GROUP_SIZE = N_EXPERTS // N_GROUPS
LN_EPS = 1e-5
Q_SCALE = HEAD_DIM ** -0.5

LANES = 128
SUBLANES = 8
VMEM_LIMIT_BYTES = 48 * 1024 * 1024

NEG_BIG = -1e30

BF16 = jnp.bfloat16
F32 = jnp.float32


def _params(*semantics):
    return pltpu.CompilerParams(dimension_semantics=semantics, vmem_limit_bytes=VMEM_LIMIT_BYTES)


def _softplus(z):
    return jnp.maximum(z, 0.0) + jnp.log(1.0 + jnp.exp(-jnp.abs(z)))


def _sigmoid(z):
    return 1.0 / (1.0 + jnp.exp(-z))


def _layer_norm(y, g, b):
    mu = jnp.mean(y, axis=-1, keepdims=True)
    d = y - mu
    var = jnp.mean(d * d, axis=-1, keepdims=True)
    return d * lax.rsqrt(var + LN_EPS) * g + b


def _proj_ab_kernel(x_ref, w_ref, q_ref, k_ref, v_ref, xr_ref, gr_ref):
    xb = x_ref[...].astype(BF16)

    def chunk(c):
        return jnp.dot(xb, w_ref[:, c * SB_WIDTH:(c + 1) * SB_WIDTH], preferred_element_type=F32)

    for c, (ref, scale) in enumerate(((q_ref, Q_SCALE), (k_ref, None), (v_ref, None))):
        r = chunk(c)
        if scale is not None:
            r = r * scale
        for h in range(SB_HEADS):
            ref[h] = r[:, h * HEAD_DIM:(h + 1) * HEAD_DIM].astype(BF16)
    xr_ref[...] = chunk(3)
    gr_ref[...] = chunk(4)


def _proj_ab(x, w_bf16, tm):
    b, s, d = x.shape
    n = w_bf16.shape[1]
    heads = jax.ShapeDtypeStruct((b, SB_HEADS, s, HEAD_DIM), BF16)
    flat = jax.ShapeDtypeStruct((b, s, LRU_WIDTH), F32)
    head_spec = pl.BlockSpec((None, SB_HEADS, tm, HEAD_DIM), lambda bi, i: (bi, 0, i, 0))
    flat_spec = pl.BlockSpec((None, tm, LRU_WIDTH), lambda bi, i: (bi, i, 0))
    return pl.pallas_call(
        _proj_ab_kernel,
        grid=(b, s // tm),
        in_specs=[pl.BlockSpec((None, tm, d), lambda bi, i: (bi, i, 0)),
                  pl.BlockSpec((d, n), lambda bi, i: (0, 0))],
        out_specs=[head_spec, head_spec, head_spec, flat_spec, flat_spec],
        out_shape=[heads, heads, heads, flat, flat],
        compiler_params=_params("parallel", "parallel"),
        name="proj_ab",
    )(x, w_bf16)


SB_DEAD_LOG_WEIGHT = -105.0


def _sb_attn_kernel(q_ref, k_ref, v_ref, o_ref, *, tq, hp):
    i = pl.program_id(2)
    row = lax.broadcasted_iota(jnp.int32, (tq, tq), 0)
    col = lax.broadcasted_iota(jnp.int32, (tq, tq), 1)
    later = (row > col).astype(BF16)
    causal = col < row

    def block(jb, carries, accs, masked):
        start = pl.multiple_of(jb * tq, tq)
        new_carries, new_accs = [], []
        for h in range(hp):
            kj = k_ref[h, pl.ds(start, tq), :]
            vj = v_ref[h, pl.ds(start, tq), :]
            z = lax.dot_general(q_ref[h], kj, (((1,), (1,)), ((), ())), preferred_element_type=F32)
            sp = _softplus(z)
            log_keep = -sp
            if masked:
                log_keep = jnp.where(causal, log_keep, 0.0)
            after = jnp.dot(log_keep.astype(BF16), later, preferred_element_type=F32)
            w = jnp.exp((z - sp) + after + carries[h])
            if masked:
                w = jnp.where(causal, w, 0.0)
            new_accs.append(accs[h] + jnp.dot(w.astype(BF16), vj, preferred_element_type=F32))
            new_carries.append(carries[h] + jnp.sum(log_keep, axis=1, keepdims=True))
        return tuple(new_carries), tuple(new_accs)

    def live(carries):
        return functools.reduce(jnp.maximum, [jnp.max(c) for c in carries])

    carries, accs = block(i, (jnp.zeros((tq, 1), F32),) * hp, (jnp.zeros((tq, HEAD_DIM), F32),) * hp, True)

    def cond(state):
        return (state[0] < i) & (state[1] > SB_DEAD_LOG_WEIGHT)

    def body(state):
        step, _, carries, accs = state
        carries, accs = block(i - 1 - step, carries, accs, False)
        return step + 1, live(carries), carries, accs

    _, _, _, accs = lax.while_loop(cond, body, (jnp.int32(0), live(carries), carries, accs))
    for h in range(hp):
        o_ref[h] = accs[h].astype(o_ref.dtype)


def _sb_attention(q, k, v, tq, hp):
    b, h, s, dh = q.shape
    return pl.pallas_call(
        functools.partial(_sb_attn_kernel, tq=tq, hp=hp),
        grid=(b, h // hp, s // tq),
        in_specs=[pl.BlockSpec((None, hp, tq, dh), lambda bi, hi, i: (bi, hi, i, 0)),
                  pl.BlockSpec((None, hp, s, dh), lambda bi, hi, i: (bi, hi, 0, 0)),
                  pl.BlockSpec((None, hp, s, dh), lambda bi, hi, i: (bi, hi, 0, 0))],
        out_specs=pl.BlockSpec((None, hp, tq, dh), lambda bi, hi, i: (bi, hi, i, 0)),
        out_shape=jax.ShapeDtypeStruct((b, h, s, dh), BF16),
        compiler_params=_params("parallel", "parallel", "parallel"),
        name="sb_attention",
    )(q, k, v)


def _gelu_tanh(x):
    return 0.5 * x * (1.0 + jnp.tanh(math.sqrt(2.0 / math.pi) * (x + 0.044715 * (x * x * x))))


def _lru_kernel(xr_ref, gr_ref, cw_ref, cb_ref, wr_ref, br_ref, wi_ref, bi_ref, lam_ref, y_ref,
                xbuf, hprev, *, ts):
    @pl.when(pl.program_id(1) == 0)
    def _():
        xbuf[0:SUBLANES, :] = jnp.zeros((SUBLANES, LRU_WIDTH), F32)
        hprev[...] = jnp.zeros_like(hprev)

    xbuf[SUBLANES:SUBLANES + ts, :] = xr_ref[...]
    xc = cb_ref[...] + cw_ref[CONV_WIDTH - 1:CONV_WIDTH, :] * xbuf[SUBLANES:SUBLANES + ts, :]
    for kk in range(CONV_WIDTH - 1):
        off = SUBLANES - (CONV_WIDTH - 1) + kk
        xc = xc + cw_ref[kk:kk + 1, :] * xbuf[off:off + ts, :]
    xbuf[0:SUBLANES, :] = xbuf[ts:ts + SUBLANES, :]

    xcb = xc.astype(BF16)
    r = _sigmoid(jnp.dot(xcb, wr_ref[...], preferred_element_type=F32) + br_ref[...])
    gi = _sigmoid(jnp.dot(xcb, wi_ref[...], preferred_element_type=F32) + bi_ref[...])
    log_a = (-LRU_C) * r * _softplus(-lam_ref[...])
    a = jnp.exp(log_a)
    u = jnp.sqrt(1.0 - a * a) * (gi * xc)

    row = lax.broadcasted_iota(jnp.int32, (ts, LRU_WIDTH), 0)
    d = 1
    while d < ts:
        keep = row >= d
        a_sh = jnp.where(keep, pltpu.roll(a, d, axis=0), 1.0)
        u_sh = jnp.where(keep, pltpu.roll(u, d, axis=0), 0.0)
        u = a * u_sh + u
        a = a * a_sh
        d *= 2
    h = a * hprev[0:1, :] + u
    hprev[...] = jnp.broadcast_to(h[ts - 1:ts, :], hprev.shape)
    y_ref[...] = (_gelu_tanh(gr_ref[...]) * h).astype(y_ref.dtype)


def _lru(xr, gr, conv_w, conv_b, wr_bd, b_r, wi_bd, b_i, lam, ts):
    b, s, w = xr.shape
    seq_spec = pl.BlockSpec((None, ts, w), lambda bi, i: (bi, i, 0))

    def full(shape):
        return pl.BlockSpec(shape, lambda bi, i: (0,) * len(shape))

    return pl.pallas_call(
        functools.partial(_lru_kernel, ts=ts),
        grid=(b, s // ts),
        in_specs=[seq_spec, seq_spec, full((CONV_WIDTH, w)), full((1, w)), full((w, w)), full((1, w)),
                  full((w, w)), full((1, w)), full((1, w))],
        out_specs=seq_spec,
        out_shape=jax.ShapeDtypeStruct((b, s, w), BF16),
        scratch_shapes=[pltpu.VMEM((ts + 2 * SUBLANES, w), F32), pltpu.VMEM((SUBLANES, w), F32)],
        compiler_params=_params("parallel", "arbitrary"),
        name="rg_lru",
    )(xr, gr, conv_w, conv_b.reshape(1, w), wr_bd, b_r.reshape(1, w), wi_bd, b_i.reshape(1, w),
      lam.reshape(1, w))


def _block_diag(w):
    n, c, d = w.shape
    eye = jnp.eye(n, dtype=w.dtype)
    return (eye[:, None, :, None] * w[:, :, None, :]).reshape(n * c, n * d)


def _proj_rope_kernel(x_ref, pos_ref, freq_ref, w_ref, q_ref, k_ref, v_ref):
    tm = x_ref.shape[0]
    xb = x_ref[...].astype(BF16)
    ang = pos_ref[...].astype(F32) * freq_ref[...]
    cos = jnp.cos(ang)
    sin = jnp.sin(ang)
    lane = lax.broadcasted_iota(jnp.int32, (tm, LANES), 1)
    first_half = (lane % HEAD_DIM) < (HEAD_DIM // 2)
    heads_per_slab = LANES // HEAD_DIM

    def rope(r):
        upper = pltpu.roll(r, LANES - HEAD_DIM // 2, axis=1)
        lower = pltpu.roll(r, HEAD_DIM // 2, axis=1)
        return r * cos + jnp.where(first_half, -upper, lower) * sin

    def emit(ref, n_heads, col0, rotary, scale):
        for slab in range(n_heads // heads_per_slab):
            c0 = col0 + slab * LANES
            r = jnp.dot(xb, w_ref[:, c0:c0 + LANES], preferred_element_type=F32)
            if rotary:
                r = rope(r)
            if scale is not None:
                r = r * scale
            for j in range(heads_per_slab):
                ref[slab * heads_per_slab + j] = r[:, j * HEAD_DIM:(j + 1) * HEAD_DIM].astype(BF16)

    emit(q_ref, SWA_HEADS, 0, True, Q_SCALE)
    emit(k_ref, SWA_KV_HEADS, SWA_HEADS * HEAD_DIM, True, None)
    emit(v_ref, SWA_KV_HEADS, (SWA_HEADS + SWA_KV_HEADS) * HEAD_DIM, False, None)


def _proj_rope(x, positions, w_bf16, tm):
    b, s, d = x.shape
    n = w_bf16.shape[1]
    half = HEAD_DIM // 2
    inv_freq = ROPE_THETA ** (-jnp.arange(half, dtype=F32) / half)
    freq_row = jnp.tile(inv_freq, LANES // half).reshape(1, LANES)

    def heads(nh):
        return (jax.ShapeDtypeStruct((b, nh, s, HEAD_DIM), BF16),
                pl.BlockSpec((None, nh, tm, HEAD_DIM), lambda bi, i: (bi, 0, i, 0)))

    (qs, qspec), (ks, kspec), (vs, vspec) = heads(SWA_HEADS), heads(SWA_KV_HEADS), heads(SWA_KV_HEADS)
    return pl.pallas_call(
        _proj_rope_kernel,
        grid=(b, s // tm),
        in_specs=[pl.BlockSpec((None, tm, d), lambda bi, i: (bi, i, 0)),
                  pl.BlockSpec((None, tm, 1), lambda bi, i: (bi, i, 0)),
                  pl.BlockSpec((1, LANES), lambda bi, i: (0, 0)),
                  pl.BlockSpec((d, n), lambda bi, i: (0, 0))],
        out_specs=[qspec, kspec, vspec],
        out_shape=[qs, ks, vs],
        compiler_params=_params("parallel", "parallel"),
        name="proj_rope",
    )(x, positions.reshape(b, s, 1), freq_row, w_bf16)


def _swa_kernel(q_ref, kp_ref, kc_ref, vp_ref, vc_ref, sink_ref, o_ref):
    i = pl.program_id(1)
    w = SWA_WINDOW
    row = lax.broadcasted_iota(jnp.int32, (w, 2 * w), 0)
    col = lax.broadcasted_iota(jnp.int32, (w, 2 * w), 1)
    dist = row + w - col
    visible = (dist >= 0) & (dist < w) & ((col >= w) | (i > 0))
    outs = []
    for kv in range(SWA_KV_HEADS):
        kk = jnp.concatenate([kp_ref[kv], kc_ref[kv]], axis=0)
        vv = jnp.concatenate([vp_ref[kv], vc_ref[kv]], axis=0)
        for g in range(SWA_GROUP):
            hd = kv * SWA_GROUP + g
            s = lax.dot_general(q_ref[hd], kk, (((1,), (1,)), ((), ())), preferred_element_type=F32)
            s = jnp.where(visible, s, NEG_BIG)
            sink = sink_ref[hd:hd + 1, 0:1]
            m = jnp.maximum(jnp.max(s, axis=1, keepdims=True), sink)
            p = jnp.exp(s - m)
            denom = jnp.sum(p, axis=1, keepdims=True) + jnp.exp(sink - m)
            o = jnp.dot(p.astype(BF16), vv, preferred_element_type=F32)
            outs.append(o / denom)
    o_ref[...] = jnp.concatenate(outs, axis=-1).astype(o_ref.dtype)


def _swa(q, k, v, sinks):
    b, nh, s, dh = q.shape
    nkv = k.shape[1]
    w = SWA_WINDOW
    cur = pl.BlockSpec((None, nkv, w, dh), lambda bi, i: (bi, 0, i, 0))
    prev = pl.BlockSpec((None, nkv, w, dh), lambda bi, i: (bi, 0, jnp.maximum(i - 1, 0), 0))
    sink_tile = jnp.broadcast_to(sinks.astype(F32)[:, None], (nh, LANES))
    return pl.pallas_call(
        _swa_kernel,
        grid=(b, s // w),
        in_specs=[pl.BlockSpec((None, nh, w, dh), lambda bi, i: (bi, 0, i, 0)),
                  prev, cur, prev, cur,
                  pl.BlockSpec((nh, LANES), lambda bi, i: (0, 0))],
        out_specs=pl.BlockSpec((None, w, nh * dh), lambda bi, i: (bi, i, 0)),
        out_shape=jax.ShapeDtypeStruct((b, s, nh * dh), BF16),
        compiler_params=_params("parallel", "parallel"),
        name="swa",
    )(q, k, k, v, v, sink_tile)


def _route(logits_t):
    rows = [logits_t[e:e + 1, :] for e in range(N_EXPERTS)]
    mx = functools.reduce(jnp.maximum, rows)
    ex = [jnp.exp(r - mx) for r in rows]
    total = functools.reduce(lambda p, q: p + q, ex)
    probs = [e / total for e in ex]

    group_score = []
    for g in range(N_GROUPS):
        a, b, c, d = probs[g * GROUP_SIZE:(g + 1) * GROUP_SIZE]
        hi1, lo1 = jnp.maximum(a, b), jnp.minimum(a, b)
        hi2, lo2 = jnp.maximum(c, d), jnp.minimum(c, d)
        top1 = jnp.maximum(hi1, hi2)
        top2 = jnp.maximum(jnp.minimum(hi1, hi2), jnp.maximum(lo1, lo2))
        group_score.append(top1 + top2)
    best = functools.reduce(jnp.maximum, group_score)
    g_sel = jnp.full(best.shape, N_GROUPS - 1, jnp.int32)
    for g in range(N_GROUPS - 2, -1, -1):
        g_sel = jnp.where(group_score[g] == best, g, g_sel)

    in_group = []
    for j in range(GROUP_SIZE):
        val = probs[(N_GROUPS - 1) * GROUP_SIZE + j]
        for g in range(N_GROUPS - 2, -1, -1):
            val = jnp.where(g_sel == g, probs[g * GROUP_SIZE + j], val)
        in_group.append(val)

    def first_argmax(vals):
        m = functools.reduce(jnp.maximum, vals)
        idx = jnp.full(m.shape, GROUP_SIZE - 1, jnp.int32)
        for j in range(GROUP_SIZE - 2, -1, -1):
            idx = jnp.where(vals[j] == m, j, idx)
        return m, idx

    w1, i1 = first_argmax(in_group)
    rest = [jnp.where(i1 == j, -1.0, in_group[j]) for j in range(GROUP_SIZE)]
    w2, i2 = first_argmax(rest)
    norm = w1 + w2
    e1 = g_sel * GROUP_SIZE + i1
    e2 = g_sel * GROUP_SIZE + i2
    gates = [jnp.where(e1 == e, w1 / norm, 0.0) + jnp.where(e2 == e, w2 / norm, 0.0)
             for e in range(N_EXPERTS)]
    return jnp.concatenate(gates, axis=0)


def _mix_out_kernel(*refs, n_head_major, alpha):
    x_ref = refs[0]
    y_refs = refs[1:1 + len(n_head_major)]
    w_refs = refs[1 + len(n_head_major):1 + 2 * len(n_head_major)]
    g_ref, b_ref, wrt_ref, brt_ref, x1_ref, x1b_ref, gates_ref = refs[1 + 2 * len(n_head_major):]
    h = alpha * x_ref[...]
    for y_ref, w_ref, nh in zip(y_refs, w_refs, n_head_major):
        if nh:
            y = jnp.concatenate([y_ref[j] for j in range(nh)], axis=-1)
        else:
            y = y_ref[...]
        h = h + jnp.dot(y, w_ref[...], preferred_element_type=F32)
    x1 = _layer_norm(h, g_ref[...], b_ref[...])
    x1_ref[...] = x1
    x1b_ref[...] = x1.astype(BF16)
    logits_t = lax.dot_general(wrt_ref[...], x1, (((1,), (1,)), ((), ())),
                               precision=lax.Precision.HIGHEST,
                               preferred_element_type=F32) + brt_ref[...]
    gates_ref[...] = _route(logits_t)


def _mix_out(x, ys, ws, ln_g, ln_b, w_router, b_router, alpha, tm):
    b, s, d = x.shape
    n_head_major = tuple(y.shape[1] if hm else 0 for y, hm in ys)
    y_specs = []
    for (y, hm) in ys:
        if hm:
            y_specs.append(pl.BlockSpec((None, y.shape[1], tm, y.shape[3]), lambda bi, i: (bi, 0, i, 0)))
        else:
            y_specs.append(pl.BlockSpec((None, tm, y.shape[2]), lambda bi, i: (bi, i, 0)))
    w_specs = [pl.BlockSpec(w.shape, lambda bi, i: (0, 0)) for w in ws]
    row = pl.BlockSpec((1, d), lambda bi, i: (0, 0))
    tok = pl.BlockSpec((None, tm, d), lambda bi, i: (bi, i, 0))
    return pl.pallas_call(
        functools.partial(_mix_out_kernel, n_head_major=n_head_major, alpha=alpha),
        grid=(b, s // tm),
        in_specs=[tok] + y_specs + w_specs + [row, row,
                  pl.BlockSpec((N_EXPERTS, d), lambda bi, i: (0, 0)),
                  pl.BlockSpec((N_EXPERTS, 1), lambda bi, i: (0, 0))],
        out_specs=[tok, tok, pl.BlockSpec((None, N_EXPERTS, tm), lambda bi, i: (bi, 0, i))],
        out_shape=[jax.ShapeDtypeStruct((b, s, d), F32), jax.ShapeDtypeStruct((b, s, d), BF16),
                   jax.ShapeDtypeStruct((b, N_EXPERTS, s), F32)],
        compiler_params=_params("parallel", "parallel"),
        name="mix_out_ln_router",
    )(x, *[y for y, _ in ys], *ws, ln_g.reshape(1, d), ln_b.reshape(1, d),
      w_router.T.astype(F32), b_router.astype(F32).reshape(N_EXPERTS, 1))


def _moe_kernel(x_ref, gates_ref, wg_ref, wu_ref, wd_ref, o_ref):
    e = pl.program_id(1)

    @pl.when(e == 0)
    def _():
        o_ref[...] = jnp.zeros_like(o_ref)

    x = x_ref[...]
    gates = gates_ref[...]
    lane = lax.broadcasted_iota(jnp.int32, gates.shape, 1)
    gate = jnp.sum(jnp.where(lane == e, gates, 0.0), axis=1, keepdims=True)
    hg = jnp.dot(x, wg_ref[...], preferred_element_type=F32)
    hu = jnp.dot(x, wu_ref[...], preferred_element_type=F32)
    hidden = (hg * _sigmoid(hg)) * hu * gate
    o_ref[...] += jnp.dot(hidden.astype(BF16), wd_ref[...], preferred_element_type=F32)


def _moe(x1b, gates, wg, wu, wd, tm):
    t, d = x1b.shape
    ne, _, f = wg.shape
    return pl.pallas_call(
        _moe_kernel,
        grid=(t // tm, ne),
        in_specs=[pl.BlockSpec((tm, d), lambda i, e: (i, 0)),
                  pl.BlockSpec((tm, ne), lambda i, e: (i, 0)),
                  pl.BlockSpec((None, d, f), lambda i, e: (e, 0, 0)),
                  pl.BlockSpec((None, d, f), lambda i, e: (e, 0, 0)),
                  pl.BlockSpec((None, f, d), lambda i, e: (e, 0, 0))],
        out_specs=pl.BlockSpec((tm, d), lambda i, e: (i, 0)),
        out_shape=jax.ShapeDtypeStruct((t, d), F32),
        compiler_params=_params("parallel", "arbitrary"),
        name="moe",
    )(x1b, gates, wg, wu, wd)


def _ln_ple_kernel(x1_ref, m_ref, p_ref, g_ref, b_ref, wg_ref, bg_ref, wp_ref, o_ref, *, alpha):
    x2 = _layer_norm(alpha * x1_ref[...] + m_ref[...], g_ref[...], b_ref[...])
    gate = _sigmoid(jnp.dot(x2.astype(BF16), wg_ref[...], preferred_element_type=F32) + bg_ref[...])
    emb = jnp.dot(p_ref[...].astype(BF16), wp_ref[...], preferred_element_type=F32)
    o_ref[...] = x2 + gate * emb


def _ln_ple(x1, m, p, ln_g, ln_b, wg, bg, wp, alpha, tm):
    t, d = x1.shape
    pd = p.shape[1]
    tok = pl.BlockSpec((tm, d), lambda i: (i, 0))
    row = pl.BlockSpec((1, d), lambda i: (0, 0))
    return pl.pallas_call(
        functools.partial(_ln_ple_kernel, alpha=alpha),
        grid=(t // tm,),
        in_specs=[tok, tok, pl.BlockSpec((tm, pd), lambda i: (i, 0)), row, row,
                  pl.BlockSpec((d, d), lambda i: (0, 0)), row, pl.BlockSpec((pd, d), lambda i: (0, 0))],
        out_specs=tok,
        out_shape=jax.ShapeDtypeStruct((t, d), F32),
        compiler_params=_params("parallel"),
        name="ln_ple",
    )(x1, m, p, ln_g.reshape(1, d), ln_b.reshape(1, d), wg, bg.reshape(1, d), wp)


def _pick_tile(n, target):
    t = min(n, target)
    while n % t:
        t //= 2
    return t


def kernel(x, p, positions, w_in_ab, w_out_ab, conv_w, conv_b, lru_w_r, lru_b_r, lru_w_i, lru_b_i, lru_lambda, w_qkv_c, w_out_c, sinks_c, ln_mix_g, ln_mix_b, ln_ffn_g, ln_ffn_b, w_router, b_router, exp_w_gate, exp_w_up, exp_w_down, ple_w_proj, ple_w_gate, ple_b_gate):
    b, s, d = x.shape
    depth = p.shape[0]
    t = b * s
    alpha = (2 * depth) ** 0.25
    tm = _pick_tile(s, 512)
    for i in range(depth):
        j = i // 2
        if i % 2 == 0:
            q, k, v, xr, gr = _proj_ab(x, w_in_ab[j].astype(BF16), tm)
            y_sb = _sb_attention(q, k, v, _pick_tile(s, 256), 4)
            y_lru = _lru(xr, gr, conv_w[j], conv_b[j], _block_diag(lru_w_r[j]).astype(BF16), lru_b_r[j],
                         _block_diag(lru_w_i[j]).astype(BF16), lru_b_i[j], lru_lambda[j],
                         _pick_tile(s, 256))
            w_out = w_out_ab[j].astype(BF16)
            ys = [(y_sb, True), (y_lru, False)]
            ws = [w_out[:SB_WIDTH], w_out[SB_WIDTH:]]
        else:
            q, k, v = _proj_rope(x, positions, w_qkv_c[j].astype(BF16), tm)
            y = _swa(q, k, v, sinks_c[j])
            ys = [(y, False)]
            ws = [w_out_c[j].astype(BF16)]
        x1, x1b, gates_t = _mix_out(x, ys, ws, ln_mix_g[i], ln_mix_b[i], w_router, b_router, alpha, tm)
        gates = gates_t.transpose(0, 2, 1).reshape(t, N_EXPERTS)
        m = _moe(x1b.reshape(t, d), gates, exp_w_gate[i].astype(BF16), exp_w_up[i].astype(BF16),
                 exp_w_down[i].astype(BF16), _pick_tile(t, 1024))
        x = _ln_ple(x1.reshape(t, d), m, p[i].reshape(t, -1), ln_ffn_g[i], ln_ffn_b[i],
                    ple_w_gate[i].astype(BF16), ple_b_gate[i], ple_w_proj[i].astype(BF16), alpha,
                    _pick_tile(t, 512)).reshape(b, s, d)
    return x
```

```python
import functools
import math

import jax
import jax.numpy as jnp
from jax import lax
from jax.experimental import pallas as pl
from jax.experimental.pallas import tpu as pltpu

HEAD_DIM = 64
SB_HEADS = 8
SB_WIDTH = SB_HEADS * HEAD_DIM
LRU_WIDTH = 512
LRU_BLOCKS = 8
LRU_C = 8.0
CONV_WIDTH = 4
SWA_HEADS = 16
SWA_KV_HEADS = 4
SWA_GROUP = SWA_HEADS // SWA_KV_HEADS
SWA_WINDOW = 128
ROPE_THETA = 10000.0
N_EXPERTS = 16
N_GROUPS = 4
GROUP_SIZE = N_EXPERTS // N_GROUPS
LN_EPS = 1e-5
Q_SCALE = HEAD_DIM ** -0.5

LANES = 128
SUBLANES = 8
VMEM_LIMIT_BYTES = 48 * 1024 * 1024

NEG_BIG = -1e30

BF16 = jnp.bfloat16
F32 = jnp.float32


def _params(*semantics):
    return pltpu.CompilerParams(dimension_semantics=semantics, vmem_limit_bytes=VMEM_LIMIT_BYTES)


def _softplus(z):
    return jnp.maximum(z, 0.0) + jnp.log(1.0 + jnp.exp(-jnp.abs(z)))


def _sigmoid(z):
    return 1.0 / (1.0 + jnp.exp(-z))


def _layer_norm(y, g, b):
    mu = jnp.mean(y, axis=-1, keepdims=True)
    d = y - mu
    var = jnp.mean(d * d, axis=-1, keepdims=True)
    return d * lax.rsqrt(var + LN_EPS) * g + b


def _proj_ab_kernel(x_ref, w_ref, q_ref, k_ref, v_ref, xr_ref, gr_ref):
    xb = x_ref[...].astype(BF16)

    def chunk(c):
        return jnp.dot(xb, w_ref[:, c * SB_WIDTH:(c + 1) * SB_WIDTH], preferred_element_type=F32)

    for c, (ref, scale) in enumerate(((q_ref, Q_SCALE), (k_ref, None), (v_ref, None))):
        r = chunk(c)
        if scale is not None:
            r = r * scale
        for h in range(SB_HEADS):
            ref[h] = r[:, h * HEAD_DIM:(h + 1) * HEAD_DIM].astype(BF16)
    xr_ref[...] = chunk(3)
    gr_ref[...] = chunk(4)


def _proj_ab(x, w_bf16, tm):
    b, s, d = x.shape
    n = w_bf16.shape[1]
    heads = jax.ShapeDtypeStruct((b, SB_HEADS, s, HEAD_DIM), BF16)
    flat = jax.ShapeDtypeStruct((b, s, LRU_WIDTH), F32)
    head_spec = pl.BlockSpec((None, SB_HEADS, tm, HEAD_DIM), lambda bi, i: (bi, 0, i, 0))
    flat_spec = pl.BlockSpec((None, tm, LRU_WIDTH), lambda bi, i: (bi, i, 0))
    return pl.pallas_call(
        _proj_ab_kernel,
        grid=(b, s // tm),
        in_specs=[pl.BlockSpec((None, tm, d), lambda bi, i: (bi, i, 0)),
                  pl.BlockSpec((d, n), lambda bi, i: (0, 0))],
        out_specs=[head_spec, head_spec, head_spec, flat_spec, flat_spec],
        out_shape=[heads, heads, heads, flat, flat],
        compiler_params=_params("parallel", "parallel"),
        name="proj_ab",
    )(x, w_bf16)


SB_DEAD_LOG_WEIGHT = -105.0


def _sb_attn_kernel(q_ref, k_ref, v_ref, o_ref, *, tq, hp):
    i = pl.program_id(2)
    row = lax.broadcasted_iota(jnp.int32, (tq, tq), 0)
    col = lax.broadcasted_iota(jnp.int32, (tq, tq), 1)
    later = (row > col).astype(BF16)
    causal = col < row

    def block(jb, carries, accs, masked):
        start = pl.multiple_of(jb * tq, tq)
        new_carries, new_accs = [], []
        for h in range(hp):
            kj = k_ref[h, pl.ds(start, tq), :]
            vj = v_ref[h, pl.ds(start, tq), :]
            z = lax.dot_general(q_ref[h], kj, (((1,), (1,)), ((), ())), preferred_element_type=F32)
            sp = _softplus(z)
            log_keep = -sp
            if masked:
                log_keep = jnp.where(causal, log_keep, 0.0)
            after = jnp.dot(log_keep.astype(BF16), later, preferred_element_type=F32)
            w = jnp.exp((z - sp) + after + carries[h])
            if masked:
                w = jnp.where(causal, w, 0.0)
            new_accs.append(accs[h] + jnp.dot(w.astype(BF16), vj, preferred_element_type=F32))
            new_carries.append(carries[h] + jnp.sum(log_keep, axis=1, keepdims=True))
        return tuple(new_carries), tuple(new_accs)

    def live(carries):
        return functools.reduce(jnp.maximum, [jnp.max(c) for c in carries])

    carries, accs = block(i, (jnp.zeros((tq, 1), F32),) * hp, (jnp.zeros((tq, HEAD_DIM), F32),) * hp, True)

    def cond(state):
        return (state[0] < i) & (state[1] > SB_DEAD_LOG_WEIGHT)

    def body(state):
        step, _, carries, accs = state
        carries, accs = block(i - 1 - step, carries, accs, False)
        return step + 1, live(carries), carries, accs

    _, _, _, accs = lax.while_loop(cond, body, (jnp.int32(0), live(carries), carries, accs))
    for h in range(hp):
        o_ref[h] = accs[h].astype(o_ref.dtype)


def _sb_attention(q, k, v, tq, hp):
    b, h, s, dh = q.shape
    return pl.pallas_call(
        functools.partial(_sb_attn_kernel, tq=tq, hp=hp),
        grid=(b, h // hp, s // tq),
        in_specs=[pl.BlockSpec((None, hp, tq, dh), lambda bi, hi, i: (bi, hi, i, 0)),
                  pl.BlockSpec((None, hp, s, dh), lambda bi, hi, i: (bi, hi, 0, 0)),
                  pl.BlockSpec((None, hp, s, dh), lambda bi, hi, i: (bi, hi, 0, 0))],
        out_specs=pl.BlockSpec((None, hp, tq, dh), lambda bi, hi, i: (bi, hi, i, 0)),
        out_shape=jax.ShapeDtypeStruct((b, h, s, dh), BF16),
        compiler_params=_params("parallel", "parallel", "parallel"),
        name="sb_attention",
    )(q, k, v)


def _gelu_tanh(x):
    return 0.5 * x * (1.0 + jnp.tanh(math.sqrt(2.0 / math.pi) * (x + 0.044715 * (x * x * x))))


def _lru_kernel(xr_ref, gr_ref, cw_ref, cb_ref, wr_ref, br_ref, wi_ref, bi_ref, lam_ref, y_ref,
                xbuf, hprev, *, ts):
    @pl.when(pl.program_id(1) == 0)
    def _():
        xbuf[0:SUBLANES, :] = jnp.zeros((SUBLANES, LRU_WIDTH), F32)
        hprev[...] = jnp.zeros_like(hprev)

    xbuf[SUBLANES:SUBLANES + ts, :] = xr_ref[...]
    xc = cb_ref[...] + cw_ref[CONV_WIDTH - 1:CONV_WIDTH, :] * xbuf[SUBLANES:SUBLANES + ts, :]
    for kk in range(CONV_WIDTH - 1):
        off = SUBLANES - (CONV_WIDTH - 1) + kk
        xc = xc + cw_ref[kk:kk + 1, :] * xbuf[off:off + ts, :]
    xbuf[0:SUBLANES, :] = xbuf[ts:ts + SUBLANES, :]

    xcb = xc.astype(BF16)
    r = _sigmoid(jnp.dot(xcb, wr_ref[...], preferred_element_type=F32) + br_ref[...])
    gi = _sigmoid(jnp.dot(xcb, wi_ref[...], preferred_element_type=F32) + bi_ref[...])
    log_a = (-LRU_C) * r * _softplus(-lam_ref[...])
    a = jnp.exp(log_a)
    u = jnp.sqrt(1.0 - a * a) * (gi * xc)

    row = lax.broadcasted_iota(jnp.int32, (ts, LRU_WIDTH), 0)
    d = 1
    while d < ts:
        keep = row >= d
        a_sh = jnp.where(keep, pltpu.roll(a, d, axis=0), 1.0)
        u_sh = jnp.where(keep, pltpu.roll(u, d, axis=0), 0.0)
        u = a * u_sh + u
        a = a * a_sh
        d *= 2
    h = a * hprev[0:1, :] + u
    hprev[...] = jnp.broadcast_to(h[ts - 1:ts, :], hprev.shape)
    y_ref[...] = (_gelu_tanh(gr_ref[...]) * h).astype(y_ref.dtype)


def _lru(xr, gr, conv_w, conv_b, wr_bd, b_r, wi_bd, b_i, lam, ts):
    b, s, w = xr.shape
    seq_spec = pl.BlockSpec((None, ts, w), lambda bi, i: (bi, i, 0))

    def full(shape):
        return pl.BlockSpec(shape, lambda bi, i: (0,) * len(shape))

    return pl.pallas_call(
        functools.partial(_lru_kernel, ts=ts),
        grid=(b, s // ts),
        in_specs=[seq_spec, seq_spec, full((CONV_WIDTH, w)), full((1, w)), full((w, w)), full((1, w)),
                  full((w, w)), full((1, w)), full((1, w))],
        out_specs=seq_spec,
        out_shape=jax.ShapeDtypeStruct((b, s, w), BF16),
        scratch_shapes=[pltpu.VMEM((ts + 2 * SUBLANES, w), F32), pltpu.VMEM((SUBLANES, w), F32)],
        compiler_params=_params("parallel", "arbitrary"),
        name="rg_lru",
    )(xr, gr, conv_w, conv_b.reshape(1, w), wr_bd, b_r.reshape(1, w), wi_bd, b_i.reshape(1, w),
      lam.reshape(1, w))


def _block_diag(w):
    n, c, d = w.shape
    eye = jnp.eye(n, dtype=w.dtype)
    return (eye[:, None, :, None] * w[:, :, None, :]).reshape(n * c, n * d)


def _proj_rope_kernel(x_ref, pos_ref, freq_ref, w_ref, q_ref, k_ref, vt_ref):
    tm = x_ref.shape[0]
    xb = x_ref[...].astype(BF16)
    ang = pos_ref[...].astype(F32) * freq_ref[...]
    cos = jnp.cos(ang)
    sin = jnp.sin(ang)
    lane = lax.broadcasted_iota(jnp.int32, (tm, LANES), 1)
    first_half = (lane % HEAD_DIM) < (HEAD_DIM // 2)
    heads_per_slab = LANES // HEAD_DIM

    def rope(r):
        upper = pltpu.roll(r, LANES - HEAD_DIM // 2, axis=1)
        lower = pltpu.roll(r, HEAD_DIM // 2, axis=1)
        return r * cos + jnp.where(first_half, -upper, lower) * sin

    def emit(ref, n_heads, col0, rotary, scale):
        for slab in range(n_heads // heads_per_slab):
            c0 = col0 + slab * LANES
            r = jnp.dot(xb, w_ref[:, c0:c0 + LANES], preferred_element_type=F32)
            if rotary:
                r = rope(r)
            if scale is not None:
                r = r * scale
            for j in range(heads_per_slab):
                ref[slab * heads_per_slab + j] = r[:, j * HEAD_DIM:(j + 1) * HEAD_DIM].astype(BF16)

    emit(q_ref, SWA_HEADS, 0, True, Q_SCALE)
    emit(k_ref, SWA_KV_HEADS, SWA_HEADS * HEAD_DIM, True, None)
    v0 = (SWA_HEADS + SWA_KV_HEADS) * HEAD_DIM
    for slab in range(SWA_KV_HEADS // heads_per_slab):
        r = jnp.dot(xb, w_ref[:, v0 + slab * LANES:v0 + (slab + 1) * LANES], preferred_element_type=F32)
        rt = r.T
        for j in range(heads_per_slab):
            vt_ref[slab * heads_per_slab + j] = rt[j * HEAD_DIM:(j + 1) * HEAD_DIM, :].astype(BF16)


def _proj_rope(x, positions, w_bf16, tm):
    b, s, d = x.shape
    n = w_bf16.shape[1]
    half = HEAD_DIM // 2
    inv_freq = ROPE_THETA ** (-jnp.arange(half, dtype=F32) / half)
    freq_row = jnp.tile(inv_freq, LANES // half).reshape(1, LANES)

    def heads(nh):
        return (jax.ShapeDtypeStruct((b, nh, s, HEAD_DIM), BF16),
                pl.BlockSpec((None, nh, tm, HEAD_DIM), lambda bi, i: (bi, 0, i, 0)))

    (qs, qspec), (ks, kspec) = heads(SWA_HEADS), heads(SWA_KV_HEADS)
    vs = jax.ShapeDtypeStruct((b, SWA_KV_HEADS, HEAD_DIM, s), BF16)
    vspec = pl.BlockSpec((None, SWA_KV_HEADS, HEAD_DIM, tm), lambda bi, i: (bi, 0, 0, i))
    return pl.pallas_call(
        _proj_rope_kernel,
        grid=(b, s // tm),
        in_specs=[pl.BlockSpec((None, tm, d), lambda bi, i: (bi, i, 0)),
                  pl.BlockSpec((None, tm, 1), lambda bi, i: (bi, i, 0)),
                  pl.BlockSpec((1, LANES), lambda bi, i: (0, 0)),
                  pl.BlockSpec((d, n), lambda bi, i: (0, 0))],
        out_specs=[qspec, kspec, vspec],
        out_shape=[qs, ks, vs],
        compiler_params=_params("parallel", "parallel"),
        name="proj_rope",
    )(x, positions.reshape(b, s, 1), freq_row, w_bf16)


def _reduce_rows(x, op):
    while x.shape[0] > SUBLANES:
        half = x.shape[0] // 2
        x = op(x[:half], x[half:])
    for shift in (4, 2, 1):
        x = op(x, pltpu.roll(x, shift, axis=0))
    return x[0:1]


def _swa_kernel(q_ref, kp_ref, kc_ref, vtp_ref, vtc_ref, sink_ref, o_ref):
    i = pl.program_id(1)
    w = SWA_WINDOW
    key = lax.broadcasted_iota(jnp.int32, (2 * w, w), 0)
    qry = lax.broadcasted_iota(jnp.int32, (2 * w, w), 1)
    dist = qry + w - key
    visible = (dist >= 0) & (dist < w) & ((key >= w) | (i > 0))
    bias = jnp.where(visible, 0.0, NEG_BIG)
    outs = []
    for kv in range(SWA_KV_HEADS):
        kk = jnp.concatenate([kp_ref[kv], kc_ref[kv]], axis=0)
        vvt = jnp.concatenate([vtp_ref[kv], vtc_ref[kv]], axis=1)
        for g in range(SWA_GROUP):
            hd = kv * SWA_GROUP + g
            st = lax.dot_general(kk, q_ref[hd], (((1,), (1,)), ((), ())), preferred_element_type=F32)
            st = st + bias
            sink = sink_ref[hd:hd + 1, :]
            m = jnp.maximum(_reduce_rows(st, jnp.maximum), sink)
            p = jnp.exp(st - m)
            denom = _reduce_rows(p, jnp.add) + jnp.exp(sink - m)
            ot = jnp.dot(vvt, p.astype(BF16), preferred_element_type=F32)
            outs.append(ot / denom)
    o_ref[...] = jnp.concatenate(outs, axis=0).T.astype(o_ref.dtype)


def _swa(q, k, vt, sinks):
    b, nh, s, dh = q.shape
    nkv = k.shape[1]
    w = SWA_WINDOW
    cur = pl.BlockSpec((None, nkv, w, dh), lambda bi, i: (bi, 0, i, 0))
    prev = pl.BlockSpec((None, nkv, w, dh), lambda bi, i: (bi, 0, jnp.maximum(i - 1, 0), 0))
    cur_t = pl.BlockSpec((None, nkv, dh, w), lambda bi, i: (bi, 0, 0, i))
    prev_t = pl.BlockSpec((None, nkv, dh, w), lambda bi, i: (bi, 0, 0, jnp.maximum(i - 1, 0)))
    sink_tile = jnp.broadcast_to(sinks.astype(F32)[:, None], (nh, w))
    return pl.pallas_call(
        _swa_kernel,
        grid=(b, s // w),
        in_specs=[pl.BlockSpec((None, nh, w, dh), lambda bi, i: (bi, 0, i, 0)),
                  prev, cur, prev_t, cur_t,
                  pl.BlockSpec((nh, w), lambda bi, i: (0, 0))],
        out_specs=pl.BlockSpec((None, w, nh * dh), lambda bi, i: (bi, i, 0)),
        out_shape=jax.ShapeDtypeStruct((b, s, nh * dh), BF16),
        compiler_params=_params("parallel", "parallel"),
        name="swa",
    )(q, k, k, vt, vt, sink_tile)


def _route(logits_t):
    rows = [logits_t[e:e + 1, :] for e in range(N_EXPERTS)]
    mx = functools.reduce(jnp.maximum, rows)
    ex = [jnp.exp(r - mx) for r in rows]
    total = functools.reduce(lambda p, q: p + q, ex)
    probs = [e / total for e in ex]

    group_score = []
    for g in range(N_GROUPS):
        a, b, c, d = probs[g * GROUP_SIZE:(g + 1) * GROUP_SIZE]
        hi1, lo1 = jnp.maximum(a, b), jnp.minimum(a, b)
        hi2, lo2 = jnp.maximum(c, d), jnp.minimum(c, d)
        top1 = jnp.maximum(hi1, hi2)
        top2 = jnp.maximum(jnp.minimum(hi1, hi2), jnp.maximum(lo1, lo2))
        group_score.append(top1 + top2)
    best = functools.reduce(jnp.maximum, group_score)
    g_sel = jnp.full(best.shape, N_GROUPS - 1, jnp.int32)
    for g in range(N_GROUPS - 2, -1, -1):
        g_sel = jnp.where(group_score[g] == best, g, g_sel)

    in_group = []
    for j in range(GROUP_SIZE):
        val = probs[(N_GROUPS - 1) * GROUP_SIZE + j]
        for g in range(N_GROUPS - 2, -1, -1):
            val = jnp.where(g_sel == g, probs[g * GROUP_SIZE + j], val)
        in_group.append(val)

    def first_argmax(vals):
        m = functools.reduce(jnp.maximum, vals)
        idx = jnp.full(m.shape, GROUP_SIZE - 1, jnp.int32)
        for j in range(GROUP_SIZE - 2, -1, -1):
            idx = jnp.where(vals[j] == m, j, idx)
        return m, idx

    w1, i1 = first_argmax(in_group)
    rest = [jnp.where(i1 == j, -1.0, in_group[j]) for j in range(GROUP_SIZE)]
    w2, i2 = first_argmax(rest)
    norm = w1 + w2
    e1 = g_sel * GROUP_SIZE + i1
    e2 = g_sel * GROUP_SIZE + i2
    gates = [jnp.where(e1 == e, w1 / norm, 0.0) + jnp.where(e2 == e, w2 / norm, 0.0)
             for e in range(N_EXPERTS)]
    return jnp.concatenate(gates, axis=0)


def _mix_out_kernel(*refs, n_head_major, alpha):
    x_ref = refs[0]
    y_refs = refs[1:1 + len(n_head_major)]
    w_refs = refs[1 + len(n_head_major):1 + 2 * len(n_head_major)]
    g_ref, b_ref, wrt_ref, brt_ref, x1_ref, x1b_ref, gates_ref = refs[1 + 2 * len(n_head_major):]
    h = alpha * x_ref[...]
    for y_ref, w_ref, nh in zip(y_refs, w_refs, n_head_major):
        if nh:
            y = jnp.concatenate([y_ref[j] for j in range(nh)], axis=-1)
        else:
            y = y_ref[...]
        h = h + jnp.dot(y, w_ref[...], preferred_element_type=F32)
    x1 = _layer_norm(h, g_ref[...], b_ref[...])
    x1_ref[...] = x1
    x1b_ref[...] = x1.astype(BF16)
    logits_t = lax.dot_general(wrt_ref[...], x1, (((1,), (1,)), ((), ())),
                               precision=lax.Precision.HIGHEST,
                               preferred_element_type=F32) + brt_ref[...]
    gates_ref[...] = _route(logits_t)


def _mix_out(x, ys, ws, ln_g, ln_b, w_router, b_router, alpha, tm):
    b, s, d = x.shape
    n_head_major = tuple(y.shape[1] if hm else 0 for y, hm in ys)
    y_specs = []
    for (y, hm) in ys:
        if hm:
            y_specs.append(pl.BlockSpec((None, y.shape[1], tm, y.shape[3]), lambda bi, i: (bi, 0, i, 0)))
        else:
            y_specs.append(pl.BlockSpec((None, tm, y.shape[2]), lambda bi, i: (bi, i, 0)))
    w_specs = [pl.BlockSpec(w.shape, lambda bi, i: (0, 0)) for w in ws]
    row = pl.BlockSpec((1, d), lambda bi, i: (0, 0))
    tok = pl.BlockSpec((None, tm, d), lambda bi, i: (bi, i, 0))
    return pl.pallas_call(
        functools.partial(_mix_out_kernel, n_head_major=n_head_major, alpha=alpha),
        grid=(b, s // tm),
        in_specs=[tok] + y_specs + w_specs + [row, row,
                  pl.BlockSpec((N_EXPERTS, d), lambda bi, i: (0, 0)),
                  pl.BlockSpec((N_EXPERTS, 1), lambda bi, i: (0, 0))],
        out_specs=[tok, tok, pl.BlockSpec((None, N_EXPERTS, tm), lambda bi, i: (bi, 0, i))],
        out_shape=[jax.ShapeDtypeStruct((b, s, d), F32), jax.ShapeDtypeStruct((b, s, d), BF16),
                   jax.ShapeDtypeStruct((b, N_EXPERTS, s), F32)],
        compiler_params=_params("parallel", "parallel"),
        name="mix_out_ln_router",
    )(x, *[y for y, _ in ys], *ws, ln_g.reshape(1, d), ln_b.reshape(1, d),
      w_router.T.astype(F32), b_router.astype(F32).reshape(N_EXPERTS, 1))


def _moe_kernel(x_ref, gates_ref, wg_ref, wu_ref, wd_ref, o_ref):
    e = pl.program_id(1)

    @pl.when(e == 0)
    def _():
        o_ref[...] = jnp.zeros_like(o_ref)

    x = x_ref[...]
    gates = gates_ref[...]
    lane = lax.broadcasted_iota(jnp.int32, gates.shape, 1)
    gate = jnp.sum(jnp.where(lane == e, gates, 0.0), axis=1, keepdims=True)
    hg = jnp.dot(x, wg_ref[...], preferred_element_type=F32)
    hu = jnp.dot(x, wu_ref[...], preferred_element_type=F32)
    hidden = (hg * _sigmoid(hg)) * hu * gate
    o_ref[...] += jnp.dot(hidden.astype(BF16), wd_ref[...], preferred_element_type=F32)


def _moe(x1b, gates, wg, wu, wd, tm):
    t, d = x1b.shape
    ne, _, f = wg.shape
    return pl.pallas_call(
        _moe_kernel,
        grid=(t // tm, ne),
        in_specs=[pl.BlockSpec((tm, d), lambda i, e: (i, 0)),
                  pl.BlockSpec((tm, ne), lambda i, e: (i, 0)),
                  pl.BlockSpec((None, d, f), lambda i, e: (e, 0, 0)),
                  pl.BlockSpec((None, d, f), lambda i, e: (e, 0, 0)),
                  pl.BlockSpec((None, f, d), lambda i, e: (e, 0, 0))],
        out_specs=pl.BlockSpec((tm, d), lambda i, e: (i, 0)),
        out_shape=jax.ShapeDtypeStruct((t, d), F32),
        compiler_params=_params("parallel", "arbitrary"),
        name="moe",
    )(x1b, gates, wg, wu, wd)


def _ln_ple_kernel(x1_ref, m_ref, p_ref, g_ref, b_ref, wg_ref, bg_ref, wp_ref, o_ref, *, alpha):
    x2 = _layer_norm(alpha * x1_ref[...] + m_ref[...], g_ref[...], b_ref[...])
    gate = _sigmoid(jnp.dot(x2.astype(BF16), wg_ref[...], preferred_element_type=F32) + bg_ref[...])
    emb = jnp.dot(p_ref[...].astype(BF16), wp_ref[...], preferred_element_type=F32)
    o_ref[...] = x2 + gate * emb


def _ln_ple(x1, m, p, ln_g, ln_b, wg, bg, wp, alpha, tm):
    t, d = x1.shape
    pd = p.shape[1]
    tok = pl.BlockSpec((tm, d), lambda i: (i, 0))
    row = pl.BlockSpec((1, d), lambda i: (0, 0))
    return pl.pallas_call(
        functools.partial(_ln_ple_kernel, alpha=alpha),
        grid=(t // tm,),
        in_specs=[tok, tok, pl.BlockSpec((tm, pd), lambda i: (i, 0)), row, row,
                  pl.BlockSpec((d, d), lambda i: (0, 0)), row, pl.BlockSpec((pd, d), lambda i: (0, 0))],
        out_specs=tok,
        out_shape=jax.ShapeDtypeStruct((t, d), F32),
        compiler_params=_params("parallel"),
        name="ln_ple",
    )(x1, m, p, ln_g.reshape(1, d), ln_b.reshape(1, d), wg, bg.reshape(1, d), wp)


def _pick_tile(n, target):
    t = min(n, target)
    while n % t:
        t //= 2
    return t


def kernel(x, p, positions, w_in_ab, w_out_ab, conv_w, conv_b, lru_w_r, lru_b_r, lru_w_i, lru_b_i, lru_lambda, w_qkv_c, w_out_c, sinks_c, ln_mix_g, ln_mix_b, ln_ffn_g, ln_ffn_b, w_router, b_router, exp_w_gate, exp_w_up, exp_w_down, ple_w_proj, ple_w_gate, ple_b_gate):
    b, s, d = x.shape
    depth = p.shape[0]
    t = b * s
    alpha = (2 * depth) ** 0.25
    tm = _pick_tile(s, 512)
    for i in range(depth):
        j = i // 2
        if i % 2 == 0:
            q, k, v, xr, gr = _proj_ab(x, w_in_ab[j].astype(BF16), tm)
            y_sb = _sb_attention(q, k, v, _pick_tile(s, 256), 4)
            y_lru = _lru(xr, gr, conv_w[j], conv_b[j], _block_diag(lru_w_r[j]).astype(BF16), lru_b_r[j],
                         _block_diag(lru_w_i[j]).astype(BF16), lru_b_i[j], lru_lambda[j],
                         _pick_tile(s, 256))
            w_out = w_out_ab[j].astype(BF16)
            ys = [(y_sb, True), (y_lru, False)]
            ws = [w_out[:SB_WIDTH], w_out[SB_WIDTH:]]
        else:
            q, k, v = _proj_rope(x, positions, w_qkv_c[j].astype(BF16), tm)
            y = _swa(q, k, v, sinks_c[j])
            ys = [(y, False)]
            ws = [w_out_c[j].astype(BF16)]
        x1, x1b, gates_t = _mix_out(x, ys, ws, ln_mix_g[i], ln_mix_b[i], w_router, b_router, alpha, tm)
        gates = gates_t.transpose(0, 2, 1).reshape(t, N_EXPERTS)
        m = _moe(x1b.reshape(t, d), gates, exp_w_gate[i].astype(BF16), exp_w_up[i].astype(BF16),
                 exp_w_down[i].astype(BF16), _pick_tile(t, 1024))
        x = _ln_ple(x1.reshape(t, d), m, p[i].reshape(t, -1), ln_ffn_g[i], ln_ffn_b[i],
                    ple_w_gate[i].astype(BF16), ple_b_gate[i], ple_w_proj[i].astype(BF16), alpha,
                    _pick_tile(t, 512)).reshape(b, s, d)
    return x
```

```python
import functools
import math

import jax
import jax.numpy as jnp
from jax import lax
from jax.experimental import pallas as pl
from jax.experimental.pallas import tpu as pltpu

HEAD_DIM = 64
SB_HEADS = 8
SB_WIDTH = SB_HEADS * HEAD_DIM
LRU_WIDTH = 512
LRU_BLOCKS = 8
LRU_C = 8.0
CONV_WIDTH = 4
SWA_HEADS = 16
SWA_KV_HEADS = 4
SWA_GROUP = SWA_HEADS // SWA_KV_HEADS
SWA_WINDOW = 128
ROPE_THETA = 10000.0
N_EXPERTS = 16
N_GROUPS = 4
GROUP_SIZE = N_EXPERTS // N_GROUPS
LN_EPS = 1e-5
Q_SCALE = HEAD_DIM ** -0.5

LANES = 128
SUBLANES = 8
VMEM_LIMIT_BYTES = 48 * 1024 * 1024

NEG_BIG = -1e30

BF16 = jnp.bfloat16
F32 = jnp.float32


def _params(*semantics):
    return pltpu.CompilerParams(dimension_semantics=semantics, vmem_limit_bytes=VMEM_LIMIT_BYTES)


def _softplus(z):
    return jnp.maximum(z, 0.0) + jnp.log(1.0 + jnp.exp(-jnp.abs(z)))


def _sigmoid(z):
    return 1.0 / (1.0 + jnp.exp(-z))


def _layer_norm(y, g, b):
    mu = jnp.mean(y, axis=-1, keepdims=True)
    d = y - mu
    var = jnp.mean(d * d, axis=-1, keepdims=True)
    return d * lax.rsqrt(var + LN_EPS) * g + b


def _proj_ab_kernel(x_ref, w_ref, q_ref, k_ref, v_ref, xr_ref, gr_ref):
    xb = x_ref[...].astype(BF16)

    def chunk(c):
        return jnp.dot(xb, w_ref[:, c * SB_WIDTH:(c + 1) * SB_WIDTH], preferred_element_type=F32)

    for c, (ref, scale) in enumerate(((q_ref, Q_SCALE), (k_ref, None), (v_ref, None))):
        r = chunk(c)
        if scale is not None:
            r = r * scale
        for h in range(SB_HEADS):
            ref[h] = r[:, h * HEAD_DIM:(h + 1) * HEAD_DIM].astype(BF16)
    xr_ref[...] = chunk(3)
    gr_ref[...] = chunk(4)


def _proj_ab(x, w_bf16, tm):
    b, s, d = x.shape
    n = w_bf16.shape[1]
    heads = jax.ShapeDtypeStruct((b, SB_HEADS, s, HEAD_DIM), BF16)
    flat = jax.ShapeDtypeStruct((b, s, LRU_WIDTH), F32)
    head_spec = pl.BlockSpec((None, SB_HEADS, tm, HEAD_DIM), lambda bi, i: (bi, 0, i, 0))
    flat_spec = pl.BlockSpec((None, tm, LRU_WIDTH), lambda bi, i: (bi, i, 0))
    return pl.pallas_call(
        _proj_ab_kernel,
        grid=(b, s // tm),
        in_specs=[pl.BlockSpec((None, tm, d), lambda bi, i: (bi, i, 0)),
                  pl.BlockSpec((d, n), lambda bi, i: (0, 0))],
        out_specs=[head_spec, head_spec, head_spec, flat_spec, flat_spec],
        out_shape=[heads, heads, heads, flat, flat],
        compiler_params=_params("parallel", "parallel"),
        name="proj_ab",
    )(x, w_bf16)


SB_DEAD_LOG_WEIGHT = -105.0


def _sb_attn_kernel(q_ref, k_ref, v_ref, o_ref, *, tq, hp):
    i = pl.program_id(2)
    row = lax.broadcasted_iota(jnp.int32, (tq, tq), 0)
    col = lax.broadcasted_iota(jnp.int32, (tq, tq), 1)
    later = (row > col).astype(BF16)
    causal = col < row

    def block(jb, carries, accs, masked):
        start = pl.multiple_of(jb * tq, tq)
        new_carries, new_accs = [], []
        for h in range(hp):
            kj = k_ref[h, pl.ds(start, tq), :]
            vj = v_ref[h, pl.ds(start, tq), :]
            z = lax.dot_general(q_ref[h], kj, (((1,), (1,)), ((), ())), preferred_element_type=F32)
            sp = _softplus(z)
            log_keep = -sp
            if masked:
                log_keep = jnp.where(causal, log_keep, 0.0)
            after = jnp.dot(log_keep.astype(BF16), later, preferred_element_type=F32)
            w = jnp.exp((z - sp) + after + carries[h])
            if masked:
                w = jnp.where(causal, w, 0.0)
            new_accs.append(accs[h] + jnp.dot(w.astype(BF16), vj, preferred_element_type=F32))
            new_carries.append(carries[h] + jnp.sum(log_keep, axis=1, keepdims=True))
        return tuple(new_carries), tuple(new_accs)

    def live(carries):
        return functools.reduce(jnp.maximum, [jnp.max(c) for c in carries])

    carries, accs = block(i, (jnp.zeros((tq, 1), F32),) * hp, (jnp.zeros((tq, HEAD_DIM), F32),) * hp, True)

    def cond(state):
        return (state[0] < i) & (state[1] > SB_DEAD_LOG_WEIGHT)

    def body(state):
        step, _, carries, accs = state
        carries, accs = block(i - 1 - step, carries, accs, False)
        return step + 1, live(carries), carries, accs

    _, _, _, accs = lax.while_loop(cond, body, (jnp.int32(0), live(carries), carries, accs))
    for h in range(hp):
        o_ref[h] = accs[h].astype(o_ref.dtype)


def _sb_attention(q, k, v, tq, hp):
    b, h, s, dh = q.shape
    return pl.pallas_call(
        functools.partial(_sb_attn_kernel, tq=tq, hp=hp),
        grid=(b, h // hp, s // tq),
        in_specs=[pl.BlockSpec((None, hp, tq, dh), lambda bi, hi, i: (bi, hi, i, 0)),
                  pl.BlockSpec((None, hp, s, dh), lambda bi, hi, i: (bi, hi, 0, 0)),
                  pl.BlockSpec((None, hp, s, dh), lambda bi, hi, i: (bi, hi, 0, 0))],
        out_specs=pl.BlockSpec((None, hp, tq, dh), lambda bi, hi, i: (bi, hi, i, 0)),
        out_shape=jax.ShapeDtypeStruct((b, h, s, dh), BF16),
        compiler_params=_params("parallel", "parallel", "parallel"),
        name="sb_attention",
    )(q, k, v)


def _gelu_tanh(x):
    return 0.5 * x * (1.0 + jnp.tanh(math.sqrt(2.0 / math.pi) * (x + 0.044715 * (x * x * x))))


def _lru_kernel(xr_ref, gr_ref, cw_ref, cb_ref, wr_ref, br_ref, wi_ref, bi_ref, lam_ref, y_ref,
                xbuf, hprev, *, ts):
    @pl.when(pl.program_id(1) == 0)
    def _():
        xbuf[0:SUBLANES, :] = jnp.zeros((SUBLANES, LRU_WIDTH), F32)
        hprev[...] = jnp.zeros_like(hprev)

    xbuf[SUBLANES:SUBLANES + ts, :] = xr_ref[...]
    xc = cb_ref[...] + cw_ref[CONV_WIDTH - 1:CONV_WIDTH, :] * xbuf[SUBLANES:SUBLANES + ts, :]
    for kk in range(CONV_WIDTH - 1):
        off = SUBLANES - (CONV_WIDTH - 1) + kk
        xc = xc + cw_ref[kk:kk + 1, :] * xbuf[off:off + ts, :]
    xbuf[0:SUBLANES, :] = xbuf[ts:ts + SUBLANES, :]

    xcb = xc.astype(BF16)
    r = _sigmoid(jnp.dot(xcb, wr_ref[...], preferred_element_type=F32) + br_ref[...])
    gi = _sigmoid(jnp.dot(xcb, wi_ref[...], preferred_element_type=F32) + bi_ref[...])
    log_a = (-LRU_C) * r * _softplus(-lam_ref[...])
    a = jnp.exp(log_a)
    u = jnp.sqrt(1.0 - a * a) * (gi * xc)

    row = lax.broadcasted_iota(jnp.int32, (ts, LRU_WIDTH), 0)
    d = 1
    while d < ts:
        keep = row >= d
        a_sh = jnp.where(keep, pltpu.roll(a, d, axis=0), 1.0)
        u_sh = jnp.where(keep, pltpu.roll(u, d, axis=0), 0.0)
        u = a * u_sh + u
        a = a * a_sh
        d *= 2
    h = a * hprev[0:1, :] + u
    hprev[...] = jnp.broadcast_to(h[ts - 1:ts, :], hprev.shape)
    y_ref[...] = (_gelu_tanh(gr_ref[...]) * h).astype(y_ref.dtype)


def _lru(xr, gr, conv_w, conv_b, wr_bd, b_r, wi_bd, b_i, lam, ts):
    b, s, w = xr.shape
    seq_spec = pl.BlockSpec((None, ts, w), lambda bi, i: (bi, i, 0))

    def full(shape):
        return pl.BlockSpec(shape, lambda bi, i: (0,) * len(shape))

    return pl.pallas_call(
        functools.partial(_lru_kernel, ts=ts),
        grid=(b, s // ts),
        in_specs=[seq_spec, seq_spec, full((CONV_WIDTH, w)), full((1, w)), full((w, w)), full((1, w)),
                  full((w, w)), full((1, w)), full((1, w))],
        out_specs=seq_spec,
        out_shape=jax.ShapeDtypeStruct((b, s, w), BF16),
        scratch_shapes=[pltpu.VMEM((ts + 2 * SUBLANES, w), F32), pltpu.VMEM((SUBLANES, w), F32)],
        compiler_params=_params("parallel", "arbitrary"),
        name="rg_lru",
    )(xr, gr, conv_w, conv_b.reshape(1, w), wr_bd, b_r.reshape(1, w), wi_bd, b_i.reshape(1, w),
      lam.reshape(1, w))


def _block_diag(w):
    n, c, d = w.shape
    eye = jnp.eye(n, dtype=w.dtype)
    return (eye[:, None, :, None] * w[:, :, None, :]).reshape(n * c, n * d)


def _proj_rope_kernel(x_ref, pos_ref, freq_ref, w_ref, q_ref, k_ref, vt_ref):
    tm = x_ref.shape[0]
    xb = x_ref[...].astype(BF16)
    ang = pos_ref[...].astype(F32) * freq_ref[...]
    cos = jnp.cos(ang)
    sin = jnp.sin(ang)
    lane = lax.broadcasted_iota(jnp.int32, (tm, LANES), 1)
    first_half = (lane % HEAD_DIM) < (HEAD_DIM // 2)
    heads_per_slab = LANES // HEAD_DIM

    def rope(r):
        upper = pltpu.roll(r, LANES - HEAD_DIM // 2, axis=1)
        lower = pltpu.roll(r, HEAD_DIM // 2, axis=1)
        return r * cos + jnp.where(first_half, -upper, lower) * sin

    def emit(ref, n_heads, col0, rotary, scale):
        for slab in range(n_heads // heads_per_slab):
            c0 = col0 + slab * LANES
            r = jnp.dot(xb, w_ref[:, c0:c0 + LANES], preferred_element_type=F32)
            if rotary:
                r = rope(r)
            if scale is not None:
                r = r * scale
            for j in range(heads_per_slab):
                ref[slab * heads_per_slab + j] = r[:, j * HEAD_DIM:(j + 1) * HEAD_DIM].astype(BF16)

    emit(q_ref, SWA_HEADS, 0, True, Q_SCALE)
    emit(k_ref, SWA_KV_HEADS, SWA_HEADS * HEAD_DIM, True, None)
    v0 = (SWA_HEADS + SWA_KV_HEADS) * HEAD_DIM
    for slab in range(SWA_KV_HEADS // heads_per_slab):
        r = jnp.dot(xb, w_ref[:, v0 + slab * LANES:v0 + (slab + 1) * LANES], preferred_element_type=F32)
        rt = r.T
        for j in range(heads_per_slab):
            vt_ref[slab * heads_per_slab + j] = rt[j * HEAD_DIM:(j + 1) * HEAD_DIM, :].astype(BF16)


def _proj_rope(x, positions, w_bf16, tm):
    b, s, d = x.shape
    n = w_bf16.shape[1]
    half = HEAD_DIM // 2
    inv_freq = ROPE_THETA ** (-jnp.arange(half, dtype=F32) / half)
    freq_row = jnp.tile(inv_freq, LANES // half).reshape(1, LANES)

    def heads(nh):
        return (jax.ShapeDtypeStruct((b, nh, s, HEAD_DIM), BF16),
                pl.BlockSpec((None, nh, tm, HEAD_DIM), lambda bi, i: (bi, 0, i, 0)))

    (qs, qspec), (ks, kspec) = heads(SWA_HEADS), heads(SWA_KV_HEADS)
    vs = jax.ShapeDtypeStruct((b, SWA_KV_HEADS, HEAD_DIM, s), BF16)
    vspec = pl.BlockSpec((None, SWA_KV_HEADS, HEAD_DIM, tm), lambda bi, i: (bi, 0, 0, i))
    return pl.pallas_call(
        _proj_rope_kernel,
        grid=(b, s // tm),
        in_specs=[pl.BlockSpec((None, tm, d), lambda bi, i: (bi, i, 0)),
                  pl.BlockSpec((None, tm, 1), lambda bi, i: (bi, i, 0)),
                  pl.BlockSpec((1, LANES), lambda bi, i: (0, 0)),
                  pl.BlockSpec((d, n), lambda bi, i: (0, 0))],
        out_specs=[qspec, kspec, vspec],
        out_shape=[qs, ks, vs],
        compiler_params=_params("parallel", "parallel"),
        name="proj_rope",
    )(x, positions.reshape(b, s, 1), freq_row, w_bf16)


def _reduce_rows(x, op):
    while x.shape[0] > SUBLANES:
        half = x.shape[0] // 2
        x = op(x[:half], x[half:])
    for shift in (4, 2, 1):
        x = op(x, pltpu.roll(x, shift, axis=0))
    return x[0:1]


def _swa_kernel(q_ref, kp_ref, kc_ref, vtp_ref, vtc_ref, sink_ref, o_ref):
    i = pl.program_id(1)
    w = SWA_WINDOW
    key = lax.broadcasted_iota(jnp.int32, (2 * w, w), 0)
    qry = lax.broadcasted_iota(jnp.int32, (2 * w, w), 1)
    dist = qry + w - key
    visible = (dist >= 0) & (dist < w) & ((key >= w) | (i > 0))
    bias = jnp.where(visible, 0.0, NEG_BIG)
    outs = []
    for kv in range(SWA_KV_HEADS):
        kk = jnp.concatenate([kp_ref[kv], kc_ref[kv]], axis=0)
        vvt = jnp.concatenate([vtp_ref[kv], vtc_ref[kv]], axis=1)
        for g in range(SWA_GROUP):
            hd = kv * SWA_GROUP + g
            st = lax.dot_general(kk, q_ref[hd], (((1,), (1,)), ((), ())), preferred_element_type=F32)
            st = st + bias
            sink = sink_ref[hd:hd + 1, :]
            m = jnp.maximum(_reduce_rows(st, jnp.maximum), sink)
            p = jnp.exp(st - m)
            denom = _reduce_rows(p, jnp.add) + jnp.exp(sink - m)
            ot = jnp.dot(vvt, p.astype(BF16), preferred_element_type=F32)
            outs.append(ot / denom)
    o_ref[...] = jnp.concatenate(outs, axis=0).T.astype(o_ref.dtype)


def _swa(q, k, vt, sinks):
    b, nh, s, dh = q.shape
    nkv = k.shape[1]
    w = SWA_WINDOW
    cur = pl.BlockSpec((None, nkv, w, dh), lambda bi, i: (bi, 0, i, 0))
    prev = pl.BlockSpec((None, nkv, w, dh), lambda bi, i: (bi, 0, jnp.maximum(i - 1, 0), 0))
    cur_t = pl.BlockSpec((None, nkv, dh, w), lambda bi, i: (bi, 0, 0, i))
    prev_t = pl.BlockSpec((None, nkv, dh, w), lambda bi, i: (bi, 0, 0, jnp.maximum(i - 1, 0)))
    sink_tile = jnp.broadcast_to(sinks.astype(F32)[:, None], (nh, w))
    return pl.pallas_call(
        _swa_kernel,
        grid=(b, s // w),
        in_specs=[pl.BlockSpec((None, nh, w, dh), lambda bi, i: (bi, 0, i, 0)),
                  prev, cur, prev_t, cur_t,
                  pl.BlockSpec((nh, w), lambda bi, i: (0, 0))],
        out_specs=pl.BlockSpec((None, w, nh * dh), lambda bi, i: (bi, i, 0)),
        out_shape=jax.ShapeDtypeStruct((b, s, nh * dh), BF16),
        compiler_params=_params("parallel", "parallel"),
        name="swa",
    )(q, k, k, vt, vt, sink_tile)


PAIRS_PER_GROUP = GROUP_SIZE * (GROUP_SIZE - 1) // 2
N_BUCKETS = N_GROUPS * PAIRS_PER_GROUP
ROUTE_ROWS = SUBLANES


def _route(logits_t):
    rows = [logits_t[e:e + 1, :] for e in range(N_EXPERTS)]
    mx = functools.reduce(jnp.maximum, rows)
    ex = [jnp.exp(r - mx) for r in rows]
    total = functools.reduce(lambda p, q: p + q, ex)
    probs = [e / total for e in ex]

    group_score = []
    for g in range(N_GROUPS):
        a, b, c, d = probs[g * GROUP_SIZE:(g + 1) * GROUP_SIZE]
        hi1, lo1 = jnp.maximum(a, b), jnp.minimum(a, b)
        hi2, lo2 = jnp.maximum(c, d), jnp.minimum(c, d)
        top1 = jnp.maximum(hi1, hi2)
        top2 = jnp.maximum(jnp.minimum(hi1, hi2), jnp.maximum(lo1, lo2))
        group_score.append(top1 + top2)
    best = functools.reduce(jnp.maximum, group_score)
    g_sel = jnp.full(best.shape, N_GROUPS - 1, jnp.int32)
    for g in range(N_GROUPS - 2, -1, -1):
        g_sel = jnp.where(group_score[g] == best, g, g_sel)

    in_group = []
    for j in range(GROUP_SIZE):
        val = probs[(N_GROUPS - 1) * GROUP_SIZE + j]
        for g in range(N_GROUPS - 2, -1, -1):
            val = jnp.where(g_sel == g, probs[g * GROUP_SIZE + j], val)
        in_group.append(val)

    def first_argmax(vals):
        m = functools.reduce(jnp.maximum, vals)
        idx = jnp.full(m.shape, GROUP_SIZE - 1, jnp.int32)
        for j in range(GROUP_SIZE - 2, -1, -1):
            idx = jnp.where(vals[j] == m, j, idx)
        return m, idx

    w1, i1 = first_argmax(in_group)
    rest = [jnp.where(i1 == j, -1.0, in_group[j]) for j in range(GROUP_SIZE)]
    w2, i2 = first_argmax(rest)
    norm = w1 + w2
    first_is_lo = i1 < i2
    i_lo = jnp.minimum(i1, i2)
    i_hi = jnp.maximum(i1, i2)
    pair = jnp.where(i_lo == 0, i_hi - 1, jnp.where(i_lo == 1, i_hi + 1, PAIRS_PER_GROUP - 1))
    bucket = (g_sel * PAIRS_PER_GROUP + pair).astype(F32)
    w_lo = jnp.where(first_is_lo, w1, w2) / norm
    w_hi = jnp.where(first_is_lo, w2, w1) / norm
    return bucket, w_lo, w_hi


def _mix_out_kernel(*refs, n_head_major, alpha):
    x_ref = refs[0]
    y_refs = refs[1:1 + len(n_head_major)]
    w_refs = refs[1 + len(n_head_major):1 + 2 * len(n_head_major)]
    g_ref, b_ref, wrt_ref, brt_ref, x1e_ref, route_ref = refs[1 + 2 * len(n_head_major):]
    tm, d = x_ref.shape
    h = alpha * x_ref[...]
    for y_ref, w_ref, nh in zip(y_refs, w_refs, n_head_major):
        if nh:
            y = jnp.concatenate([y_ref[j] for j in range(nh)], axis=-1)
        else:
            y = y_ref[...]
        h = h + jnp.dot(y, w_ref[...], preferred_element_type=F32)
    x1 = _layer_norm(h, g_ref[...], b_ref[...])
    logits_t = lax.dot_general(wrt_ref[...], x1, (((1,), (1,)), ((), ())),
                               precision=lax.Precision.HIGHEST,
                               preferred_element_type=F32) + brt_ref[...]
    route = jnp.concatenate(list(_route(logits_t)) + [jnp.zeros((ROUTE_ROWS - 3, tm), F32)], axis=0)
    route_ref[...] = route
    x1e_ref[:, 0:d] = x1
    x1e_ref[:, d:d + LANES] = jnp.concatenate([route, jnp.zeros((LANES - ROUTE_ROWS, tm), F32)], axis=0).T


def _mix_out(x, ys, ws, ln_g, ln_b, w_router, b_router, alpha, tm):
    b, s, d = x.shape
    n_head_major = tuple(y.shape[1] if hm else 0 for y, hm in ys)
    y_specs = []
    for (y, hm) in ys:
        if hm:
            y_specs.append(pl.BlockSpec((None, y.shape[1], tm, y.shape[3]), lambda bi, i: (bi, 0, i, 0)))
        else:
            y_specs.append(pl.BlockSpec((None, tm, y.shape[2]), lambda bi, i: (bi, i, 0)))
    w_specs = [pl.BlockSpec(w.shape, lambda bi, i: (0, 0)) for w in ws]
    row = pl.BlockSpec((1, d), lambda bi, i: (0, 0))
    tok = pl.BlockSpec((None, tm, d), lambda bi, i: (bi, i, 0))
    return pl.pallas_call(
        functools.partial(_mix_out_kernel, n_head_major=n_head_major, alpha=alpha),
        grid=(b, s // tm),
        in_specs=[tok] + y_specs + w_specs + [row, row,
                  pl.BlockSpec((N_EXPERTS, d), lambda bi, i: (0, 0)),
                  pl.BlockSpec((N_EXPERTS, 1), lambda bi, i: (0, 0))],
        out_specs=[pl.BlockSpec((None, tm, d + LANES), lambda bi, i: (bi, i, 0)),
                   pl.BlockSpec((None, ROUTE_ROWS, tm), lambda bi, i: (bi, 0, i))],
        out_shape=[jax.ShapeDtypeStruct((b, s, d + LANES), F32),
                   jax.ShapeDtypeStruct((b, ROUTE_ROWS, s), F32)],
        compiler_params=_params("parallel", "parallel"),
        name="mix_out_ln_router",
    )(x, *[y for y, _ in ys], *ws, ln_g.reshape(1, d), ln_b.reshape(1, d),
      w_router.T.astype(F32), b_router.astype(F32).reshape(N_EXPERTS, 1))


MOE_TILE = 256
PLAN_COLS = 256
META_ROWS = SUBLANES


def _num_moe_tiles(t):
    return t // MOE_TILE + N_BUCKETS


def _plan_kernel(bid_ref, pos_ref, meta_ref):
    r, c = bid_ref.shape
    bid = bid_ref[...]
    before = (lax.broadcasted_iota(jnp.int32, (c, c), 0)
              < lax.broadcasted_iota(jnp.int32, (c, c), 1)).astype(BF16)
    rows_before = (lax.broadcasted_iota(jnp.int32, (r, r), 1)
                   < lax.broadcasted_iota(jnp.int32, (r, r), 0)).astype(BF16)
    tile_start = lax.broadcasted_iota(jnp.int32, (1, LANES), 1).astype(F32) * MOE_TILE

    def body(b, state):
        base, pos, tile_bucket = state
        ind = (bid == lax.convert_element_type(b, F32)).astype(F32)
        within = jnp.dot(ind.astype(BF16), before, preferred_element_type=F32)
        row_total = jnp.sum(ind, axis=1, keepdims=True)
        row_off = jnp.dot(rows_before, jnp.broadcast_to(row_total, (r, LANES)).astype(BF16),
                          preferred_element_type=F32)[:, 0:1]
        count = jnp.sum(row_total, axis=0, keepdims=True)
        padded = jnp.floor((count + (MOE_TILE - 1)) * (1.0 / MOE_TILE)) * MOE_TILE
        pos = pos + ind * (base + row_off + within)
        end = base + padded
        tile_bucket = tile_bucket + (tile_start >= end).astype(F32)
        return end, pos, tile_bucket

    total, pos, tile_bucket = lax.fori_loop(
        0, N_BUCKETS, body, (jnp.zeros((1, 1), F32), jnp.zeros((r, c), F32), jnp.zeros((1, LANES), F32)))
    pos_ref[...] = pos.astype(jnp.int32)

    tb = jnp.minimum(tile_bucket, N_BUCKETS - 1.0)
    group = sum((tb >= g * PAIRS_PER_GROUP).astype(F32) for g in range(1, N_GROUPS))
    pair = tb - group * PAIRS_PER_GROUP
    i_lo = jnp.where(pair < 3, 0.0, jnp.where(pair < 5, 1.0, 2.0))
    i_hi = jnp.where(pair < 3, pair + 1.0, jnp.where(pair < 5, pair - 1.0, 3.0))
    meta = jnp.concatenate([group * GROUP_SIZE + i_lo, group * GROUP_SIZE + i_hi,
                            (tile_start < total).astype(F32),
                            jnp.zeros((META_ROWS - 3, LANES), F32)], axis=0)
    meta_ref[...] = meta.astype(jnp.int32)


def _plan(bucket_ids):
    r, c = bucket_ids.shape
    return pl.pallas_call(
        _plan_kernel,
        out_shape=[jax.ShapeDtypeStruct((r, c), jnp.int32), jax.ShapeDtypeStruct((META_ROWS, LANES), jnp.int32)],
        compiler_params=pltpu.CompilerParams(vmem_limit_bytes=VMEM_LIMIT_BYTES),
        name="moe_plan",
    )(bucket_ids)


def _row_copy(src_ref, src_row, dst_ref, dst_row, sem):
    return pltpu.make_async_copy(src_ref.at[pl.ds(src_row, 1), :], dst_ref.at[pl.ds(dst_row, 1), :], sem)


def _dispatch_kernel(pos_ref, x_ref, init_ref, o_ref, sem):
    del init_ref
    tm = x_ref.shape[0]
    base = pl.program_id(0) * tm

    def issue(r, carry):
        _row_copy(x_ref, r, o_ref, pos_ref[base + r], sem).start()
        return carry

    lax.fori_loop(0, tm, issue, 0, unroll=8)
    pltpu.make_async_copy(x_ref, o_ref.at[pl.ds(0, tm), :], sem).wait()


def _dispatch(pos, x1e, n_rows, tm):
    t, dw = x1e.shape
    return pl.pallas_call(
        _dispatch_kernel,
        grid_spec=pltpu.PrefetchScalarGridSpec(
            num_scalar_prefetch=1,
            grid=(t // tm,),
            in_specs=[pl.BlockSpec((tm, dw), lambda i, pos_ref: (i, 0)),
                      pl.BlockSpec(memory_space=pl.ANY)],
            out_specs=pl.BlockSpec(memory_space=pl.ANY),
            scratch_shapes=[pltpu.SemaphoreType.DMA(())]),
        out_shape=jax.ShapeDtypeStruct((n_rows, dw), F32),
        input_output_aliases={2: 0},
        compiler_params=_params("arbitrary"),
        name="moe_dispatch",
    )(pos, x1e, jnp.zeros((n_rows, dw), F32))


def _moe_tile_kernel(elo_ref, ehi_ref, valid_ref, x_ref, wgl_ref, wul_ref, wdl_ref, wgh_ref, wuh_ref,
                     wdh_ref, o_ref):
    del elo_ref, ehi_ref
    d = o_ref.shape[1]
    valid = valid_ref[pl.program_id(0)]

    @pl.when(valid == 0)
    def _():
        o_ref[...] = jnp.zeros_like(o_ref)

    @pl.when(valid != 0)
    def _():
        x = x_ref[:, 0:d].astype(BF16)
        acc = None
        for wg_ref, wu_ref, wd_ref, lane in ((wgl_ref, wul_ref, wdl_ref, d + 1),
                                              (wgh_ref, wuh_ref, wdh_ref, d + 2)):
            weight = x_ref[:, lane:lane + 1]
            hg = jnp.dot(x, wg_ref[...], preferred_element_type=F32)
            hu = jnp.dot(x, wu_ref[...], preferred_element_type=F32)
            hidden = (hg * _sigmoid(hg)) * hu * weight
            y = jnp.dot(hidden.astype(BF16), wd_ref[...], preferred_element_type=F32)
            acc = y if acc is None else acc + y
        o_ref[...] = acc


def _moe_tiles(e_lo, e_hi, valid, xs, wg, wu, wd):
    n_rows, dw = xs.shape
    _, d, f = wg.shape
    up_lo = pl.BlockSpec((None, d, f), lambda k, lo, hi, ok: (lo[k], 0, 0))
    up_hi = pl.BlockSpec((None, d, f), lambda k, lo, hi, ok: (hi[k], 0, 0))
    down_lo = pl.BlockSpec((None, f, d), lambda k, lo, hi, ok: (lo[k], 0, 0))
    down_hi = pl.BlockSpec((None, f, d), lambda k, lo, hi, ok: (hi[k], 0, 0))
    return pl.pallas_call(
        _moe_tile_kernel,
        grid_spec=pltpu.PrefetchScalarGridSpec(
            num_scalar_prefetch=3,
            grid=(n_rows // MOE_TILE,),
            in_specs=[pl.BlockSpec((MOE_TILE, dw), lambda k, lo, hi, ok: (k, 0)),
                      up_lo, up_lo, down_lo, up_hi, up_hi, down_hi],
            out_specs=pl.BlockSpec((MOE_TILE, d), lambda k, lo, hi, ok: (k, 0))),
        out_shape=jax.ShapeDtypeStruct((n_rows, d), F32),
        compiler_params=_params("parallel"),
        name="moe_tiles",
    )(e_lo, e_hi, valid, xs, wg, wu, wd, wg, wu, wd)


def _ln_ple_kernel(pos_ref, x1_ref, ys_ref, p_ref, g_ref, b_ref, wg_ref, bg_ref, wp_ref, o_ref,
                   m_buf, sem, *, alpha):
    tm = x1_ref.shape[0]
    base = pl.program_id(0) * tm

    def issue(r, carry):
        _row_copy(ys_ref, pos_ref[base + r], m_buf, r, sem).start()
        return carry

    lax.fori_loop(0, tm, issue, 0, unroll=8)
    emb = jnp.dot(p_ref[...].astype(BF16), wp_ref[...], preferred_element_type=F32)
    pltpu.make_async_copy(ys_ref.at[pl.ds(0, tm), :], m_buf, sem).wait()
    x2 = _layer_norm(alpha * x1_ref[...] + m_buf[...], g_ref[...], b_ref[...])
    gate = _sigmoid(jnp.dot(x2.astype(BF16), wg_ref[...], preferred_element_type=F32) + bg_ref[...])
    o_ref[...] = x2 + gate * emb


def _ln_ple(pos, x1e, ys, p, ln_g, ln_b, wg, bg, wp, alpha, tm):
    t = x1e.shape[0]
    d = ys.shape[1]
    pd = p.shape[1]
    tok = pl.BlockSpec((tm, d), lambda i, pos_ref: (i, 0))
    row = pl.BlockSpec((1, d), lambda i, pos_ref: (0, 0))
    return pl.pallas_call(
        functools.partial(_ln_ple_kernel, alpha=alpha),
        grid_spec=pltpu.PrefetchScalarGridSpec(
            num_scalar_prefetch=1,
            grid=(t // tm,),
            in_specs=[tok, pl.BlockSpec(memory_space=pl.ANY),
                      pl.BlockSpec((tm, pd), lambda i, pos_ref: (i, 0)), row, row,
                      pl.BlockSpec((d, d), lambda i, pos_ref: (0, 0)), row,
                      pl.BlockSpec((pd, d), lambda i, pos_ref: (0, 0))],
            out_specs=tok,
            scratch_shapes=[pltpu.VMEM((tm, d), F32), pltpu.SemaphoreType.DMA(())]),
        out_shape=jax.ShapeDtypeStruct((t, d), F32),
        compiler_params=_params("arbitrary"),
        name="ln_ple",
    )(pos, x1e, ys, p, ln_g.reshape(1, d), ln_b.reshape(1, d), wg, bg.reshape(1, d), wp)


def _pick_tile(n, target):
    t = min(n, target)
    while n % t:
        t //= 2
    return t


def kernel(x, p, positions, w_in_ab, w_out_ab, conv_w, conv_b, lru_w_r, lru_b_r, lru_w_i, lru_b_i, lru_lambda, w_qkv_c, w_out_c, sinks_c, ln_mix_g, ln_mix_b, ln_ffn_g, ln_ffn_b, w_router, b_router, exp_w_gate, exp_w_up, exp_w_down, ple_w_proj, ple_w_gate, ple_b_gate):
    b, s, d = x.shape
    depth = p.shape[0]
    t = b * s
    alpha = (2 * depth) ** 0.25
    tm = _pick_tile(s, 512)
    assert t % MOE_TILE == 0 and t % PLAN_COLS == 0
    n_tiles = _num_moe_tiles(t)
    assert n_tiles <= LANES
    for i in range(depth):
        j = i // 2
        if i % 2 == 0:
            q, k, v, xr, gr = _proj_ab(x, w_in_ab[j].astype(BF16), tm)
            y_sb = _sb_attention(q, k, v, _pick_tile(s, 256), 4)
            y_lru = _lru(xr, gr, conv_w[j], conv_b[j], _block_diag(lru_w_r[j]).astype(BF16), lru_b_r[j],
                         _block_diag(lru_w_i[j]).astype(BF16), lru_b_i[j], lru_lambda[j],
                         _pick_tile(s, 256))
            w_out = w_out_ab[j].astype(BF16)
            ys = [(y_sb, True), (y_lru, False)]
            ws = [w_out[:SB_WIDTH], w_out[SB_WIDTH:]]
        else:
            q, k, v = _proj_rope(x, positions, w_qkv_c[j].astype(BF16), tm)
            y = _swa(q, k, v, sinks_c[j])
            ys = [(y, False)]
            ws = [w_out_c[j].astype(BF16)]
        x1e, route = _mix_out(x, ys, ws, ln_mix_g[i], ln_mix_b[i], w_router, b_router, alpha, tm)
        x1e = x1e.reshape(t, d + LANES)
        pos, meta = _plan(route[:, 0, :].reshape(t // PLAN_COLS, PLAN_COLS))
        pos = pos.reshape(t)
        xs = _dispatch(pos, x1e, n_tiles * MOE_TILE, _pick_tile(t, 1024))
        ys_moe = _moe_tiles(meta[0, :n_tiles], meta[1, :n_tiles], meta[2, :n_tiles], xs,
                            exp_w_gate[i].astype(BF16), exp_w_up[i].astype(BF16), exp_w_down[i].astype(BF16))
        x = _ln_ple(pos, x1e, ys_moe, p[i].reshape(t, -1), ln_ffn_g[i], ln_ffn_b[i],
                    ple_w_gate[i].astype(BF16), ple_b_gate[i], ple_w_proj[i].astype(BF16), alpha,
                    _pick_tile(t, 512)).reshape(b, s, d)
    return x
```

```python
import functools
import math

import jax
import jax.numpy as jnp
from jax import lax
from jax.experimental import pallas as pl
from jax.experimental.pallas import tpu as pltpu

HEAD_DIM = 64
SB_HEADS = 8
SB_WIDTH = SB_HEADS * HEAD_DIM
LRU_WIDTH = 512
LRU_BLOCKS = 8
LRU_C = 8.0
CONV_WIDTH = 4
SWA_HEADS = 16
SWA_KV_HEADS = 4
SWA_GROUP = SWA_HEADS // SWA_KV_HEADS
SWA_WINDOW = 128
ROPE_THETA = 10000.0
N_EXPERTS = 16
N_GROUPS = 4
GROUP_SIZE = N_EXPERTS // N_GROUPS
LN_EPS = 1e-5
Q_SCALE = HEAD_DIM ** -0.5

LANES = 128
SUBLANES = 8
VMEM_LIMIT_BYTES = 48 * 1024 * 1024

NEG_BIG = -1e30

BF16 = jnp.bfloat16
F32 = jnp.float32


def _params(*semantics):
    return pltpu.CompilerParams(dimension_semantics=semantics, vmem_limit_bytes=VMEM_LIMIT_BYTES)


def _softplus(z):
    return jnp.maximum(z, 0.0) + jnp.log(1.0 + jnp.exp(-jnp.abs(z)))


def _sigmoid(z):
    return 1.0 / (1.0 + jnp.exp(-z))


def _layer_norm(y, g, b):
    mu = jnp.mean(y, axis=-1, keepdims=True)
    d = y - mu
    var = jnp.mean(d * d, axis=-1, keepdims=True)
    return d * lax.rsqrt(var + LN_EPS) * g + b


def _proj_ab_kernel(x_ref, w_ref, q_ref, k_ref, v_ref, xr_ref, gr_ref):
    xb = x_ref[...].astype(BF16)

    def chunk(c):
        return jnp.dot(xb, w_ref[:, c * SB_WIDTH:(c + 1) * SB_WIDTH], preferred_element_type=F32)

    for c, (ref, scale) in enumerate(((q_ref, Q_SCALE), (k_ref, None), (v_ref, None))):
        r = chunk(c)
        if scale is not None:
            r = r * scale
        for h in range(SB_HEADS):
            ref[h] = r[:, h * HEAD_DIM:(h + 1) * HEAD_DIM].astype(BF16)
    xr_ref[...] = chunk(3)
    gr_ref[...] = chunk(4)


def _proj_ab(x, w_bf16, tm):
    b, s, d = x.shape
    n = w_bf16.shape[1]
    heads = jax.ShapeDtypeStruct((b, SB_HEADS, s, HEAD_DIM), BF16)
    flat = jax.ShapeDtypeStruct((b, s, LRU_WIDTH), F32)
    head_spec = pl.BlockSpec((None, SB_HEADS, tm, HEAD_DIM), lambda bi, i: (bi, 0, i, 0))
    flat_spec = pl.BlockSpec((None, tm, LRU_WIDTH), lambda bi, i: (bi, i, 0))
    return pl.pallas_call(
        _proj_ab_kernel,
        grid=(b, s // tm),
        in_specs=[pl.BlockSpec((None, tm, d), lambda bi, i: (bi, i, 0)),
                  pl.BlockSpec((d, n), lambda bi, i: (0, 0))],
        out_specs=[head_spec, head_spec, head_spec, flat_spec, flat_spec],
        out_shape=[heads, heads, heads, flat, flat],
        compiler_params=_params("parallel", "parallel"),
        name="proj_ab",
    )(x, w_bf16)


SB_DEAD_LOG_WEIGHT = -105.0


def _sb_attn_kernel(q_ref, k_ref, v_ref, o_ref, *, tq, hp):
    i = pl.program_id(2)
    row = lax.broadcasted_iota(jnp.int32, (tq, tq), 0)
    col = lax.broadcasted_iota(jnp.int32, (tq, tq), 1)
    later = (row > col).astype(BF16)
    causal = col < row

    def block(jb, carries, accs, masked):
        start = pl.multiple_of(jb * tq, tq)
        new_carries, new_accs = [], []
        for h in range(hp):
            kj = k_ref[h, pl.ds(start, tq), :]
            vj = v_ref[h, pl.ds(start, tq), :]
            z = lax.dot_general(q_ref[h], kj, (((1,), (1,)), ((), ())), preferred_element_type=F32)
            sp = _softplus(z)
            log_keep = -sp
            if masked:
                log_keep = jnp.where(causal, log_keep, 0.0)
            after = jnp.dot(log_keep.astype(BF16), later, preferred_element_type=F32)
            w = jnp.exp((z - sp) + after + carries[h])
            if masked:
                w = jnp.where(causal, w, 0.0)
            new_accs.append(accs[h] + jnp.dot(w.astype(BF16), vj, preferred_element_type=F32))
            new_carries.append(carries[h] + jnp.sum(log_keep, axis=1, keepdims=True))
        return tuple(new_carries), tuple(new_accs)

    def live(carries):
        return functools.reduce(jnp.maximum, [jnp.max(c) for c in carries])

    carries, accs = block(i, (jnp.zeros((tq, 1), F32),) * hp, (jnp.zeros((tq, HEAD_DIM), F32),) * hp, True)

    def cond(state):
        return (state[0] < i) & (state[1] > SB_DEAD_LOG_WEIGHT)

    def body(state):
        step, _, carries, accs = state
        carries, accs = block(i - 1 - step, carries, accs, False)
        return step + 1, live(carries), carries, accs

    _, _, _, accs = lax.while_loop(cond, body, (jnp.int32(0), live(carries), carries, accs))
    for h in range(hp):
        o_ref[h] = accs[h].astype(o_ref.dtype)


def _sb_attention(q, k, v, tq, hp):
    b, h, s, dh = q.shape
    return pl.pallas_call(
        functools.partial(_sb_attn_kernel, tq=tq, hp=hp),
        grid=(b, h // hp, s // tq),
        in_specs=[pl.BlockSpec((None, hp, tq, dh), lambda bi, hi, i: (bi, hi, i, 0)),
                  pl.BlockSpec((None, hp, s, dh), lambda bi, hi, i: (bi, hi, 0, 0)),
                  pl.BlockSpec((None, hp, s, dh), lambda bi, hi, i: (bi, hi, 0, 0))],
        out_specs=pl.BlockSpec((None, hp, tq, dh), lambda bi, hi, i: (bi, hi, i, 0)),
        out_shape=jax.ShapeDtypeStruct((b, h, s, dh), BF16),
        compiler_params=_params("parallel", "parallel", "parallel"),
        name="sb_attention",
    )(q, k, v)


def _gelu_tanh(x):
    return 0.5 * x * (1.0 + jnp.tanh(math.sqrt(2.0 / math.pi) * (x + 0.044715 * (x * x * x))))


def _lru_kernel(xr_ref, gr_ref, cw_ref, cb_ref, wr_ref, br_ref, wi_ref, bi_ref, lam_ref, y_ref,
                xbuf, hprev, *, ts):
    @pl.when(pl.program_id(1) == 0)
    def _():
        xbuf[0:SUBLANES, :] = jnp.zeros((SUBLANES, LRU_WIDTH), F32)
        hprev[...] = jnp.zeros_like(hprev)

    xbuf[SUBLANES:SUBLANES + ts, :] = xr_ref[...]
    xc = cb_ref[...] + cw_ref[CONV_WIDTH - 1:CONV_WIDTH, :] * xbuf[SUBLANES:SUBLANES + ts, :]
    for kk in range(CONV_WIDTH - 1):
        off = SUBLANES - (CONV_WIDTH - 1) + kk
        xc = xc + cw_ref[kk:kk + 1, :] * xbuf[off:off + ts, :]
    xbuf[0:SUBLANES, :] = xbuf[ts:ts + SUBLANES, :]

    xcb = xc.astype(BF16)
    r = _sigmoid(jnp.dot(xcb, wr_ref[...], preferred_element_type=F32) + br_ref[...])
    gi = _sigmoid(jnp.dot(xcb, wi_ref[...], preferred_element_type=F32) + bi_ref[...])
    log_a = (-LRU_C) * r * _softplus(-lam_ref[...])
    a = jnp.exp(log_a)
    u = jnp.sqrt(1.0 - a * a) * (gi * xc)

    row = lax.broadcasted_iota(jnp.int32, (ts, LRU_WIDTH), 0)
    d = 1
    while d < ts:
        keep = row >= d
        a_sh = jnp.where(keep, pltpu.roll(a, d, axis=0), 1.0)
        u_sh = jnp.where(keep, pltpu.roll(u, d, axis=0), 0.0)
        u = a * u_sh + u
        a = a * a_sh
        d *= 2
    h = a * hprev[0:1, :] + u
    hprev[...] = jnp.broadcast_to(h[ts - 1:ts, :], hprev.shape)
    y_ref[...] = (_gelu_tanh(gr_ref[...]) * h).astype(y_ref.dtype)


def _lru(xr, gr, conv_w, conv_b, wr_bd, b_r, wi_bd, b_i, lam, ts):
    b, s, w = xr.shape
    seq_spec = pl.BlockSpec((None, ts, w), lambda bi, i: (bi, i, 0))

    def full(shape):
        return pl.BlockSpec(shape, lambda bi, i: (0,) * len(shape))

    return pl.pallas_call(
        functools.partial(_lru_kernel, ts=ts),
        grid=(b, s // ts),
        in_specs=[seq_spec, seq_spec, full((CONV_WIDTH, w)), full((1, w)), full((w, w)), full((1, w)),
                  full((w, w)), full((1, w)), full((1, w))],
        out_specs=seq_spec,
        out_shape=jax.ShapeDtypeStruct((b, s, w), BF16),
        scratch_shapes=[pltpu.VMEM((ts + 2 * SUBLANES, w), F32), pltpu.VMEM((SUBLANES, w), F32)],
        compiler_params=_params("parallel", "arbitrary"),
        name="rg_lru",
    )(xr, gr, conv_w, conv_b.reshape(1, w), wr_bd, b_r.reshape(1, w), wi_bd, b_i.reshape(1, w),
      lam.reshape(1, w))


def _block_diag(w):
    n, c, d = w.shape
    eye = jnp.eye(n, dtype=w.dtype)
    return (eye[:, None, :, None] * w[:, :, None, :]).reshape(n * c, n * d)


def _proj_rope_kernel(x_ref, pos_ref, freq_ref, w_ref, q_ref, k_ref, vt_ref):
    tm = x_ref.shape[0]
    xb = x_ref[...].astype(BF16)
    ang_t = freq_ref[...] * pos_ref[...].astype(F32)
    reps = LANES // (HEAD_DIM // 2)
    cos = jnp.concatenate([jnp.cos(ang_t)] * reps, axis=0).T
    sin = jnp.concatenate([jnp.sin(ang_t)] * reps, axis=0).T
    lane = lax.broadcasted_iota(jnp.int32, (tm, LANES), 1)
    first_half = (lane % HEAD_DIM) < (HEAD_DIM // 2)
    heads_per_slab = LANES // HEAD_DIM

    def rope(r):
        upper = pltpu.roll(r, LANES - HEAD_DIM // 2, axis=1)
        lower = pltpu.roll(r, HEAD_DIM // 2, axis=1)
        return r * cos + jnp.where(first_half, -upper, lower) * sin

    def emit(ref, n_heads, col0, rotary, scale):
        for slab in range(n_heads // heads_per_slab):
            c0 = col0 + slab * LANES
            r = jnp.dot(xb, w_ref[:, c0:c0 + LANES], preferred_element_type=F32)
            if rotary:
                r = rope(r)
            if scale is not None:
                r = r * scale
            for j in range(heads_per_slab):
                ref[slab * heads_per_slab + j] = r[:, j * HEAD_DIM:(j + 1) * HEAD_DIM].astype(BF16)

    emit(q_ref, SWA_HEADS, 0, True, Q_SCALE)
    emit(k_ref, SWA_KV_HEADS, SWA_HEADS * HEAD_DIM, True, None)
    v0 = (SWA_HEADS + SWA_KV_HEADS) * HEAD_DIM
    for slab in range(SWA_KV_HEADS // heads_per_slab):
        r = jnp.dot(xb, w_ref[:, v0 + slab * LANES:v0 + (slab + 1) * LANES], preferred_element_type=F32)
        rt = r.T
        for j in range(heads_per_slab):
            vt_ref[slab * heads_per_slab + j] = rt[j * HEAD_DIM:(j + 1) * HEAD_DIM, :].astype(BF16)


def _proj_rope(x, positions, w_bf16, tm):
    b, s, d = x.shape
    n = w_bf16.shape[1]
    half = HEAD_DIM // 2
    inv_freq = (ROPE_THETA ** (-jnp.arange(half, dtype=F32) / half)).reshape(half, 1)

    def heads(nh):
        return (jax.ShapeDtypeStruct((b, nh, s, HEAD_DIM), BF16),
                pl.BlockSpec((None, nh, tm, HEAD_DIM), lambda bi, i: (bi, 0, i, 0)))

    (qs, qspec), (ks, kspec) = heads(SWA_HEADS), heads(SWA_KV_HEADS)
    vs = jax.ShapeDtypeStruct((b, SWA_KV_HEADS, HEAD_DIM, s), BF16)
    vspec = pl.BlockSpec((None, SWA_KV_HEADS, HEAD_DIM, tm), lambda bi, i: (bi, 0, 0, i))
    return pl.pallas_call(
        _proj_rope_kernel,
        grid=(b, s // tm),
        in_specs=[pl.BlockSpec((None, tm, d), lambda bi, i: (bi, i, 0)),
                  pl.BlockSpec((None, 1, tm), lambda bi, i: (bi, 0, i)),
                  pl.BlockSpec((half, 1), lambda bi, i: (0, 0)),
                  pl.BlockSpec((d, n), lambda bi, i: (0, 0))],
        out_specs=[qspec, kspec, vspec],
        out_shape=[qs, ks, vs],
        compiler_params=_params("parallel", "parallel"),
        name="proj_rope",
    )(x, positions.reshape(b, 1, s), inv_freq, w_bf16)


def _reduce_rows(x, op):
    while x.shape[0] > SUBLANES:
        half = x.shape[0] // 2
        x = op(x[:half], x[half:])
    for shift in (4, 2, 1):
        x = op(x, pltpu.roll(x, shift, axis=0))
    return x[0:1]


def _swa_kernel(q_ref, kp_ref, kc_ref, vtp_ref, vtc_ref, sink_ref, o_ref):
    i = pl.program_id(1)
    w = SWA_WINDOW
    key = lax.broadcasted_iota(jnp.int32, (2 * w, w), 0)
    qry = lax.broadcasted_iota(jnp.int32, (2 * w, w), 1)
    dist = qry + w - key
    visible = (dist >= 0) & (dist < w) & ((key >= w) | (i > 0))
    bias = jnp.concatenate([jnp.where(visible, 0.0, NEG_BIG)] * SWA_GROUP, axis=1)
    outs = []
    for kv in range(SWA_KV_HEADS):
        kk = jnp.concatenate([kp_ref[kv], kc_ref[kv]], axis=0)
        vvt = jnp.concatenate([vtp_ref[kv], vtc_ref[kv]], axis=1)
        qg = jnp.concatenate([q_ref[kv * SWA_GROUP + g] for g in range(SWA_GROUP)], axis=0)
        st = lax.dot_general(kk, qg, (((1,), (1,)), ((), ())), preferred_element_type=F32) + bias
        sink = sink_ref[kv:kv + 1, :]
        m = jnp.maximum(_reduce_rows(st, jnp.maximum), sink)
        p = jnp.exp(st - m)
        denom = _reduce_rows(p, jnp.add) + jnp.exp(sink - m)
        ot = jnp.dot(vvt, p.astype(BF16), preferred_element_type=F32) / denom
        outs.extend(ot[:, g * w:(g + 1) * w] for g in range(SWA_GROUP))
    o_ref[...] = jnp.concatenate(outs, axis=0).T.astype(o_ref.dtype)


def _swa(q, k, vt, sinks):
    b, nh, s, dh = q.shape
    nkv = k.shape[1]
    w = SWA_WINDOW
    cur = pl.BlockSpec((None, nkv, w, dh), lambda bi, i: (bi, 0, i, 0))
    prev = pl.BlockSpec((None, nkv, w, dh), lambda bi, i: (bi, 0, jnp.maximum(i - 1, 0), 0))
    cur_t = pl.BlockSpec((None, nkv, dh, w), lambda bi, i: (bi, 0, 0, i))
    prev_t = pl.BlockSpec((None, nkv, dh, w), lambda bi, i: (bi, 0, 0, jnp.maximum(i - 1, 0)))
    sink_tile = jnp.repeat(sinks.astype(F32).reshape(nkv, nh // nkv), w, axis=1)
    return pl.pallas_call(
        _swa_kernel,
        grid=(b, s // w),
        in_specs=[pl.BlockSpec((None, nh, w, dh), lambda bi, i: (bi, 0, i, 0)),
                  prev, cur, prev_t, cur_t,
                  pl.BlockSpec(sink_tile.shape, lambda bi, i: (0, 0))],
        out_specs=pl.BlockSpec((None, w, nh * dh), lambda bi, i: (bi, i, 0)),
        out_shape=jax.ShapeDtypeStruct((b, s, nh * dh), BF16),
        compiler_params=_params("parallel", "parallel"),
        name="swa",
    )(q, k, k, vt, vt, sink_tile)


PAIRS_PER_GROUP = GROUP_SIZE * (GROUP_SIZE - 1) // 2
N_BUCKETS = N_GROUPS * PAIRS_PER_GROUP
ROUTE_ROWS = SUBLANES


def _route(logits_t):
    rows = [logits_t[e:e + 1, :] for e in range(N_EXPERTS)]
    mx = functools.reduce(jnp.maximum, rows)
    ex = [jnp.exp(r - mx) for r in rows]
    total = functools.reduce(lambda p, q: p + q, ex)
    probs = [e / total for e in ex]

    group_score = []
    for g in range(N_GROUPS):
        a, b, c, d = probs[g * GROUP_SIZE:(g + 1) * GROUP_SIZE]
        hi1, lo1 = jnp.maximum(a, b), jnp.minimum(a, b)
        hi2, lo2 = jnp.maximum(c, d), jnp.minimum(c, d)
        top1 = jnp.maximum(hi1, hi2)
        top2 = jnp.maximum(jnp.minimum(hi1, hi2), jnp.maximum(lo1, lo2))
        group_score.append(top1 + top2)
    best = functools.reduce(jnp.maximum, group_score)
    g_sel = jnp.full(best.shape, N_GROUPS - 1, jnp.int32)
    for g in range(N_GROUPS - 2, -1, -1):
        g_sel = jnp.where(group_score[g] == best, g, g_sel)

    in_group = []
    for j in range(GROUP_SIZE):
        val = probs[(N_GROUPS - 1) * GROUP_SIZE + j]
        for g in range(N_GROUPS - 2, -1, -1):
            val = jnp.where(g_sel == g, probs[g * GROUP_SIZE + j], val)
        in_group.append(val)

    def first_argmax(vals):
        m = functools.reduce(jnp.maximum, vals)
        idx = jnp.full(m.shape, GROUP_SIZE - 1, jnp.int32)
        for j in range(GROUP_SIZE - 2, -1, -1):
            idx = jnp.where(vals[j] == m, j, idx)
        return m, idx

    w1, i1 = first_argmax(in_group)
    rest = [jnp.where(i1 == j, -1.0, in_group[j]) for j in range(GROUP_SIZE)]
    w2, i2 = first_argmax(rest)
    norm = w1 + w2
    first_is_lo = i1 < i2
    i_lo = jnp.minimum(i1, i2)
    i_hi = jnp.maximum(i1, i2)
    pair = jnp.where(i_lo == 0, i_hi - 1, jnp.where(i_lo == 1, i_hi + 1, PAIRS_PER_GROUP - 1))
    bucket = (g_sel * PAIRS_PER_GROUP + pair).astype(F32)
    w_lo = jnp.where(first_is_lo, w1, w2) / norm
    w_hi = jnp.where(first_is_lo, w2, w1) / norm
    return bucket, w_lo, w_hi


def _mix_out_kernel(*refs, n_head_major, alpha):
    x_ref = refs[0]
    y_refs = refs[1:1 + len(n_head_major)]
    w_refs = refs[1 + len(n_head_major):1 + 2 * len(n_head_major)]
    g_ref, b_ref, wrh_ref, wrl_ref, brt_ref, x1e_ref, route_ref = refs[1 + 2 * len(n_head_major):]
    tm, d = x_ref.shape
    h = alpha * x_ref[...]
    for y_ref, w_ref, nh in zip(y_refs, w_refs, n_head_major):
        if nh:
            y = jnp.concatenate([y_ref[j] for j in range(nh)], axis=-1)
        else:
            y = y_ref[...]
        h = h + jnp.dot(y, w_ref[...], preferred_element_type=F32)
    x1 = _layer_norm(h, g_ref[...], b_ref[...])
    x_hi = x1.astype(BF16)
    x_lo = (x1 - x_hi.astype(F32)).astype(BF16)

    def nt_dot(w, xv):
        return lax.dot_general(w, xv, (((1,), (1,)), ((), ())), preferred_element_type=F32)

    logits_t = (nt_dot(wrh_ref[...], x_hi) + nt_dot(wrh_ref[...], x_lo) + nt_dot(wrl_ref[...], x_hi)
                + brt_ref[...])
    route = jnp.concatenate(list(_route(logits_t)) + [jnp.zeros((ROUTE_ROWS - 3, tm), F32)], axis=0)
    route_ref[...] = route
    x1e_ref[:, 0:d] = x1
    x1e_ref[:, d:d + LANES] = jnp.concatenate([route, jnp.zeros((LANES - ROUTE_ROWS, tm), F32)], axis=0).T


def _mix_out(x, ys, ws, ln_g, ln_b, w_router, b_router, alpha, tm):
    b, s, d = x.shape
    n_head_major = tuple(y.shape[1] if hm else 0 for y, hm in ys)
    y_specs = []
    for (y, hm) in ys:
        if hm:
            y_specs.append(pl.BlockSpec((None, y.shape[1], tm, y.shape[3]), lambda bi, i: (bi, 0, i, 0)))
        else:
            y_specs.append(pl.BlockSpec((None, tm, y.shape[2]), lambda bi, i: (bi, i, 0)))
    w_specs = [pl.BlockSpec(w.shape, lambda bi, i: (0, 0)) for w in ws]
    row = pl.BlockSpec((1, d), lambda bi, i: (0, 0))
    tok = pl.BlockSpec((None, tm, d), lambda bi, i: (bi, i, 0))
    wr_t = w_router.T.astype(F32)
    wr_hi = wr_t.astype(BF16)
    wr_lo = (wr_t - wr_hi.astype(F32)).astype(BF16)
    wr_spec = pl.BlockSpec((N_EXPERTS, d), lambda bi, i: (0, 0))
    return pl.pallas_call(
        functools.partial(_mix_out_kernel, n_head_major=n_head_major, alpha=alpha),
        grid=(b, s // tm),
        in_specs=[tok] + y_specs + w_specs + [row, row, wr_spec, wr_spec,
                  pl.BlockSpec((N_EXPERTS, 1), lambda bi, i: (0, 0))],
        out_specs=[pl.BlockSpec((None, tm, d + LANES), lambda bi, i: (bi, i, 0)),
                   pl.BlockSpec((None, ROUTE_ROWS, tm), lambda bi, i: (bi, 0, i))],
        out_shape=[jax.ShapeDtypeStruct((b, s, d + LANES), F32),
                   jax.ShapeDtypeStruct((b, ROUTE_ROWS, s), F32)],
        compiler_params=_params("parallel", "parallel"),
        name="mix_out_ln_router",
    )(x, *[y for y, _ in ys], *ws, ln_g.reshape(1, d), ln_b.reshape(1, d),
      wr_hi, wr_lo, b_router.astype(F32).reshape(N_EXPERTS, 1))


MOE_TILE = 256
PLAN_COLS = 256
META_ROWS = SUBLANES


def _num_moe_tiles(t):
    return t // MOE_TILE + N_BUCKETS


def _plan_kernel(bid_ref, pos_ref, meta_ref):
    r, c = bid_ref.shape
    bid = bid_ref[...]
    before = (lax.broadcasted_iota(jnp.int32, (c, c), 0)
              < lax.broadcasted_iota(jnp.int32, (c, c), 1)).astype(BF16)
    rows_before = (lax.broadcasted_iota(jnp.int32, (r, r), 1)
                   < lax.broadcasted_iota(jnp.int32, (r, r), 0)).astype(BF16)
    tile_start = lax.broadcasted_iota(jnp.int32, (1, LANES), 1).astype(F32) * MOE_TILE

    def body(b, state):
        base, pos, tile_bucket = state
        ind = (bid == lax.convert_element_type(b, F32)).astype(F32)
        within = jnp.dot(ind.astype(BF16), before, preferred_element_type=F32)
        row_total = jnp.sum(ind, axis=1, keepdims=True)
        row_off = jnp.dot(rows_before, jnp.broadcast_to(row_total, (r, LANES)).astype(BF16),
                          preferred_element_type=F32)[:, 0:1]
        count = jnp.sum(row_total, axis=0, keepdims=True)
        padded = jnp.floor((count + (MOE_TILE - 1)) * (1.0 / MOE_TILE)) * MOE_TILE
        pos = pos + ind * (base + row_off + within)
        end = base + padded
        tile_bucket = tile_bucket + (tile_start >= end).astype(F32)
        return end, pos, tile_bucket

    total, pos, tile_bucket = lax.fori_loop(
        0, N_BUCKETS, body, (jnp.zeros((1, 1), F32), jnp.zeros((r, c), F32), jnp.zeros((1, LANES), F32)))
    pos_ref[...] = pos.astype(jnp.int32)

    tb = jnp.minimum(tile_bucket, N_BUCKETS - 1.0)
    group = sum((tb >= g * PAIRS_PER_GROUP).astype(F32) for g in range(1, N_GROUPS))
    pair = tb - group * PAIRS_PER_GROUP
    i_lo = jnp.where(pair < 3, 0.0, jnp.where(pair < 5, 1.0, 2.0))
    i_hi = jnp.where(pair < 3, pair + 1.0, jnp.where(pair < 5, pair - 1.0, 3.0))
    meta = jnp.concatenate([group * GROUP_SIZE + i_lo, group * GROUP_SIZE + i_hi,
                            (tile_start < total).astype(F32),
                            jnp.zeros((META_ROWS - 3, LANES), F32)], axis=0)
    meta_ref[...] = meta.astype(jnp.int32)


def _plan(bucket_ids):
    r, c = bucket_ids.shape
    return pl.pallas_call(
        _plan_kernel,
        out_shape=[jax.ShapeDtypeStruct((r, c), jnp.int32), jax.ShapeDtypeStruct((META_ROWS, LANES), jnp.int32)],
        compiler_params=pltpu.CompilerParams(vmem_limit_bytes=VMEM_LIMIT_BYTES),
        name="moe_plan",
    )(bucket_ids)


def _row_copy(src_ref, src_row, dst_ref, dst_row, sem):
    return pltpu.make_async_copy(src_ref.at[pl.ds(src_row, 1), :], dst_ref.at[pl.ds(dst_row, 1), :], sem)


def _dispatch_kernel(pos_ref, x_ref, init_ref, o_ref, sem):
    del init_ref
    tm = x_ref.shape[0]
    base = pl.program_id(0) * tm

    def issue(r, carry):
        _row_copy(x_ref, r, o_ref, pos_ref[base + r], sem).start()
        return carry

    lax.fori_loop(0, tm, issue, 0, unroll=8)
    pltpu.make_async_copy(x_ref, o_ref.at[pl.ds(0, tm), :], sem).wait()


def _dispatch(pos, x1e, n_rows, tm):
    t, dw = x1e.shape
    return pl.pallas_call(
        _dispatch_kernel,
        grid_spec=pltpu.PrefetchScalarGridSpec(
            num_scalar_prefetch=1,
            grid=(t // tm,),
            in_specs=[pl.BlockSpec((tm, dw), lambda i, pos_ref: (i, 0)),
                      pl.BlockSpec(memory_space=pl.ANY)],
            out_specs=pl.BlockSpec(memory_space=pl.ANY),
            scratch_shapes=[pltpu.SemaphoreType.DMA(())]),
        out_shape=jax.ShapeDtypeStruct((n_rows, dw), F32),
        input_output_aliases={2: 0},
        compiler_params=_params("arbitrary"),
        name="moe_dispatch",
    )(pos, x1e, jnp.zeros((n_rows, dw), F32))


def _moe_tile_kernel(elo_ref, ehi_ref, valid_ref, x_ref, wgl_ref, wul_ref, wdl_ref, wgh_ref, wuh_ref,
                     wdh_ref, o_ref):
    del elo_ref, ehi_ref
    d = o_ref.shape[1]
    valid = valid_ref[pl.program_id(0)]

    @pl.when(valid == 0)
    def _():
        o_ref[...] = jnp.zeros_like(o_ref)

    @pl.when(valid != 0)
    def _():
        x = x_ref[:, 0:d].astype(BF16)
        acc = None
        for wg_ref, wu_ref, wd_ref, lane in ((wgl_ref, wul_ref, wdl_ref, d + 1),
                                              (wgh_ref, wuh_ref, wdh_ref, d + 2)):
            weight = x_ref[:, lane:lane + 1]
            hg = jnp.dot(x, wg_ref[...], preferred_element_type=F32)
            hu = jnp.dot(x, wu_ref[...], preferred_element_type=F32)
            hidden = (hg * _sigmoid(hg)) * hu * weight
            y = jnp.dot(hidden.astype(BF16), wd_ref[...], preferred_element_type=F32)
            acc = y if acc is None else acc + y
        o_ref[...] = acc


def _moe_tiles(e_lo, e_hi, valid, xs, wg, wu, wd):
    n_rows, dw = xs.shape
    _, d, f = wg.shape
    up_lo = pl.BlockSpec((None, d, f), lambda k, lo, hi, ok: (lo[k], 0, 0))
    up_hi = pl.BlockSpec((None, d, f), lambda k, lo, hi, ok: (hi[k], 0, 0))
    down_lo = pl.BlockSpec((None, f, d), lambda k, lo, hi, ok: (lo[k], 0, 0))
    down_hi = pl.BlockSpec((None, f, d), lambda k, lo, hi, ok: (hi[k], 0, 0))
    return pl.pallas_call(
        _moe_tile_kernel,
        grid_spec=pltpu.PrefetchScalarGridSpec(
            num_scalar_prefetch=3,
            grid=(n_rows // MOE_TILE,),
            in_specs=[pl.BlockSpec((MOE_TILE, dw), lambda k, lo, hi, ok: (k, 0)),
                      up_lo, up_lo, down_lo, up_hi, up_hi, down_hi],
            out_specs=pl.BlockSpec((MOE_TILE, d), lambda k, lo, hi, ok: (k, 0))),
        out_shape=jax.ShapeDtypeStruct((n_rows, d), F32),
        compiler_params=_params("parallel"),
        name="moe_tiles",
    )(e_lo, e_hi, valid, xs, wg, wu, wd, wg, wu, wd)


def _ln_ple_kernel(pos_ref, x1_ref, ys_ref, p_ref, g_ref, b_ref, wg_ref, bg_ref, wp_ref, o_ref,
                   m_even, m_odd, sem_even, sem_odd, *, alpha):
    tm = x1_ref.shape[0]
    i = pl.program_id(0)
    last = pl.num_programs(0) - 1

    def wait(buf, sem):
        pltpu.make_async_copy(ys_ref.at[pl.ds(0, tm), :], buf, sem).wait()

    @pl.when(i == 0)
    def _():
        def issue(r, carry):
            _row_copy(ys_ref, pos_ref[r], m_even, r, sem_even).start()
            return carry

        lax.fori_loop(0, tm, issue, 0, unroll=8)

    def step(cur, cur_sem, nxt, nxt_sem):
        wait(cur, cur_sem)
        base = jnp.minimum(i + 1, last) * tm
        for r in range(tm):
            _row_copy(ys_ref, pos_ref[base + r], nxt, r, nxt_sem).start()
        emb = jnp.dot(p_ref[...].astype(BF16), wp_ref[...], preferred_element_type=F32)
        x2 = _layer_norm(alpha * x1_ref[...] + cur[...], g_ref[...], b_ref[...])
        gate = _sigmoid(jnp.dot(x2.astype(BF16), wg_ref[...], preferred_element_type=F32) + bg_ref[...])
        o_ref[...] = x2 + gate * emb

        @pl.when(i == last)
        def _():
            wait(nxt, nxt_sem)

    @pl.when(i % 2 == 0)
    def _():
        step(m_even, sem_even, m_odd, sem_odd)

    @pl.when(i % 2 == 1)
    def _():
        step(m_odd, sem_odd, m_even, sem_even)


def _ln_ple(pos, x1e, ys, p, ln_g, ln_b, wg, bg, wp, alpha, tm):
    t = x1e.shape[0]
    d = ys.shape[1]
    pd = p.shape[1]
    tok = pl.BlockSpec((tm, d), lambda i, pos_ref: (i, 0))
    row = pl.BlockSpec((1, d), lambda i, pos_ref: (0, 0))
    return pl.pallas_call(
        functools.partial(_ln_ple_kernel, alpha=alpha),
        grid_spec=pltpu.PrefetchScalarGridSpec(
            num_scalar_prefetch=1,
            grid=(t // tm,),
            in_specs=[tok, pl.BlockSpec(memory_space=pl.ANY),
                      pl.BlockSpec((tm, pd), lambda i, pos_ref: (i, 0)), row, row,
                      pl.BlockSpec((d, d), lambda i, pos_ref: (0, 0)), row,
                      pl.BlockSpec((pd, d), lambda i, pos_ref: (0, 0))],
            out_specs=tok,
            scratch_shapes=[pltpu.VMEM((tm, d), F32), pltpu.VMEM((tm, d), F32),
                            pltpu.SemaphoreType.DMA(()), pltpu.SemaphoreType.DMA(())]),
        out_shape=jax.ShapeDtypeStruct((t, d), F32),
        compiler_params=_params("arbitrary"),
        name="ln_ple",
    )(pos, x1e, ys, p, ln_g.reshape(1, d), ln_b.reshape(1, d), wg, bg.reshape(1, d), wp)


def _pick_tile(n, target):
    t = min(n, target)
    while n % t:
        t //= 2
    return t


def kernel(x, p, positions, w_in_ab, w_out_ab, conv_w, conv_b, lru_w_r, lru_b_r, lru_w_i, lru_b_i, lru_lambda, w_qkv_c, w_out_c, sinks_c, ln_mix_g, ln_mix_b, ln_ffn_g, ln_ffn_b, w_router, b_router, exp_w_gate, exp_w_up, exp_w_down, ple_w_proj, ple_w_gate, ple_b_gate):
    b, s, d = x.shape
    depth = p.shape[0]
    t = b * s
    alpha = (2 * depth) ** 0.25
    tm = _pick_tile(s, 512)
    assert t % MOE_TILE == 0 and t % PLAN_COLS == 0
    n_tiles = _num_moe_tiles(t)
    assert n_tiles <= LANES
    for i in range(depth):
        j = i // 2
        if i % 2 == 0:
            q, k, v, xr, gr = _proj_ab(x, w_in_ab[j].astype(BF16), tm)
            y_sb = _sb_attention(q, k, v, _pick_tile(s, 256), 4)
            y_lru = _lru(xr, gr, conv_w[j], conv_b[j], _block_diag(lru_w_r[j]).astype(BF16), lru_b_r[j],
                         _block_diag(lru_w_i[j]).astype(BF16), lru_b_i[j], lru_lambda[j],
                         _pick_tile(s, 256))
            w_out = w_out_ab[j].astype(BF16)
            ys = [(y_sb, True), (y_lru, False)]
            ws = [w_out[:SB_WIDTH], w_out[SB_WIDTH:]]
        else:
            q, k, v = _proj_rope(x, positions, w_qkv_c[j].astype(BF16), tm)
            y = _swa(q, k, v, sinks_c[j])
            ys = [(y, False)]
            ws = [w_out_c[j].astype(BF16)]
        x1e, route = _mix_out(x, ys, ws, ln_mix_g[i], ln_mix_b[i], w_router, b_router, alpha, tm)
        x1e = x1e.reshape(t, d + LANES)
        pos, meta = _plan(route[:, 0, :].reshape(t // PLAN_COLS, PLAN_COLS))
        pos = pos.reshape(t)
        xs = _dispatch(pos, x1e, n_tiles * MOE_TILE, _pick_tile(t, 1024))
        ys_moe = _moe_tiles(meta[0, :n_tiles], meta[1, :n_tiles], meta[2, :n_tiles], xs,
                            exp_w_gate[i].astype(BF16), exp_w_up[i].astype(BF16), exp_w_down[i].astype(BF16))
        x = _ln_ple(pos, x1e, ys_moe, p[i].reshape(t, -1), ln_ffn_g[i], ln_ffn_b[i],
                    ple_w_gate[i].astype(BF16), ple_b_gate[i], ple_w_proj[i].astype(BF16), alpha,
                    _pick_tile(t, 512)).reshape(b, s, d)
    return x
```

```python
import functools
import math

import jax
import jax.numpy as jnp
from jax import lax
from jax.experimental import pallas as pl
from jax.experimental.pallas import tpu as pltpu

HEAD_DIM = 64
SB_HEADS = 8
SB_WIDTH = SB_HEADS * HEAD_DIM
LRU_WIDTH = 512
LRU_BLOCKS = 8
LRU_C = 8.0
CONV_WIDTH = 4
SWA_HEADS = 16
SWA_KV_HEADS = 4
SWA_GROUP = SWA_HEADS // SWA_KV_HEADS
SWA_WINDOW = 128
ROPE_THETA = 10000.0
N_EXPERTS = 16
N_GROUPS = 4
GROUP_SIZE = N_EXPERTS // N_GROUPS
LN_EPS = 1e-5
Q_SCALE = HEAD_DIM ** -0.5

LANES = 128
SUBLANES = 8
VMEM_LIMIT_BYTES = 48 * 1024 * 1024

NEG_BIG = -1e30

BF16 = jnp.bfloat16
F32 = jnp.float32


def _params(*semantics):
    return pltpu.CompilerParams(dimension_semantics=semantics, vmem_limit_bytes=VMEM_LIMIT_BYTES)


def _softplus(z):
    return jnp.maximum(z, 0.0) + jnp.log(1.0 + jnp.exp(-jnp.abs(z)))


def _sigmoid(z):
    return 1.0 / (1.0 + jnp.exp(-z))


def _layer_norm(y, g, b):
    mu = jnp.mean(y, axis=-1, keepdims=True)
    d = y - mu
    var = jnp.mean(d * d, axis=-1, keepdims=True)
    return d * lax.rsqrt(var + LN_EPS) * g + b


def _proj_ab_kernel(x_ref, w_ref, q_ref, k_ref, v_ref, xr_ref, gr_ref):
    xb = x_ref[...].astype(BF16)

    def chunk(c):
        return jnp.dot(xb, w_ref[:, c * SB_WIDTH:(c + 1) * SB_WIDTH], preferred_element_type=F32)

    for c, (ref, scale) in enumerate(((q_ref, Q_SCALE), (k_ref, None), (v_ref, None))):
        r = chunk(c)
        if scale is not None:
            r = r * scale
        for h in range(SB_HEADS):
            ref[h] = r[:, h * HEAD_DIM:(h + 1) * HEAD_DIM].astype(BF16)
    xr_ref[...] = chunk(3)
    gr_ref[...] = chunk(4)


def _proj_ab(x, w_bf16, tm):
    b, s, d = x.shape
    n = w_bf16.shape[1]
    heads = jax.ShapeDtypeStruct((b, SB_HEADS, s, HEAD_DIM), BF16)
    flat = jax.ShapeDtypeStruct((b, s, LRU_WIDTH), F32)
    head_spec = pl.BlockSpec((None, SB_HEADS, tm, HEAD_DIM), lambda bi, i: (bi, 0, i, 0))
    flat_spec = pl.BlockSpec((None, tm, LRU_WIDTH), lambda bi, i: (bi, i, 0))
    return pl.pallas_call(
        _proj_ab_kernel,
        grid=(b, s // tm),
        in_specs=[pl.BlockSpec((None, tm, d), lambda bi, i: (bi, i, 0)),
                  pl.BlockSpec((d, n), lambda bi, i: (0, 0))],
        out_specs=[head_spec, head_spec, head_spec, flat_spec, flat_spec],
        out_shape=[heads, heads, heads, flat, flat],
        compiler_params=_params("parallel", "parallel"),
        name="proj_ab",
    )(x, w_bf16)


SB_DEAD_LOG_WEIGHT = -105.0


def _sb_attn_kernel(q_ref, k_ref, v_ref, o_ref, *, tq, hp):
    i = pl.program_id(2)
    row = lax.broadcasted_iota(jnp.int32, (tq, tq), 0)
    col = lax.broadcasted_iota(jnp.int32, (tq, tq), 1)
    later = (row > col).astype(BF16)
    causal = col < row

    def block(jb, carries, accs, masked):
        start = pl.multiple_of(jb * tq, tq)
        new_carries, new_accs = [], []
        for h in range(hp):
            kj = k_ref[h, pl.ds(start, tq), :]
            vj = v_ref[h, pl.ds(start, tq), :]
            z = lax.dot_general(q_ref[h], kj, (((1,), (1,)), ((), ())), preferred_element_type=F32)
            sp = _softplus(z)
            log_keep = -sp
            if masked:
                log_keep = jnp.where(causal, log_keep, 0.0)
            after = jnp.dot(log_keep.astype(BF16), later, preferred_element_type=F32)
            w = jnp.exp((z - sp) + after + carries[h])
            if masked:
                w = jnp.where(causal, w, 0.0)
            new_accs.append(accs[h] + jnp.dot(w.astype(BF16), vj, preferred_element_type=F32))
            new_carries.append(carries[h] + jnp.sum(log_keep, axis=1, keepdims=True))
        return tuple(new_carries), tuple(new_accs)

    def live(carries):
        return functools.reduce(jnp.maximum, [jnp.max(c) for c in carries])

    carries, accs = block(i, (jnp.zeros((tq, 1), F32),) * hp, (jnp.zeros((tq, HEAD_DIM), F32),) * hp, True)

    def cond(state):
        return (state[0] < i) & (state[1] > SB_DEAD_LOG_WEIGHT)

    def body(state):
        step, _, carries, accs = state
        carries, accs = block(i - 1 - step, carries, accs, False)
        return step + 1, live(carries), carries, accs

    _, _, _, accs = lax.while_loop(cond, body, (jnp.int32(0), live(carries), carries, accs))
    for h in range(hp):
        o_ref[h] = accs[h].astype(o_ref.dtype)


def _sb_attention(q, k, v, tq, hp):
    b, h, s, dh = q.shape
    return pl.pallas_call(
        functools.partial(_sb_attn_kernel, tq=tq, hp=hp),
        grid=(b, h // hp, s // tq),
        in_specs=[pl.BlockSpec((None, hp, tq, dh), lambda bi, hi, i: (bi, hi, i, 0)),
                  pl.BlockSpec((None, hp, s, dh), lambda bi, hi, i: (bi, hi, 0, 0)),
                  pl.BlockSpec((None, hp, s, dh), lambda bi, hi, i: (bi, hi, 0, 0))],
        out_specs=pl.BlockSpec((None, hp, tq, dh), lambda bi, hi, i: (bi, hi, i, 0)),
        out_shape=jax.ShapeDtypeStruct((b, h, s, dh), BF16),
        compiler_params=_params("parallel", "parallel", "parallel"),
        name="sb_attention",
    )(q, k, v)


def _gelu_tanh(x):
    return 0.5 * x * (1.0 + jnp.tanh(math.sqrt(2.0 / math.pi) * (x + 0.044715 * (x * x * x))))


def _lru_kernel(xr_ref, gr_ref, cw_ref, cb_ref, wr_ref, br_ref, wi_ref, bi_ref, lam_ref, y_ref,
                xbuf, hprev, *, ts):
    @pl.when(pl.program_id(1) == 0)
    def _():
        xbuf[0:SUBLANES, :] = jnp.zeros((SUBLANES, LRU_WIDTH), F32)
        hprev[...] = jnp.zeros_like(hprev)

    xbuf[SUBLANES:SUBLANES + ts, :] = xr_ref[...]
    xc = cb_ref[...] + cw_ref[CONV_WIDTH - 1:CONV_WIDTH, :] * xbuf[SUBLANES:SUBLANES + ts, :]
    for kk in range(CONV_WIDTH - 1):
        off = SUBLANES - (CONV_WIDTH - 1) + kk
        xc = xc + cw_ref[kk:kk + 1, :] * xbuf[off:off + ts, :]
    xbuf[0:SUBLANES, :] = xbuf[ts:ts + SUBLANES, :]

    xcb = xc.astype(BF16)
    r = _sigmoid(jnp.dot(xcb, wr_ref[...], preferred_element_type=F32) + br_ref[...])
    gi = _sigmoid(jnp.dot(xcb, wi_ref[...], preferred_element_type=F32) + bi_ref[...])
    log_a = (-LRU_C) * r * _softplus(-lam_ref[...])
    a = jnp.exp(log_a)
    u = jnp.sqrt(1.0 - a * a) * (gi * xc)

    row = lax.broadcasted_iota(jnp.int32, (ts, LRU_WIDTH), 0)
    d = 1
    while d < ts:
        if d < SUBLANES:
            keep = row >= d
            a_sh = jnp.where(keep, pltpu.roll(a, d, axis=0), 1.0)
            u_sh = jnp.where(keep, pltpu.roll(u, d, axis=0), 0.0)
            u = a * u_sh + u
            a = a * a_sh
        else:
            u = jnp.concatenate([u[:d], a[d:] * u[:ts - d] + u[d:]], axis=0)
            a = jnp.concatenate([a[:d], a[d:] * a[:ts - d]], axis=0)
        d *= 2
    h = a * hprev[0:1, :] + u
    hprev[...] = jnp.broadcast_to(h[ts - 1:ts, :], hprev.shape)
    y_ref[...] = (_gelu_tanh(gr_ref[...]) * h).astype(y_ref.dtype)


def _lru(xr, gr, conv_w, conv_b, wr_bd, b_r, wi_bd, b_i, lam, ts):
    b, s, w = xr.shape
    seq_spec = pl.BlockSpec((None, ts, w), lambda bi, i: (bi, i, 0))

    def full(shape):
        return pl.BlockSpec(shape, lambda bi, i: (0,) * len(shape))

    return pl.pallas_call(
        functools.partial(_lru_kernel, ts=ts),
        grid=(b, s // ts),
        in_specs=[seq_spec, seq_spec, full((CONV_WIDTH, w)), full((1, w)), full((w, w)), full((1, w)),
                  full((w, w)), full((1, w)), full((1, w))],
        out_specs=seq_spec,
        out_shape=jax.ShapeDtypeStruct((b, s, w), BF16),
        scratch_shapes=[pltpu.VMEM((ts + 2 * SUBLANES, w), F32), pltpu.VMEM((SUBLANES, w), F32)],
        compiler_params=_params("parallel", "arbitrary"),
        name="rg_lru",
    )(xr, gr, conv_w, conv_b.reshape(1, w), wr_bd, b_r.reshape(1, w), wi_bd, b_i.reshape(1, w),
      lam.reshape(1, w))


def _block_diag(w):
    n, c, d = w.shape
    eye = jnp.eye(n, dtype=w.dtype)
    return (eye[:, None, :, None] * w[:, :, None, :]).reshape(n * c, n * d)


def _proj_rope_kernel(x_ref, pos_ref, freq_ref, w_ref, q_ref, k_ref, vt_ref):
    tm = x_ref.shape[0]
    xb = x_ref[...].astype(BF16)
    ang_t = freq_ref[...] * pos_ref[...].astype(F32)
    reps = LANES // (HEAD_DIM // 2)
    cos = jnp.concatenate([jnp.cos(ang_t)] * reps, axis=0).T
    sin = jnp.concatenate([jnp.sin(ang_t)] * reps, axis=0).T
    lane = lax.broadcasted_iota(jnp.int32, (tm, LANES), 1)
    first_half = (lane % HEAD_DIM) < (HEAD_DIM // 2)
    heads_per_slab = LANES // HEAD_DIM

    def rope(r):
        upper = pltpu.roll(r, LANES - HEAD_DIM // 2, axis=1)
        lower = pltpu.roll(r, HEAD_DIM // 2, axis=1)
        return r * cos + jnp.where(first_half, -upper, lower) * sin

    def emit(ref, n_heads, col0, rotary, scale):
        for slab in range(n_heads // heads_per_slab):
            c0 = col0 + slab * LANES
            r = jnp.dot(xb, w_ref[:, c0:c0 + LANES], preferred_element_type=F32)
            if rotary:
                r = rope(r)
            if scale is not None:
                r = r * scale
            for j in range(heads_per_slab):
                ref[slab * heads_per_slab + j] = r[:, j * HEAD_DIM:(j + 1) * HEAD_DIM].astype(BF16)

    emit(q_ref, SWA_HEADS, 0, True, Q_SCALE)
    emit(k_ref, SWA_KV_HEADS, SWA_HEADS * HEAD_DIM, True, None)
    v0 = (SWA_HEADS + SWA_KV_HEADS) * HEAD_DIM
    for slab in range(SWA_KV_HEADS // heads_per_slab):
        r = jnp.dot(xb, w_ref[:, v0 + slab * LANES:v0 + (slab + 1) * LANES], preferred_element_type=F32)
        rt = r.T
        for j in range(heads_per_slab):
            vt_ref[slab * heads_per_slab + j] = rt[j * HEAD_DIM:(j + 1) * HEAD_DIM, :].astype(BF16)


def _proj_rope(x, positions, w_bf16, tm):
    b, s, d = x.shape
    n = w_bf16.shape[1]
    half = HEAD_DIM // 2
    inv_freq = (ROPE_THETA ** (-jnp.arange(half, dtype=F32) / half)).reshape(half, 1)

    def heads(nh):
        return (jax.ShapeDtypeStruct((b, nh, s, HEAD_DIM), BF16),
                pl.BlockSpec((None, nh, tm, HEAD_DIM), lambda bi, i: (bi, 0, i, 0)))

    (qs, qspec), (ks, kspec) = heads(SWA_HEADS), heads(SWA_KV_HEADS)
    vs = jax.ShapeDtypeStruct((b, SWA_KV_HEADS, HEAD_DIM, s), BF16)
    vspec = pl.BlockSpec((None, SWA_KV_HEADS, HEAD_DIM, tm), lambda bi, i: (bi, 0, 0, i))
    return pl.pallas_call(
        _proj_rope_kernel,
        grid=(b, s // tm),
        in_specs=[pl.BlockSpec((None, tm, d), lambda bi, i: (bi, i, 0)),
                  pl.BlockSpec((None, 1, tm), lambda bi, i: (bi, 0, i)),
                  pl.BlockSpec((half, 1), lambda bi, i: (0, 0)),
                  pl.BlockSpec((d, n), lambda bi, i: (0, 0))],
        out_specs=[qspec, kspec, vspec],
        out_shape=[qs, ks, vs],
        compiler_params=_params("parallel", "parallel"),
        name="proj_rope",
    )(x, positions.reshape(b, 1, s), inv_freq, w_bf16)


def _reduce_rows(x, op):
    while x.shape[0] > SUBLANES:
        half = x.shape[0] // 2
        x = op(x[:half], x[half:])
    for shift in (4, 2, 1):
        x = op(x, pltpu.roll(x, shift, axis=0))
    return x[0:1]


def _swa_kernel(q_ref, kp_ref, kc_ref, vtp_ref, vtc_ref, sink_ref, o_ref):
    i = pl.program_id(1)
    w = SWA_WINDOW
    key = lax.broadcasted_iota(jnp.int32, (2 * w, w), 0)
    qry = lax.broadcasted_iota(jnp.int32, (2 * w, w), 1)
    dist = qry + w - key
    visible = (dist >= 0) & (dist < w) & ((key >= w) | (i > 0))
    bias = jnp.concatenate([jnp.where(visible, 0.0, NEG_BIG)] * SWA_GROUP, axis=1)
    outs = []
    for kv in range(SWA_KV_HEADS):
        kk = jnp.concatenate([kp_ref[kv], kc_ref[kv]], axis=0)
        vvt = jnp.concatenate([vtp_ref[kv], vtc_ref[kv]], axis=1)
        qg = jnp.concatenate([q_ref[kv * SWA_GROUP + g] for g in range(SWA_GROUP)], axis=0)
        st = lax.dot_general(kk, qg, (((1,), (1,)), ((), ())), preferred_element_type=F32) + bias
        sink = sink_ref[kv:kv + 1, :]
        m = jnp.maximum(_reduce_rows(st, jnp.maximum), sink)
        p = jnp.exp(st - m)
        denom = _reduce_rows(p, jnp.add) + jnp.exp(sink - m)
        ot = jnp.dot(vvt, p.astype(BF16), preferred_element_type=F32) / denom
        outs.extend(ot[:, g * w:(g + 1) * w] for g in range(SWA_GROUP))
    o_ref[...] = jnp.concatenate(outs, axis=0).T.astype(o_ref.dtype)


def _swa(q, k, vt, sinks):
    b, nh, s, dh = q.shape
    nkv = k.shape[1]
    w = SWA_WINDOW
    cur = pl.BlockSpec((None, nkv, w, dh), lambda bi, i: (bi, 0, i, 0))
    prev = pl.BlockSpec((None, nkv, w, dh), lambda bi, i: (bi, 0, jnp.maximum(i - 1, 0), 0))
    cur_t = pl.BlockSpec((None, nkv, dh, w), lambda bi, i: (bi, 0, 0, i))
    prev_t = pl.BlockSpec((None, nkv, dh, w), lambda bi, i: (bi, 0, 0, jnp.maximum(i - 1, 0)))
    sink_tile = jnp.repeat(sinks.astype(F32).reshape(nkv, nh // nkv), w, axis=1)
    return pl.pallas_call(
        _swa_kernel,
        grid=(b, s // w),
        in_specs=[pl.BlockSpec((None, nh, w, dh), lambda bi, i: (bi, 0, i, 0)),
                  prev, cur, prev_t, cur_t,
                  pl.BlockSpec(sink_tile.shape, lambda bi, i: (0, 0))],
        out_specs=pl.BlockSpec((None, w, nh * dh), lambda bi, i: (bi, i, 0)),
        out_shape=jax.ShapeDtypeStruct((b, s, nh * dh), BF16),
        compiler_params=_params("parallel", "parallel"),
        name="swa",
    )(q, k, k, vt, vt, sink_tile)


PAIRS_PER_GROUP = GROUP_SIZE * (GROUP_SIZE - 1) // 2
N_BUCKETS = N_GROUPS * PAIRS_PER_GROUP
ROUTE_ROWS = SUBLANES


def _route(logits_t):
    rows = [logits_t[e:e + 1, :] for e in range(N_EXPERTS)]
    mx = functools.reduce(jnp.maximum, rows)
    ex = [jnp.exp(r - mx) for r in rows]
    total = functools.reduce(lambda p, q: p + q, ex)
    probs = [e / total for e in ex]

    group_score = []
    for g in range(N_GROUPS):
        a, b, c, d = probs[g * GROUP_SIZE:(g + 1) * GROUP_SIZE]
        hi1, lo1 = jnp.maximum(a, b), jnp.minimum(a, b)
        hi2, lo2 = jnp.maximum(c, d), jnp.minimum(c, d)
        top1 = jnp.maximum(hi1, hi2)
        top2 = jnp.maximum(jnp.minimum(hi1, hi2), jnp.maximum(lo1, lo2))
        group_score.append(top1 + top2)
    best = functools.reduce(jnp.maximum, group_score)
    g_sel = jnp.full(best.shape, N_GROUPS - 1, jnp.int32)
    for g in range(N_GROUPS - 2, -1, -1):
        g_sel = jnp.where(group_score[g] == best, g, g_sel)

    in_group = []
    for j in range(GROUP_SIZE):
        val = probs[(N_GROUPS - 1) * GROUP_SIZE + j]
        for g in range(N_GROUPS - 2, -1, -1):
            val = jnp.where(g_sel == g, probs[g * GROUP_SIZE + j], val)
        in_group.append(val)

    def first_argmax(vals):
        m = functools.reduce(jnp.maximum, vals)
        idx = jnp.full(m.shape, GROUP_SIZE - 1, jnp.int32)
        for j in range(GROUP_SIZE - 2, -1, -1):
            idx = jnp.where(vals[j] == m, j, idx)
        return m, idx

    w1, i1 = first_argmax(in_group)
    rest = [jnp.where(i1 == j, -1.0, in_group[j]) for j in range(GROUP_SIZE)]
    w2, i2 = first_argmax(rest)
    norm = w1 + w2
    first_is_lo = i1 < i2
    i_lo = jnp.minimum(i1, i2)
    i_hi = jnp.maximum(i1, i2)
    pair = jnp.where(i_lo == 0, i_hi - 1, jnp.where(i_lo == 1, i_hi + 1, PAIRS_PER_GROUP - 1))
    bucket = (g_sel * PAIRS_PER_GROUP + pair).astype(F32)
    w_lo = jnp.where(first_is_lo, w1, w2) / norm
    w_hi = jnp.where(first_is_lo, w2, w1) / norm
    return bucket, w_lo, w_hi


def _mix_out_kernel(*refs, n_head_major, alpha):
    x_ref = refs[0]
    y_refs = refs[1:1 + len(n_head_major)]
    w_refs = refs[1 + len(n_head_major):1 + 2 * len(n_head_major)]
    g_ref, b_ref, wrh_ref, wrl_ref, brt_ref, x1e_ref, route_ref = refs[1 + 2 * len(n_head_major):]
    tm, d = x_ref.shape
    h = alpha * x_ref[...]
    for y_ref, w_ref, nh in zip(y_refs, w_refs, n_head_major):
        if nh:
            y = jnp.concatenate([y_ref[j] for j in range(nh)], axis=-1)
        else:
            y = y_ref[...]
        h = h + jnp.dot(y, w_ref[...], preferred_element_type=F32)
    x1 = _layer_norm(h, g_ref[...], b_ref[...])
    x_hi = x1.astype(BF16)
    x_lo = (x1 - x_hi.astype(F32)).astype(BF16)

    def nt_dot(w, xv):
        return lax.dot_general(w, xv, (((1,), (1,)), ((), ())), preferred_element_type=F32)

    logits_t = (nt_dot(wrh_ref[...], x_hi) + nt_dot(wrh_ref[...], x_lo) + nt_dot(wrl_ref[...], x_hi)
                + brt_ref[...])
    route = jnp.concatenate(list(_route(logits_t)) + [jnp.zeros((ROUTE_ROWS - 3, tm), F32)], axis=0)
    route_ref[...] = route
    x1e_ref[:, 0:d] = x1
    x1e_ref[:, d:d + LANES] = jnp.concatenate([route, jnp.zeros((LANES - ROUTE_ROWS, tm), F32)], axis=0).T


def _mix_out(x, ys, ws, ln_g, ln_b, w_router, b_router, alpha, tm):
    b, s, d = x.shape
    n_head_major = tuple(y.shape[1] if hm else 0 for y, hm in ys)
    y_specs = []
    for (y, hm) in ys:
        if hm:
            y_specs.append(pl.BlockSpec((None, y.shape[1], tm, y.shape[3]), lambda bi, i: (bi, 0, i, 0)))
        else:
            y_specs.append(pl.BlockSpec((None, tm, y.shape[2]), lambda bi, i: (bi, i, 0)))
    w_specs = [pl.BlockSpec(w.shape, lambda bi, i: (0, 0)) for w in ws]
    row = pl.BlockSpec((1, d), lambda bi, i: (0, 0))
    tok = pl.BlockSpec((None, tm, d), lambda bi, i: (bi, i, 0))
    wr_t = w_router.T.astype(F32)
    wr_hi = wr_t.astype(BF16)
    wr_lo = (wr_t - wr_hi.astype(F32)).astype(BF16)
    wr_spec = pl.BlockSpec((N_EXPERTS, d), lambda bi, i: (0, 0))
    return pl.pallas_call(
        functools.partial(_mix_out_kernel, n_head_major=n_head_major, alpha=alpha),
        grid=(b, s // tm),
        in_specs=[tok] + y_specs + w_specs + [row, row, wr_spec, wr_spec,
                  pl.BlockSpec((N_EXPERTS, 1), lambda bi, i: (0, 0))],
        out_specs=[pl.BlockSpec((None, tm, d + LANES), lambda bi, i: (bi, i, 0)),
                   pl.BlockSpec((None, ROUTE_ROWS, tm), lambda bi, i: (bi, 0, i))],
        out_shape=[jax.ShapeDtypeStruct((b, s, d + LANES), F32),
                   jax.ShapeDtypeStruct((b, ROUTE_ROWS, s), F32)],
        compiler_params=_params("parallel", "parallel"),
        name="mix_out_ln_router",
    )(x, *[y for y, _ in ys], *ws, ln_g.reshape(1, d), ln_b.reshape(1, d),
      wr_hi, wr_lo, b_router.astype(F32).reshape(N_EXPERTS, 1))


MOE_TILE = 256
PLAN_COLS = 256
META_ROWS = SUBLANES


def _num_moe_tiles(t):
    return t // MOE_TILE + N_BUCKETS


def _plan_kernel(bid_ref, pos_ref, meta_ref):
    r, c = bid_ref.shape
    bid = bid_ref[...]
    before = (lax.broadcasted_iota(jnp.int32, (c, c), 0)
              < lax.broadcasted_iota(jnp.int32, (c, c), 1)).astype(BF16)
    rows_before = (lax.broadcasted_iota(jnp.int32, (r, r), 1)
                   < lax.broadcasted_iota(jnp.int32, (r, r), 0)).astype(BF16)
    lane = lax.broadcasted_iota(jnp.int32, (1, LANES), 1)
    tile_start = lane.astype(F32) * MOE_TILE

    def body(b, state):
        base, pos, tile_bucket, last_tile = state
        ind = (bid == lax.convert_element_type(b, F32)).astype(F32)
        within = jnp.dot(ind.astype(BF16), before, preferred_element_type=F32)
        row_total = jnp.sum(ind, axis=1, keepdims=True)
        row_off = jnp.dot(rows_before, jnp.broadcast_to(row_total, (r, LANES)).astype(BF16),
                          preferred_element_type=F32)[:, 0:1]
        count = jnp.sum(row_total, axis=0, keepdims=True)
        padded = jnp.floor((count + (MOE_TILE - 1)) * (1.0 / MOE_TILE)) * MOE_TILE
        pos = pos + ind * (base + row_off + within)
        end = base + padded
        tile_bucket = tile_bucket + (tile_start >= end).astype(F32)
        last_row = jnp.where(count > 0.0, end - MOE_TILE, -1.0)
        last_tile = jnp.where(lane == b, last_row, last_tile)
        return end, pos, tile_bucket, last_tile

    total, pos, tile_bucket, last_tile = lax.fori_loop(
        0, N_BUCKETS, body, (jnp.zeros((1, 1), F32), jnp.zeros((r, c), F32), jnp.zeros((1, LANES), F32),
                             jnp.full((1, LANES), -1.0, F32)))
    pos_ref[...] = pos.astype(jnp.int32)

    tb = jnp.minimum(tile_bucket, N_BUCKETS - 1.0)
    group = sum((tb >= g * PAIRS_PER_GROUP).astype(F32) for g in range(1, N_GROUPS))
    pair = tb - group * PAIRS_PER_GROUP
    i_lo = jnp.where(pair < 3, 0.0, jnp.where(pair < 5, 1.0, 2.0))
    i_hi = jnp.where(pair < 3, pair + 1.0, jnp.where(pair < 5, pair - 1.0, 3.0))
    meta = jnp.concatenate([group * GROUP_SIZE + i_lo, group * GROUP_SIZE + i_hi,
                            (tile_start < total).astype(F32),
                            jnp.where(lane == N_BUCKETS, total, last_tile),
                            jnp.zeros((META_ROWS - 4, LANES), F32)], axis=0)
    meta_ref[...] = meta.astype(jnp.int32)


def _plan(bucket_ids):
    r, c = bucket_ids.shape
    return pl.pallas_call(
        _plan_kernel,
        out_shape=[jax.ShapeDtypeStruct((r, c), jnp.int32), jax.ShapeDtypeStruct((META_ROWS, LANES), jnp.int32)],
        compiler_params=pltpu.CompilerParams(vmem_limit_bytes=VMEM_LIMIT_BYTES),
        name="moe_plan",
    )(bucket_ids)


def _row_copy(src_ref, src_row, dst_ref, dst_row, sem):
    return pltpu.make_async_copy(src_ref.at[pl.ds(src_row, 1), :], dst_ref.at[pl.ds(dst_row, 1), :], sem)


def _dispatch_kernel(pos_ref, last_ref, x_ref, o_ref, zeros, sem, zsem, *, first_spare):
    tm, dw = x_ref.shape
    i = pl.program_id(0)

    @pl.when(i == 0)
    def _():
        zeros[...] = jnp.zeros_like(zeros)
        rows_in_use = last_ref[N_BUCKETS]
        clears = [(last_ref[b] >= 0, last_ref[b]) for b in range(N_BUCKETS)]
        clears += [(k * MOE_TILE >= rows_in_use, k * MOE_TILE)
                   for k in range(first_spare, o_ref.shape[0] // MOE_TILE)]

        def clear(row):
            start = row if isinstance(row, int) else pl.multiple_of(row, MOE_TILE)
            return pltpu.make_async_copy(zeros, o_ref.at[pl.ds(start, MOE_TILE), :], zsem)

        for needed, row in clears:
            @pl.when(needed)
            def _():
                clear(row).start()
        for needed, row in clears:
            @pl.when(needed)
            def _():
                clear(row).wait()

    base = i * tm
    for r in range(tm):
        _row_copy(x_ref, r, o_ref, pos_ref[base + r], sem).start()
    pltpu.make_async_copy(x_ref, o_ref.at[pl.ds(0, tm), :], sem).wait()


def _dispatch(pos, last_tile_rows, x1e, n_rows, tm):
    t, dw = x1e.shape
    return pl.pallas_call(
        functools.partial(_dispatch_kernel, first_spare=t // MOE_TILE),
        grid_spec=pltpu.PrefetchScalarGridSpec(
            num_scalar_prefetch=2,
            grid=(t // tm,),
            in_specs=[pl.BlockSpec((tm, dw), lambda i, pos_ref, last_ref: (i, 0))],
            out_specs=pl.BlockSpec(memory_space=pl.ANY),
            scratch_shapes=[pltpu.VMEM((MOE_TILE, dw), F32), pltpu.SemaphoreType.DMA(()),
                            pltpu.SemaphoreType.DMA(())]),
        out_shape=jax.ShapeDtypeStruct((n_rows, dw), F32),
        compiler_params=_params("arbitrary"),
        name="moe_dispatch",
    )(pos, last_tile_rows, x1e)


def _moe_tile_kernel(elo_ref, ehi_ref, valid_ref, x_ref, wgl_ref, wul_ref, wdl_ref, wgh_ref, wuh_ref,
                     wdh_ref, o_ref, wgl_b, wul_b, wdl_b, wgh_b, wuh_b, wdh_b):
    d = o_ref.shape[1]
    k = pl.program_id(0)
    prev = jnp.maximum(k - 1, 0)
    valid = valid_ref[k] != 0

    def refresh(e_ref, srcs, dsts):
        @pl.when(valid & ((k == 0) | (e_ref[k] != e_ref[prev])))
        def _():
            for src, dst in zip(srcs, dsts):
                dst[...] = src[...].astype(BF16)

    refresh(elo_ref, (wgl_ref, wul_ref, wdl_ref), (wgl_b, wul_b, wdl_b))
    refresh(ehi_ref, (wgh_ref, wuh_ref, wdh_ref), (wgh_b, wuh_b, wdh_b))

    @pl.when(jnp.logical_not(valid))
    def _():
        o_ref[...] = jnp.zeros_like(o_ref)

    @pl.when(valid)
    def _():
        x = x_ref[:, 0:d].astype(BF16)
        acc = None
        for wg_b, wu_b, wd_b, lane in ((wgl_b, wul_b, wdl_b, d + 1), (wgh_b, wuh_b, wdh_b, d + 2)):
            weight = x_ref[:, lane:lane + 1]
            hg = jnp.dot(x, wg_b[...], preferred_element_type=F32)
            hu = jnp.dot(x, wu_b[...], preferred_element_type=F32)
            hidden = (hg * _sigmoid(hg)) * hu * weight
            y = jnp.dot(hidden.astype(BF16), wd_b[...], preferred_element_type=F32)
            acc = y if acc is None else acc + y
        o_ref[...] = acc


def _moe_tiles(e_lo, e_hi, valid, xs, wg, wu, wd, layer):
    n_rows, dw = xs.shape
    _, _, d, f = wg.shape
    up_lo = pl.BlockSpec((None, None, d, f), lambda k, lo, hi, ok: (layer, lo[k], 0, 0))
    up_hi = pl.BlockSpec((None, None, d, f), lambda k, lo, hi, ok: (layer, hi[k], 0, 0))
    down_lo = pl.BlockSpec((None, None, f, d), lambda k, lo, hi, ok: (layer, lo[k], 0, 0))
    down_hi = pl.BlockSpec((None, None, f, d), lambda k, lo, hi, ok: (layer, hi[k], 0, 0))
    up_b = pltpu.VMEM((d, f), BF16)
    down_b = pltpu.VMEM((f, d), BF16)
    return pl.pallas_call(
        _moe_tile_kernel,
        grid_spec=pltpu.PrefetchScalarGridSpec(
            num_scalar_prefetch=3,
            grid=(n_rows // MOE_TILE,),
            in_specs=[pl.BlockSpec((MOE_TILE, dw), lambda k, lo, hi, ok: (k, 0)),
                      up_lo, up_lo, down_lo, up_hi, up_hi, down_hi],
            out_specs=pl.BlockSpec((MOE_TILE, d), lambda k, lo, hi, ok: (k, 0)),
            scratch_shapes=[up_b, up_b, down_b, up_b, up_b, down_b]),
        out_shape=jax.ShapeDtypeStruct((n_rows, d), F32),
        compiler_params=_params("arbitrary"),
        name="moe_tiles",
    )(e_lo, e_hi, valid, xs, wg, wu, wd, wg, wu, wd)


def _ln_ple_kernel(pos_ref, x1_ref, ys_ref, p_ref, g_ref, b_ref, wg_ref, bg_ref, wp_ref, o_ref,
                   m_even, m_odd, sem_even, sem_odd, *, alpha):
    tm = x1_ref.shape[0]
    i = pl.program_id(0)
    last = pl.num_programs(0) - 1

    def wait(buf, sem):
        pltpu.make_async_copy(ys_ref.at[pl.ds(0, tm), :], buf, sem).wait()

    @pl.when(i == 0)
    def _():
        def issue(r, carry):
            _row_copy(ys_ref, pos_ref[r], m_even, r, sem_even).start()
            return carry

        lax.fori_loop(0, tm, issue, 0, unroll=8)

    def step(cur, cur_sem, nxt, nxt_sem):
        wait(cur, cur_sem)
        base = jnp.minimum(i + 1, last) * tm
        for r in range(tm):
            _row_copy(ys_ref, pos_ref[base + r], nxt, r, nxt_sem).start()
        emb = jnp.dot(p_ref[...].astype(BF16), wp_ref[...], preferred_element_type=F32)
        x2 = _layer_norm(alpha * x1_ref[...] + cur[...], g_ref[...], b_ref[...])
        gate = _sigmoid(jnp.dot(x2.astype(BF16), wg_ref[...], preferred_element_type=F32) + bg_ref[...])
        o_ref[...] = x2 + gate * emb

        @pl.when(i == last)
        def _():
            wait(nxt, nxt_sem)

    @pl.when(i % 2 == 0)
    def _():
        step(m_even, sem_even, m_odd, sem_odd)

    @pl.when(i % 2 == 1)
    def _():
        step(m_odd, sem_odd, m_even, sem_even)


def _ln_ple(pos, x1e, ys, p, ln_g, ln_b, wg, bg, wp, alpha, tm):
    t = x1e.shape[0]
    d = ys.shape[1]
    pd = p.shape[1]
    tok = pl.BlockSpec((tm, d), lambda i, pos_ref: (i, 0))
    row = pl.BlockSpec((1, d), lambda i, pos_ref: (0, 0))
    return pl.pallas_call(
        functools.partial(_ln_ple_kernel, alpha=alpha),
        grid_spec=pltpu.PrefetchScalarGridSpec(
            num_scalar_prefetch=1,
            grid=(t // tm,),
            in_specs=[tok, pl.BlockSpec(memory_space=pl.ANY),
                      pl.BlockSpec((tm, pd), lambda i, pos_ref: (i, 0)), row, row,
                      pl.BlockSpec((d, d), lambda i, pos_ref: (0, 0)), row,
                      pl.BlockSpec((pd, d), lambda i, pos_ref: (0, 0))],
            out_specs=tok,
            scratch_shapes=[pltpu.VMEM((tm, d), F32), pltpu.VMEM((tm, d), F32),
                            pltpu.SemaphoreType.DMA(()), pltpu.SemaphoreType.DMA(())]),
        out_shape=jax.ShapeDtypeStruct((t, d), F32),
        compiler_params=_params("arbitrary"),
        name="ln_ple",
    )(pos, x1e, ys, p, ln_g.reshape(1, d), ln_b.reshape(1, d), wg, bg.reshape(1, d), wp)


def _pick_tile(n, target):
    t = min(n, target)
    while n % t:
        t //= 2
    return t


def kernel(x, p, positions, w_in_ab, w_out_ab, conv_w, conv_b, lru_w_r, lru_b_r, lru_w_i, lru_b_i, lru_lambda, w_qkv_c, w_out_c, sinks_c, ln_mix_g, ln_mix_b, ln_ffn_g, ln_ffn_b, w_router, b_router, exp_w_gate, exp_w_up, exp_w_down, ple_w_proj, ple_w_gate, ple_b_gate):
    b, s, d = x.shape
    depth = p.shape[0]
    t = b * s
    alpha = (2 * depth) ** 0.25
    tm = _pick_tile(s, 512)
    assert t % MOE_TILE == 0 and t % PLAN_COLS == 0
    n_tiles = _num_moe_tiles(t)
    assert n_tiles <= LANES
    for i in range(depth):
        j = i // 2
        if i % 2 == 0:
            q, k, v, xr, gr = _proj_ab(x, w_in_ab[j].astype(BF16), tm)
            y_sb = _sb_attention(q, k, v, _pick_tile(s, 256), 4)
            y_lru = _lru(xr, gr, conv_w[j], conv_b[j], _block_diag(lru_w_r[j]).astype(BF16), lru_b_r[j],
                         _block_diag(lru_w_i[j]).astype(BF16), lru_b_i[j], lru_lambda[j],
                         _pick_tile(s, 256))
            w_out = w_out_ab[j].astype(BF16)
            ys = [(y_sb, True), (y_lru, False)]
            ws = [w_out[:SB_WIDTH], w_out[SB_WIDTH:]]
        else:
            q, k, v = _proj_rope(x, positions, w_qkv_c[j].astype(BF16), tm)
            y = _swa(q, k, v, sinks_c[j])
            ys = [(y, False)]
            ws = [w_out_c[j].astype(BF16)]
        x1e, route = _mix_out(x, ys, ws, ln_mix_g[i], ln_mix_b[i], w_router, b_router, alpha, tm)
        x1e = x1e.reshape(t, d + LANES)
        pos, meta = _plan(route[:, 0, :].reshape(t // PLAN_COLS, PLAN_COLS))
        pos = pos.reshape(t)
        xs = _dispatch(pos, meta[3, :N_BUCKETS + 1], x1e, n_tiles * MOE_TILE, _pick_tile(t, 512))
        ys_moe = _moe_tiles(meta[0, :n_tiles], meta[1, :n_tiles], meta[2, :n_tiles], xs,
                            exp_w_gate, exp_w_up, exp_w_down, i)
        x = _ln_ple(pos, x1e, ys_moe, p[i].reshape(t, -1), ln_ffn_g[i], ln_ffn_b[i],
                    ple_w_gate[i].astype(BF16), ple_b_gate[i], ple_w_proj[i].astype(BF16), alpha,
                    _pick_tile(t, 512)).reshape(b, s, d)
    return x
```

```python
import functools
import math

import jax
import jax.numpy as jnp
from jax import lax
from jax.experimental import pallas as pl
from jax.experimental.pallas import tpu as pltpu

HEAD_DIM = 64
SB_HEADS = 8
SB_WIDTH = SB_HEADS * HEAD_DIM
LRU_WIDTH = 512
LRU_BLOCKS = 8
LRU_C = 8.0
CONV_WIDTH = 4
SWA_HEADS = 16
SWA_KV_HEADS = 4
SWA_GROUP = SWA_HEADS // SWA_KV_HEADS
SWA_WINDOW = 128
ROPE_THETA = 10000.0
N_EXPERTS = 16
N_GROUPS = 4
GROUP_SIZE = N_EXPERTS // N_GROUPS
LN_EPS = 1e-5
Q_SCALE = HEAD_DIM ** -0.5

LANES = 128
SUBLANES = 8
VMEM_LIMIT_BYTES = 48 * 1024 * 1024

NEG_BIG = -1e30

BF16 = jnp.bfloat16
F32 = jnp.float32


def _params(*semantics):
    return pltpu.CompilerParams(dimension_semantics=semantics, vmem_limit_bytes=VMEM_LIMIT_BYTES)


def _softplus(z):
    return jnp.maximum(z, 0.0) + jnp.log(1.0 + jnp.exp(-jnp.abs(z)))


def _sigmoid(z):
    return 1.0 / (1.0 + jnp.exp(-z))


def _layer_norm(y, g, b):
    mu = jnp.mean(y, axis=-1, keepdims=True)
    d = y - mu
    var = jnp.mean(d * d, axis=-1, keepdims=True)
    return d * lax.rsqrt(var + LN_EPS) * g + b


def _proj_ab_kernel(x_ref, w_ref, q_ref, k_ref, v_ref, xr_ref, gr_ref):
    xb = x_ref[...].astype(BF16)

    def chunk(c):
        return jnp.dot(xb, w_ref[:, c * SB_WIDTH:(c + 1) * SB_WIDTH], preferred_element_type=F32)

    for c, (ref, scale) in enumerate(((q_ref, Q_SCALE), (k_ref, None), (v_ref, None))):
        r = chunk(c)
        if scale is not None:
            r = r * scale
        for h in range(SB_HEADS):
            ref[h] = r[:, h * HEAD_DIM:(h + 1) * HEAD_DIM].astype(BF16)
    xr_ref[...] = chunk(3)
    gr_ref[...] = chunk(4)


def _proj_ab(x, w_bf16, tm):
    b, s, d = x.shape
    n = w_bf16.shape[1]
    heads = jax.ShapeDtypeStruct((b, SB_HEADS, s, HEAD_DIM), BF16)
    flat = jax.ShapeDtypeStruct((b, s, LRU_WIDTH), F32)
    head_spec = pl.BlockSpec((None, SB_HEADS, tm, HEAD_DIM), lambda bi, i: (bi, 0, i, 0))
    flat_spec = pl.BlockSpec((None, tm, LRU_WIDTH), lambda bi, i: (bi, i, 0))
    return pl.pallas_call(
        _proj_ab_kernel,
        grid=(b, s // tm),
        in_specs=[pl.BlockSpec((None, tm, d), lambda bi, i: (bi, i, 0)),
                  pl.BlockSpec((d, n), lambda bi, i: (0, 0))],
        out_specs=[head_spec, head_spec, head_spec, flat_spec, flat_spec],
        out_shape=[heads, heads, heads, flat, flat],
        compiler_params=_params("parallel", "parallel"),
        name="proj_ab",
    )(x, w_bf16)


SB_DEAD_LOG_WEIGHT = -105.0


def _sb_attn_kernel(q_ref, k_ref, v_ref, o_ref, *, tq, hp):
    i = pl.program_id(2)
    row = lax.broadcasted_iota(jnp.int32, (tq, tq), 0)
    col = lax.broadcasted_iota(jnp.int32, (tq, tq), 1)
    later = (row > col).astype(BF16)
    causal = col < row

    def block(jb, carries, accs, masked):
        start = pl.multiple_of(jb * tq, tq)
        new_carries, new_accs = [], []
        for h in range(hp):
            kj = k_ref[h, pl.ds(start, tq), :]
            vj = v_ref[h, pl.ds(start, tq), :]
            z = lax.dot_general(q_ref[h], kj, (((1,), (1,)), ((), ())), preferred_element_type=F32)
            sp = _softplus(z)
            log_keep = -sp
            if masked:
                log_keep = jnp.where(causal, log_keep, 0.0)
            after = jnp.dot(log_keep.astype(BF16), later, preferred_element_type=F32)
            w = jnp.exp((z - sp) + after + carries[h])
            if masked:
                w = jnp.where(causal, w, 0.0)
            new_accs.append(accs[h] + jnp.dot(w.astype(BF16), vj, preferred_element_type=F32))
            new_carries.append(carries[h] + jnp.sum(log_keep, axis=1, keepdims=True))
        return tuple(new_carries), tuple(new_accs)

    def live(carries):
        return functools.reduce(jnp.maximum, [jnp.max(c) for c in carries])

    carries, accs = block(i, (jnp.zeros((tq, 1), F32),) * hp, (jnp.zeros((tq, HEAD_DIM), F32),) * hp, True)

    def cond(state):
        return (state[0] < i) & (state[1] > SB_DEAD_LOG_WEIGHT)

    def body(state):
        step, _, carries, accs = state
        carries, accs = block(i - 1 - step, carries, accs, False)
        return step + 1, live(carries), carries, accs

    _, _, _, accs = lax.while_loop(cond, body, (jnp.int32(0), live(carries), carries, accs))
    for h in range(hp):
        o_ref[h] = accs[h].astype(o_ref.dtype)


def _sb_attention(q, k, v, tq, hp):
    b, h, s, dh = q.shape
    return pl.pallas_call(
        functools.partial(_sb_attn_kernel, tq=tq, hp=hp),
        grid=(b, h // hp, s // tq),
        in_specs=[pl.BlockSpec((None, hp, tq, dh), lambda bi, hi, i: (bi, hi, i, 0)),
                  pl.BlockSpec((None, hp, s, dh), lambda bi, hi, i: (bi, hi, 0, 0)),
                  pl.BlockSpec((None, hp, s, dh), lambda bi, hi, i: (bi, hi, 0, 0))],
        out_specs=pl.BlockSpec((None, hp, tq, dh), lambda bi, hi, i: (bi, hi, i, 0)),
        out_shape=jax.ShapeDtypeStruct((b, h, s, dh), BF16),
        compiler_params=_params("parallel", "parallel", "parallel"),
        name="sb_attention",
    )(q, k, v)


def _gelu_tanh(x):
    return 0.5 * x * (1.0 + jnp.tanh(math.sqrt(2.0 / math.pi) * (x + 0.044715 * (x * x * x))))


def _lru_kernel(xr_ref, gr_ref, cw_ref, cb_ref, wr_ref, br_ref, wi_ref, bi_ref, lam_ref, y_ref,
                xbuf, hprev, *, ts):
    @pl.when(pl.program_id(1) == 0)
    def _():
        xbuf[0:SUBLANES, :] = jnp.zeros((SUBLANES, LRU_WIDTH), F32)
        hprev[...] = jnp.zeros_like(hprev)

    xbuf[SUBLANES:SUBLANES + ts, :] = xr_ref[...]
    xc = cb_ref[...] + cw_ref[CONV_WIDTH - 1:CONV_WIDTH, :] * xbuf[SUBLANES:SUBLANES + ts, :]
    for kk in range(CONV_WIDTH - 1):
        off = SUBLANES - (CONV_WIDTH - 1) + kk
        xc = xc + cw_ref[kk:kk + 1, :] * xbuf[off:off + ts, :]
    xbuf[0:SUBLANES, :] = xbuf[ts:ts + SUBLANES, :]

    xcb = xc.astype(BF16)
    r = _sigmoid(jnp.dot(xcb, wr_ref[...], preferred_element_type=F32) + br_ref[...])
    gi = _sigmoid(jnp.dot(xcb, wi_ref[...], preferred_element_type=F32) + bi_ref[...])
    log_a = (-LRU_C) * r * _softplus(-lam_ref[...])
    a = jnp.exp(log_a)
    u = jnp.sqrt(1.0 - a * a) * (gi * xc)

    row = lax.broadcasted_iota(jnp.int32, (ts, LRU_WIDTH), 0)
    d = 1
    while d < ts:
        if d < SUBLANES:
            keep = row >= d
            a_sh = jnp.where(keep, pltpu.roll(a, d, axis=0), 1.0)
            u_sh = jnp.where(keep, pltpu.roll(u, d, axis=0), 0.0)
            u = a * u_sh + u
            a = a * a_sh
        else:
            u = jnp.concatenate([u[:d], a[d:] * u[:ts - d] + u[d:]], axis=0)
            a = jnp.concatenate([a[:d], a[d:] * a[:ts - d]], axis=0)
        d *= 2
    h = a * hprev[0:1, :] + u
    hprev[...] = jnp.broadcast_to(h[ts - 1:ts, :], hprev.shape)
    y_ref[...] = (_gelu_tanh(gr_ref[...]) * h).astype(y_ref.dtype)


def _lru(xr, gr, conv_w, conv_b, wr_bd, b_r, wi_bd, b_i, lam, ts):
    b, s, w = xr.shape
    seq_spec = pl.BlockSpec((None, ts, w), lambda bi, i: (bi, i, 0))

    def full(shape):
        return pl.BlockSpec(shape, lambda bi, i: (0,) * len(shape))

    return pl.pallas_call(
        functools.partial(_lru_kernel, ts=ts),
        grid=(b, s // ts),
        in_specs=[seq_spec, seq_spec, full((CONV_WIDTH, w)), full((1, w)), full((w, w)), full((1, w)),
                  full((w, w)), full((1, w)), full((1, w))],
        out_specs=seq_spec,
        out_shape=jax.ShapeDtypeStruct((b, s, w), BF16),
        scratch_shapes=[pltpu.VMEM((ts + 2 * SUBLANES, w), F32), pltpu.VMEM((SUBLANES, w), F32)],
        compiler_params=_params("parallel", "arbitrary"),
        name="rg_lru",
    )(xr, gr, conv_w, conv_b.reshape(1, w), wr_bd, b_r.reshape(1, w), wi_bd, b_i.reshape(1, w),
      lam.reshape(1, w))


def _block_diag(w):
    n, c, d = w.shape
    eye = jnp.eye(n, dtype=w.dtype)
    return (eye[:, None, :, None] * w[:, :, None, :]).reshape(n * c, n * d)


def _proj_rope_kernel(x_ref, pos_ref, freq_ref, w_ref, q_ref, k_ref, vt_ref):
    tm = x_ref.shape[0]
    xb = x_ref[...].astype(BF16)
    ang_t = freq_ref[...] * pos_ref[...].astype(F32)
    reps = LANES // (HEAD_DIM // 2)
    cos = jnp.concatenate([jnp.cos(ang_t)] * reps, axis=0).T
    sin = jnp.concatenate([jnp.sin(ang_t)] * reps, axis=0).T
    lane = lax.broadcasted_iota(jnp.int32, (tm, LANES), 1)
    first_half = (lane % HEAD_DIM) < (HEAD_DIM // 2)
    heads_per_slab = LANES // HEAD_DIM

    def rope(r):
        upper = pltpu.roll(r, LANES - HEAD_DIM // 2, axis=1)
        lower = pltpu.roll(r, HEAD_DIM // 2, axis=1)
        return r * cos + jnp.where(first_half, -upper, lower) * sin

    def emit(ref, n_heads, col0, rotary, scale):
        for slab in range(n_heads // heads_per_slab):
            c0 = col0 + slab * LANES
            r = jnp.dot(xb, w_ref[:, c0:c0 + LANES], preferred_element_type=F32)
            if rotary:
                r = rope(r)
            if scale is not None:
                r = r * scale
            for j in range(heads_per_slab):
                ref[slab * heads_per_slab + j] = r[:, j * HEAD_DIM:(j + 1) * HEAD_DIM].astype(BF16)

    emit(q_ref, SWA_HEADS, 0, True, Q_SCALE)
    emit(k_ref, SWA_KV_HEADS, SWA_HEADS * HEAD_DIM, True, None)
    v0 = (SWA_HEADS + SWA_KV_HEADS) * HEAD_DIM
    for slab in range(SWA_KV_HEADS // heads_per_slab):
        r = jnp.dot(xb, w_ref[:, v0 + slab * LANES:v0 + (slab + 1) * LANES], preferred_element_type=F32)
        rt = r.T
        for j in range(heads_per_slab):
            vt_ref[slab * heads_per_slab + j] = rt[j * HEAD_DIM:(j + 1) * HEAD_DIM, :].astype(BF16)


def _proj_rope(x, positions, w_bf16, tm):
    b, s, d = x.shape
    n = w_bf16.shape[1]
    half = HEAD_DIM // 2
    inv_freq = (ROPE_THETA ** (-jnp.arange(half, dtype=F32) / half)).reshape(half, 1)

    def heads(nh):
        return (jax.ShapeDtypeStruct((b, nh, s, HEAD_DIM), BF16),
                pl.BlockSpec((None, nh, tm, HEAD_DIM), lambda bi, i: (bi, 0, i, 0)))

    (qs, qspec), (ks, kspec) = heads(SWA_HEADS), heads(SWA_KV_HEADS)
    vs = jax.ShapeDtypeStruct((b, SWA_KV_HEADS, HEAD_DIM, s), BF16)
    vspec = pl.BlockSpec((None, SWA_KV_HEADS, HEAD_DIM, tm), lambda bi, i: (bi, 0, 0, i))
    return pl.pallas_call(
        _proj_rope_kernel,
        grid=(b, s // tm),
        in_specs=[pl.BlockSpec((None, tm, d), lambda bi, i: (bi, i, 0)),
                  pl.BlockSpec((None, 1, tm), lambda bi, i: (bi, 0, i)),
                  pl.BlockSpec((half, 1), lambda bi, i: (0, 0)),
                  pl.BlockSpec((d, n), lambda bi, i: (0, 0))],
        out_specs=[qspec, kspec, vspec],
        out_shape=[qs, ks, vs],
        compiler_params=_params("parallel", "parallel"),
        name="proj_rope",
    )(x, positions.reshape(b, 1, s), inv_freq, w_bf16)


def _reduce_rows(x, op):
    while x.shape[0] > SUBLANES:
        half = x.shape[0] // 2
        x = op(x[:half], x[half:])
    for shift in (4, 2, 1):
        x = op(x, pltpu.roll(x, shift, axis=0))
    return x[0:1]


def _swa_kernel(q_ref, kp_ref, kc_ref, vtp_ref, vtc_ref, sink_ref, o_ref):
    i = pl.program_id(1)
    w = SWA_WINDOW
    key = lax.broadcasted_iota(jnp.int32, (2 * w, w), 0)
    qry = lax.broadcasted_iota(jnp.int32, (2 * w, w), 1)
    dist = qry + w - key
    visible = (dist >= 0) & (dist < w) & ((key >= w) | (i > 0))
    bias = jnp.concatenate([jnp.where(visible, 0.0, NEG_BIG)] * SWA_GROUP, axis=1)
    outs = []
    for kv in range(SWA_KV_HEADS):
        kk = jnp.concatenate([kp_ref[kv], kc_ref[kv]], axis=0)
        vvt = jnp.concatenate([vtp_ref[kv], vtc_ref[kv]], axis=1)
        qg = jnp.concatenate([q_ref[kv * SWA_GROUP + g] for g in range(SWA_GROUP)], axis=0)
        st = lax.dot_general(kk, qg, (((1,), (1,)), ((), ())), preferred_element_type=F32) + bias
        sink = sink_ref[kv:kv + 1, :]
        m = jnp.maximum(_reduce_rows(st, jnp.maximum), sink)
        p = jnp.exp(st - m)
        denom = _reduce_rows(p, jnp.add) + jnp.exp(sink - m)
        ot = jnp.dot(vvt, p.astype(BF16), preferred_element_type=F32) / denom
        outs.extend(ot[:, g * w:(g + 1) * w] for g in range(SWA_GROUP))
    o_ref[...] = jnp.concatenate(outs, axis=0).T.astype(o_ref.dtype)


def _swa(q, k, vt, sinks):
    b, nh, s, dh = q.shape
    nkv = k.shape[1]
    w = SWA_WINDOW
    cur = pl.BlockSpec((None, nkv, w, dh), lambda bi, i: (bi, 0, i, 0))
    prev = pl.BlockSpec((None, nkv, w, dh), lambda bi, i: (bi, 0, jnp.maximum(i - 1, 0), 0))
    cur_t = pl.BlockSpec((None, nkv, dh, w), lambda bi, i: (bi, 0, 0, i))
    prev_t = pl.BlockSpec((None, nkv, dh, w), lambda bi, i: (bi, 0, 0, jnp.maximum(i - 1, 0)))
    sink_tile = jnp.repeat(sinks.astype(F32).reshape(nkv, nh // nkv), w, axis=1)
    return pl.pallas_call(
        _swa_kernel,
        grid=(b, s // w),
        in_specs=[pl.BlockSpec((None, nh, w, dh), lambda bi, i: (bi, 0, i, 0)),
                  prev, cur, prev_t, cur_t,
                  pl.BlockSpec(sink_tile.shape, lambda bi, i: (0, 0))],
        out_specs=pl.BlockSpec((None, w, nh * dh), lambda bi, i: (bi, i, 0)),
        out_shape=jax.ShapeDtypeStruct((b, s, nh * dh), BF16),
        compiler_params=_params("parallel", "parallel"),
        name="swa",
    )(q, k, k, vt, vt, sink_tile)


PAIRS_PER_GROUP = GROUP_SIZE * (GROUP_SIZE - 1) // 2
N_BUCKETS = N_GROUPS * PAIRS_PER_GROUP
ROUTE_ROWS = SUBLANES


def _route(logits_t):
    rows = [logits_t[e:e + 1, :] for e in range(N_EXPERTS)]
    mx = functools.reduce(jnp.maximum, rows)
    ex = [jnp.exp(r - mx) for r in rows]
    total = functools.reduce(lambda p, q: p + q, ex)
    probs = [e / total for e in ex]

    group_score = []
    for g in range(N_GROUPS):
        a, b, c, d = probs[g * GROUP_SIZE:(g + 1) * GROUP_SIZE]
        hi1, lo1 = jnp.maximum(a, b), jnp.minimum(a, b)
        hi2, lo2 = jnp.maximum(c, d), jnp.minimum(c, d)
        top1 = jnp.maximum(hi1, hi2)
        top2 = jnp.maximum(jnp.minimum(hi1, hi2), jnp.maximum(lo1, lo2))
        group_score.append(top1 + top2)
    best = functools.reduce(jnp.maximum, group_score)
    g_sel = jnp.full(best.shape, N_GROUPS - 1, jnp.int32)
    for g in range(N_GROUPS - 2, -1, -1):
        g_sel = jnp.where(group_score[g] == best, g, g_sel)

    in_group = []
    for j in range(GROUP_SIZE):
        val = probs[(N_GROUPS - 1) * GROUP_SIZE + j]
        for g in range(N_GROUPS - 2, -1, -1):
            val = jnp.where(g_sel == g, probs[g * GROUP_SIZE + j], val)
        in_group.append(val)

    def first_argmax(vals):
        m = functools.reduce(jnp.maximum, vals)
        idx = jnp.full(m.shape, GROUP_SIZE - 1, jnp.int32)
        for j in range(GROUP_SIZE - 2, -1, -1):
            idx = jnp.where(vals[j] == m, j, idx)
        return m, idx

    w1, i1 = first_argmax(in_group)
    rest = [jnp.where(i1 == j, -1.0, in_group[j]) for j in range(GROUP_SIZE)]
    w2, i2 = first_argmax(rest)
    norm = w1 + w2
    first_is_lo = i1 < i2
    i_lo = jnp.minimum(i1, i2)
    i_hi = jnp.maximum(i1, i2)
    pair = jnp.where(i_lo == 0, i_hi - 1, jnp.where(i_lo == 1, i_hi + 1, PAIRS_PER_GROUP - 1))
    bucket = (g_sel * PAIRS_PER_GROUP + pair).astype(F32)
    w_lo = jnp.where(first_is_lo, w1, w2) / norm
    w_hi = jnp.where(first_is_lo, w2, w1) / norm
    return bucket, w_lo, w_hi


def _mix_out_kernel(*refs, n_head_major, alpha):
    x_ref = refs[0]
    y_refs = refs[1:1 + len(n_head_major)]
    w_refs = refs[1 + len(n_head_major):1 + 2 * len(n_head_major)]
    g_ref, b_ref, wrh_ref, wrl_ref, brt_ref, x1e_ref, route_ref = refs[1 + 2 * len(n_head_major):]
    tm, d = x_ref.shape
    h = alpha * x_ref[...]
    for y_ref, w_ref, nh in zip(y_refs, w_refs, n_head_major):
        if nh:
            y = jnp.concatenate([y_ref[j] for j in range(nh)], axis=-1)
        else:
            y = y_ref[...]
        h = h + jnp.dot(y, w_ref[...], preferred_element_type=F32)
    x1 = _layer_norm(h, g_ref[...], b_ref[...])
    x_hi = x1.astype(BF16)
    x_lo = (x1 - x_hi.astype(F32)).astype(BF16)

    def nt_dot(w, xv):
        return lax.dot_general(w, xv, (((1,), (1,)), ((), ())), preferred_element_type=F32)

    logits_t = (nt_dot(wrh_ref[...], x_hi) + nt_dot(wrh_ref[...], x_lo) + nt_dot(wrl_ref[...], x_hi)
                + brt_ref[...])
    route = jnp.concatenate(list(_route(logits_t)) + [jnp.zeros((ROUTE_ROWS - 3, tm), F32)], axis=0)
    route_ref[...] = route
    x1e_ref[:, 0:d] = x1
    x1e_ref[:, d:d + LANES] = jnp.concatenate([route, jnp.zeros((LANES - ROUTE_ROWS, tm), F32)], axis=0).T


def _mix_out(x, ys, ws, ln_g, ln_b, w_router, b_router, alpha, tm):
    b, s, d = x.shape
    n_head_major = tuple(y.shape[1] if hm else 0 for y, hm in ys)
    y_specs = []
    for (y, hm) in ys:
        if hm:
            y_specs.append(pl.BlockSpec((None, y.shape[1], tm, y.shape[3]), lambda bi, i: (bi, 0, i, 0)))
        else:
            y_specs.append(pl.BlockSpec((None, tm, y.shape[2]), lambda bi, i: (bi, i, 0)))
    w_specs = [pl.BlockSpec(w.shape, lambda bi, i: (0, 0)) for w in ws]
    row = pl.BlockSpec((1, d), lambda bi, i: (0, 0))
    tok = pl.BlockSpec((None, tm, d), lambda bi, i: (bi, i, 0))
    wr_t = w_router.T.astype(F32)
    wr_hi = wr_t.astype(BF16)
    wr_lo = (wr_t - wr_hi.astype(F32)).astype(BF16)
    wr_spec = pl.BlockSpec((N_EXPERTS, d), lambda bi, i: (0, 0))
    return pl.pallas_call(
        functools.partial(_mix_out_kernel, n_head_major=n_head_major, alpha=alpha),
        grid=(b, s // tm),
        in_specs=[tok] + y_specs + w_specs + [row, row, wr_spec, wr_spec,
                  pl.BlockSpec((N_EXPERTS, 1), lambda bi, i: (0, 0))],
        out_specs=[pl.BlockSpec((None, tm, d + LANES), lambda bi, i: (bi, i, 0)),
                   pl.BlockSpec((None, ROUTE_ROWS, tm), lambda bi, i: (bi, 0, i))],
        out_shape=[jax.ShapeDtypeStruct((b, s, d + LANES), F32),
                   jax.ShapeDtypeStruct((b, ROUTE_ROWS, s), F32)],
        compiler_params=_params("parallel", "parallel"),
        name="mix_out_ln_router",
    )(x, *[y for y, _ in ys], *ws, ln_g.reshape(1, d), ln_b.reshape(1, d),
      wr_hi, wr_lo, b_router.astype(F32).reshape(N_EXPERTS, 1))


MOE_TILE = 256
PLAN_COLS = 256
META_ROWS = SUBLANES


def _num_moe_tiles(t):
    return t // MOE_TILE + N_BUCKETS


def _plan_kernel(bid_ref, pos_ref, meta_ref):
    r, c = bid_ref.shape
    bid = bid_ref[...]
    before = (lax.broadcasted_iota(jnp.int32, (c, c), 0)
              < lax.broadcasted_iota(jnp.int32, (c, c), 1)).astype(BF16)
    rows_before = (lax.broadcasted_iota(jnp.int32, (r, r), 1)
                   < lax.broadcasted_iota(jnp.int32, (r, r), 0)).astype(BF16)
    lane = lax.broadcasted_iota(jnp.int32, (1, LANES), 1)
    tile_start = lane.astype(F32) * MOE_TILE

    def body(b, state):
        base, pos, tile_bucket, last_tile = state
        ind = (bid == lax.convert_element_type(b, F32)).astype(F32)
        within = jnp.dot(ind.astype(BF16), before, preferred_element_type=F32)
        row_total = jnp.sum(ind, axis=1, keepdims=True)
        row_off = jnp.dot(rows_before, jnp.broadcast_to(row_total, (r, LANES)).astype(BF16),
                          preferred_element_type=F32)[:, 0:1]
        count = jnp.sum(row_total, axis=0, keepdims=True)
        padded = jnp.floor((count + (MOE_TILE - 1)) * (1.0 / MOE_TILE)) * MOE_TILE
        pos = pos + ind * (base + row_off + within)
        end = base + padded
        tile_bucket = tile_bucket + (tile_start >= end).astype(F32)
        last_row = jnp.where(count > 0.0, end - MOE_TILE, -1.0)
        last_tile = jnp.where(lane == b, last_row, last_tile)
        return end, pos, tile_bucket, last_tile

    total, pos, tile_bucket, last_tile = lax.fori_loop(
        0, N_BUCKETS, body, (jnp.zeros((1, 1), F32), jnp.zeros((r, c), F32), jnp.zeros((1, LANES), F32),
                             jnp.full((1, LANES), -1.0, F32)))
    pos_ref[...] = pos.astype(jnp.int32)

    tb = jnp.minimum(tile_bucket, N_BUCKETS - 1.0)
    group = sum((tb >= g * PAIRS_PER_GROUP).astype(F32) for g in range(1, N_GROUPS))
    pair = tb - group * PAIRS_PER_GROUP
    i_lo = jnp.where(pair < 3, 0.0, jnp.where(pair < 5, 1.0, 2.0))
    i_hi = jnp.where(pair < 3, pair + 1.0, jnp.where(pair < 5, pair - 1.0, 3.0))
    meta = jnp.concatenate([group * GROUP_SIZE + i_lo, group * GROUP_SIZE + i_hi,
                            (tile_start < total).astype(F32),
                            jnp.where(lane == N_BUCKETS, total, last_tile),
                            jnp.zeros((META_ROWS - 4, LANES), F32)], axis=0)
    meta_ref[...] = meta.astype(jnp.int32)


def _plan(bucket_ids):
    r, c = bucket_ids.shape
    return pl.pallas_call(
        _plan_kernel,
        out_shape=[jax.ShapeDtypeStruct((r, c), jnp.int32), jax.ShapeDtypeStruct((META_ROWS, LANES), jnp.int32)],
        compiler_params=pltpu.CompilerParams(vmem_limit_bytes=VMEM_LIMIT_BYTES),
        name="moe_plan",
    )(bucket_ids)


def _row_copy(src_ref, src_row, dst_ref, dst_row, sem):
    return pltpu.make_async_copy(src_ref.at[pl.ds(src_row, 1)], dst_ref.at[pl.ds(dst_row, 1)], sem)


def _dispatch_kernel(pos_ref, last_ref, x_ref, o_ref, zeros, sem, zsem, *, first_spare):
    tm = x_ref.shape[0]
    i = pl.program_id(0)

    @pl.when(i == 0)
    def _():
        zeros[...] = jnp.zeros_like(zeros)
        rows_in_use = last_ref[N_BUCKETS]
        clears = [(last_ref[b] >= 0, last_ref[b]) for b in range(N_BUCKETS)]
        clears += [(k * MOE_TILE >= rows_in_use, k * MOE_TILE)
                   for k in range(first_spare, o_ref.shape[0] // MOE_TILE)]

        def clear(row):
            start = row if isinstance(row, int) else pl.multiple_of(row, MOE_TILE)
            return pltpu.make_async_copy(zeros, o_ref.at[pl.ds(start, MOE_TILE)], zsem)

        for needed, row in clears:
            @pl.when(needed)
            def _():
                clear(row).start()
        for needed, row in clears:
            @pl.when(needed)
            def _():
                clear(row).wait()

    base = i * tm
    for r in range(tm):
        _row_copy(x_ref, r, o_ref, pos_ref[base + r], sem).start()
    pltpu.make_async_copy(x_ref, o_ref.at[pl.ds(0, tm)], sem).wait()


def _dispatch(pos, last_tile_rows, xt, n_rows, tm):
    t = xt.shape[0]
    tile = xt.shape[1:]
    return pl.pallas_call(
        functools.partial(_dispatch_kernel, first_spare=t // MOE_TILE),
        grid_spec=pltpu.PrefetchScalarGridSpec(
            num_scalar_prefetch=2,
            grid=(t // tm,),
            in_specs=[pl.BlockSpec((tm,) + tile, lambda i, pos_ref, last_ref: (i,) + (0,) * len(tile))],
            out_specs=pl.BlockSpec(memory_space=pl.ANY),
            scratch_shapes=[pltpu.VMEM((MOE_TILE,) + tile, xt.dtype), pltpu.SemaphoreType.DMA(()),
                            pltpu.SemaphoreType.DMA(())]),
        out_shape=jax.ShapeDtypeStruct((n_rows,) + tile, xt.dtype),
        compiler_params=_params("arbitrary"),
        name="moe_dispatch",
    )(pos, last_tile_rows, xt)


def _moe_tile_kernel(elo_ref, ehi_ref, valid_ref, x_ref, wgl_ref, wul_ref, wdl_ref, wgh_ref, wuh_ref,
                     wdh_ref, o_ref, wgl_b, wul_b, wdl_b, wgh_b, wuh_b, wdh_b):
    d = wgl_ref.shape[0]
    k = pl.program_id(0)
    prev = jnp.maximum(k - 1, 0)
    valid = valid_ref[k] != 0

    def refresh(e_ref, srcs, dsts):
        @pl.when(valid & ((k == 0) | (e_ref[k] != e_ref[prev])))
        def _():
            for src, dst in zip(srcs, dsts):
                dst[...] = src[...].astype(BF16)

    refresh(elo_ref, (wgl_ref, wul_ref, wdl_ref), (wgl_b, wul_b, wdl_b))
    refresh(ehi_ref, (wgh_ref, wuh_ref, wdh_ref), (wgh_b, wuh_b, wdh_b))

    @pl.when(jnp.logical_not(valid))
    def _():
        o_ref[...] = jnp.zeros_like(o_ref)

    @pl.when(valid)
    def _():
        x = x_ref[:, 0:d].astype(BF16)
        acc = None
        for wg_b, wu_b, wd_b, lane in ((wgl_b, wul_b, wdl_b, d + 1), (wgh_b, wuh_b, wdh_b, d + 2)):
            weight = x_ref[:, lane:lane + 1]
            hg = jnp.dot(x, wg_b[...], preferred_element_type=F32)
            hu = jnp.dot(x, wu_b[...], preferred_element_type=F32)
            hidden = (hg * _sigmoid(hg)) * hu * weight
            y = jnp.dot(hidden.astype(BF16), wd_b[...], preferred_element_type=F32)
            acc = y if acc is None else acc + y
        for j in range(d // LANES):
            o_ref[:, j, :] = acc[:, j * LANES:(j + 1) * LANES]


def _moe_tiles(e_lo, e_hi, valid, xs, wg, wu, wd, layer):
    n_rows, dw = xs.shape
    _, _, d, f = wg.shape
    assert d % LANES == 0
    up_lo = pl.BlockSpec((None, None, d, f), lambda k, lo, hi, ok: (layer, lo[k], 0, 0))
    up_hi = pl.BlockSpec((None, None, d, f), lambda k, lo, hi, ok: (layer, hi[k], 0, 0))
    down_lo = pl.BlockSpec((None, None, f, d), lambda k, lo, hi, ok: (layer, lo[k], 0, 0))
    down_hi = pl.BlockSpec((None, None, f, d), lambda k, lo, hi, ok: (layer, hi[k], 0, 0))
    up_b = pltpu.VMEM((d, f), BF16)
    down_b = pltpu.VMEM((f, d), BF16)
    return pl.pallas_call(
        _moe_tile_kernel,
        grid_spec=pltpu.PrefetchScalarGridSpec(
            num_scalar_prefetch=3,
            grid=(n_rows // MOE_TILE,),
            in_specs=[pl.BlockSpec((MOE_TILE, dw), lambda k, lo, hi, ok: (k, 0)),
                      up_lo, up_lo, down_lo, up_hi, up_hi, down_hi],
            out_specs=pl.BlockSpec((MOE_TILE, d // LANES, LANES), lambda k, lo, hi, ok: (k, 0, 0)),
            scratch_shapes=[up_b, up_b, down_b, up_b, up_b, down_b]),
        out_shape=jax.ShapeDtypeStruct((n_rows, d // LANES, LANES), F32),
        compiler_params=_params("arbitrary"),
        name="moe_tiles",
    )(e_lo, e_hi, valid, xs, wg, wu, wd, wg, wu, wd)


def _ln_ple_kernel(pos_ref, x1_ref, ys_ref, p_ref, g_ref, b_ref, wg_ref, bg_ref, wp_ref, o_ref,
                   m_even, m_odd, sem_even, sem_odd, *, alpha):
    tm = x1_ref.shape[0]
    i = pl.program_id(0)
    last = pl.num_programs(0) - 1

    def wait(buf, sem):
        pltpu.make_async_copy(ys_ref.at[pl.ds(0, tm)], buf, sem).wait()

    @pl.when(i == 0)
    def _():
        def issue(r, carry):
            _row_copy(ys_ref, pos_ref[r], m_even, r, sem_even).start()
            return carry

        lax.fori_loop(0, tm, issue, 0, unroll=8)

    def step(cur, cur_sem, nxt, nxt_sem):
        wait(cur, cur_sem)
        base = jnp.minimum(i + 1, last) * tm
        for r in range(tm):
            _row_copy(ys_ref, pos_ref[base + r], nxt, r, nxt_sem).start()
        emb = jnp.dot(p_ref[...].astype(BF16), wp_ref[...], preferred_element_type=F32)
        m = jnp.concatenate([cur[:, j, :] for j in range(cur.shape[1])], axis=1)
        x2 = _layer_norm(alpha * x1_ref[...] + m, g_ref[...], b_ref[...])
        gate = _sigmoid(jnp.dot(x2.astype(BF16), wg_ref[...], preferred_element_type=F32) + bg_ref[...])
        o_ref[...] = x2 + gate * emb

        @pl.when(i == last)
        def _():
            wait(nxt, nxt_sem)

    @pl.when(i % 2 == 0)
    def _():
        step(m_even, sem_even, m_odd, sem_odd)

    @pl.when(i % 2 == 1)
    def _():
        step(m_odd, sem_odd, m_even, sem_even)


def _ln_ple(pos, x1e, ys, p, ln_g, ln_b, wg, bg, wp, alpha, tm):
    t = x1e.shape[0]
    d = ys.shape[1] * ys.shape[2]
    pd = p.shape[1]
    tok = pl.BlockSpec((tm, d), lambda i, pos_ref: (i, 0))
    row = pl.BlockSpec((1, d), lambda i, pos_ref: (0, 0))
    return pl.pallas_call(
        functools.partial(_ln_ple_kernel, alpha=alpha),
        grid_spec=pltpu.PrefetchScalarGridSpec(
            num_scalar_prefetch=1,
            grid=(t // tm,),
            in_specs=[tok, pl.BlockSpec(memory_space=pl.ANY),
                      pl.BlockSpec((tm, pd), lambda i, pos_ref: (i, 0)), row, row,
                      pl.BlockSpec((d, d), lambda i, pos_ref: (0, 0)), row,
                      pl.BlockSpec((pd, d), lambda i, pos_ref: (0, 0))],
            out_specs=tok,
            scratch_shapes=[pltpu.VMEM((tm,) + ys.shape[1:], F32), pltpu.VMEM((tm,) + ys.shape[1:], F32),
                            pltpu.SemaphoreType.DMA(()), pltpu.SemaphoreType.DMA(())]),
        out_shape=jax.ShapeDtypeStruct((t, d), F32),
        compiler_params=_params("arbitrary"),
        name="ln_ple",
    )(pos, x1e, ys, p, ln_g.reshape(1, d), ln_b.reshape(1, d), wg, bg.reshape(1, d), wp)


def _pick_tile(n, target):
    t = min(n, target)
    while n % t:
        t //= 2
    return t


def kernel(x, p, positions, w_in_ab, w_out_ab, conv_w, conv_b, lru_w_r, lru_b_r, lru_w_i, lru_b_i, lru_lambda, w_qkv_c, w_out_c, sinks_c, ln_mix_g, ln_mix_b, ln_ffn_g, ln_ffn_b, w_router, b_router, exp_w_gate, exp_w_up, exp_w_down, ple_w_proj, ple_w_gate, ple_b_gate):
    b, s, d = x.shape
    depth = p.shape[0]
    t = b * s
    alpha = (2 * depth) ** 0.25
    tm = _pick_tile(s, 512)
    assert t % MOE_TILE == 0 and t % PLAN_COLS == 0
    n_tiles = _num_moe_tiles(t)
    assert n_tiles <= LANES
    for i in range(depth):
        j = i // 2
        if i % 2 == 0:
            q, k, v, xr, gr = _proj_ab(x, w_in_ab[j].astype(BF16), tm)
            y_sb = _sb_attention(q, k, v, _pick_tile(s, 256), 4)
            y_lru = _lru(xr, gr, conv_w[j], conv_b[j], _block_diag(lru_w_r[j]).astype(BF16), lru_b_r[j],
                         _block_diag(lru_w_i[j]).astype(BF16), lru_b_i[j], lru_lambda[j],
                         _pick_tile(s, 256))
            w_out = w_out_ab[j].astype(BF16)
            ys = [(y_sb, True), (y_lru, False)]
            ws = [w_out[:SB_WIDTH], w_out[SB_WIDTH:]]
        else:
            q, k, v = _proj_rope(x, positions, w_qkv_c[j].astype(BF16), tm)
            y = _swa(q, k, v, sinks_c[j])
            ys = [(y, False)]
            ws = [w_out_c[j].astype(BF16)]
        x1e, route = _mix_out(x, ys, ws, ln_mix_g[i], ln_mix_b[i], w_router, b_router, alpha, tm)
        x1e = x1e.reshape(t, d + LANES)
        pos, meta = _plan(route[:, 0, :].reshape(t // PLAN_COLS, PLAN_COLS))
        pos = pos.reshape(t)
        xs = _dispatch(pos, meta[3, :N_BUCKETS + 1], x1e, n_tiles * MOE_TILE, _pick_tile(t, 512))
        ys_moe = _moe_tiles(meta[0, :n_tiles], meta[1, :n_tiles], meta[2, :n_tiles], xs,
                            exp_w_gate, exp_w_up, exp_w_down, i)
        x = _ln_ple(pos, x1e, ys_moe, p[i].reshape(t, -1), ln_ffn_g[i], ln_ffn_b[i],
                    ple_w_gate[i].astype(BF16), ple_b_gate[i], ple_w_proj[i].astype(BF16), alpha,
                    _pick_tile(t, 512)).reshape(b, s, d)
    return x
```

```python
import functools
import math

import jax
import jax.numpy as jnp
from jax import lax
from jax.experimental import pallas as pl
from jax.experimental.pallas import tpu as pltpu

HEAD_DIM = 64
SB_HEADS = 8
SB_WIDTH = SB_HEADS * HEAD_DIM
LRU_WIDTH = 512
LRU_BLOCKS = 8
LRU_C = 8.0
CONV_WIDTH = 4
SWA_HEADS = 16
SWA_KV_HEADS = 4
SWA_GROUP = SWA_HEADS // SWA_KV_HEADS
SWA_WINDOW = 128
ROPE_THETA = 10000.0
N_EXPERTS = 16
N_GROUPS = 4
GROUP_SIZE = N_EXPERTS // N_GROUPS
LN_EPS = 1e-5
Q_SCALE = HEAD_DIM ** -0.5

LANES = 128
SUBLANES = 8
VMEM_LIMIT_BYTES = 48 * 1024 * 1024

NEG_BIG = -1e30

BF16 = jnp.bfloat16
F32 = jnp.float32


def _params(*semantics):
    return pltpu.CompilerParams(dimension_semantics=semantics, vmem_limit_bytes=VMEM_LIMIT_BYTES)


def _softplus(z):
    return jnp.maximum(z, 0.0) + jnp.log(1.0 + jnp.exp(-jnp.abs(z)))


def _sigmoid(z):
    return 1.0 / (1.0 + jnp.exp(-z))


def _layer_norm(y, g, b):
    mu = jnp.mean(y, axis=-1, keepdims=True)
    d = y - mu
    var = jnp.mean(d * d, axis=-1, keepdims=True)
    return d * lax.rsqrt(var + LN_EPS) * g + b


def _proj_ab_kernel(x_ref, w_ref, q_ref, k_ref, v_ref, xr_ref, gr_ref):
    xb = x_ref[...].astype(BF16)

    def chunk(c):
        return jnp.dot(xb, w_ref[:, c * SB_WIDTH:(c + 1) * SB_WIDTH], preferred_element_type=F32)

    for c, (ref, scale) in enumerate(((q_ref, Q_SCALE), (k_ref, None), (v_ref, None))):
        r = chunk(c)
        if scale is not None:
            r = r * scale
        for h in range(SB_HEADS):
            ref[h] = r[:, h * HEAD_DIM:(h + 1) * HEAD_DIM].astype(BF16)
    xr_ref[...] = chunk(3)
    gr_ref[...] = chunk(4)


def _proj_ab(x, w_bf16, tm):
    b, s, d = x.shape
    n = w_bf16.shape[1]
    heads = jax.ShapeDtypeStruct((b, SB_HEADS, s, HEAD_DIM), BF16)
    flat = jax.ShapeDtypeStruct((b, s, LRU_WIDTH), F32)
    head_spec = pl.BlockSpec((None, SB_HEADS, tm, HEAD_DIM), lambda bi, i: (bi, 0, i, 0))
    flat_spec = pl.BlockSpec((None, tm, LRU_WIDTH), lambda bi, i: (bi, i, 0))
    return pl.pallas_call(
        _proj_ab_kernel,
        grid=(b, s // tm),
        in_specs=[pl.BlockSpec((None, tm, d), lambda bi, i: (bi, i, 0)),
                  pl.BlockSpec((d, n), lambda bi, i: (0, 0))],
        out_specs=[head_spec, head_spec, head_spec, flat_spec, flat_spec],
        out_shape=[heads, heads, heads, flat, flat],
        compiler_params=_params("parallel", "parallel"),
        name="proj_ab",
    )(x, w_bf16)


SB_DEAD_LOG_WEIGHT = -105.0


def _sb_attn_kernel(q_ref, k_ref, v_ref, o_ref, *, tq, hp):
    i = pl.program_id(2)
    row = lax.broadcasted_iota(jnp.int32, (tq, tq), 0)
    col = lax.broadcasted_iota(jnp.int32, (tq, tq), 1)
    later = (row > col).astype(BF16)
    causal = col < row

    def block(jb, carries, accs, masked):
        start = pl.multiple_of(jb * tq, tq)
        new_carries, new_accs = [], []
        for h in range(hp):
            kj = k_ref[h, pl.ds(start, tq), :]
            vj = v_ref[h, pl.ds(start, tq), :]
            z = lax.dot_general(q_ref[h], kj, (((1,), (1,)), ((), ())), preferred_element_type=F32)
            sp = _softplus(z)
            log_keep = -sp
            if masked:
                log_keep = jnp.where(causal, log_keep, 0.0)
            after = jnp.dot(log_keep.astype(BF16), later, preferred_element_type=F32)
            w = jnp.exp((z - sp) + after + carries[h])
            if masked:
                w = jnp.where(causal, w, 0.0)
            new_accs.append(accs[h] + jnp.dot(w.astype(BF16), vj, preferred_element_type=F32))
            new_carries.append(carries[h] + jnp.sum(log_keep, axis=1, keepdims=True))
        return tuple(new_carries), tuple(new_accs)

    def live(carries):
        return functools.reduce(jnp.maximum, [jnp.max(c) for c in carries])

    zero = ((jnp.zeros((tq, 1), F32),) * hp, (jnp.zeros((tq, HEAD_DIM), F32),) * hp)

    def first_two():
        return block(i - 1, *block(i, *zero, True), False)

    carries, accs = lax.cond(i > 0, first_two, lambda: block(i, *zero, True))
    done = jnp.minimum(i, 1)

    def cond(state):
        return (state[0] < i) & (state[1] > SB_DEAD_LOG_WEIGHT)

    def body(state):
        step, _, carries, accs = state
        carries, accs = block(i - 1 - step, carries, accs, False)
        return step + 1, live(carries), carries, accs

    _, _, _, accs = lax.while_loop(cond, body, (done, live(carries), carries, accs))
    for h in range(hp):
        o_ref[h] = accs[h].astype(o_ref.dtype)


def _sb_attention(q, k, v, tq, hp):
    b, h, s, dh = q.shape
    return pl.pallas_call(
        functools.partial(_sb_attn_kernel, tq=tq, hp=hp),
        grid=(b, h // hp, s // tq),
        in_specs=[pl.BlockSpec((None, hp, tq, dh), lambda bi, hi, i: (bi, hi, i, 0)),
                  pl.BlockSpec((None, hp, s, dh), lambda bi, hi, i: (bi, hi, 0, 0)),
                  pl.BlockSpec((None, hp, s, dh), lambda bi, hi, i: (bi, hi, 0, 0))],
        out_specs=pl.BlockSpec((None, hp, tq, dh), lambda bi, hi, i: (bi, hi, i, 0)),
        out_shape=jax.ShapeDtypeStruct((b, h, s, dh), BF16),
        compiler_params=_params("parallel", "parallel", "parallel"),
        name="sb_attention",
    )(q, k, v)


def _gelu_tanh(x):
    return 0.5 * x * (1.0 + jnp.tanh(math.sqrt(2.0 / math.pi) * (x + 0.044715 * (x * x * x))))


def _lru_kernel(xr_ref, gr_ref, cw_ref, cb_ref, wr_ref, br_ref, wi_ref, bi_ref, lam_ref, y_ref,
                xbuf, hprev, *, ts):
    @pl.when(pl.program_id(1) == 0)
    def _():
        xbuf[0:SUBLANES, :] = jnp.zeros((SUBLANES, LRU_WIDTH), F32)
        hprev[...] = jnp.zeros_like(hprev)

    xbuf[SUBLANES:SUBLANES + ts, :] = xr_ref[...]
    xc = cb_ref[...] + cw_ref[CONV_WIDTH - 1:CONV_WIDTH, :] * xbuf[SUBLANES:SUBLANES + ts, :]
    for kk in range(CONV_WIDTH - 1):
        off = SUBLANES - (CONV_WIDTH - 1) + kk
        xc = xc + cw_ref[kk:kk + 1, :] * xbuf[off:off + ts, :]
    xbuf[0:SUBLANES, :] = xbuf[ts:ts + SUBLANES, :]

    xcb = xc.astype(BF16)
    r = _sigmoid(jnp.dot(xcb, wr_ref[...], preferred_element_type=F32) + br_ref[...])
    gi = _sigmoid(jnp.dot(xcb, wi_ref[...], preferred_element_type=F32) + bi_ref[...])
    log_a = (-LRU_C) * r * _softplus(-lam_ref[...])
    a = jnp.exp(log_a)
    u = jnp.sqrt(1.0 - a * a) * (gi * xc)

    row = lax.broadcasted_iota(jnp.int32, (ts, LRU_WIDTH), 0)
    d = 1
    while d < ts:
        if d < SUBLANES:
            keep = row >= d
            a_sh = jnp.where(keep, pltpu.roll(a, d, axis=0), 1.0)
            u_sh = jnp.where(keep, pltpu.roll(u, d, axis=0), 0.0)
            u = a * u_sh + u
            a = a * a_sh
        else:
            u = jnp.concatenate([u[:d], a[d:] * u[:ts - d] + u[d:]], axis=0)
            a = jnp.concatenate([a[:d], a[d:] * a[:ts - d]], axis=0)
        d *= 2
    h = a * hprev[0:1, :] + u
    hprev[...] = jnp.broadcast_to(h[ts - 1:ts, :], hprev.shape)
    y_ref[...] = (_gelu_tanh(gr_ref[...]) * h).astype(y_ref.dtype)


def _lru(xr, gr, conv_w, conv_b, wr_bd, b_r, wi_bd, b_i, lam, ts):
    b, s, w = xr.shape
    seq_spec = pl.BlockSpec((None, ts, w), lambda bi, i: (bi, i, 0))

    def full(shape):
        return pl.BlockSpec(shape, lambda bi, i: (0,) * len(shape))

    return pl.pallas_call(
        functools.partial(_lru_kernel, ts=ts),
        grid=(b, s // ts),
        in_specs=[seq_spec, seq_spec, full((CONV_WIDTH, w)), full((1, w)), full((w, w)), full((1, w)),
                  full((w, w)), full((1, w)), full((1, w))],
        out_specs=seq_spec,
        out_shape=jax.ShapeDtypeStruct((b, s, w), BF16),
        scratch_shapes=[pltpu.VMEM((ts + 2 * SUBLANES, w), F32), pltpu.VMEM((SUBLANES, w), F32)],
        compiler_params=_params("parallel", "arbitrary"),
        name="rg_lru",
    )(xr, gr, conv_w, conv_b.reshape(1, w), wr_bd, b_r.reshape(1, w), wi_bd, b_i.reshape(1, w),
      lam.reshape(1, w))


def _block_diag(w):
    n, c, d = w.shape
    eye = jnp.eye(n, dtype=w.dtype)
    return (eye[:, None, :, None] * w[:, :, None, :]).reshape(n * c, n * d)


def _proj_rope_kernel(x_ref, pos_ref, freq_ref, w_ref, q_ref, k_ref, vt_ref):
    tm = x_ref.shape[0]
    xb = x_ref[...].astype(BF16)
    ang_t = freq_ref[...] * pos_ref[...].astype(F32)
    reps = LANES // (HEAD_DIM // 2)
    cos = jnp.concatenate([jnp.cos(ang_t)] * reps, axis=0).T
    sin = jnp.concatenate([jnp.sin(ang_t)] * reps, axis=0).T
    lane = lax.broadcasted_iota(jnp.int32, (tm, LANES), 1)
    first_half = (lane % HEAD_DIM) < (HEAD_DIM // 2)
    heads_per_slab = LANES // HEAD_DIM

    def rope(r):
        upper = pltpu.roll(r, LANES - HEAD_DIM // 2, axis=1)
        lower = pltpu.roll(r, HEAD_DIM // 2, axis=1)
        return r * cos + jnp.where(first_half, -upper, lower) * sin

    def emit(ref, n_heads, col0, rotary, scale):
        for slab in range(n_heads // heads_per_slab):
            c0 = col0 + slab * LANES
            r = jnp.dot(xb, w_ref[:, c0:c0 + LANES], preferred_element_type=F32)
            if rotary:
                r = rope(r)
            if scale is not None:
                r = r * scale
            for j in range(heads_per_slab):
                ref[slab * heads_per_slab + j] = r[:, j * HEAD_DIM:(j + 1) * HEAD_DIM].astype(BF16)

    emit(q_ref, SWA_HEADS, 0, True, Q_SCALE)
    emit(k_ref, SWA_KV_HEADS, SWA_HEADS * HEAD_DIM, True, None)
    v0 = (SWA_HEADS + SWA_KV_HEADS) * HEAD_DIM
    for slab in range(SWA_KV_HEADS // heads_per_slab):
        r = jnp.dot(xb, w_ref[:, v0 + slab * LANES:v0 + (slab + 1) * LANES], preferred_element_type=F32)
        rt = r.T
        for j in range(heads_per_slab):
            vt_ref[slab * heads_per_slab + j] = rt[j * HEAD_DIM:(j + 1) * HEAD_DIM, :].astype(BF16)


def _proj_rope(x, positions, w_bf16, tm):
    b, s, d = x.shape
    n = w_bf16.shape[1]
    half = HEAD_DIM // 2
    inv_freq = (ROPE_THETA ** (-jnp.arange(half, dtype=F32) / half)).reshape(half, 1)

    def heads(nh):
        return (jax.ShapeDtypeStruct((b, nh, s, HEAD_DIM), BF16),
                pl.BlockSpec((None, nh, tm, HEAD_DIM), lambda bi, i: (bi, 0, i, 0)))

    (qs, qspec), (ks, kspec) = heads(SWA_HEADS), heads(SWA_KV_HEADS)
    vs = jax.ShapeDtypeStruct((b, SWA_KV_HEADS, HEAD_DIM, s), BF16)
    vspec = pl.BlockSpec((None, SWA_KV_HEADS, HEAD_DIM, tm), lambda bi, i: (bi, 0, 0, i))
    return pl.pallas_call(
        _proj_rope_kernel,
        grid=(b, s // tm),
        in_specs=[pl.BlockSpec((None, tm, d), lambda bi, i: (bi, i, 0)),
                  pl.BlockSpec((None, 1, tm), lambda bi, i: (bi, 0, i)),
                  pl.BlockSpec((half, 1), lambda bi, i: (0, 0)),
                  pl.BlockSpec((d, n), lambda bi, i: (0, 0))],
        out_specs=[qspec, kspec, vspec],
        out_shape=[qs, ks, vs],
        compiler_params=_params("parallel", "parallel"),
        name="proj_rope",
    )(x, positions.reshape(b, 1, s), inv_freq, w_bf16)


def _reduce_rows(x, op):
    while x.shape[0] > SUBLANES:
        half = x.shape[0] // 2
        x = op(x[:half], x[half:])
    for shift in (4, 2, 1):
        x = op(x, pltpu.roll(x, shift, axis=0))
    return x[0:1]


SWA_BLOCKS_PER_STEP = 4


def _swa_kernel(q_ref, kp_ref, kc_ref, vtp_ref, vtc_ref, sink_ref, o_ref):
    i = pl.program_id(1)
    w = SWA_WINDOW
    key = lax.broadcasted_iota(jnp.int32, (2 * w, w), 0)
    qry = lax.broadcasted_iota(jnp.int32, (2 * w, w), 1)
    dist = qry + w - key
    band = (dist >= 0) & (dist < w)
    for blk in range(SWA_BLOCKS_PER_STEP):
        visible = band if blk else band & ((key >= w) | (i > 0))
        bias = jnp.concatenate([jnp.where(visible, 0.0, NEG_BIG)] * SWA_GROUP, axis=1)
        outs = []
        for kv in range(SWA_KV_HEADS):
            if blk:
                kk = kc_ref[kv, (blk - 1) * w:(blk + 1) * w, :]
                vvt = vtc_ref[kv, :, (blk - 1) * w:(blk + 1) * w]
            else:
                kk = jnp.concatenate([kp_ref[kv], kc_ref[kv, 0:w, :]], axis=0)
                vvt = jnp.concatenate([vtp_ref[kv], vtc_ref[kv, :, 0:w]], axis=1)
            qg = jnp.concatenate([q_ref[kv * SWA_GROUP + g, blk * w:(blk + 1) * w, :]
                                  for g in range(SWA_GROUP)], axis=0)
            st = lax.dot_general(kk, qg, (((1,), (1,)), ((), ())), preferred_element_type=F32) + bias
            sink = sink_ref[kv:kv + 1, :]
            m = jnp.maximum(_reduce_rows(st, jnp.maximum), sink)
            p = jnp.exp(st - m)
            denom = _reduce_rows(p, jnp.add) + jnp.exp(sink - m)
            ot = jnp.dot(vvt, p.astype(BF16), preferred_element_type=F32) / denom
            outs.extend(ot[:, g * w:(g + 1) * w] for g in range(SWA_GROUP))
        o_ref[blk * w:(blk + 1) * w, :] = jnp.concatenate(outs, axis=0).T.astype(o_ref.dtype)


def _swa(q, k, vt, sinks):
    b, nh, s, dh = q.shape
    nkv = k.shape[1]
    w = SWA_WINDOW
    n = SWA_BLOCKS_PER_STEP
    assert s % (n * w) == 0
    cur = pl.BlockSpec((None, nkv, n * w, dh), lambda bi, i: (bi, 0, i, 0))
    prev = pl.BlockSpec((None, nkv, w, dh), lambda bi, i: (bi, 0, jnp.maximum(n * i - 1, 0), 0))
    cur_t = pl.BlockSpec((None, nkv, dh, n * w), lambda bi, i: (bi, 0, 0, i))
    prev_t = pl.BlockSpec((None, nkv, dh, w), lambda bi, i: (bi, 0, 0, jnp.maximum(n * i - 1, 0)))
    sink_tile = jnp.repeat(sinks.astype(F32).reshape(nkv, nh // nkv), w, axis=1)
    return pl.pallas_call(
        _swa_kernel,
        grid=(b, s // (n * w)),
        in_specs=[pl.BlockSpec((None, nh, n * w, dh), lambda bi, i: (bi, 0, i, 0)),
                  prev, cur, prev_t, cur_t,
                  pl.BlockSpec(sink_tile.shape, lambda bi, i: (0, 0))],
        out_specs=pl.BlockSpec((None, n * w, nh * dh), lambda bi, i: (bi, i, 0)),
        out_shape=jax.ShapeDtypeStruct((b, s, nh * dh), BF16),
        compiler_params=_params("parallel", "parallel"),
        name="swa",
    )(q, k, k, vt, vt, sink_tile)


PAIRS_PER_GROUP = GROUP_SIZE * (GROUP_SIZE - 1) // 2
N_BUCKETS = N_GROUPS * PAIRS_PER_GROUP
ROUTE_ROWS = SUBLANES


def _route(logits_t):
    rows = [logits_t[e:e + 1, :] for e in range(N_EXPERTS)]
    mx = functools.reduce(jnp.maximum, rows)
    ex = [jnp.exp(r - mx) for r in rows]
    total = functools.reduce(lambda p, q: p + q, ex)
    probs = [e / total for e in ex]

    group_score = []
    for g in range(N_GROUPS):
        a, b, c, d = probs[g * GROUP_SIZE:(g + 1) * GROUP_SIZE]
        hi1, lo1 = jnp.maximum(a, b), jnp.minimum(a, b)
        hi2, lo2 = jnp.maximum(c, d), jnp.minimum(c, d)
        top1 = jnp.maximum(hi1, hi2)
        top2 = jnp.maximum(jnp.minimum(hi1, hi2), jnp.maximum(lo1, lo2))
        group_score.append(top1 + top2)
    best = functools.reduce(jnp.maximum, group_score)
    g_sel = jnp.full(best.shape, N_GROUPS - 1, jnp.int32)
    for g in range(N_GROUPS - 2, -1, -1):
        g_sel = jnp.where(group_score[g] == best, g, g_sel)

    in_group = []
    for j in range(GROUP_SIZE):
        val = probs[(N_GROUPS - 1) * GROUP_SIZE + j]
        for g in range(N_GROUPS - 2, -1, -1):
            val = jnp.where(g_sel == g, probs[g * GROUP_SIZE + j], val)
        in_group.append(val)

    def first_argmax(vals):
        m = functools.reduce(jnp.maximum, vals)
        idx = jnp.full(m.shape, GROUP_SIZE - 1, jnp.int32)
        for j in range(GROUP_SIZE - 2, -1, -1):
            idx = jnp.where(vals[j] == m, j, idx)
        return m, idx

    w1, i1 = first_argmax(in_group)
    rest = [jnp.where(i1 == j, -1.0, in_group[j]) for j in range(GROUP_SIZE)]
    w2, i2 = first_argmax(rest)
    norm = w1 + w2
    first_is_lo = i1 < i2
    i_lo = jnp.minimum(i1, i2)
    i_hi = jnp.maximum(i1, i2)
    pair = jnp.where(i_lo == 0, i_hi - 1, jnp.where(i_lo == 1, i_hi + 1, PAIRS_PER_GROUP - 1))
    bucket = (g_sel * PAIRS_PER_GROUP + pair).astype(F32)
    w_lo = jnp.where(first_is_lo, w1, w2) / norm
    w_hi = jnp.where(first_is_lo, w2, w1) / norm
    return bucket, w_lo, w_hi


def _mix_out_kernel(*refs, n_head_major, alpha):
    x_ref = refs[0]
    y_refs = refs[1:1 + len(n_head_major)]
    w_refs = refs[1 + len(n_head_major):1 + 2 * len(n_head_major)]
    g_ref, b_ref, wrh_ref, wrl_ref, brt_ref, x1e_ref, route_ref = refs[1 + 2 * len(n_head_major):]
    tm, d = x_ref.shape
    h = alpha * x_ref[...]
    for y_ref, w_ref, nh in zip(y_refs, w_refs, n_head_major):
        if nh:
            y = jnp.concatenate([y_ref[j] for j in range(nh)], axis=-1)
        else:
            y = y_ref[...]
        h = h + jnp.dot(y, w_ref[...], preferred_element_type=F32)
    x1 = _layer_norm(h, g_ref[...], b_ref[...])
    x_hi = x1.astype(BF16)
    x_lo = (x1 - x_hi.astype(F32)).astype(BF16)

    def nt_dot(w, xv):
        return lax.dot_general(w, xv, (((1,), (1,)), ((), ())), preferred_element_type=F32)

    logits_t = (nt_dot(wrh_ref[...], x_hi) + nt_dot(wrh_ref[...], x_lo) + nt_dot(wrl_ref[...], x_hi)
                + brt_ref[...])
    route = jnp.concatenate(list(_route(logits_t)) + [jnp.zeros((ROUTE_ROWS - 3, tm), F32)], axis=0)
    route_ref[...] = route
    x1e_ref[:, 0:d] = x1
    x1e_ref[:, d:d + LANES] = jnp.concatenate([route, jnp.zeros((LANES - ROUTE_ROWS, tm), F32)], axis=0).T


def _mix_out(x, ys, ws, ln_g, ln_b, w_router, b_router, alpha, tm):
    b, s, d = x.shape
    n_head_major = tuple(y.shape[1] if hm else 0 for y, hm in ys)
    y_specs = []
    for (y, hm) in ys:
        if hm:
            y_specs.append(pl.BlockSpec((None, y.shape[1], tm, y.shape[3]), lambda bi, i: (bi, 0, i, 0)))
        else:
            y_specs.append(pl.BlockSpec((None, tm, y.shape[2]), lambda bi, i: (bi, i, 0)))
    w_specs = [pl.BlockSpec(w.shape, lambda bi, i: (0, 0)) for w in ws]
    row = pl.BlockSpec((1, d), lambda bi, i: (0, 0))
    tok = pl.BlockSpec((None, tm, d), lambda bi, i: (bi, i, 0))
    wr_t = w_router.T.astype(F32)
    wr_hi = wr_t.astype(BF16)
    wr_lo = (wr_t - wr_hi.astype(F32)).astype(BF16)
    wr_spec = pl.BlockSpec((N_EXPERTS, d), lambda bi, i: (0, 0))
    return pl.pallas_call(
        functools.partial(_mix_out_kernel, n_head_major=n_head_major, alpha=alpha),
        grid=(b, s // tm),
        in_specs=[tok] + y_specs + w_specs + [row, row, wr_spec, wr_spec,
                  pl.BlockSpec((N_EXPERTS, 1), lambda bi, i: (0, 0))],
        out_specs=[pl.BlockSpec((None, tm, d + LANES), lambda bi, i: (bi, i, 0)),
                   pl.BlockSpec((None, ROUTE_ROWS, tm), lambda bi, i: (bi, 0, i))],
        out_shape=[jax.ShapeDtypeStruct((b, s, d + LANES), F32),
                   jax.ShapeDtypeStruct((b, ROUTE_ROWS, s), F32)],
        compiler_params=_params("parallel", "parallel"),
        name="mix_out_ln_router",
    )(x, *[y for y, _ in ys], *ws, ln_g.reshape(1, d), ln_b.reshape(1, d),
      wr_hi, wr_lo, b_router.astype(F32).reshape(N_EXPERTS, 1))


MOE_TILE = 256
PLAN_COLS = 256
META_ROWS = SUBLANES


def _num_moe_tiles(t):
    return t // MOE_TILE + N_BUCKETS


def _plan_kernel(bid_ref, pos_ref, meta_ref):
    r, c = bid_ref.shape
    bid = bid_ref[...]
    before = (lax.broadcasted_iota(jnp.int32, (c, c), 0)
              < lax.broadcasted_iota(jnp.int32, (c, c), 1)).astype(BF16)
    rows_before = (lax.broadcasted_iota(jnp.int32, (r, r), 1)
                   < lax.broadcasted_iota(jnp.int32, (r, r), 0)).astype(BF16)
    lane = lax.broadcasted_iota(jnp.int32, (1, LANES), 1)
    tile_start = lane.astype(F32) * MOE_TILE

    def body(b, state):
        base, pos, tile_bucket, last_tile = state
        ind = (bid == lax.convert_element_type(b, F32)).astype(F32)
        within = jnp.dot(ind.astype(BF16), before, preferred_element_type=F32)
        row_total = jnp.sum(ind, axis=1, keepdims=True)
        row_off = jnp.dot(rows_before, jnp.broadcast_to(row_total, (r, LANES)).astype(BF16),
                          preferred_element_type=F32)[:, 0:1]
        count = jnp.sum(row_total, axis=0, keepdims=True)
        padded = jnp.floor((count + (MOE_TILE - 1)) * (1.0 / MOE_TILE)) * MOE_TILE
        pos = pos + ind * (base + row_off + within)
        end = base + padded
        tile_bucket = tile_bucket + (tile_start >= end).astype(F32)
        last_row = jnp.where(count > 0.0, end - MOE_TILE, -1.0)
        last_tile = jnp.where(lane == b, last_row, last_tile)
        return end, pos, tile_bucket, last_tile

    total, pos, tile_bucket, last_tile = lax.fori_loop(
        0, N_BUCKETS, body, (jnp.zeros((1, 1), F32), jnp.zeros((r, c), F32), jnp.zeros((1, LANES), F32),
                             jnp.full((1, LANES), -1.0, F32)))
    pos_ref[...] = pos.astype(jnp.int32)

    tb = jnp.minimum(tile_bucket, N_BUCKETS - 1.0)
    group = sum((tb >= g * PAIRS_PER_GROUP).astype(F32) for g in range(1, N_GROUPS))
    pair = tb - group * PAIRS_PER_GROUP
    i_lo = jnp.where(pair < 3, 0.0, jnp.where(pair < 5, 1.0, 2.0))
    i_hi = jnp.where(pair < 3, pair + 1.0, jnp.where(pair < 5, pair - 1.0, 3.0))
    meta = jnp.concatenate([group * GROUP_SIZE + i_lo, group * GROUP_SIZE + i_hi,
                            (tile_start < total).astype(F32),
                            jnp.where(lane == N_BUCKETS, total, last_tile),
                            jnp.zeros((META_ROWS - 4, LANES), F32)], axis=0)
    meta_ref[...] = meta.astype(jnp.int32)


def _plan(bucket_ids):
    r, c = bucket_ids.shape
    return pl.pallas_call(
        _plan_kernel,
        out_shape=[jax.ShapeDtypeStruct((r, c), jnp.int32), jax.ShapeDtypeStruct((META_ROWS, LANES), jnp.int32)],
        compiler_params=pltpu.CompilerParams(vmem_limit_bytes=VMEM_LIMIT_BYTES),
        name="moe_plan",
    )(bucket_ids)


DMA_PRIORITIES = 2


def _row_copy(src_ref, src_row, dst_ref, dst_row, sem):
    return pltpu.make_async_copy(src_ref.at[pl.ds(src_row, 1)], dst_ref.at[pl.ds(dst_row, 1)], sem)


def _dispatch_kernel(pos_ref, last_ref, x_ref, o_ref, zeros, sem, zsem, *, first_spare):
    tm = x_ref.shape[0]
    i = pl.program_id(0)

    @pl.when(i == 0)
    def _():
        zeros[...] = jnp.zeros_like(zeros)
        rows_in_use = last_ref[N_BUCKETS]
        clears = [(last_ref[b] >= 0, last_ref[b]) for b in range(N_BUCKETS)]
        clears += [(k * MOE_TILE >= rows_in_use, k * MOE_TILE)
                   for k in range(first_spare, o_ref.shape[0] // MOE_TILE)]

        def clear(row):
            start = row if isinstance(row, int) else pl.multiple_of(row, MOE_TILE)
            return pltpu.make_async_copy(zeros, o_ref.at[pl.ds(start, MOE_TILE)], zsem)

        for needed, row in clears:
            @pl.when(needed)
            def _():
                clear(row).start()
        for needed, row in clears:
            @pl.when(needed)
            def _():
                clear(row).wait()

    base = i * tm
    for r in range(tm):
        _row_copy(x_ref, r, o_ref, pos_ref[base + r], sem).start(priority=r % DMA_PRIORITIES)
    pltpu.make_async_copy(x_ref, o_ref.at[pl.ds(0, tm)], sem).wait()


def _dispatch(pos, last_tile_rows, xt, n_rows, tm):
    t = xt.shape[0]
    tile = xt.shape[1:]
    return pl.pallas_call(
        functools.partial(_dispatch_kernel, first_spare=t // MOE_TILE),
        grid_spec=pltpu.PrefetchScalarGridSpec(
            num_scalar_prefetch=2,
            grid=(t // tm,),
            in_specs=[pl.BlockSpec((tm,) + tile, lambda i, pos_ref, last_ref: (i,) + (0,) * len(tile))],
            out_specs=pl.BlockSpec(memory_space=pl.ANY),
            scratch_shapes=[pltpu.VMEM((MOE_TILE,) + tile, xt.dtype), pltpu.SemaphoreType.DMA(()),
                            pltpu.SemaphoreType.DMA(())]),
        out_shape=jax.ShapeDtypeStruct((n_rows,) + tile, xt.dtype),
        compiler_params=_params("arbitrary"),
        name="moe_dispatch",
    )(pos, last_tile_rows, xt)


def _moe_tile_kernel(elo_ref, ehi_ref, valid_ref, x_ref, wgl_ref, wul_ref, wdl_ref, wgh_ref, wuh_ref,
                     wdh_ref, o_ref, wgl_b, wul_b, wdl_b, wgh_b, wuh_b, wdh_b):
    d = wgl_ref.shape[0]
    k = pl.program_id(0)
    prev = jnp.maximum(k - 1, 0)
    valid = valid_ref[k] != 0

    def refresh(e_ref, srcs, dsts):
        @pl.when(valid & ((k == 0) | (e_ref[k] != e_ref[prev])))
        def _():
            for src, dst in zip(srcs, dsts):
                dst[...] = src[...].astype(BF16)

    refresh(elo_ref, (wgl_ref, wul_ref, wdl_ref), (wgl_b, wul_b, wdl_b))
    refresh(ehi_ref, (wgh_ref, wuh_ref, wdh_ref), (wgh_b, wuh_b, wdh_b))

    @pl.when(jnp.logical_not(valid))
    def _():
        o_ref[...] = jnp.zeros_like(o_ref)

    @pl.when(valid)
    def _():
        x = x_ref[:, 0:d].astype(BF16)
        acc = None
        for wg_b, wu_b, wd_b, lane in ((wgl_b, wul_b, wdl_b, d + 1), (wgh_b, wuh_b, wdh_b, d + 2)):
            weight = x_ref[:, lane:lane + 1]
            hg = jnp.dot(x, wg_b[...], preferred_element_type=F32)
            hu = jnp.dot(x, wu_b[...], preferred_element_type=F32)
            hidden = (hg * _sigmoid(hg)) * hu * weight
            y = jnp.dot(hidden.astype(BF16), wd_b[...], preferred_element_type=F32)
            acc = y if acc is None else acc + y
        o_ref[...] = acc


def _moe_tiles(e_lo, e_hi, valid, xs, wg, wu, wd, layer):
    n_rows, dw = xs.shape
    _, _, d, f = wg.shape
    up_lo = pl.BlockSpec((None, None, d, f), lambda k, lo, hi, ok: (layer, lo[k], 0, 0))
    up_hi = pl.BlockSpec((None, None, d, f), lambda k, lo, hi, ok: (layer, hi[k], 0, 0))
    down_lo = pl.BlockSpec((None, None, f, d), lambda k, lo, hi, ok: (layer, lo[k], 0, 0))
    down_hi = pl.BlockSpec((None, None, f, d), lambda k, lo, hi, ok: (layer, hi[k], 0, 0))
    up_b = pltpu.VMEM((d, f), BF16)
    down_b = pltpu.VMEM((f, d), BF16)
    return pl.pallas_call(
        _moe_tile_kernel,
        grid_spec=pltpu.PrefetchScalarGridSpec(
            num_scalar_prefetch=3,
            grid=(n_rows // MOE_TILE,),
            in_specs=[pl.BlockSpec((MOE_TILE, dw), lambda k, lo, hi, ok: (k, 0)),
                      up_lo, up_lo, down_lo, up_hi, up_hi, down_hi],
            out_specs=pl.BlockSpec((MOE_TILE, d), lambda k, lo, hi, ok: (k, 0)),
            scratch_shapes=[up_b, up_b, down_b, up_b, up_b, down_b]),
        out_shape=jax.ShapeDtypeStruct((n_rows, d), F32),
        compiler_params=_params("arbitrary"),
        name="moe_tiles",
    )(e_lo, e_hi, valid, xs, wg, wu, wd, wg, wu, wd)


def _ln_ple_kernel(pos_ref, x1_ref, ys_ref, p_ref, g_ref, b_ref, wg_ref, bg_ref, wp_ref, o_ref,
                   m_even, m_odd, sem_even, sem_odd, *, alpha):
    tm = x1_ref.shape[0]
    i = pl.program_id(0)
    last = pl.num_programs(0) - 1

    def wait(buf, sem):
        pltpu.make_async_copy(ys_ref.at[pl.ds(0, tm)], buf, sem).wait()

    @pl.when(i == 0)
    def _():
        def issue(r, carry):
            _row_copy(ys_ref, pos_ref[r], m_even, r, sem_even).start()
            return carry

        lax.fori_loop(0, tm, issue, 0, unroll=8)

    def step(cur, cur_sem, nxt, nxt_sem):
        wait(cur, cur_sem)
        base = jnp.minimum(i + 1, last) * tm
        for r in range(tm):
            _row_copy(ys_ref, pos_ref[base + r], nxt, r, nxt_sem).start(priority=r % DMA_PRIORITIES)
        emb = jnp.dot(p_ref[...].astype(BF16), wp_ref[...], preferred_element_type=F32)
        x2 = _layer_norm(alpha * x1_ref[...] + cur[...], g_ref[...], b_ref[...])
        gate = _sigmoid(jnp.dot(x2.astype(BF16), wg_ref[...], preferred_element_type=F32) + bg_ref[...])
        o_ref[...] = x2 + gate * emb

        @pl.when(i == last)
        def _():
            wait(nxt, nxt_sem)

    @pl.when(i % 2 == 0)
    def _():
        step(m_even, sem_even, m_odd, sem_odd)

    @pl.when(i % 2 == 1)
    def _():
        step(m_odd, sem_odd, m_even, sem_even)


def _ln_ple(pos, x1e, ys, p, ln_g, ln_b, wg, bg, wp, alpha, tm):
    t = x1e.shape[0]
    d = ys.shape[1]
    pd = p.shape[1]
    tok = pl.BlockSpec((tm, d), lambda i, pos_ref: (i, 0))
    row = pl.BlockSpec((1, d), lambda i, pos_ref: (0, 0))
    return pl.pallas_call(
        functools.partial(_ln_ple_kernel, alpha=alpha),
        grid_spec=pltpu.PrefetchScalarGridSpec(
            num_scalar_prefetch=1,
            grid=(t // tm,),
            in_specs=[tok, pl.BlockSpec(memory_space=pl.ANY),
                      pl.BlockSpec((tm, pd), lambda i, pos_ref: (i, 0)), row, row,
                      pl.BlockSpec((d, d), lambda i, pos_ref: (0, 0)), row,
                      pl.BlockSpec((pd, d), lambda i, pos_ref: (0, 0))],
            out_specs=tok,
            scratch_shapes=[pltpu.VMEM((tm,) + ys.shape[1:], F32), pltpu.VMEM((tm,) + ys.shape[1:], F32),
                            pltpu.SemaphoreType.DMA(()), pltpu.SemaphoreType.DMA(())]),
        out_shape=jax.ShapeDtypeStruct((t, d), F32),
        compiler_params=_params("arbitrary"),
        name="ln_ple",
    )(pos, x1e, ys, p, ln_g.reshape(1, d), ln_b.reshape(1, d), wg, bg.reshape(1, d), wp)


def _pick_tile(n, target):
    t = min(n, target)
    while n % t:
        t //= 2
    return t


def kernel(x, p, positions, w_in_ab, w_out_ab, conv_w, conv_b, lru_w_r, lru_b_r, lru_w_i, lru_b_i, lru_lambda, w_qkv_c, w_out_c, sinks_c, ln_mix_g, ln_mix_b, ln_ffn_g, ln_ffn_b, w_router, b_router, exp_w_gate, exp_w_up, exp_w_down, ple_w_proj, ple_w_gate, ple_b_gate):
    b, s, d = x.shape
    depth = p.shape[0]
    t = b * s
    alpha = (2 * depth) ** 0.25
    tm = _pick_tile(s, 512)
    assert t % MOE_TILE == 0 and t % PLAN_COLS == 0
    n_tiles = _num_moe_tiles(t)
    assert n_tiles <= LANES
    for i in range(depth):
        j = i // 2
        if i % 2 == 0:
            q, k, v, xr, gr = _proj_ab(x, w_in_ab[j].astype(BF16), tm)
            y_sb = _sb_attention(q, k, v, _pick_tile(s, 256), 4)
            y_lru = _lru(xr, gr, conv_w[j], conv_b[j], _block_diag(lru_w_r[j]).astype(BF16), lru_b_r[j],
                         _block_diag(lru_w_i[j]).astype(BF16), lru_b_i[j], lru_lambda[j],
                         _pick_tile(s, 256))
            w_out = w_out_ab[j].astype(BF16)
            ys = [(y_sb, True), (y_lru, False)]
            ws = [w_out[:SB_WIDTH], w_out[SB_WIDTH:]]
        else:
            q, k, v = _proj_rope(x, positions, w_qkv_c[j].astype(BF16), tm)
            y = _swa(q, k, v, sinks_c[j])
            ys = [(y, False)]
            ws = [w_out_c[j].astype(BF16)]
        x1e, route = _mix_out(x, ys, ws, ln_mix_g[i], ln_mix_b[i], w_router, b_router, alpha, tm)
        x1e = x1e.reshape(t, d + LANES)
        pos, meta = _plan(route[:, 0, :].reshape(t // PLAN_COLS, PLAN_COLS))
        pos = pos.reshape(t)
        xs = _dispatch(pos, meta[3, :N_BUCKETS + 1], x1e, n_tiles * MOE_TILE, _pick_tile(t, 512))
        ys_moe = _moe_tiles(meta[0, :n_tiles], meta[1, :n_tiles], meta[2, :n_tiles], xs,
                            exp_w_gate, exp_w_up, exp_w_down, i)
        x = _ln_ple(pos, x1e, ys_moe, p[i].reshape(t, -1), ln_ffn_g[i], ln_ffn_b[i],
                    ple_w_gate[i].astype(BF16), ple_b_gate[i], ple_w_proj[i].astype(BF16), alpha,
                    _pick_tile(t, 512)).reshape(b, s, d)
    return x
```

```python
import functools
import math

import jax
import jax.numpy as jnp
from jax import lax
from jax.experimental import pallas as pl
from jax.experimental.pallas import tpu as pltpu

HEAD_DIM = 64
SB_HEADS = 8
SB_WIDTH = SB_HEADS * HEAD_DIM
LRU_WIDTH = 512
LRU_BLOCKS = 8
LRU_C = 8.0
CONV_WIDTH = 4
SWA_HEADS = 16
SWA_KV_HEADS = 4
SWA_GROUP = SWA_HEADS // SWA_KV_HEADS
SWA_WINDOW = 128
ROPE_THETA = 10000.0
N_EXPERTS = 16
N_GROUPS = 4
GROUP_SIZE = N_EXPERTS // N_GROUPS
LN_EPS = 1e-5
Q_SCALE = HEAD_DIM ** -0.5

LANES = 128
SUBLANES = 8
VMEM_LIMIT_BYTES = 48 * 1024 * 1024

NEG_BIG = -1e30

BF16 = jnp.bfloat16
F32 = jnp.float32


def _params(*semantics):
    return pltpu.CompilerParams(dimension_semantics=semantics, vmem_limit_bytes=VMEM_LIMIT_BYTES)


def _softplus(z):
    return jnp.maximum(z, 0.0) + jnp.log(1.0 + jnp.exp(-jnp.abs(z)))


def _sigmoid(z):
    return 1.0 / (1.0 + jnp.exp(-z))


def _layer_norm(y, g, b):
    mu = jnp.mean(y, axis=-1, keepdims=True)
    d = y - mu
    var = jnp.mean(d * d, axis=-1, keepdims=True)
    return d * lax.rsqrt(var + LN_EPS) * g + b


def _proj_ab_kernel(x_ref, w_ref, q_ref, k_ref, v_ref, xr_ref, gr_ref):
    xb = x_ref[...].astype(BF16)

    def chunk(c):
        return jnp.dot(xb, w_ref[:, c * SB_WIDTH:(c + 1) * SB_WIDTH], preferred_element_type=F32)

    for c, (ref, scale) in enumerate(((q_ref, Q_SCALE), (k_ref, None), (v_ref, None))):
        r = chunk(c)
        if scale is not None:
            r = r * scale
        for h in range(SB_HEADS):
            ref[h] = r[:, h * HEAD_DIM:(h + 1) * HEAD_DIM].astype(BF16)
    xr_ref[...] = chunk(3)
    gr_ref[...] = chunk(4)


def _proj_ab(x, w_bf16, tm):
    b, s, d = x.shape
    n = w_bf16.shape[1]
    heads = jax.ShapeDtypeStruct((b, SB_HEADS, s, HEAD_DIM), BF16)
    flat = jax.ShapeDtypeStruct((b, s, LRU_WIDTH), F32)
    head_spec = pl.BlockSpec((None, SB_HEADS, tm, HEAD_DIM), lambda bi, i: (bi, 0, i, 0))
    flat_spec = pl.BlockSpec((None, tm, LRU_WIDTH), lambda bi, i: (bi, i, 0))
    return pl.pallas_call(
        _proj_ab_kernel,
        grid=(b, s // tm),
        in_specs=[pl.BlockSpec((None, tm, d), lambda bi, i: (bi, i, 0)),
                  pl.BlockSpec((d, n), lambda bi, i: (0, 0))],
        out_specs=[head_spec, head_spec, head_spec, flat_spec, flat_spec],
        out_shape=[heads, heads, heads, flat, flat],
        compiler_params=_params("parallel", "parallel"),
        name="proj_ab",
    )(x, w_bf16)


SB_DEAD_LOG_WEIGHT = -105.0
SB_MERGED_BLOCKS = 2


def _sb_attn_kernel(q_ref, k_ref, v_ref, o_ref, *, tq, hp):
    i = pl.program_id(2)
    row = lax.broadcasted_iota(jnp.int32, (tq, tq), 0)
    col = lax.broadcasted_iota(jnp.int32, (tq, tq), 1)
    later = (row > col).astype(BF16)
    causal = col < row

    def block(jb, carries, accs, masked):
        start = pl.multiple_of(jb * tq, tq)
        new_carries, new_accs = [], []
        for h in range(hp):
            kj = k_ref[h, pl.ds(start, tq), :]
            vj = v_ref[h, pl.ds(start, tq), :]
            z = lax.dot_general(q_ref[h], kj, (((1,), (1,)), ((), ())), preferred_element_type=F32)
            sp = _softplus(z)
            log_keep = -sp
            if masked:
                log_keep = jnp.where(causal, log_keep, 0.0)
            after = jnp.dot(log_keep.astype(BF16), later, preferred_element_type=F32)
            w = jnp.exp((z - sp) + after + carries[h])
            if masked:
                w = jnp.where(causal, w, 0.0)
            new_accs.append(accs[h] + jnp.dot(w.astype(BF16), vj, preferred_element_type=F32))
            new_carries.append(carries[h] + jnp.sum(log_keep, axis=1, keepdims=True))
        return tuple(new_carries), tuple(new_accs)

    def live(carries):
        return functools.reduce(jnp.maximum, [jnp.max(c) for c in carries])

    zero = ((jnp.zeros((tq, 1), F32),) * hp, (jnp.zeros((tq, HEAD_DIM), F32),) * hp)

    def first(n):
        def run():
            state = block(i, *zero, True)
            for j in range(1, n):
                state = block(i - j, *state, False)
            return state
        return run

    done = jnp.minimum(i, SB_MERGED_BLOCKS - 1)
    carries, accs = lax.switch(done, [first(n) for n in range(1, SB_MERGED_BLOCKS + 1)])

    def cond(state):
        return (state[0] < i) & (state[1] > SB_DEAD_LOG_WEIGHT)

    def body(state):
        step, _, carries, accs = state
        carries, accs = block(i - 1 - step, carries, accs, False)
        return step + 1, live(carries), carries, accs

    _, _, _, accs = lax.while_loop(cond, body, (done, live(carries), carries, accs))
    for h in range(hp):
        o_ref[h] = accs[h].astype(o_ref.dtype)


def _sb_attention(q, k, v, tq, hp):
    b, h, s, dh = q.shape
    return pl.pallas_call(
        functools.partial(_sb_attn_kernel, tq=tq, hp=hp),
        grid=(b, h // hp, s // tq),
        in_specs=[pl.BlockSpec((None, hp, tq, dh), lambda bi, hi, i: (bi, hi, i, 0)),
                  pl.BlockSpec((None, hp, s, dh), lambda bi, hi, i: (bi, hi, 0, 0)),
                  pl.BlockSpec((None, hp, s, dh), lambda bi, hi, i: (bi, hi, 0, 0))],
        out_specs=pl.BlockSpec((None, hp, tq, dh), lambda bi, hi, i: (bi, hi, i, 0)),
        out_shape=jax.ShapeDtypeStruct((b, h, s, dh), BF16),
        compiler_params=_params("parallel", "parallel", "parallel"),
        name="sb_attention",
    )(q, k, v)


def _gelu_tanh(x):
    return 0.5 * x * (1.0 + jnp.tanh(math.sqrt(2.0 / math.pi) * (x + 0.044715 * (x * x * x))))


def _lru_kernel(xr_ref, gr_ref, cw_ref, cb_ref, wr_ref, br_ref, wi_ref, bi_ref, lam_ref, y_ref,
                xbuf, hprev, *, ts):
    @pl.when(pl.program_id(1) == 0)
    def _():
        xbuf[0:SUBLANES, :] = jnp.zeros((SUBLANES, LRU_WIDTH), F32)
        hprev[...] = jnp.zeros_like(hprev)

    xbuf[SUBLANES:SUBLANES + ts, :] = xr_ref[...]
    xc = cb_ref[...] + cw_ref[CONV_WIDTH - 1:CONV_WIDTH, :] * xbuf[SUBLANES:SUBLANES + ts, :]
    for kk in range(CONV_WIDTH - 1):
        off = SUBLANES - (CONV_WIDTH - 1) + kk
        xc = xc + cw_ref[kk:kk + 1, :] * xbuf[off:off + ts, :]
    xbuf[0:SUBLANES, :] = xbuf[ts:ts + SUBLANES, :]

    xcb = xc.astype(BF16)
    r = _sigmoid(jnp.dot(xcb, wr_ref[...], preferred_element_type=F32) + br_ref[...])
    gi = _sigmoid(jnp.dot(xcb, wi_ref[...], preferred_element_type=F32) + bi_ref[...])
    log_a = (-LRU_C) * r * _softplus(-lam_ref[...])
    a = jnp.exp(log_a)
    u = jnp.sqrt(1.0 - a * a) * (gi * xc)

    row = lax.broadcasted_iota(jnp.int32, (ts, LRU_WIDTH), 0)
    d = 1
    while d < ts:
        if d < SUBLANES:
            keep = row >= d
            a_sh = jnp.where(keep, pltpu.roll(a, d, axis=0), 1.0)
            u_sh = jnp.where(keep, pltpu.roll(u, d, axis=0), 0.0)
            u = a * u_sh + u
            a = a * a_sh
        else:
            u = jnp.concatenate([u[:d], a[d:] * u[:ts - d] + u[d:]], axis=0)
            a = jnp.concatenate([a[:d], a[d:] * a[:ts - d]], axis=0)
        d *= 2
    h = a * hprev[0:1, :] + u
    hprev[...] = jnp.broadcast_to(h[ts - 1:ts, :], hprev.shape)
    y_ref[...] = (_gelu_tanh(gr_ref[...]) * h).astype(y_ref.dtype)


def _lru(xr, gr, conv_w, conv_b, wr_bd, b_r, wi_bd, b_i, lam, ts):
    b, s, w = xr.shape
    seq_spec = pl.BlockSpec((None, ts, w), lambda bi, i: (bi, i, 0))

    def full(shape):
        return pl.BlockSpec(shape, lambda bi, i: (0,) * len(shape))

    return pl.pallas_call(
        functools.partial(_lru_kernel, ts=ts),
        grid=(b, s // ts),
        in_specs=[seq_spec, seq_spec, full((CONV_WIDTH, w)), full((1, w)), full((w, w)), full((1, w)),
                  full((w, w)), full((1, w)), full((1, w))],
        out_specs=seq_spec,
        out_shape=jax.ShapeDtypeStruct((b, s, w), BF16),
        scratch_shapes=[pltpu.VMEM((ts + 2 * SUBLANES, w), F32), pltpu.VMEM((SUBLANES, w), F32)],
        compiler_params=_params("parallel", "arbitrary"),
        name="rg_lru",
    )(xr, gr, conv_w, conv_b.reshape(1, w), wr_bd, b_r.reshape(1, w), wi_bd, b_i.reshape(1, w),
      lam.reshape(1, w))


def _block_diag(w):
    n, c, d = w.shape
    eye = jnp.eye(n, dtype=w.dtype)
    return (eye[:, None, :, None] * w[:, :, None, :]).reshape(n * c, n * d)


def _proj_rope_kernel(x_ref, pos_ref, freq_ref, w_ref, q_ref, k_ref, vt_ref):
    tm = x_ref.shape[0]
    xb = x_ref[...].astype(BF16)
    ang_t = freq_ref[...] * pos_ref[...].astype(F32)
    reps = LANES // (HEAD_DIM // 2)
    cos = jnp.concatenate([jnp.cos(ang_t)] * reps, axis=0).T
    sin = jnp.concatenate([jnp.sin(ang_t)] * reps, axis=0).T
    lane = lax.broadcasted_iota(jnp.int32, (tm, LANES), 1)
    first_half = (lane % HEAD_DIM) < (HEAD_DIM // 2)
    heads_per_slab = LANES // HEAD_DIM

    def rope(r):
        upper = pltpu.roll(r, LANES - HEAD_DIM // 2, axis=1)
        lower = pltpu.roll(r, HEAD_DIM // 2, axis=1)
        return r * cos + jnp.where(first_half, -upper, lower) * sin

    def emit(ref, n_heads, col0, rotary, scale):
        for slab in range(n_heads // heads_per_slab):
            c0 = col0 + slab * LANES
            r = jnp.dot(xb, w_ref[:, c0:c0 + LANES], preferred_element_type=F32)
            if rotary:
                r = rope(r)
            if scale is not None:
                r = r * scale
            for j in range(heads_per_slab):
                ref[slab * heads_per_slab + j] = r[:, j * HEAD_DIM:(j + 1) * HEAD_DIM].astype(BF16)

    emit(q_ref, SWA_HEADS, 0, True, Q_SCALE)
    emit(k_ref, SWA_KV_HEADS, SWA_HEADS * HEAD_DIM, True, None)
    v0 = (SWA_HEADS + SWA_KV_HEADS) * HEAD_DIM
    for slab in range(SWA_KV_HEADS // heads_per_slab):
        r = jnp.dot(xb, w_ref[:, v0 + slab * LANES:v0 + (slab + 1) * LANES], preferred_element_type=F32)
        rt = r.T
        for j in range(heads_per_slab):
            vt_ref[slab * heads_per_slab + j] = rt[j * HEAD_DIM:(j + 1) * HEAD_DIM, :].astype(BF16)


def _proj_rope(x, positions, w_bf16, tm):
    b, s, d = x.shape
    n = w_bf16.shape[1]
    half = HEAD_DIM // 2
    inv_freq = (ROPE_THETA ** (-jnp.arange(half, dtype=F32) / half)).reshape(half, 1)

    def heads(nh):
        return (jax.ShapeDtypeStruct((b, nh, s, HEAD_DIM), BF16),
                pl.BlockSpec((None, nh, tm, HEAD_DIM), lambda bi, i: (bi, 0, i, 0)))

    (qs, qspec), (ks, kspec) = heads(SWA_HEADS), heads(SWA_KV_HEADS)
    vs = jax.ShapeDtypeStruct((b, SWA_KV_HEADS, HEAD_DIM, s), BF16)
    vspec = pl.BlockSpec((None, SWA_KV_HEADS, HEAD_DIM, tm), lambda bi, i: (bi, 0, 0, i))
    return pl.pallas_call(
        _proj_rope_kernel,
        grid=(b, s // tm),
        in_specs=[pl.BlockSpec((None, tm, d), lambda bi, i: (bi, i, 0)),
                  pl.BlockSpec((None, 1, tm), lambda bi, i: (bi, 0, i)),
                  pl.BlockSpec((half, 1), lambda bi, i: (0, 0)),
                  pl.BlockSpec((d, n), lambda bi, i: (0, 0))],
        out_specs=[qspec, kspec, vspec],
        out_shape=[qs, ks, vs],
        compiler_params=_params("parallel", "parallel"),
        name="proj_rope",
    )(x, positions.reshape(b, 1, s), inv_freq, w_bf16)


def _reduce_rows(x, op):
    while x.shape[0] > SUBLANES:
        half = x.shape[0] // 2
        x = op(x[:half], x[half:])
    for shift in (4, 2, 1):
        x = op(x, pltpu.roll(x, shift, axis=0))
    return x[0:1]


SWA_BLOCKS_PER_STEP = 4


def _swa_kernel(q_ref, kp_ref, kc_ref, vtp_ref, vtc_ref, sink_ref, o_ref):
    i = pl.program_id(1)
    w = SWA_WINDOW
    key = lax.broadcasted_iota(jnp.int32, (2 * w, w), 0)
    qry = lax.broadcasted_iota(jnp.int32, (2 * w, w), 1)
    dist = qry + w - key
    band = (dist >= 0) & (dist < w)
    for blk in range(SWA_BLOCKS_PER_STEP):
        visible = band if blk else band & ((key >= w) | (i > 0))
        bias = jnp.concatenate([jnp.where(visible, 0.0, NEG_BIG)] * SWA_GROUP, axis=1)
        outs = []
        for kv in range(SWA_KV_HEADS):
            if blk:
                kk = kc_ref[kv, (blk - 1) * w:(blk + 1) * w, :]
                vvt = vtc_ref[kv, :, (blk - 1) * w:(blk + 1) * w]
            else:
                kk = jnp.concatenate([kp_ref[kv], kc_ref[kv, 0:w, :]], axis=0)
                vvt = jnp.concatenate([vtp_ref[kv], vtc_ref[kv, :, 0:w]], axis=1)
            qg = jnp.concatenate([q_ref[kv * SWA_GROUP + g, blk * w:(blk + 1) * w, :]
                                  for g in range(SWA_GROUP)], axis=0)
            st = lax.dot_general(kk, qg, (((1,), (1,)), ((), ())), preferred_element_type=F32) + bias
            sink = sink_ref[kv:kv + 1, :]
            m = jnp.maximum(_reduce_rows(st, jnp.maximum), sink)
            p = jnp.exp(st - m)
            denom = _reduce_rows(p, jnp.add) + jnp.exp(sink - m)
            ot = jnp.dot(vvt, p.astype(BF16), preferred_element_type=F32) / denom
            outs.extend(ot[:, g * w:(g + 1) * w] for g in range(SWA_GROUP))
        o_ref[blk * w:(blk + 1) * w, :] = jnp.concatenate(outs, axis=0).T.astype(o_ref.dtype)


def _swa(q, k, vt, sinks):
    b, nh, s, dh = q.shape
    nkv = k.shape[1]
    w = SWA_WINDOW
    n = SWA_BLOCKS_PER_STEP
    assert s % (n * w) == 0
    cur = pl.BlockSpec((None, nkv, n * w, dh), lambda bi, i: (bi, 0, i, 0))
    prev = pl.BlockSpec((None, nkv, w, dh), lambda bi, i: (bi, 0, jnp.maximum(n * i - 1, 0), 0))
    cur_t = pl.BlockSpec((None, nkv, dh, n * w), lambda bi, i: (bi, 0, 0, i))
    prev_t = pl.BlockSpec((None, nkv, dh, w), lambda bi, i: (bi, 0, 0, jnp.maximum(n * i - 1, 0)))
    sink_tile = jnp.repeat(sinks.astype(F32).reshape(nkv, nh // nkv), w, axis=1)
    return pl.pallas_call(
        _swa_kernel,
        grid=(b, s // (n * w)),
        in_specs=[pl.BlockSpec((None, nh, n * w, dh), lambda bi, i: (bi, 0, i, 0)),
                  prev, cur, prev_t, cur_t,
                  pl.BlockSpec(sink_tile.shape, lambda bi, i: (0, 0))],
        out_specs=pl.BlockSpec((None, n * w, nh * dh), lambda bi, i: (bi, i, 0)),
        out_shape=jax.ShapeDtypeStruct((b, s, nh * dh), BF16),
        compiler_params=_params("parallel", "parallel"),
        name="swa",
    )(q, k, k, vt, vt, sink_tile)


PAIRS_PER_GROUP = GROUP_SIZE * (GROUP_SIZE - 1) // 2
N_BUCKETS = N_GROUPS * PAIRS_PER_GROUP
ROUTE_ROWS = SUBLANES


def _route(logits_t):
    rows = [logits_t[e:e + 1, :] for e in range(N_EXPERTS)]
    mx = functools.reduce(jnp.maximum, rows)
    ex = [jnp.exp(r - mx) for r in rows]
    total = functools.reduce(lambda p, q: p + q, ex)
    probs = [e / total for e in ex]

    group_score = []
    for g in range(N_GROUPS):
        a, b, c, d = probs[g * GROUP_SIZE:(g + 1) * GROUP_SIZE]
        hi1, lo1 = jnp.maximum(a, b), jnp.minimum(a, b)
        hi2, lo2 = jnp.maximum(c, d), jnp.minimum(c, d)
        top1 = jnp.maximum(hi1, hi2)
        top2 = jnp.maximum(jnp.minimum(hi1, hi2), jnp.maximum(lo1, lo2))
        group_score.append(top1 + top2)
    best = functools.reduce(jnp.maximum, group_score)
    g_sel = jnp.full(best.shape, N_GROUPS - 1, jnp.int32)
    for g in range(N_GROUPS - 2, -1, -1):
        g_sel = jnp.where(group_score[g] == best, g, g_sel)

    in_group = []
    for j in range(GROUP_SIZE):
        val = probs[(N_GROUPS - 1) * GROUP_SIZE + j]
        for g in range(N_GROUPS - 2, -1, -1):
            val = jnp.where(g_sel == g, probs[g * GROUP_SIZE + j], val)
        in_group.append(val)

    def first_argmax(vals):
        m = functools.reduce(jnp.maximum, vals)
        idx = jnp.full(m.shape, GROUP_SIZE - 1, jnp.int32)
        for j in range(GROUP_SIZE - 2, -1, -1):
            idx = jnp.where(vals[j] == m, j, idx)
        return m, idx

    w1, i1 = first_argmax(in_group)
    rest = [jnp.where(i1 == j, -1.0, in_group[j]) for j in range(GROUP_SIZE)]
    w2, i2 = first_argmax(rest)
    norm = w1 + w2
    first_is_lo = i1 < i2
    i_lo = jnp.minimum(i1, i2)
    i_hi = jnp.maximum(i1, i2)
    pair = jnp.where(i_lo == 0, i_hi - 1, jnp.where(i_lo == 1, i_hi + 1, PAIRS_PER_GROUP - 1))
    bucket = (g_sel * PAIRS_PER_GROUP + pair).astype(F32)
    w_lo = jnp.where(first_is_lo, w1, w2) / norm
    w_hi = jnp.where(first_is_lo, w2, w1) / norm
    return bucket, w_lo, w_hi


def _mix_out_kernel(*refs, n_head_major, alpha):
    x_ref = refs[0]
    y_refs = refs[1:1 + len(n_head_major)]
    w_refs = refs[1 + len(n_head_major):1 + 2 * len(n_head_major)]
    g_ref, b_ref, wrh_ref, wrl_ref, brt_ref, x1e_ref, route_ref = refs[1 + 2 * len(n_head_major):]
    tm, d = x_ref.shape
    h = alpha * x_ref[...]
    for y_ref, w_ref, nh in zip(y_refs, w_refs, n_head_major):
        if nh:
            y = jnp.concatenate([y_ref[j] for j in range(nh)], axis=-1)
        else:
            y = y_ref[...]
        h = h + jnp.dot(y, w_ref[...], preferred_element_type=F32)
    x1 = _layer_norm(h, g_ref[...], b_ref[...])
    x_hi = x1.astype(BF16)
    x_lo = (x1 - x_hi.astype(F32)).astype(BF16)

    def nt_dot(w, xv):
        return lax.dot_general(w, xv, (((1,), (1,)), ((), ())), preferred_element_type=F32)

    logits_t = (nt_dot(wrh_ref[...], x_hi) + nt_dot(wrh_ref[...], x_lo) + nt_dot(wrl_ref[...], x_hi)
                + brt_ref[...])
    route = jnp.concatenate(list(_route(logits_t)) + [jnp.zeros((ROUTE_ROWS - 3, tm), F32)], axis=0)
    route_ref[...] = route
    x1e_ref[:, 0:d] = x1
    x1e_ref[:, d:d + LANES] = jnp.concatenate([route, jnp.zeros((LANES - ROUTE_ROWS, tm), F32)], axis=0).T


def _mix_out(x, ys, ws, ln_g, ln_b, w_router, b_router, alpha, tm):
    b, s, d = x.shape
    n_head_major = tuple(y.shape[1] if hm else 0 for y, hm in ys)
    y_specs = []
    for (y, hm) in ys:
        if hm:
            y_specs.append(pl.BlockSpec((None, y.shape[1], tm, y.shape[3]), lambda bi, i: (bi, 0, i, 0)))
        else:
            y_specs.append(pl.BlockSpec((None, tm, y.shape[2]), lambda bi, i: (bi, i, 0)))
    w_specs = [pl.BlockSpec(w.shape, lambda bi, i: (0, 0)) for w in ws]
    row = pl.BlockSpec((1, d), lambda bi, i: (0, 0))
    tok = pl.BlockSpec((None, tm, d), lambda bi, i: (bi, i, 0))
    wr_t = w_router.T.astype(F32)
    wr_hi = wr_t.astype(BF16)
    wr_lo = (wr_t - wr_hi.astype(F32)).astype(BF16)
    wr_spec = pl.BlockSpec((N_EXPERTS, d), lambda bi, i: (0, 0))
    return pl.pallas_call(
        functools.partial(_mix_out_kernel, n_head_major=n_head_major, alpha=alpha),
        grid=(b, s // tm),
        in_specs=[tok] + y_specs + w_specs + [row, row, wr_spec, wr_spec,
                  pl.BlockSpec((N_EXPERTS, 1), lambda bi, i: (0, 0))],
        out_specs=[pl.BlockSpec((None, tm, d + LANES), lambda bi, i: (bi, i, 0)),
                   pl.BlockSpec((None, ROUTE_ROWS, tm), lambda bi, i: (bi, 0, i))],
        out_shape=[jax.ShapeDtypeStruct((b, s, d + LANES), F32),
                   jax.ShapeDtypeStruct((b, ROUTE_ROWS, s), F32)],
        compiler_params=_params("parallel", "parallel"),
        name="mix_out_ln_router",
    )(x, *[y for y, _ in ys], *ws, ln_g.reshape(1, d), ln_b.reshape(1, d),
      wr_hi, wr_lo, b_router.astype(F32).reshape(N_EXPERTS, 1))


MOE_TILE = 256
PLAN_COLS = 256
META_ROWS = SUBLANES


def _num_moe_tiles(t):
    return t // MOE_TILE + N_BUCKETS


def _plan_kernel(bid_ref, pos_ref, meta_ref):
    r, c = bid_ref.shape
    bid = bid_ref[...]
    before = (lax.broadcasted_iota(jnp.int32, (c, c), 0)
              < lax.broadcasted_iota(jnp.int32, (c, c), 1)).astype(BF16)
    rows_before = (lax.broadcasted_iota(jnp.int32, (r, r), 1)
                   < lax.broadcasted_iota(jnp.int32, (r, r), 0)).astype(BF16)
    tile_start = lax.broadcasted_iota(jnp.int32, (1, LANES), 1).astype(F32) * MOE_TILE

    def body(b, state):
        base, pos, tile_bucket = state
        ind = (bid == lax.convert_element_type(b, F32)).astype(F32)
        within = jnp.dot(ind.astype(BF16), before, preferred_element_type=F32)
        row_total = jnp.sum(ind, axis=1, keepdims=True)
        row_off = jnp.dot(rows_before, jnp.broadcast_to(row_total, (r, LANES)).astype(BF16),
                          preferred_element_type=F32)[:, 0:1]
        count = jnp.sum(row_total, axis=0, keepdims=True)
        padded = jnp.floor((count + (MOE_TILE - 1)) * (1.0 / MOE_TILE)) * MOE_TILE
        pos = pos + ind * (base + row_off + within)
        end = base + padded
        tile_bucket = tile_bucket + (tile_start >= end).astype(F32)
        return end, pos, tile_bucket

    total, pos, tile_bucket = lax.fori_loop(
        0, N_BUCKETS, body, (jnp.zeros((1, 1), F32), jnp.zeros((r, c), F32), jnp.zeros((1, LANES), F32)))
    pos_ref[...] = pos.astype(jnp.int32)

    tb = jnp.minimum(tile_bucket, N_BUCKETS - 1.0)
    group = sum((tb >= g * PAIRS_PER_GROUP).astype(F32) for g in range(1, N_GROUPS))
    pair = tb - group * PAIRS_PER_GROUP
    i_lo = jnp.where(pair < 3, 0.0, jnp.where(pair < 5, 1.0, 2.0))
    i_hi = jnp.where(pair < 3, pair + 1.0, jnp.where(pair < 5, pair - 1.0, 3.0))
    meta = jnp.concatenate([group * GROUP_SIZE + i_lo, group * GROUP_SIZE + i_hi,
                            (tile_start < total).astype(F32),
                            jnp.zeros((META_ROWS - 3, LANES), F32)], axis=0)
    meta_ref[...] = meta.astype(jnp.int32)


def _plan(bucket_ids):
    r, c = bucket_ids.shape
    return pl.pallas_call(
        _plan_kernel,
        out_shape=[jax.ShapeDtypeStruct((r, c), jnp.int32), jax.ShapeDtypeStruct((META_ROWS, LANES), jnp.int32)],
        compiler_params=pltpu.CompilerParams(vmem_limit_bytes=VMEM_LIMIT_BYTES),
        name="moe_plan",
    )(bucket_ids)


DMA_PRIORITIES = 2


def _row_copy(src_ref, src_row, dst_ref, dst_row, sem):
    return pltpu.make_async_copy(src_ref.at[pl.ds(src_row, 1)], dst_ref.at[pl.ds(dst_row, 1)], sem)


def _moe_kernel(elo_ref, ehi_ref, valid_ref, pos_ref, x_hbm, wgl_ref, wul_ref, wdl_ref, wgh_ref, wuh_ref,
                wdh_ref, m_hbm, inv, x_even, x_odd, o_even, o_odd, gsem_even, gsem_odd, ssem_even, ssem_odd,
                wgl_b, wul_b, wdl_b, wgh_b, wuh_b, wdh_b, *, n_tokens):
    d = wgl_ref.shape[0]
    k = pl.program_id(0)
    last = pl.num_programs(0) - 1
    valid = valid_ref[k] != 0

    def gather(tile, r, buf, sem):
        src = jnp.minimum(inv[tile * MOE_TILE + r], n_tokens - 1)
        return _row_copy(x_hbm, src, buf, r, sem)

    def scatter(tile, r, buf, sem, to_spare):
        tok = inv[tile * MOE_TILE + r]
        dst = jnp.where(to_spare | (tok >= n_tokens), n_tokens + r, tok)
        return _row_copy(buf, r, m_hbm, dst, sem)

    def gather_done(buf, sem):
        pltpu.make_async_copy(x_hbm.at[pl.ds(0, MOE_TILE)], buf, sem).wait()

    def scatter_done(buf, sem):
        pltpu.make_async_copy(buf, m_hbm.at[pl.ds(0, MOE_TILE)], sem).wait()

    @pl.when(k == 0)
    def _():
        def fill(r, carry):
            inv[r] = n_tokens
            return carry

        lax.fori_loop(0, inv.shape[0], fill, 0, unroll=8)

        def put(t, carry):
            inv[pos_ref[t]] = t
            return carry

        lax.fori_loop(0, n_tokens, put, 0, unroll=8)

        def issue(r, carry):
            gather(0, r, x_even, gsem_even).start()
            return carry

        lax.fori_loop(0, MOE_TILE, issue, 0, unroll=8)
        o_odd[...] = jnp.zeros_like(o_odd)

    prev = jnp.maximum(k - 1, 0)

    def refresh(e_ref, srcs, dsts):
        @pl.when(valid & ((k == 0) | (e_ref[k] != e_ref[prev])))
        def _():
            for src, dst in zip(srcs, dsts):
                dst[...] = src[...].astype(BF16)

    def step(x_cur, o_cur, gsem_cur, ssem_cur, x_nxt, o_prv, gsem_nxt, ssem_prv):
        gather_done(x_cur, gsem_cur)

        @pl.when(k > 0)
        def _():
            scatter_done(o_cur, ssem_cur)

        refresh(elo_ref, (wgl_ref, wul_ref, wdl_ref), (wgl_b, wul_b, wdl_b))
        refresh(ehi_ref, (wgh_ref, wuh_ref, wdh_ref), (wgh_b, wuh_b, wdh_b))
        nxt = jnp.minimum(k + 1, last)
        first = k == 0

        def issue_row(r, alternate):
            gather(nxt, r, x_nxt, gsem_nxt).start(priority=r % DMA_PRIORITIES if alternate else 0)
            scatter(prev, r, o_prv, ssem_prv, first).start(
                priority=(r + 1) % DMA_PRIORITIES if alternate else 0)

        @pl.when(valid)
        def _():
            for r in range(MOE_TILE):
                issue_row(r, True)
            x = x_cur[:, 0:d].astype(BF16)
            acc = None
            for wg_b, wu_b, wd_b, lane in ((wgl_b, wul_b, wdl_b, d + 1), (wgh_b, wuh_b, wdh_b, d + 2)):
                weight = x_cur[:, lane:lane + 1]
                hg = jnp.dot(x, wg_b[...], preferred_element_type=F32)
                hu = jnp.dot(x, wu_b[...], preferred_element_type=F32)
                hidden = (hg * _sigmoid(hg)) * hu * weight
                y = jnp.dot(hidden.astype(BF16), wd_b[...], preferred_element_type=F32)
                acc = y if acc is None else acc + y
            o_cur[...] = acc

        @pl.when(jnp.logical_not(valid))
        def _():
            def issue(r, carry):
                issue_row(r, False)
                return carry

            lax.fori_loop(0, MOE_TILE, issue, 0, unroll=8)
            o_cur[...] = jnp.zeros_like(o_cur)

        @pl.when(k == last)
        def _():
            gather_done(x_nxt, gsem_nxt)
            scatter_done(o_prv, ssem_prv)

            def issue(r, carry):
                scatter(k, r, o_cur, ssem_cur, False).start()
                return carry

            lax.fori_loop(0, MOE_TILE, issue, 0, unroll=8)
            scatter_done(o_cur, ssem_cur)

    @pl.when(k % 2 == 0)
    def _():
        step(x_even, o_even, gsem_even, ssem_even, x_odd, o_odd, gsem_odd, ssem_odd)

    @pl.when(k % 2 == 1)
    def _():
        step(x_odd, o_odd, gsem_odd, ssem_odd, x_even, o_even, gsem_even, ssem_even)


def _moe(e_lo, e_hi, valid, pos, x1e, wg, wu, wd, layer):
    t, dw = x1e.shape
    n_tiles = e_lo.shape[0]
    _, _, d, f = wg.shape
    up_lo = pl.BlockSpec((None, None, d, f), lambda k, lo, hi, ok, pos_ref: (layer, lo[k], 0, 0))
    up_hi = pl.BlockSpec((None, None, d, f), lambda k, lo, hi, ok, pos_ref: (layer, hi[k], 0, 0))
    down_lo = pl.BlockSpec((None, None, f, d), lambda k, lo, hi, ok, pos_ref: (layer, lo[k], 0, 0))
    down_hi = pl.BlockSpec((None, None, f, d), lambda k, lo, hi, ok, pos_ref: (layer, hi[k], 0, 0))
    up_b = pltpu.VMEM((d, f), BF16)
    down_b = pltpu.VMEM((f, d), BF16)
    x_buf = pltpu.VMEM((MOE_TILE, dw), F32)
    o_buf = pltpu.VMEM((MOE_TILE, d), F32)
    dma_sem = pltpu.SemaphoreType.DMA(())
    return pl.pallas_call(
        functools.partial(_moe_kernel, n_tokens=t),
        grid_spec=pltpu.PrefetchScalarGridSpec(
            num_scalar_prefetch=4,
            grid=(n_tiles,),
            in_specs=[pl.BlockSpec(memory_space=pl.ANY),
                      up_lo, up_lo, down_lo, up_hi, up_hi, down_hi],
            out_specs=pl.BlockSpec(memory_space=pl.ANY),
            scratch_shapes=[pltpu.SMEM((n_tiles * MOE_TILE,), jnp.int32), x_buf, x_buf, o_buf, o_buf,
                            dma_sem, dma_sem, dma_sem, dma_sem,
                            up_b, up_b, down_b, up_b, up_b, down_b]),
        out_shape=jax.ShapeDtypeStruct((t + MOE_TILE, d), F32),
        compiler_params=_params("arbitrary"),
        name="moe",
    )(e_lo, e_hi, valid, pos, x1e, wg, wu, wd, wg, wu, wd)


def _ln_ple_kernel(x1_ref, m_ref, p_ref, g_ref, b_ref, wg_ref, bg_ref, wp_ref, o_ref, *, alpha):
    x2 = _layer_norm(alpha * x1_ref[...] + m_ref[...], g_ref[...], b_ref[...])
    gate = _sigmoid(jnp.dot(x2.astype(BF16), wg_ref[...], preferred_element_type=F32) + bg_ref[...])
    emb = jnp.dot(p_ref[...].astype(BF16), wp_ref[...], preferred_element_type=F32)
    o_ref[...] = x2 + gate * emb


def _ln_ple(x1e, m, p, ln_g, ln_b, wg, bg, wp, alpha, tm):
    t = x1e.shape[0]
    d = m.shape[1]
    pd = p.shape[1]
    tok = pl.BlockSpec((tm, d), lambda i: (i, 0))
    row = pl.BlockSpec((1, d), lambda i: (0, 0))
    return pl.pallas_call(
        functools.partial(_ln_ple_kernel, alpha=alpha),
        grid=(t // tm,),
        in_specs=[tok, tok, pl.BlockSpec((tm, pd), lambda i: (i, 0)), row, row,
                  pl.BlockSpec((d, d), lambda i: (0, 0)), row, pl.BlockSpec((pd, d), lambda i: (0, 0))],
        out_specs=tok,
        out_shape=jax.ShapeDtypeStruct((t, d), F32),
        compiler_params=_params("parallel"),
        name="ln_ple",
    )(x1e, m, p, ln_g.reshape(1, d), ln_b.reshape(1, d), wg, bg.reshape(1, d), wp)


def _pick_tile(n, target):
    t = min(n, target)
    while n % t:
        t //= 2
    return t


def kernel(x, p, positions, w_in_ab, w_out_ab, conv_w, conv_b, lru_w_r, lru_b_r, lru_w_i, lru_b_i, lru_lambda, w_qkv_c, w_out_c, sinks_c, ln_mix_g, ln_mix_b, ln_ffn_g, ln_ffn_b, w_router, b_router, exp_w_gate, exp_w_up, exp_w_down, ple_w_proj, ple_w_gate, ple_b_gate):
    b, s, d = x.shape
    depth = p.shape[0]
    t = b * s
    alpha = (2 * depth) ** 0.25
    tm = _pick_tile(s, 512)
    assert t % MOE_TILE == 0 and t % PLAN_COLS == 0
    n_tiles = _num_moe_tiles(t)
    assert n_tiles <= LANES
    for i in range(depth):
        j = i // 2
        if i % 2 == 0:
            q, k, v, xr, gr = _proj_ab(x, w_in_ab[j].astype(BF16), tm)
            y_sb = _sb_attention(q, k, v, _pick_tile(s, 256), 4)
            y_lru = _lru(xr, gr, conv_w[j], conv_b[j], _block_diag(lru_w_r[j]).astype(BF16), lru_b_r[j],
                         _block_diag(lru_w_i[j]).astype(BF16), lru_b_i[j], lru_lambda[j],
                         _pick_tile(s, 256))
            w_out = w_out_ab[j].astype(BF16)
            ys = [(y_sb, True), (y_lru, False)]
            ws = [w_out[:SB_WIDTH], w_out[SB_WIDTH:]]
        else:
            q, k, v = _proj_rope(x, positions, w_qkv_c[j].astype(BF16), tm)
            y = _swa(q, k, v, sinks_c[j])
            ys = [(y, False)]
            ws = [w_out_c[j].astype(BF16)]
        x1e, route = _mix_out(x, ys, ws, ln_mix_g[i], ln_mix_b[i], w_router, b_router, alpha, tm)
        x1e = x1e.reshape(t, d + LANES)
        pos, meta = _plan(route[:, 0, :].reshape(t // PLAN_COLS, PLAN_COLS))
        pos = pos.reshape(t)
        m = _moe(meta[0, :n_tiles], meta[1, :n_tiles], meta[2, :n_tiles], pos, x1e,
                 exp_w_gate, exp_w_up, exp_w_down, i)
        x = _ln_ple(x1e, m, p[i].reshape(t, -1), ln_ffn_g[i], ln_ffn_b[i],
                    ple_w_gate[i].astype(BF16), ple_b_gate[i], ple_w_proj[i].astype(BF16), alpha,
                    _pick_tile(t, 512)).reshape(b, s, d)
    return x
```

```python
import functools
import math

import jax
import jax.numpy as jnp
from jax import lax
from jax.experimental import pallas as pl
from jax.experimental.pallas import tpu as pltpu

HEAD_DIM = 64
SB_HEADS = 8
SB_WIDTH = SB_HEADS * HEAD_DIM
LRU_WIDTH = 512
LRU_BLOCKS = 8
LRU_C = 8.0
CONV_WIDTH = 4
SWA_HEADS = 16
SWA_KV_HEADS = 4
SWA_GROUP = SWA_HEADS // SWA_KV_HEADS
SWA_WINDOW = 128
ROPE_THETA = 10000.0
N_EXPERTS = 16
N_GROUPS = 4
GROUP_SIZE = N_EXPERTS // N_GROUPS
LN_EPS = 1e-5
Q_SCALE = HEAD_DIM ** -0.5

LANES = 128
SUBLANES = 8
VMEM_LIMIT_BYTES = 48 * 1024 * 1024

NEG_BIG = -1e30

BF16 = jnp.bfloat16
F32 = jnp.float32


def _params(*semantics):
    return pltpu.CompilerParams(dimension_semantics=semantics, vmem_limit_bytes=VMEM_LIMIT_BYTES)


def _softplus(z):
    return jnp.maximum(z, 0.0) + jnp.log(1.0 + jnp.exp(-jnp.abs(z)))


def _sigmoid(z):
    return 1.0 / (1.0 + jnp.exp(-z))


def _layer_norm(y, g, b):
    mu = jnp.mean(y, axis=-1, keepdims=True)
    d = y - mu
    var = jnp.mean(d * d, axis=-1, keepdims=True)
    return d * lax.rsqrt(var + LN_EPS) * g + b


def _proj_ab_kernel(x_ref, w_ref, q_ref, k_ref, v_ref, xr_ref, gr_ref):
    xb = x_ref[...].astype(BF16)

    def chunk(c):
        return jnp.dot(xb, w_ref[:, c * SB_WIDTH:(c + 1) * SB_WIDTH], preferred_element_type=F32)

    for c, (ref, scale) in enumerate(((q_ref, Q_SCALE), (k_ref, None), (v_ref, None))):
        r = chunk(c)
        if scale is not None:
            r = r * scale
        for h in range(SB_HEADS):
            ref[h] = r[:, h * HEAD_DIM:(h + 1) * HEAD_DIM].astype(BF16)
    xr_ref[...] = chunk(3)
    gr_ref[...] = chunk(4)


def _proj_ab(x, w_bf16, tm):
    b, s, d = x.shape
    n = w_bf16.shape[1]
    heads = jax.ShapeDtypeStruct((b, SB_HEADS, s, HEAD_DIM), BF16)
    flat = jax.ShapeDtypeStruct((b, s, LRU_WIDTH), F32)
    head_spec = pl.BlockSpec((None, SB_HEADS, tm, HEAD_DIM), lambda bi, i: (bi, 0, i, 0))
    flat_spec = pl.BlockSpec((None, tm, LRU_WIDTH), lambda bi, i: (bi, i, 0))
    return pl.pallas_call(
        _proj_ab_kernel,
        grid=(b, s // tm),
        in_specs=[pl.BlockSpec((None, tm, d), lambda bi, i: (bi, i, 0)),
                  pl.BlockSpec((d, n), lambda bi, i: (0, 0))],
        out_specs=[head_spec, head_spec, head_spec, flat_spec, flat_spec],
        out_shape=[heads, heads, heads, flat, flat],
        compiler_params=_params("parallel", "parallel"),
        name="proj_ab",
    )(x, w_bf16)


SB_DEAD_LOG_WEIGHT = -105.0
SB_MERGED_BLOCKS = 2


def _sb_attn_kernel(q_ref, k_ref, v_ref, o_ref, *, tq, hp):
    i = pl.program_id(2)
    row = lax.broadcasted_iota(jnp.int32, (tq, tq), 0)
    col = lax.broadcasted_iota(jnp.int32, (tq, tq), 1)
    later = (row > col).astype(BF16)
    causal = col < row

    def block(jb, carries, accs, masked):
        start = pl.multiple_of(jb * tq, tq)
        new_carries, new_accs = [], []
        for h in range(hp):
            kj = k_ref[h, pl.ds(start, tq), :]
            vj = v_ref[h, pl.ds(start, tq), :]
            z = lax.dot_general(q_ref[h], kj, (((1,), (1,)), ((), ())), preferred_element_type=F32)
            sp = _softplus(z)
            log_keep = -sp
            if masked:
                log_keep = jnp.where(causal, log_keep, 0.0)
            after = jnp.dot(log_keep.astype(BF16), later, preferred_element_type=F32)
            w = jnp.exp((z - sp) + after + carries[h])
            if masked:
                w = jnp.where(causal, w, 0.0)
            new_accs.append(accs[h] + jnp.dot(w.astype(BF16), vj, preferred_element_type=F32))
            new_carries.append(carries[h] + jnp.sum(log_keep, axis=1, keepdims=True))
        return tuple(new_carries), tuple(new_accs)

    def live(carries):
        return functools.reduce(jnp.maximum, [jnp.max(c) for c in carries])

    zero = ((jnp.zeros((tq, 1), F32),) * hp, (jnp.zeros((tq, HEAD_DIM), F32),) * hp)

    def first(n):
        def run():
            state = block(i, *zero, True)
            for j in range(1, n):
                state = block(i - j, *state, False)
            return state
        return run

    done = jnp.minimum(i, SB_MERGED_BLOCKS - 1)
    carries, accs = lax.switch(done, [first(n) for n in range(1, SB_MERGED_BLOCKS + 1)])

    def cond(state):
        return (state[0] < i) & (state[1] > SB_DEAD_LOG_WEIGHT)

    def body(state):
        step, _, carries, accs = state
        carries, accs = block(i - 1 - step, carries, accs, False)
        return step + 1, live(carries), carries, accs

    _, _, _, accs = lax.while_loop(cond, body, (done, live(carries), carries, accs))
    for h in range(hp):
        o_ref[h] = accs[h].astype(o_ref.dtype)


def _sb_attention(q, k, v, tq, hp):
    b, h, s, dh = q.shape
    return pl.pallas_call(
        functools.partial(_sb_attn_kernel, tq=tq, hp=hp),
        grid=(b, h // hp, s // tq),
        in_specs=[pl.BlockSpec((None, hp, tq, dh), lambda bi, hi, i: (bi, hi, i, 0)),
                  pl.BlockSpec((None, hp, s, dh), lambda bi, hi, i: (bi, hi, 0, 0), pipeline_mode=pl.Buffered(1)),
                  pl.BlockSpec((None, hp, s, dh), lambda bi, hi, i: (bi, hi, 0, 0), pipeline_mode=pl.Buffered(1))],
        out_specs=pl.BlockSpec((None, hp, tq, dh), lambda bi, hi, i: (bi, hi, i, 0)),
        out_shape=jax.ShapeDtypeStruct((b, h, s, dh), BF16),
        compiler_params=_params("parallel", "parallel", "parallel"),
        name="sb_attention",
    )(q, k, v)


def _gelu_tanh(x):
    return 0.5 * x * (1.0 + jnp.tanh(math.sqrt(2.0 / math.pi) * (x + 0.044715 * (x * x * x))))


def _lru_kernel(xr_ref, gr_ref, cw_ref, cb_ref, wr_ref, br_ref, wi_ref, bi_ref, lam_ref, y_ref,
                xbuf, hprev, *, ts):
    @pl.when(pl.program_id(1) == 0)
    def _():
        xbuf[0:SUBLANES, :] = jnp.zeros((SUBLANES, LRU_WIDTH), F32)
        hprev[...] = jnp.zeros_like(hprev)

    xbuf[SUBLANES:SUBLANES + ts, :] = xr_ref[...]
    xc = cb_ref[...] + cw_ref[CONV_WIDTH - 1:CONV_WIDTH, :] * xbuf[SUBLANES:SUBLANES + ts, :]
    for kk in range(CONV_WIDTH - 1):
        off = SUBLANES - (CONV_WIDTH - 1) + kk
        xc = xc + cw_ref[kk:kk + 1, :] * xbuf[off:off + ts, :]
    xbuf[0:SUBLANES, :] = xbuf[ts:ts + SUBLANES, :]

    xcb = xc.astype(BF16)
    r = _sigmoid(jnp.dot(xcb, wr_ref[...], preferred_element_type=F32) + br_ref[...])
    gi = _sigmoid(jnp.dot(xcb, wi_ref[...], preferred_element_type=F32) + bi_ref[...])
    log_a = (-LRU_C) * r * _softplus(-lam_ref[...])
    a = jnp.exp(log_a)
    u = jnp.sqrt(1.0 - a * a) * (gi * xc)

    row = lax.broadcasted_iota(jnp.int32, (ts, LRU_WIDTH), 0)
    d = 1
    while d < ts:
        if d < SUBLANES:
            keep = row >= d
            a_sh = jnp.where(keep, pltpu.roll(a, d, axis=0), 1.0)
            u_sh = jnp.where(keep, pltpu.roll(u, d, axis=0), 0.0)
            u = a * u_sh + u
            a = a * a_sh
        else:
            u = jnp.concatenate([u[:d], a[d:] * u[:ts - d] + u[d:]], axis=0)
            a = jnp.concatenate([a[:d], a[d:] * a[:ts - d]], axis=0)
        d *= 2
    h = a * hprev[0:1, :] + u
    hprev[...] = jnp.broadcast_to(h[ts - 1:ts, :], hprev.shape)
    y_ref[...] = (_gelu_tanh(gr_ref[...]) * h).astype(y_ref.dtype)


def _lru(xr, gr, conv_w, conv_b, wr_bd, b_r, wi_bd, b_i, lam, ts):
    b, s, w = xr.shape
    seq_spec = pl.BlockSpec((None, ts, w), lambda bi, i: (bi, i, 0))

    def full(shape):
        return pl.BlockSpec(shape, lambda bi, i: (0,) * len(shape))

    return pl.pallas_call(
        functools.partial(_lru_kernel, ts=ts),
        grid=(b, s // ts),
        in_specs=[seq_spec, seq_spec, full((CONV_WIDTH, w)), full((1, w)), full((w, w)), full((1, w)),
                  full((w, w)), full((1, w)), full((1, w))],
        out_specs=seq_spec,
        out_shape=jax.ShapeDtypeStruct((b, s, w), BF16),
        scratch_shapes=[pltpu.VMEM((ts + 2 * SUBLANES, w), F32), pltpu.VMEM((SUBLANES, w), F32)],
        compiler_params=_params("parallel", "arbitrary"),
        name="rg_lru",
    )(xr, gr, conv_w, conv_b.reshape(1, w), wr_bd, b_r.reshape(1, w), wi_bd, b_i.reshape(1, w),
      lam.reshape(1, w))


def _block_diag(w):
    n, c, d = w.shape
    eye = jnp.eye(n, dtype=w.dtype)
    return (eye[:, None, :, None] * w[:, :, None, :]).reshape(n * c, n * d)


def _proj_rope_kernel(x_ref, pos_ref, freq_ref, w_ref, q_ref, k_ref, vt_ref):
    tm = x_ref.shape[0]
    xb = x_ref[...].astype(BF16)
    ang_t = freq_ref[...] * pos_ref[...].astype(F32)
    reps = LANES // (HEAD_DIM // 2)
    cos = jnp.concatenate([jnp.cos(ang_t)] * reps, axis=0).T
    sin = jnp.concatenate([jnp.sin(ang_t)] * reps, axis=0).T
    lane = lax.broadcasted_iota(jnp.int32, (tm, LANES), 1)
    first_half = (lane % HEAD_DIM) < (HEAD_DIM // 2)
    heads_per_slab = LANES // HEAD_DIM

    def rope(r):
        upper = pltpu.roll(r, LANES - HEAD_DIM // 2, axis=1)
        lower = pltpu.roll(r, HEAD_DIM // 2, axis=1)
        return r * cos + jnp.where(first_half, -upper, lower) * sin

    def emit(ref, n_heads, col0, rotary, scale):
        for slab in range(n_heads // heads_per_slab):
            c0 = col0 + slab * LANES
            r = jnp.dot(xb, w_ref[:, c0:c0 + LANES], preferred_element_type=F32)
            if rotary:
                r = rope(r)
            if scale is not None:
                r = r * scale
            for j in range(heads_per_slab):
                ref[slab * heads_per_slab + j] = r[:, j * HEAD_DIM:(j + 1) * HEAD_DIM].astype(BF16)

    emit(q_ref, SWA_HEADS, 0, True, Q_SCALE)
    emit(k_ref, SWA_KV_HEADS, SWA_HEADS * HEAD_DIM, True, None)
    v0 = (SWA_HEADS + SWA_KV_HEADS) * HEAD_DIM
    for slab in range(SWA_KV_HEADS // heads_per_slab):
        r = jnp.dot(xb, w_ref[:, v0 + slab * LANES:v0 + (slab + 1) * LANES], preferred_element_type=F32)
        rt = r.T
        for j in range(heads_per_slab):
            vt_ref[slab * heads_per_slab + j] = rt[j * HEAD_DIM:(j + 1) * HEAD_DIM, :].astype(BF16)


def _proj_rope(x, positions, w_bf16, tm):
    b, s, d = x.shape
    n = w_bf16.shape[1]
    half = HEAD_DIM // 2
    inv_freq = (ROPE_THETA ** (-jnp.arange(half, dtype=F32) / half)).reshape(half, 1)

    def heads(nh):
        return (jax.ShapeDtypeStruct((b, nh, s, HEAD_DIM), BF16),
                pl.BlockSpec((None, nh, tm, HEAD_DIM), lambda bi, i: (bi, 0, i, 0)))

    (qs, qspec), (ks, kspec) = heads(SWA_HEADS), heads(SWA_KV_HEADS)
    vs = jax.ShapeDtypeStruct((b, SWA_KV_HEADS, HEAD_DIM, s), BF16)
    vspec = pl.BlockSpec((None, SWA_KV_HEADS, HEAD_DIM, tm), lambda bi, i: (bi, 0, 0, i))
    return pl.pallas_call(
        _proj_rope_kernel,
        grid=(b, s // tm),
        in_specs=[pl.BlockSpec((None, tm, d), lambda bi, i: (bi, i, 0)),
                  pl.BlockSpec((None, 1, tm), lambda bi, i: (bi, 0, i)),
                  pl.BlockSpec((half, 1), lambda bi, i: (0, 0)),
                  pl.BlockSpec((d, n), lambda bi, i: (0, 0))],
        out_specs=[qspec, kspec, vspec],
        out_shape=[qs, ks, vs],
        compiler_params=_params("parallel", "parallel"),
        name="proj_rope",
    )(x, positions.reshape(b, 1, s), inv_freq, w_bf16)


def _reduce_rows(x, op):
    while x.shape[0] > SUBLANES:
        half = x.shape[0] // 2
        x = op(x[:half], x[half:])
    for shift in (4, 2, 1):
        x = op(x, pltpu.roll(x, shift, axis=0))
    return x[0:1]


SWA_BLOCKS_PER_STEP = 4


def _swa_kernel(q_ref, kp_ref, kc_ref, vtp_ref, vtc_ref, sink_ref, o_ref):
    i = pl.program_id(1)
    w = SWA_WINDOW
    key = lax.broadcasted_iota(jnp.int32, (2 * w, w), 0)
    qry = lax.broadcasted_iota(jnp.int32, (2 * w, w), 1)
    dist = qry + w - key
    band = (dist >= 0) & (dist < w)
    for blk in range(SWA_BLOCKS_PER_STEP):
        visible = band if blk else band & ((key >= w) | (i > 0))
        bias = jnp.concatenate([jnp.where(visible, 0.0, NEG_BIG)] * SWA_GROUP, axis=1)
        outs = []
        for kv in range(SWA_KV_HEADS):
            if blk:
                kk = kc_ref[kv, (blk - 1) * w:(blk + 1) * w, :]
                vvt = vtc_ref[kv, :, (blk - 1) * w:(blk + 1) * w]
            else:
                kk = jnp.concatenate([kp_ref[kv], kc_ref[kv, 0:w, :]], axis=0)
                vvt = jnp.concatenate([vtp_ref[kv], vtc_ref[kv, :, 0:w]], axis=1)
            qg = jnp.concatenate([q_ref[kv * SWA_GROUP + g, blk * w:(blk + 1) * w, :]
                                  for g in range(SWA_GROUP)], axis=0)
            st = lax.dot_general(kk, qg, (((1,), (1,)), ((), ())), preferred_element_type=F32) + bias
            sink = sink_ref[kv:kv + 1, :]
            m = jnp.maximum(_reduce_rows(st, jnp.maximum), sink)
            p = jnp.exp(st - m)
            denom = _reduce_rows(p, jnp.add) + jnp.exp(sink - m)
            ot = jnp.dot(vvt, p.astype(BF16), preferred_element_type=F32) / denom
            outs.extend(ot[:, g * w:(g + 1) * w] for g in range(SWA_GROUP))
        o_ref[blk * w:(blk + 1) * w, :] = jnp.concatenate(outs, axis=0).T.astype(o_ref.dtype)


def _swa(q, k, vt, sinks):
    b, nh, s, dh = q.shape
    nkv = k.shape[1]
    w = SWA_WINDOW
    n = SWA_BLOCKS_PER_STEP
    assert s % (n * w) == 0
    cur = pl.BlockSpec((None, nkv, n * w, dh), lambda bi, i: (bi, 0, i, 0))
    prev = pl.BlockSpec((None, nkv, w, dh), lambda bi, i: (bi, 0, jnp.maximum(n * i - 1, 0), 0))
    cur_t = pl.BlockSpec((None, nkv, dh, n * w), lambda bi, i: (bi, 0, 0, i))
    prev_t = pl.BlockSpec((None, nkv, dh, w), lambda bi, i: (bi, 0, 0, jnp.maximum(n * i - 1, 0)))
    sink_tile = jnp.repeat(sinks.astype(F32).reshape(nkv, nh // nkv), w, axis=1)
    return pl.pallas_call(
        _swa_kernel,
        grid=(b, s // (n * w)),
        in_specs=[pl.BlockSpec((None, nh, n * w, dh), lambda bi, i: (bi, 0, i, 0)),
                  prev, cur, prev_t, cur_t,
                  pl.BlockSpec(sink_tile.shape, lambda bi, i: (0, 0))],
        out_specs=pl.BlockSpec((None, n * w, nh * dh), lambda bi, i: (bi, i, 0)),
        out_shape=jax.ShapeDtypeStruct((b, s, nh * dh), BF16),
        compiler_params=_params("parallel", "parallel"),
        name="swa",
    )(q, k, k, vt, vt, sink_tile)


PAIRS_PER_GROUP = GROUP_SIZE * (GROUP_SIZE - 1) // 2
N_BUCKETS = N_GROUPS * PAIRS_PER_GROUP
ROUTE_ROWS = SUBLANES


def _route(logits_t):
    rows = [logits_t[e:e + 1, :] for e in range(N_EXPERTS)]
    mx = functools.reduce(jnp.maximum, rows)
    ex = [jnp.exp(r - mx) for r in rows]
    total = functools.reduce(lambda p, q: p + q, ex)
    probs = [e / total for e in ex]

    group_score = []
    for g in range(N_GROUPS):
        a, b, c, d = probs[g * GROUP_SIZE:(g + 1) * GROUP_SIZE]
        hi1, lo1 = jnp.maximum(a, b), jnp.minimum(a, b)
        hi2, lo2 = jnp.maximum(c, d), jnp.minimum(c, d)
        top1 = jnp.maximum(hi1, hi2)
        top2 = jnp.maximum(jnp.minimum(hi1, hi2), jnp.maximum(lo1, lo2))
        group_score.append(top1 + top2)
    best = functools.reduce(jnp.maximum, group_score)
    g_sel = jnp.full(best.shape, N_GROUPS - 1, jnp.int32)
    for g in range(N_GROUPS - 2, -1, -1):
        g_sel = jnp.where(group_score[g] == best, g, g_sel)

    in_group = []
    for j in range(GROUP_SIZE):
        val = probs[(N_GROUPS - 1) * GROUP_SIZE + j]
        for g in range(N_GROUPS - 2, -1, -1):
            val = jnp.where(g_sel == g, probs[g * GROUP_SIZE + j], val)
        in_group.append(val)

    def first_argmax(vals):
        m = functools.reduce(jnp.maximum, vals)
        idx = jnp.full(m.shape, GROUP_SIZE - 1, jnp.int32)
        for j in range(GROUP_SIZE - 2, -1, -1):
            idx = jnp.where(vals[j] == m, j, idx)
        return m, idx

    w1, i1 = first_argmax(in_group)
    rest = [jnp.where(i1 == j, -1.0, in_group[j]) for j in range(GROUP_SIZE)]
    w2, i2 = first_argmax(rest)
    norm = w1 + w2
    first_is_lo = i1 < i2
    i_lo = jnp.minimum(i1, i2)
    i_hi = jnp.maximum(i1, i2)
    pair = jnp.where(i_lo == 0, i_hi - 1, jnp.where(i_lo == 1, i_hi + 1, PAIRS_PER_GROUP - 1))
    bucket = (g_sel * PAIRS_PER_GROUP + pair).astype(F32)
    w_lo = jnp.where(first_is_lo, w1, w2) / norm
    w_hi = jnp.where(first_is_lo, w2, w1) / norm
    return bucket, w_lo, w_hi


def _mix_out_kernel(*refs, n_head_major, alpha):
    x_ref = refs[0]
    y_refs = refs[1:1 + len(n_head_major)]
    w_refs = refs[1 + len(n_head_major):1 + 2 * len(n_head_major)]
    g_ref, b_ref, wrh_ref, wrl_ref, brt_ref, x1e_ref, route_ref = refs[1 + 2 * len(n_head_major):]
    tm, d = x_ref.shape
    h = alpha * x_ref[...]
    for y_ref, w_ref, nh in zip(y_refs, w_refs, n_head_major):
        if nh:
            y = jnp.concatenate([y_ref[j] for j in range(nh)], axis=-1)
        else:
            y = y_ref[...]
        h = h + jnp.dot(y, w_ref[...], preferred_element_type=F32)
    x1 = _layer_norm(h, g_ref[...], b_ref[...])
    x_hi = x1.astype(BF16)
    x_lo = (x1 - x_hi.astype(F32)).astype(BF16)

    def nt_dot(w, xv):
        return lax.dot_general(w, xv, (((1,), (1,)), ((), ())), preferred_element_type=F32)

    logits_t = (nt_dot(wrh_ref[...], x_hi) + nt_dot(wrh_ref[...], x_lo) + nt_dot(wrl_ref[...], x_hi)
                + brt_ref[...])
    route = jnp.concatenate(list(_route(logits_t)) + [jnp.zeros((ROUTE_ROWS - 3, tm), F32)], axis=0)
    route_ref[...] = route
    x1e_ref[:, 0:d] = x1
    x1e_ref[:, d:d + LANES] = jnp.concatenate([route, jnp.zeros((LANES - ROUTE_ROWS, tm), F32)], axis=0).T


def _mix_out(x, ys, ws, ln_g, ln_b, w_router, b_router, alpha, tm):
    b, s, d = x.shape
    n_head_major = tuple(y.shape[1] if hm else 0 for y, hm in ys)
    y_specs = []
    for (y, hm) in ys:
        if hm:
            y_specs.append(pl.BlockSpec((None, y.shape[1], tm, y.shape[3]), lambda bi, i: (bi, 0, i, 0)))
        else:
            y_specs.append(pl.BlockSpec((None, tm, y.shape[2]), lambda bi, i: (bi, i, 0)))
    w_specs = [pl.BlockSpec(w.shape, lambda bi, i: (0, 0)) for w in ws]
    row = pl.BlockSpec((1, d), lambda bi, i: (0, 0))
    tok = pl.BlockSpec((None, tm, d), lambda bi, i: (bi, i, 0))
    wr_t = w_router.T.astype(F32)
    wr_hi = wr_t.astype(BF16)
    wr_lo = (wr_t - wr_hi.astype(F32)).astype(BF16)
    wr_spec = pl.BlockSpec((N_EXPERTS, d), lambda bi, i: (0, 0))
    return pl.pallas_call(
        functools.partial(_mix_out_kernel, n_head_major=n_head_major, alpha=alpha),
        grid=(b, s // tm),
        in_specs=[tok] + y_specs + w_specs + [row, row, wr_spec, wr_spec,
                  pl.BlockSpec((N_EXPERTS, 1), lambda bi, i: (0, 0))],
        out_specs=[pl.BlockSpec((None, tm, d + LANES), lambda bi, i: (bi, i, 0)),
                   pl.BlockSpec((None, ROUTE_ROWS, tm), lambda bi, i: (bi, 0, i))],
        out_shape=[jax.ShapeDtypeStruct((b, s, d + LANES), F32),
                   jax.ShapeDtypeStruct((b, ROUTE_ROWS, s), F32)],
        compiler_params=_params("parallel", "parallel"),
        name="mix_out_ln_router",
    )(x, *[y for y, _ in ys], *ws, ln_g.reshape(1, d), ln_b.reshape(1, d),
      wr_hi, wr_lo, b_router.astype(F32).reshape(N_EXPERTS, 1))


MOE_TILE = 256
PLAN_COLS = 256
META_ROWS = SUBLANES


def _num_moe_tiles(t):
    return t // MOE_TILE + N_BUCKETS


def _plan_kernel(bid_ref, pos_ref, meta_ref):
    r, c = bid_ref.shape
    bid = bid_ref[...]
    before = (lax.broadcasted_iota(jnp.int32, (c, c), 0)
              < lax.broadcasted_iota(jnp.int32, (c, c), 1)).astype(BF16)
    rows_before = (lax.broadcasted_iota(jnp.int32, (r, r), 1)
                   < lax.broadcasted_iota(jnp.int32, (r, r), 0)).astype(BF16)
    lane = lax.broadcasted_iota(jnp.int32, (1, LANES), 1)
    tile_start = lane.astype(F32) * MOE_TILE

    def body(b, state):
        base, pos, tile_bucket, last_tile = state
        ind = (bid == lax.convert_element_type(b, F32)).astype(F32)
        within = jnp.dot(ind.astype(BF16), before, preferred_element_type=F32)
        row_total = jnp.sum(ind, axis=1, keepdims=True)
        row_off = jnp.dot(rows_before, jnp.broadcast_to(row_total, (r, LANES)).astype(BF16),
                          preferred_element_type=F32)[:, 0:1]
        count = jnp.sum(row_total, axis=0, keepdims=True)
        padded = jnp.floor((count + (MOE_TILE - 1)) * (1.0 / MOE_TILE)) * MOE_TILE
        pos = pos + ind * (base + row_off + within)
        end = base + padded
        tile_bucket = tile_bucket + (tile_start >= end).astype(F32)
        last_row = jnp.where(count > 0.0, end - MOE_TILE, -1.0)
        last_tile = jnp.where(lane == b, last_row, last_tile)
        return end, pos, tile_bucket, last_tile

    total, pos, tile_bucket, last_tile = lax.fori_loop(
        0, N_BUCKETS, body, (jnp.zeros((1, 1), F32), jnp.zeros((r, c), F32), jnp.zeros((1, LANES), F32),
                             jnp.full((1, LANES), -1.0, F32)))
    pos_ref[...] = pos.astype(jnp.int32)

    tb = jnp.minimum(tile_bucket, N_BUCKETS - 1.0)
    group = sum((tb >= g * PAIRS_PER_GROUP).astype(F32) for g in range(1, N_GROUPS))
    pair = tb - group * PAIRS_PER_GROUP
    i_lo = jnp.where(pair < 3, 0.0, jnp.where(pair < 5, 1.0, 2.0))
    i_hi = jnp.where(pair < 3, pair + 1.0, jnp.where(pair < 5, pair - 1.0, 3.0))
    meta = jnp.concatenate([group * GROUP_SIZE + i_lo, group * GROUP_SIZE + i_hi,
                            (tile_start < total).astype(F32),
                            jnp.where(lane == N_BUCKETS, total, last_tile),
                            jnp.zeros((META_ROWS - 4, LANES), F32)], axis=0)
    meta_ref[...] = meta.astype(jnp.int32)


def _plan(bucket_ids):
    r, c = bucket_ids.shape
    return pl.pallas_call(
        _plan_kernel,
        out_shape=[jax.ShapeDtypeStruct((r, c), jnp.int32), jax.ShapeDtypeStruct((META_ROWS, LANES), jnp.int32)],
        compiler_params=pltpu.CompilerParams(vmem_limit_bytes=VMEM_LIMIT_BYTES),
        name="moe_plan",
    )(bucket_ids)


DMA_PRIORITIES = 2


def _row_copy(src_ref, src_row, dst_ref, dst_row, sem):
    return pltpu.make_async_copy(src_ref.at[pl.ds(src_row, 1)], dst_ref.at[pl.ds(dst_row, 1)], sem)


def _dispatch_kernel(pos_ref, last_ref, x_ref, o_ref, zeros, sem, zsem, *, first_spare):
    tm = x_ref.shape[0]
    i = pl.program_id(0)

    @pl.when(i == 0)
    def _():
        zeros[...] = jnp.zeros_like(zeros)
        rows_in_use = last_ref[N_BUCKETS]
        clears = [(last_ref[b] >= 0, last_ref[b]) for b in range(N_BUCKETS)]
        clears += [(k * MOE_TILE >= rows_in_use, k * MOE_TILE)
                   for k in range(first_spare, o_ref.shape[0] // MOE_TILE)]

        def clear(row):
            start = row if isinstance(row, int) else pl.multiple_of(row, MOE_TILE)
            return pltpu.make_async_copy(zeros, o_ref.at[pl.ds(start, MOE_TILE)], zsem)

        for needed, row in clears:
            @pl.when(needed)
            def _():
                clear(row).start()
        for needed, row in clears:
            @pl.when(needed)
            def _():
                clear(row).wait()

    base = i * tm
    for r in range(tm):
        _row_copy(x_ref, r, o_ref, pos_ref[base + r], sem).start(priority=r % DMA_PRIORITIES)
    pltpu.make_async_copy(x_ref, o_ref.at[pl.ds(0, tm)], sem).wait()


def _dispatch(pos, last_tile_rows, xt, n_rows, tm):
    t = xt.shape[0]
    tile = xt.shape[1:]
    return pl.pallas_call(
        functools.partial(_dispatch_kernel, first_spare=t // MOE_TILE),
        grid_spec=pltpu.PrefetchScalarGridSpec(
            num_scalar_prefetch=2,
            grid=(t // tm,),
            in_specs=[pl.BlockSpec((tm,) + tile, lambda i, pos_ref, last_ref: (i,) + (0,) * len(tile))],
            out_specs=pl.BlockSpec(memory_space=pl.ANY),
            scratch_shapes=[pltpu.VMEM((MOE_TILE,) + tile, xt.dtype), pltpu.SemaphoreType.DMA(()),
                            pltpu.SemaphoreType.DMA(())]),
        out_shape=jax.ShapeDtypeStruct((n_rows,) + tile, xt.dtype),
        compiler_params=_params("arbitrary"),
        name="moe_dispatch",
    )(pos, last_tile_rows, xt)


def _moe_tile_kernel(elo_ref, ehi_ref, valid_ref, x_ref, wgl_ref, wul_ref, wdl_ref, wgh_ref, wuh_ref,
                     wdh_ref, o_ref, wgl_b, wul_b, wdl_b, wgh_b, wuh_b, wdh_b):
    d = wgl_ref.shape[0]
    k = pl.program_id(0)
    prev = jnp.maximum(k - 1, 0)
    valid = valid_ref[k] != 0

    def refresh(e_ref, srcs, dsts):
        @pl.when(valid & ((k == 0) | (e_ref[k] != e_ref[prev])))
        def _():
            for src, dst in zip(srcs, dsts):
                dst[...] = src[...].astype(BF16)

    refresh(elo_ref, (wgl_ref, wul_ref, wdl_ref), (wgl_b, wul_b, wdl_b))
    refresh(ehi_ref, (wgh_ref, wuh_ref, wdh_ref), (wgh_b, wuh_b, wdh_b))

    @pl.when(jnp.logical_not(valid))
    def _():
        o_ref[...] = jnp.zeros_like(o_ref)

    @pl.when(valid)
    def _():
        x = x_ref[:, 0:d].astype(BF16)
        acc = None
        for wg_b, wu_b, wd_b, lane in ((wgl_b, wul_b, wdl_b, d + 1), (wgh_b, wuh_b, wdh_b, d + 2)):
            weight = x_ref[:, lane:lane + 1]
            hg = jnp.dot(x, wg_b[...], preferred_element_type=F32)
            hu = jnp.dot(x, wu_b[...], preferred_element_type=F32)
            hidden = (hg * _sigmoid(hg)) * hu * weight
            y = jnp.dot(hidden.astype(BF16), wd_b[...], preferred_element_type=F32)
            acc = y if acc is None else acc + y
        o_ref[...] = acc


def _moe_tiles(e_lo, e_hi, valid, xs, wg, wu, wd, layer):
    n_rows, dw = xs.shape
    _, _, d, f = wg.shape
    up_lo = pl.BlockSpec((None, None, d, f), lambda k, lo, hi, ok: (layer, lo[k], 0, 0))
    up_hi = pl.BlockSpec((None, None, d, f), lambda k, lo, hi, ok: (layer, hi[k], 0, 0))
    down_lo = pl.BlockSpec((None, None, f, d), lambda k, lo, hi, ok: (layer, lo[k], 0, 0))
    down_hi = pl.BlockSpec((None, None, f, d), lambda k, lo, hi, ok: (layer, hi[k], 0, 0))
    up_b = pltpu.VMEM((d, f), BF16)
    down_b = pltpu.VMEM((f, d), BF16)
    return pl.pallas_call(
        _moe_tile_kernel,
        grid_spec=pltpu.PrefetchScalarGridSpec(
            num_scalar_prefetch=3,
            grid=(n_rows // MOE_TILE,),
            in_specs=[pl.BlockSpec((MOE_TILE, dw), lambda k, lo, hi, ok: (k, 0)),
                      up_lo, up_lo, down_lo, up_hi, up_hi, down_hi],
            out_specs=pl.BlockSpec((MOE_TILE, d), lambda k, lo, hi, ok: (k, 0)),
            scratch_shapes=[up_b, up_b, down_b, up_b, up_b, down_b]),
        out_shape=jax.ShapeDtypeStruct((n_rows, d), F32),
        compiler_params=_params("arbitrary"),
        name="moe_tiles",
    )(e_lo, e_hi, valid, xs, wg, wu, wd, wg, wu, wd)


def _ln_ple_kernel(pos_ref, x1_ref, ys_ref, p_ref, g_ref, b_ref, wg_ref, bg_ref, wp_ref, o_ref,
                   m_even, m_odd, sem_even, sem_odd, *, alpha):
    tm = x1_ref.shape[0]
    i = pl.program_id(0)
    last = pl.num_programs(0) - 1

    def wait(buf, sem):
        pltpu.make_async_copy(ys_ref.at[pl.ds(0, tm)], buf, sem).wait()

    @pl.when(i == 0)
    def _():
        def issue(r, carry):
            _row_copy(ys_ref, pos_ref[r], m_even, r, sem_even).start()
            return carry

        lax.fori_loop(0, tm, issue, 0, unroll=8)

    def step(cur, cur_sem, nxt, nxt_sem):
        wait(cur, cur_sem)
        base = jnp.minimum(i + 1, last) * tm
        for r in range(tm):
            _row_copy(ys_ref, pos_ref[base + r], nxt, r, nxt_sem).start(priority=r % DMA_PRIORITIES)
        emb = jnp.dot(p_ref[...].astype(BF16), wp_ref[...], preferred_element_type=F32)
        x2 = _layer_norm(alpha * x1_ref[...] + cur[...], g_ref[...], b_ref[...])
        gate = _sigmoid(jnp.dot(x2.astype(BF16), wg_ref[...], preferred_element_type=F32) + bg_ref[...])
        o_ref[...] = x2 + gate * emb

        @pl.when(i == last)
        def _():
            wait(nxt, nxt_sem)

    @pl.when(i % 2 == 0)
    def _():
        step(m_even, sem_even, m_odd, sem_odd)

    @pl.when(i % 2 == 1)
    def _():
        step(m_odd, sem_odd, m_even, sem_even)


def _ln_ple(pos, x1e, ys, p, layer, ln_g, ln_b, wg, bg, wp, alpha, tm):
    t = x1e.shape[0]
    d = ys.shape[1]
    pd = p.shape[2]
    tok = pl.BlockSpec((tm, d), lambda i, pos_ref: (i, 0))
    row = pl.BlockSpec((1, d), lambda i, pos_ref: (0, 0))
    return pl.pallas_call(
        functools.partial(_ln_ple_kernel, alpha=alpha),
        grid_spec=pltpu.PrefetchScalarGridSpec(
            num_scalar_prefetch=1,
            grid=(t // tm,),
            in_specs=[tok, pl.BlockSpec(memory_space=pl.ANY),
                      pl.BlockSpec((None, tm, pd), lambda i, pos_ref: (layer, i, 0)), row, row,
                      pl.BlockSpec((d, d), lambda i, pos_ref: (0, 0)), row,
                      pl.BlockSpec((pd, d), lambda i, pos_ref: (0, 0))],
            out_specs=tok,
            scratch_shapes=[pltpu.VMEM((tm,) + ys.shape[1:], F32), pltpu.VMEM((tm,) + ys.shape[1:], F32),
                            pltpu.SemaphoreType.DMA(()), pltpu.SemaphoreType.DMA(())]),
        out_shape=jax.ShapeDtypeStruct((t, d), F32),
        compiler_params=_params("arbitrary"),
        name="ln_ple",
    )(pos, x1e, ys, p, ln_g.reshape(1, d), ln_b.reshape(1, d), wg, bg.reshape(1, d), wp)


def _pick_tile(n, target):
    t = min(n, target)
    while n % t:
        t //= 2
    return t


def kernel(x, p, positions, w_in_ab, w_out_ab, conv_w, conv_b, lru_w_r, lru_b_r, lru_w_i, lru_b_i, lru_lambda, w_qkv_c, w_out_c, sinks_c, ln_mix_g, ln_mix_b, ln_ffn_g, ln_ffn_b, w_router, b_router, exp_w_gate, exp_w_up, exp_w_down, ple_w_proj, ple_w_gate, ple_b_gate):
    b, s, d = x.shape
    depth = p.shape[0]
    t = b * s
    alpha = (2 * depth) ** 0.25
    tm = _pick_tile(s, 1024)
    assert t % MOE_TILE == 0 and t % PLAN_COLS == 0
    n_tiles = _num_moe_tiles(t)
    assert n_tiles <= LANES
    for i in range(depth):
        j = i // 2
        if i % 2 == 0:
            q, k, v, xr, gr = _proj_ab(x, w_in_ab[j].astype(BF16), tm)
            y_sb = _sb_attention(q, k, v, _pick_tile(s, 256), SB_HEADS)
            y_lru = _lru(xr, gr, conv_w[j], conv_b[j], _block_diag(lru_w_r[j]).astype(BF16), lru_b_r[j],
                         _block_diag(lru_w_i[j]).astype(BF16), lru_b_i[j], lru_lambda[j],
                         _pick_tile(s, 256))
            w_out = w_out_ab[j].astype(BF16)
            ys = [(y_sb, True), (y_lru, False)]
            ws = [w_out[:SB_WIDTH], w_out[SB_WIDTH:]]
        else:
            q, k, v = _proj_rope(x, positions, w_qkv_c[j].astype(BF16), tm)
            y = _swa(q, k, v, sinks_c[j])
            ys = [(y, False)]
            ws = [w_out_c[j].astype(BF16)]
        x1e, route = _mix_out(x, ys, ws, ln_mix_g[i], ln_mix_b[i], w_router, b_router, alpha, tm)
        x1e = x1e.reshape(t, d + LANES)
        pos, meta = _plan(route[:, 0, :].reshape(t // PLAN_COLS, PLAN_COLS))
        pos = pos.reshape(t)
        xs = _dispatch(pos, meta[3, :N_BUCKETS + 1], x1e, n_tiles * MOE_TILE, _pick_tile(t, 512))
        ys_moe = _moe_tiles(meta[0, :n_tiles], meta[1, :n_tiles], meta[2, :n_tiles], xs,
                            exp_w_gate, exp_w_up, exp_w_down, i)
        x = _ln_ple(pos, x1e, ys_moe, p.reshape(depth, t, -1), i, ln_ffn_g[i], ln_ffn_b[i],
                    ple_w_gate[i].astype(BF16), ple_b_gate[i], ple_w_proj[i].astype(BF16), alpha,
                    _pick_tile(t, 512)).reshape(b, s, d)
    return x
```

```python
import functools
import math

import jax
import jax.numpy as jnp
from jax import lax
from jax.experimental import pallas as pl
from jax.experimental.pallas import tpu as pltpu

HEAD_DIM = 64
SB_HEADS = 8
SB_WIDTH = SB_HEADS * HEAD_DIM
LRU_WIDTH = 512
LRU_BLOCKS = 8
LRU_C = 8.0
CONV_WIDTH = 4
SWA_HEADS = 16
SWA_KV_HEADS = 4
SWA_GROUP = SWA_HEADS // SWA_KV_HEADS
SWA_WINDOW = 128
ROPE_THETA = 10000.0
N_EXPERTS = 16
N_GROUPS = 4
GROUP_SIZE = N_EXPERTS // N_GROUPS
LN_EPS = 1e-5
Q_SCALE = HEAD_DIM ** -0.5

LANES = 128
SUBLANES = 8
VMEM_LIMIT_BYTES = 48 * 1024 * 1024

NEG_BIG = -1e30

BF16 = jnp.bfloat16
F32 = jnp.float32


def _params(*semantics):
    return pltpu.CompilerParams(dimension_semantics=semantics, vmem_limit_bytes=VMEM_LIMIT_BYTES)


def _softplus(z):
    return jnp.maximum(z, 0.0) + jnp.log(1.0 + jnp.exp(-jnp.abs(z)))


def _sigmoid(z):
    return 1.0 / (1.0 + jnp.exp(-z))


def _layer_norm(y, g, b):
    mu = jnp.mean(y, axis=-1, keepdims=True)
    d = y - mu
    var = jnp.mean(d * d, axis=-1, keepdims=True)
    return d * lax.rsqrt(var + LN_EPS) * g + b


def _proj_ab_kernel(x_ref, w_ref, q_ref, k_ref, v_ref, xr_ref, gr_ref):
    xb = x_ref[...].astype(BF16)

    def chunk(c):
        return jnp.dot(xb, w_ref[:, c * SB_WIDTH:(c + 1) * SB_WIDTH], preferred_element_type=F32)

    for c, (ref, scale) in enumerate(((q_ref, Q_SCALE), (k_ref, None), (v_ref, None))):
        r = chunk(c)
        if scale is not None:
            r = r * scale
        for h in range(SB_HEADS):
            ref[h] = r[:, h * HEAD_DIM:(h + 1) * HEAD_DIM].astype(BF16)
    xr_ref[...] = chunk(3)
    gr_ref[...] = chunk(4)


def _proj_ab(x, w_bf16, tm):
    b, s, d = x.shape
    n = w_bf16.shape[1]
    heads = jax.ShapeDtypeStruct((b, SB_HEADS, s, HEAD_DIM), BF16)
    flat = jax.ShapeDtypeStruct((b, s, LRU_WIDTH), F32)
    head_spec = pl.BlockSpec((None, SB_HEADS, tm, HEAD_DIM), lambda bi, i: (bi, 0, i, 0))
    flat_spec = pl.BlockSpec((None, tm, LRU_WIDTH), lambda bi, i: (bi, i, 0))
    return pl.pallas_call(
        _proj_ab_kernel,
        grid=(b, s // tm),
        in_specs=[pl.BlockSpec((None, tm, d), lambda bi, i: (bi, i, 0)),
                  pl.BlockSpec((d, n), lambda bi, i: (0, 0))],
        out_specs=[head_spec, head_spec, head_spec, flat_spec, flat_spec],
        out_shape=[heads, heads, heads, flat, flat],
        compiler_params=_params("parallel", "parallel"),
        name="proj_ab",
    )(x, w_bf16)


SB_DEAD_LOG_WEIGHT = -105.0
SB_MERGED_BLOCKS = 2


def _sb_attn_kernel(q_ref, k_ref, v_ref, o_ref, *, tq, hp):
    i = pl.program_id(2)
    row = lax.broadcasted_iota(jnp.int32, (tq, tq), 0)
    col = lax.broadcasted_iota(jnp.int32, (tq, tq), 1)
    later = (row > col).astype(BF16)
    causal = col < row

    def block(jb, carries, accs, masked):
        start = pl.multiple_of(jb * tq, tq)
        new_carries, new_accs = [], []
        for h in range(hp):
            kj = k_ref[h, pl.ds(start, tq), :]
            vj = v_ref[h, pl.ds(start, tq), :]
            z = lax.dot_general(q_ref[h], kj, (((1,), (1,)), ((), ())), preferred_element_type=F32)
            sp = _softplus(z)
            log_keep = -sp
            if masked:
                log_keep = jnp.where(causal, log_keep, 0.0)
            after = jnp.dot(log_keep.astype(BF16), later, preferred_element_type=F32)
            w = jnp.exp((z - sp) + after + carries[h])
            if masked:
                w = jnp.where(causal, w, 0.0)
            new_accs.append(accs[h] + jnp.dot(w.astype(BF16), vj, preferred_element_type=F32))
            new_carries.append(carries[h] + jnp.sum(log_keep, axis=1, keepdims=True))
        return tuple(new_carries), tuple(new_accs)

    def live(carries):
        return functools.reduce(jnp.maximum, [jnp.max(c) for c in carries])

    zero = ((jnp.zeros((tq, 1), F32),) * hp, (jnp.zeros((tq, HEAD_DIM), F32),) * hp)

    def first(n):
        def run():
            state = block(i, *zero, True)
            for j in range(1, n):
                state = block(i - j, *state, False)
            return state
        return run

    done = jnp.minimum(i, SB_MERGED_BLOCKS - 1)
    carries, accs = lax.switch(done, [first(n) for n in range(1, SB_MERGED_BLOCKS + 1)])

    def cond(state):
        return (state[0] < i) & (state[1] > SB_DEAD_LOG_WEIGHT)

    def body(state):
        step, _, carries, accs = state
        carries, accs = block(i - 1 - step, carries, accs, False)
        return step + 1, live(carries), carries, accs

    _, _, _, accs = lax.while_loop(cond, body, (done, live(carries), carries, accs))
    for h in range(hp):
        o_ref[h] = accs[h].astype(o_ref.dtype)


def _sb_attention(q, k, v, tq, hp):
    b, h, s, dh = q.shape
    return pl.pallas_call(
        functools.partial(_sb_attn_kernel, tq=tq, hp=hp),
        grid=(b, h // hp, s // tq),
        in_specs=[pl.BlockSpec((None, hp, tq, dh), lambda bi, hi, i: (bi, hi, i, 0)),
                  pl.BlockSpec((None, hp, s, dh), lambda bi, hi, i: (bi, hi, 0, 0), pipeline_mode=pl.Buffered(1)),
                  pl.BlockSpec((None, hp, s, dh), lambda bi, hi, i: (bi, hi, 0, 0), pipeline_mode=pl.Buffered(1))],
        out_specs=pl.BlockSpec((None, hp, tq, dh), lambda bi, hi, i: (bi, hi, i, 0)),
        out_shape=jax.ShapeDtypeStruct((b, h, s, dh), BF16),
        compiler_params=_params("parallel", "parallel", "parallel"),
        name="sb_attention",
    )(q, k, v)


def _gelu_tanh(x):
    return 0.5 * x * (1.0 + jnp.tanh(math.sqrt(2.0 / math.pi) * (x + 0.044715 * (x * x * x))))


def _lru_kernel(xr_ref, gr_ref, cw_ref, cb_ref, wr_ref, br_ref, wi_ref, bi_ref, lam_ref, y_ref,
                xbuf, hprev, *, ts):
    @pl.when(pl.program_id(1) == 0)
    def _():
        xbuf[0:SUBLANES, :] = jnp.zeros((SUBLANES, LRU_WIDTH), F32)
        hprev[...] = jnp.zeros_like(hprev)

    xbuf[SUBLANES:SUBLANES + ts, :] = xr_ref[...]
    xc = cb_ref[...] + cw_ref[CONV_WIDTH - 1:CONV_WIDTH, :] * xbuf[SUBLANES:SUBLANES + ts, :]
    for kk in range(CONV_WIDTH - 1):
        off = SUBLANES - (CONV_WIDTH - 1) + kk
        xc = xc + cw_ref[kk:kk + 1, :] * xbuf[off:off + ts, :]
    xbuf[0:SUBLANES, :] = xbuf[ts:ts + SUBLANES, :]

    xcb = xc.astype(BF16)
    r = _sigmoid(jnp.dot(xcb, wr_ref[...], preferred_element_type=F32) + br_ref[...])
    gi = _sigmoid(jnp.dot(xcb, wi_ref[...], preferred_element_type=F32) + bi_ref[...])
    log_a = (-LRU_C) * r * _softplus(-lam_ref[...])
    a = jnp.exp(log_a)
    u = jnp.sqrt(1.0 - a * a) * (gi * xc)

    row = lax.broadcasted_iota(jnp.int32, (ts, LRU_WIDTH), 0)
    d = 1
    while d < ts:
        if d < SUBLANES:
            keep = row >= d
            a_sh = jnp.where(keep, pltpu.roll(a, d, axis=0), 1.0)
            u_sh = jnp.where(keep, pltpu.roll(u, d, axis=0), 0.0)
            u = a * u_sh + u
            a = a * a_sh
        else:
            u = jnp.concatenate([u[:d], a[d:] * u[:ts - d] + u[d:]], axis=0)
            a = jnp.concatenate([a[:d], a[d:] * a[:ts - d]], axis=0)
        d *= 2
    h = a * hprev[0:1, :] + u
    hprev[...] = jnp.broadcast_to(h[ts - 1:ts, :], hprev.shape)
    y_ref[...] = (_gelu_tanh(gr_ref[...]) * h).astype(y_ref.dtype)


def _lru(xr, gr, conv_w, conv_b, wr_bd, b_r, wi_bd, b_i, lam, ts):
    b, s, w = xr.shape
    seq_spec = pl.BlockSpec((None, ts, w), lambda bi, i: (bi, i, 0))

    def full(shape):
        return pl.BlockSpec(shape, lambda bi, i: (0,) * len(shape))

    return pl.pallas_call(
        functools.partial(_lru_kernel, ts=ts),
        grid=(b, s // ts),
        in_specs=[seq_spec, seq_spec, full((CONV_WIDTH, w)), full((1, w)), full((w, w)), full((1, w)),
                  full((w, w)), full((1, w)), full((1, w))],
        out_specs=seq_spec,
        out_shape=jax.ShapeDtypeStruct((b, s, w), BF16),
        scratch_shapes=[pltpu.VMEM((ts + 2 * SUBLANES, w), F32), pltpu.VMEM((SUBLANES, w), F32)],
        compiler_params=_params("parallel", "arbitrary"),
        name="rg_lru",
    )(xr, gr, conv_w, conv_b.reshape(1, w), wr_bd, b_r.reshape(1, w), wi_bd, b_i.reshape(1, w),
      lam.reshape(1, w))


def _block_diag(w):
    n, c, d = w.shape
    eye = jnp.eye(n, dtype=w.dtype)
    return (eye[:, None, :, None] * w[:, :, None, :]).reshape(n * c, n * d)


def _proj_rope_kernel(x_ref, pos_ref, freq_ref, w_ref, q_ref, k_ref, vt_ref):
    tm = x_ref.shape[0]
    xb = x_ref[...].astype(BF16)
    ang_t = freq_ref[...] * pos_ref[...].astype(F32)
    reps = LANES // (HEAD_DIM // 2)
    cos = jnp.concatenate([jnp.cos(ang_t)] * reps, axis=0).T
    sin = jnp.concatenate([jnp.sin(ang_t)] * reps, axis=0).T
    lane = lax.broadcasted_iota(jnp.int32, (tm, LANES), 1)
    first_half = (lane % HEAD_DIM) < (HEAD_DIM // 2)
    heads_per_slab = LANES // HEAD_DIM

    def rope(r):
        upper = pltpu.roll(r, LANES - HEAD_DIM // 2, axis=1)
        lower = pltpu.roll(r, HEAD_DIM // 2, axis=1)
        return r * cos + jnp.where(first_half, -upper, lower) * sin

    def emit(ref, n_heads, col0, rotary, scale):
        for slab in range(n_heads // heads_per_slab):
            c0 = col0 + slab * LANES
            r = jnp.dot(xb, w_ref[:, c0:c0 + LANES], preferred_element_type=F32)
            if rotary:
                r = rope(r)
            if scale is not None:
                r = r * scale
            for j in range(heads_per_slab):
                ref[slab * heads_per_slab + j] = r[:, j * HEAD_DIM:(j + 1) * HEAD_DIM].astype(BF16)

    emit(q_ref, SWA_HEADS, 0, True, Q_SCALE)
    emit(k_ref, SWA_KV_HEADS, SWA_HEADS * HEAD_DIM, True, None)
    v0 = (SWA_HEADS + SWA_KV_HEADS) * HEAD_DIM
    for slab in range(SWA_KV_HEADS // heads_per_slab):
        r = jnp.dot(xb, w_ref[:, v0 + slab * LANES:v0 + (slab + 1) * LANES], preferred_element_type=F32)
        rt = r.T
        for j in range(heads_per_slab):
            vt_ref[slab * heads_per_slab + j] = rt[j * HEAD_DIM:(j + 1) * HEAD_DIM, :].astype(BF16)


def _proj_rope(x, positions, w_bf16, tm):
    b, s, d = x.shape
    n = w_bf16.shape[1]
    half = HEAD_DIM // 2
    inv_freq = (ROPE_THETA ** (-jnp.arange(half, dtype=F32) / half)).reshape(half, 1)

    def heads(nh):
        return (jax.ShapeDtypeStruct((b, nh, s, HEAD_DIM), BF16),
                pl.BlockSpec((None, nh, tm, HEAD_DIM), lambda bi, i: (bi, 0, i, 0)))

    (qs, qspec), (ks, kspec) = heads(SWA_HEADS), heads(SWA_KV_HEADS)
    vs = jax.ShapeDtypeStruct((b, SWA_KV_HEADS, HEAD_DIM, s), BF16)
    vspec = pl.BlockSpec((None, SWA_KV_HEADS, HEAD_DIM, tm), lambda bi, i: (bi, 0, 0, i))
    return pl.pallas_call(
        _proj_rope_kernel,
        grid=(b, s // tm),
        in_specs=[pl.BlockSpec((None, tm, d), lambda bi, i: (bi, i, 0)),
                  pl.BlockSpec((None, 1, tm), lambda bi, i: (bi, 0, i)),
                  pl.BlockSpec((half, 1), lambda bi, i: (0, 0)),
                  pl.BlockSpec((d, n), lambda bi, i: (0, 0))],
        out_specs=[qspec, kspec, vspec],
        out_shape=[qs, ks, vs],
        compiler_params=_params("parallel", "parallel"),
        name="proj_rope",
    )(x, positions.reshape(b, 1, s), inv_freq, w_bf16)


def _reduce_rows(x, op):
    while x.shape[0] > SUBLANES:
        half = x.shape[0] // 2
        x = op(x[:half], x[half:])
    for shift in (4, 2, 1):
        x = op(x, pltpu.roll(x, shift, axis=0))
    return x[0:1]


SWA_BLOCKS_PER_STEP = 4


def _swa_kernel(q_ref, kp_ref, kc_ref, vtp_ref, vtc_ref, sink_ref, o_ref):
    i = pl.program_id(1)
    w = SWA_WINDOW
    key = lax.broadcasted_iota(jnp.int32, (2 * w, w), 0)
    qry = lax.broadcasted_iota(jnp.int32, (2 * w, w), 1)
    dist = qry + w - key
    band = (dist >= 0) & (dist < w)
    for blk in range(SWA_BLOCKS_PER_STEP):
        visible = band if blk else band & ((key >= w) | (i > 0))
        bias = jnp.concatenate([jnp.where(visible, 0.0, NEG_BIG)] * SWA_GROUP, axis=1)
        outs = []
        for kv in range(SWA_KV_HEADS):
            if blk:
                kk = kc_ref[kv, (blk - 1) * w:(blk + 1) * w, :]
                vvt = vtc_ref[kv, :, (blk - 1) * w:(blk + 1) * w]
            else:
                kk = jnp.concatenate([kp_ref[kv], kc_ref[kv, 0:w, :]], axis=0)
                vvt = jnp.concatenate([vtp_ref[kv], vtc_ref[kv, :, 0:w]], axis=1)
            qg = jnp.concatenate([q_ref[kv * SWA_GROUP + g, blk * w:(blk + 1) * w, :]
                                  for g in range(SWA_GROUP)], axis=0)
            st = lax.dot_general(kk, qg, (((1,), (1,)), ((), ())), preferred_element_type=F32) + bias
            sink = sink_ref[kv:kv + 1, :]
            m = jnp.maximum(_reduce_rows(st, jnp.maximum), sink)
            p = jnp.exp(st - m)
            denom = _reduce_rows(p, jnp.add) + jnp.exp(sink - m)
            ot = jnp.dot(vvt, p.astype(BF16), preferred_element_type=F32) / denom
            outs.extend(ot[:, g * w:(g + 1) * w] for g in range(SWA_GROUP))
        o_ref[blk * w:(blk + 1) * w, :] = jnp.concatenate(outs, axis=0).T.astype(o_ref.dtype)


def _swa(q, k, vt, sinks):
    b, nh, s, dh = q.shape
    nkv = k.shape[1]
    w = SWA_WINDOW
    n = SWA_BLOCKS_PER_STEP
    assert s % (n * w) == 0
    cur = pl.BlockSpec((None, nkv, n * w, dh), lambda bi, i: (bi, 0, i, 0))
    prev = pl.BlockSpec((None, nkv, w, dh), lambda bi, i: (bi, 0, jnp.maximum(n * i - 1, 0), 0))
    cur_t = pl.BlockSpec((None, nkv, dh, n * w), lambda bi, i: (bi, 0, 0, i))
    prev_t = pl.BlockSpec((None, nkv, dh, w), lambda bi, i: (bi, 0, 0, jnp.maximum(n * i - 1, 0)))
    sink_tile = jnp.repeat(sinks.astype(F32).reshape(nkv, nh // nkv), w, axis=1)
    return pl.pallas_call(
        _swa_kernel,
        grid=(b, s // (n * w)),
        in_specs=[pl.BlockSpec((None, nh, n * w, dh), lambda bi, i: (bi, 0, i, 0)),
                  prev, cur, prev_t, cur_t,
                  pl.BlockSpec(sink_tile.shape, lambda bi, i: (0, 0))],
        out_specs=pl.BlockSpec((None, n * w, nh * dh), lambda bi, i: (bi, i, 0)),
        out_shape=jax.ShapeDtypeStruct((b, s, nh * dh), BF16),
        compiler_params=_params("parallel", "parallel"),
        name="swa",
    )(q, k, k, vt, vt, sink_tile)


PAIRS_PER_GROUP = GROUP_SIZE * (GROUP_SIZE - 1) // 2
N_BUCKETS = N_GROUPS * PAIRS_PER_GROUP
ROUTE_ROWS = SUBLANES


def _route(logits_t):
    rows = [logits_t[e:e + 1, :] for e in range(N_EXPERTS)]
    mx = functools.reduce(jnp.maximum, rows)
    ex = [jnp.exp(r - mx) for r in rows]
    total = functools.reduce(lambda p, q: p + q, ex)
    probs = [e / total for e in ex]

    group_score = []
    for g in range(N_GROUPS):
        a, b, c, d = probs[g * GROUP_SIZE:(g + 1) * GROUP_SIZE]
        hi1, lo1 = jnp.maximum(a, b), jnp.minimum(a, b)
        hi2, lo2 = jnp.maximum(c, d), jnp.minimum(c, d)
        top1 = jnp.maximum(hi1, hi2)
        top2 = jnp.maximum(jnp.minimum(hi1, hi2), jnp.maximum(lo1, lo2))
        group_score.append(top1 + top2)
    best = functools.reduce(jnp.maximum, group_score)
    g_sel = jnp.full(best.shape, N_GROUPS - 1, jnp.int32)
    for g in range(N_GROUPS - 2, -1, -1):
        g_sel = jnp.where(group_score[g] == best, g, g_sel)

    in_group = []
    for j in range(GROUP_SIZE):
        val = probs[(N_GROUPS - 1) * GROUP_SIZE + j]
        for g in range(N_GROUPS - 2, -1, -1):
            val = jnp.where(g_sel == g, probs[g * GROUP_SIZE + j], val)
        in_group.append(val)

    def first_argmax(vals):
        m = functools.reduce(jnp.maximum, vals)
        idx = jnp.full(m.shape, GROUP_SIZE - 1, jnp.int32)
        for j in range(GROUP_SIZE - 2, -1, -1):
            idx = jnp.where(vals[j] == m, j, idx)
        return m, idx

    w1, i1 = first_argmax(in_group)
    rest = [jnp.where(i1 == j, -1.0, in_group[j]) for j in range(GROUP_SIZE)]
    w2, i2 = first_argmax(rest)
    norm = w1 + w2
    first_is_lo = i1 < i2
    i_lo = jnp.minimum(i1, i2)
    i_hi = jnp.maximum(i1, i2)
    pair = jnp.where(i_lo == 0, i_hi - 1, jnp.where(i_lo == 1, i_hi + 1, PAIRS_PER_GROUP - 1))
    bucket = (g_sel * PAIRS_PER_GROUP + pair).astype(F32)
    w_lo = jnp.where(first_is_lo, w1, w2) / norm
    w_hi = jnp.where(first_is_lo, w2, w1) / norm
    return bucket, w_lo, w_hi


def _mix_out_kernel(*refs, n_head_major, alpha):
    x_ref = refs[0]
    y_refs = refs[1:1 + len(n_head_major)]
    w_refs = refs[1 + len(n_head_major):1 + 2 * len(n_head_major)]
    g_ref, b_ref, wrh_ref, wrl_ref, brt_ref, x1e_ref, route_ref = refs[1 + 2 * len(n_head_major):]
    tm, d = x_ref.shape
    h = alpha * x_ref[...]
    for y_ref, w_ref, nh in zip(y_refs, w_refs, n_head_major):
        if nh:
            y = jnp.concatenate([y_ref[j] for j in range(nh)], axis=-1)
        else:
            y = y_ref[...]
        h = h + jnp.dot(y, w_ref[...], preferred_element_type=F32)
    x1 = _layer_norm(h, g_ref[...], b_ref[...])
    x_hi = x1.astype(BF16)
    x_lo = (x1 - x_hi.astype(F32)).astype(BF16)

    def nt_dot(w, xv):
        return lax.dot_general(w, xv, (((1,), (1,)), ((), ())), preferred_element_type=F32)

    logits_t = (nt_dot(wrh_ref[...], x_hi) + nt_dot(wrh_ref[...], x_lo) + nt_dot(wrl_ref[...], x_hi)
                + brt_ref[...])
    route = jnp.concatenate(list(_route(logits_t)) + [jnp.zeros((ROUTE_ROWS - 3, tm), F32)], axis=0)
    route_ref[...] = route
    x1e_ref[:, 0:d] = x1
    x1e_ref[:, d:d + LANES] = jnp.concatenate([route, jnp.zeros((LANES - ROUTE_ROWS, tm), F32)], axis=0).T


def _mix_out(x, ys, ws, ln_g, ln_b, w_router, b_router, alpha, tm):
    b, s, d = x.shape
    n_head_major = tuple(y.shape[1] if hm else 0 for y, hm in ys)
    y_specs = []
    for (y, hm) in ys:
        if hm:
            y_specs.append(pl.BlockSpec((None, y.shape[1], tm, y.shape[3]), lambda bi, i: (bi, 0, i, 0)))
        else:
            y_specs.append(pl.BlockSpec((None, tm, y.shape[2]), lambda bi, i: (bi, i, 0)))
    w_specs = [pl.BlockSpec(w.shape, lambda bi, i: (0, 0)) for w in ws]
    row = pl.BlockSpec((1, d), lambda bi, i: (0, 0))
    tok = pl.BlockSpec((None, tm, d), lambda bi, i: (bi, i, 0))
    wr_t = w_router.T.astype(F32)
    wr_hi = wr_t.astype(BF16)
    wr_lo = (wr_t - wr_hi.astype(F32)).astype(BF16)
    wr_spec = pl.BlockSpec((N_EXPERTS, d), lambda bi, i: (0, 0))
    return pl.pallas_call(
        functools.partial(_mix_out_kernel, n_head_major=n_head_major, alpha=alpha),
        grid=(b, s // tm),
        in_specs=[tok] + y_specs + w_specs + [row, row, wr_spec, wr_spec,
                  pl.BlockSpec((N_EXPERTS, 1), lambda bi, i: (0, 0))],
        out_specs=[pl.BlockSpec((None, tm, d + LANES), lambda bi, i: (bi, i, 0)),
                   pl.BlockSpec((None, ROUTE_ROWS, tm), lambda bi, i: (bi, 0, i))],
        out_shape=[jax.ShapeDtypeStruct((b, s, d + LANES), F32),
                   jax.ShapeDtypeStruct((b, ROUTE_ROWS, s), F32)],
        compiler_params=_params("parallel", "parallel"),
        name="mix_out_ln_router",
    )(x, *[y for y, _ in ys], *ws, ln_g.reshape(1, d), ln_b.reshape(1, d),
      wr_hi, wr_lo, b_router.astype(F32).reshape(N_EXPERTS, 1))


MOE_TILE = 256
PLAN_COLS = 256
META_ROWS = SUBLANES


def _num_moe_tiles(t):
    return t // MOE_TILE + N_BUCKETS


def _plan_kernel(bid_ref, pos_ref, meta_ref):
    r, c = bid_ref.shape
    bid = bid_ref[...]
    before = (lax.broadcasted_iota(jnp.int32, (c, c), 0)
              < lax.broadcasted_iota(jnp.int32, (c, c), 1)).astype(BF16)
    rows_before = (lax.broadcasted_iota(jnp.int32, (r, r), 1)
                   < lax.broadcasted_iota(jnp.int32, (r, r), 0)).astype(BF16)
    lane = lax.broadcasted_iota(jnp.int32, (1, LANES), 1)
    tile_start = lane.astype(F32) * MOE_TILE

    def body(b, state):
        base, pos, tile_bucket, last_tile = state
        ind = (bid == lax.convert_element_type(b, F32)).astype(F32)
        within = jnp.dot(ind.astype(BF16), before, preferred_element_type=F32)
        row_total = jnp.sum(ind, axis=1, keepdims=True)
        row_off = jnp.dot(rows_before, jnp.broadcast_to(row_total, (r, LANES)).astype(BF16),
                          preferred_element_type=F32)[:, 0:1]
        count = jnp.sum(row_total, axis=0, keepdims=True)
        padded = jnp.floor((count + (MOE_TILE - 1)) * (1.0 / MOE_TILE)) * MOE_TILE
        pos = pos + ind * (base + row_off + within)
        end = base + padded
        tile_bucket = tile_bucket + (tile_start >= end).astype(F32)
        last_row = jnp.where(count > 0.0, end - MOE_TILE, -1.0)
        last_tile = jnp.where(lane == b, last_row, last_tile)
        return end, pos, tile_bucket, last_tile

    total, pos, tile_bucket, last_tile = lax.fori_loop(
        0, N_BUCKETS, body, (jnp.zeros((1, 1), F32), jnp.zeros((r, c), F32), jnp.zeros((1, LANES), F32),
                             jnp.full((1, LANES), -1.0, F32)))
    pos_ref[...] = pos.astype(jnp.int32)

    tb = jnp.minimum(tile_bucket, N_BUCKETS - 1.0)
    group = sum((tb >= g * PAIRS_PER_GROUP).astype(F32) for g in range(1, N_GROUPS))
    pair = tb - group * PAIRS_PER_GROUP
    i_lo = jnp.where(pair < 3, 0.0, jnp.where(pair < 5, 1.0, 2.0))
    i_hi = jnp.where(pair < 3, pair + 1.0, jnp.where(pair < 5, pair - 1.0, 3.0))
    meta = jnp.concatenate([group * GROUP_SIZE + i_lo, group * GROUP_SIZE + i_hi,
                            (tile_start < total).astype(F32),
                            jnp.where(lane == N_BUCKETS, total, last_tile),
                            jnp.zeros((META_ROWS - 4, LANES), F32)], axis=0)
    meta_ref[...] = meta.astype(jnp.int32)


def _plan(bucket_ids):
    r, c = bucket_ids.shape
    return pl.pallas_call(
        _plan_kernel,
        out_shape=[jax.ShapeDtypeStruct((r, c), jnp.int32), jax.ShapeDtypeStruct((META_ROWS, LANES), jnp.int32)],
        compiler_params=pltpu.CompilerParams(vmem_limit_bytes=VMEM_LIMIT_BYTES),
        name="moe_plan",
    )(bucket_ids)


DMA_PRIORITIES = 2


def _row_copy(src_ref, src_row, dst_ref, dst_row, sem):
    return pltpu.make_async_copy(src_ref.at[pl.ds(src_row, 1)], dst_ref.at[pl.ds(dst_row, 1)], sem)


def _dispatch_kernel(pos_ref, last_ref, x_ref, o_ref, inv_ref, zeros, inv, fill, sem, zsem, isem, *,
                     first_spare, n_tokens):
    tm = x_ref.shape[0]
    i = pl.program_id(0)

    @pl.when(i == 0)
    def _():
        zeros[...] = jnp.zeros_like(zeros)
        fill[...] = jnp.full(fill.shape, n_tokens, jnp.int32)
        to_smem = pltpu.make_async_copy(fill, inv, isem)
        to_smem.start()
        rows_in_use = last_ref[N_BUCKETS]
        clears = [(last_ref[b] >= 0, last_ref[b]) for b in range(N_BUCKETS)]
        clears += [(k * MOE_TILE >= rows_in_use, k * MOE_TILE)
                   for k in range(first_spare, o_ref.shape[0] // MOE_TILE)]

        def clear(row):
            start = row if isinstance(row, int) else pl.multiple_of(row, MOE_TILE)
            return pltpu.make_async_copy(zeros, o_ref.at[pl.ds(start, MOE_TILE)], zsem)

        for needed, row in clears:
            @pl.when(needed)
            def _():
                clear(row).start()
        for needed, row in clears:
            @pl.when(needed)
            def _():
                clear(row).wait()
        to_smem.wait()

    base = i * tm
    for r in range(tm):
        dst = pos_ref[base + r]
        inv[dst] = base + r
        _row_copy(x_ref, r, o_ref, dst, sem).start(priority=r % DMA_PRIORITIES)
    pltpu.make_async_copy(x_ref, o_ref.at[pl.ds(0, tm)], sem).wait()

    @pl.when(i == pl.num_programs(0) - 1)
    def _():
        to_hbm = pltpu.make_async_copy(inv, inv_ref, isem)
        to_hbm.start()
        to_hbm.wait()


def _dispatch(pos, last_tile_rows, xt, n_rows, tm):
    t = xt.shape[0]
    tile = xt.shape[1:]
    dma_sem = pltpu.SemaphoreType.DMA(())
    return pl.pallas_call(
        functools.partial(_dispatch_kernel, first_spare=t // MOE_TILE, n_tokens=t),
        grid_spec=pltpu.PrefetchScalarGridSpec(
            num_scalar_prefetch=2,
            grid=(t // tm,),
            in_specs=[pl.BlockSpec((tm,) + tile, lambda i, pos_ref, last_ref: (i,) + (0,) * len(tile))],
            out_specs=[pl.BlockSpec(memory_space=pl.ANY), pl.BlockSpec(memory_space=pl.ANY)],
            scratch_shapes=[pltpu.VMEM((MOE_TILE,) + tile, xt.dtype), pltpu.SMEM((n_rows,), jnp.int32),
                            pltpu.VMEM((n_rows,), jnp.int32), dma_sem, dma_sem, dma_sem]),
        out_shape=[jax.ShapeDtypeStruct((n_rows,) + tile, xt.dtype),
                   jax.ShapeDtypeStruct((n_rows,), jnp.int32)],
        compiler_params=_params("arbitrary"),
        name="moe_dispatch",
    )(pos, last_tile_rows, xt)


def _moe_tile_kernel(elo_ref, ehi_ref, valid_ref, inv, x_ref, wgl_ref, wul_ref, wdl_ref, wgh_ref, wuh_ref,
                     wdh_ref, m_hbm, o_even, o_odd, ssem_even, ssem_odd, wgl_b, wul_b, wdl_b, wgh_b, wuh_b,
                     wdh_b, *, n_tokens):
    d = wgl_ref.shape[0]
    k = pl.program_id(0)
    last = pl.num_programs(0) - 1
    prev = jnp.maximum(k - 1, 0)
    valid = valid_ref[k] != 0

    def scatter(tile, r, buf, sem, to_spare):
        tok = inv[tile * MOE_TILE + r]
        dst = jnp.where(to_spare | (tok >= n_tokens), n_tokens + r, tok)
        return _row_copy(buf, r, m_hbm, dst, sem)

    def scatter_done(buf, sem):
        pltpu.make_async_copy(buf, m_hbm.at[pl.ds(0, MOE_TILE)], sem).wait()

    @pl.when(k == 0)
    def _():
        o_odd[...] = jnp.zeros_like(o_odd)

    def refresh(e_ref, srcs, dsts):
        @pl.when(valid & ((k == 0) | (e_ref[k] != e_ref[prev])))
        def _():
            for src, dst in zip(srcs, dsts):
                dst[...] = src[...].astype(BF16)

    refresh(elo_ref, (wgl_ref, wul_ref, wdl_ref), (wgl_b, wul_b, wdl_b))
    refresh(ehi_ref, (wgh_ref, wuh_ref, wdh_ref), (wgh_b, wuh_b, wdh_b))

    def step(o_cur, ssem_cur, o_prv, ssem_prv):
        @pl.when(k > 0)
        def _():
            scatter_done(o_cur, ssem_cur)

        first = k == 0

        @pl.when(valid)
        def _():
            for r in range(MOE_TILE):
                scatter(prev, r, o_prv, ssem_prv, first).start(priority=r % DMA_PRIORITIES)
            x = x_ref[:, 0:d].astype(BF16)
            acc = None
            for wg_b, wu_b, wd_b, lane in ((wgl_b, wul_b, wdl_b, d + 1), (wgh_b, wuh_b, wdh_b, d + 2)):
                weight = x_ref[:, lane:lane + 1]
                hg = jnp.dot(x, wg_b[...], preferred_element_type=F32)
                hu = jnp.dot(x, wu_b[...], preferred_element_type=F32)
                hidden = (hg * _sigmoid(hg)) * hu * weight
                y = jnp.dot(hidden.astype(BF16), wd_b[...], preferred_element_type=F32)
                acc = y if acc is None else acc + y
            o_cur[...] = acc

        @pl.when(jnp.logical_not(valid))
        def _():
            def issue(r, carry):
                scatter(prev, r, o_prv, ssem_prv, first).start()
                return carry

            lax.fori_loop(0, MOE_TILE, issue, 0, unroll=8)
            o_cur[...] = jnp.zeros_like(o_cur)

        @pl.when(k == last)
        def _():
            scatter_done(o_prv, ssem_prv)

            def issue(r, carry):
                scatter(k, r, o_cur, ssem_cur, False).start()
                return carry

            lax.fori_loop(0, MOE_TILE, issue, 0, unroll=8)
            scatter_done(o_cur, ssem_cur)

    @pl.when(k % 2 == 0)
    def _():
        step(o_even, ssem_even, o_odd, ssem_odd)

    @pl.when(k % 2 == 1)
    def _():
        step(o_odd, ssem_odd, o_even, ssem_even)


def _moe_tiles(e_lo, e_hi, valid, inv, xs, wg, wu, wd, layer, n_tokens):
    n_rows, dw = xs.shape
    _, _, d, f = wg.shape
    up_lo = pl.BlockSpec((None, None, d, f), lambda k, lo, hi, ok, iv: (layer, lo[k], 0, 0))
    up_hi = pl.BlockSpec((None, None, d, f), lambda k, lo, hi, ok, iv: (layer, hi[k], 0, 0))
    down_lo = pl.BlockSpec((None, None, f, d), lambda k, lo, hi, ok, iv: (layer, lo[k], 0, 0))
    down_hi = pl.BlockSpec((None, None, f, d), lambda k, lo, hi, ok, iv: (layer, hi[k], 0, 0))
    up_b = pltpu.VMEM((d, f), BF16)
    down_b = pltpu.VMEM((f, d), BF16)
    o_buf = pltpu.VMEM((MOE_TILE, d), F32)
    dma_sem = pltpu.SemaphoreType.DMA(())
    return pl.pallas_call(
        functools.partial(_moe_tile_kernel, n_tokens=n_tokens),
        grid_spec=pltpu.PrefetchScalarGridSpec(
            num_scalar_prefetch=4,
            grid=(n_rows // MOE_TILE,),
            in_specs=[pl.BlockSpec((MOE_TILE, dw), lambda k, lo, hi, ok, iv: (k, 0)),
                      up_lo, up_lo, down_lo, up_hi, up_hi, down_hi],
            out_specs=pl.BlockSpec(memory_space=pl.ANY),
            scratch_shapes=[o_buf, o_buf, dma_sem, dma_sem, up_b, up_b, down_b, up_b, up_b, down_b]),
        out_shape=jax.ShapeDtypeStruct((n_tokens + MOE_TILE, d), F32),
        compiler_params=_params("arbitrary"),
        name="moe_tiles",
    )(e_lo, e_hi, valid, inv, xs, wg, wu, wd, wg, wu, wd)


def _ln_ple_kernel(x1_ref, m_ref, p_ref, g_ref, b_ref, wg_ref, bg_ref, wp_ref, o_ref, *, alpha):
    x2 = _layer_norm(alpha * x1_ref[...] + m_ref[...], g_ref[...], b_ref[...])
    gate = _sigmoid(jnp.dot(x2.astype(BF16), wg_ref[...], preferred_element_type=F32) + bg_ref[...])
    emb = jnp.dot(p_ref[...].astype(BF16), wp_ref[...], preferred_element_type=F32)
    o_ref[...] = x2 + gate * emb


def _ln_ple(x1e, m, p, layer, ln_g, ln_b, wg, bg, wp, alpha, tm):
    t = x1e.shape[0]
    d = m.shape[1]
    pd = p.shape[2]
    tok = pl.BlockSpec((tm, d), lambda i: (i, 0))
    row = pl.BlockSpec((1, d), lambda i: (0, 0))
    return pl.pallas_call(
        functools.partial(_ln_ple_kernel, alpha=alpha),
        grid=(t // tm,),
        in_specs=[tok, tok, pl.BlockSpec((None, tm, pd), lambda i: (layer, i, 0)), row, row,
                  pl.BlockSpec((d, d), lambda i: (0, 0)), row, pl.BlockSpec((pd, d), lambda i: (0, 0))],
        out_specs=tok,
        out_shape=jax.ShapeDtypeStruct((t, d), F32),
        compiler_params=_params("parallel"),
        name="ln_ple",
    )(x1e, m, p, ln_g.reshape(1, d), ln_b.reshape(1, d), wg, bg.reshape(1, d), wp)


def _pick_tile(n, target):
    t = min(n, target)
    while n % t:
        t //= 2
    return t


def kernel(x, p, positions, w_in_ab, w_out_ab, conv_w, conv_b, lru_w_r, lru_b_r, lru_w_i, lru_b_i, lru_lambda, w_qkv_c, w_out_c, sinks_c, ln_mix_g, ln_mix_b, ln_ffn_g, ln_ffn_b, w_router, b_router, exp_w_gate, exp_w_up, exp_w_down, ple_w_proj, ple_w_gate, ple_b_gate):
    b, s, d = x.shape
    depth = p.shape[0]
    t = b * s
    alpha = (2 * depth) ** 0.25
    tm = _pick_tile(s, 1024)
    assert t % MOE_TILE == 0 and t % PLAN_COLS == 0
    n_tiles = _num_moe_tiles(t)
    assert n_tiles <= LANES
    for i in range(depth):
        j = i // 2
        if i % 2 == 0:
            q, k, v, xr, gr = _proj_ab(x, w_in_ab[j].astype(BF16), tm)
            y_sb = _sb_attention(q, k, v, _pick_tile(s, 256), SB_HEADS)
            y_lru = _lru(xr, gr, conv_w[j], conv_b[j], _block_diag(lru_w_r[j]).astype(BF16), lru_b_r[j],
                         _block_diag(lru_w_i[j]).astype(BF16), lru_b_i[j], lru_lambda[j],
                         _pick_tile(s, 256))
            w_out = w_out_ab[j].astype(BF16)
            ys = [(y_sb, True), (y_lru, False)]
            ws = [w_out[:SB_WIDTH], w_out[SB_WIDTH:]]
        else:
            q, k, v = _proj_rope(x, positions, w_qkv_c[j].astype(BF16), tm)
            y = _swa(q, k, v, sinks_c[j])
            ys = [(y, False)]
            ws = [w_out_c[j].astype(BF16)]
        x1e, route = _mix_out(x, ys, ws, ln_mix_g[i], ln_mix_b[i], w_router, b_router, alpha, tm)
        x1e = x1e.reshape(t, d + LANES)
        pos, meta = _plan(route[:, 0, :].reshape(t // PLAN_COLS, PLAN_COLS))
        pos = pos.reshape(t)
        xs, inv = _dispatch(pos, meta[3, :N_BUCKETS + 1], x1e, n_tiles * MOE_TILE, _pick_tile(t, 512))
        m = _moe_tiles(meta[0, :n_tiles], meta[1, :n_tiles], meta[2, :n_tiles], inv, xs,
                       exp_w_gate, exp_w_up, exp_w_down, i, t)
        x = _ln_ple(x1e, m, p.reshape(depth, t, -1), i, ln_ffn_g[i], ln_ffn_b[i],
                    ple_w_gate[i].astype(BF16), ple_b_gate[i], ple_w_proj[i].astype(BF16), alpha,
                    _pick_tile(t, 512)).reshape(b, s, d)
    return x
```

```python
import functools
import math

import jax
import jax.numpy as jnp
from jax import lax
from jax.experimental import pallas as pl
from jax.experimental.pallas import tpu as pltpu

HEAD_DIM = 64
SB_HEADS = 8
SB_WIDTH = SB_HEADS * HEAD_DIM
LRU_WIDTH = 512
LRU_BLOCKS = 8
LRU_C = 8.0
CONV_WIDTH = 4
SWA_HEADS = 16
SWA_KV_HEADS = 4
SWA_GROUP = SWA_HEADS // SWA_KV_HEADS
SWA_WINDOW = 128
ROPE_THETA = 10000.0
N_EXPERTS = 16
N_GROUPS = 4
GROUP_SIZE = N_EXPERTS // N_GROUPS
LN_EPS = 1e-5
Q_SCALE = HEAD_DIM ** -0.5

LANES = 128
SUBLANES = 8
VMEM_LIMIT_BYTES = 48 * 1024 * 1024

NEG_BIG = -1e30

BF16 = jnp.bfloat16
F32 = jnp.float32


def _params(*semantics):
    return pltpu.CompilerParams(dimension_semantics=semantics, vmem_limit_bytes=VMEM_LIMIT_BYTES)


def _softplus(z):
    return jnp.maximum(z, 0.0) + jnp.log(1.0 + jnp.exp(-jnp.abs(z)))


def _sigmoid(z):
    return 1.0 / (1.0 + jnp.exp(-z))


def _layer_norm(y, g, b):
    mu = jnp.mean(y, axis=-1, keepdims=True)
    d = y - mu
    var = jnp.mean(d * d, axis=-1, keepdims=True)
    return d * lax.rsqrt(var + LN_EPS) * g + b


def _proj_ab_kernel(x_ref, w_ref, q_ref, k_ref, v_ref, xr_ref, gr_ref):
    xb = x_ref[...].astype(BF16)

    def chunk(c):
        return jnp.dot(xb, w_ref[:, c * SB_WIDTH:(c + 1) * SB_WIDTH], preferred_element_type=F32)

    for c, (ref, scale) in enumerate(((q_ref, Q_SCALE), (k_ref, None), (v_ref, None))):
        r = chunk(c)
        if scale is not None:
            r = r * scale
        for h in range(SB_HEADS):
            ref[h] = r[:, h * HEAD_DIM:(h + 1) * HEAD_DIM].astype(BF16)
    xr_ref[...] = chunk(3)
    gr_ref[...] = chunk(4)


def _proj_ab(x, w_bf16, tm):
    b, s, d = x.shape
    n = w_bf16.shape[1]
    heads = jax.ShapeDtypeStruct((b, SB_HEADS, s, HEAD_DIM), BF16)
    flat = jax.ShapeDtypeStruct((b, s, LRU_WIDTH), F32)
    head_spec = pl.BlockSpec((None, SB_HEADS, tm, HEAD_DIM), lambda bi, i: (bi, 0, i, 0))
    flat_spec = pl.BlockSpec((None, tm, LRU_WIDTH), lambda bi, i: (bi, i, 0))
    return pl.pallas_call(
        _proj_ab_kernel,
        grid=(b, s // tm),
        in_specs=[pl.BlockSpec((None, tm, d), lambda bi, i: (bi, i, 0)),
                  pl.BlockSpec((d, n), lambda bi, i: (0, 0))],
        out_specs=[head_spec, head_spec, head_spec, flat_spec, flat_spec],
        out_shape=[heads, heads, heads, flat, flat],
        compiler_params=_params("parallel", "parallel"),
        name="proj_ab",
    )(x, w_bf16)


SB_DEAD_LOG_WEIGHT = -105.0
SB_MERGED_BLOCKS = 2


def _sb_attn_kernel(q_ref, k_ref, v_ref, o_ref, *, tq, hp):
    i = pl.program_id(2)
    row = lax.broadcasted_iota(jnp.int32, (tq, tq), 0)
    col = lax.broadcasted_iota(jnp.int32, (tq, tq), 1)
    minus_later = jnp.where(row > col, -1.0, 0.0).astype(BF16)
    causal = col < row

    def block(jb, carries, accs, masked):
        start = pl.multiple_of(jb * tq, tq)
        new_carries, new_accs = [], []
        for h in range(hp):
            kj = k_ref[h, pl.ds(start, tq), :]
            vj = v_ref[h, pl.ds(start, tq), :]
            z = lax.dot_general(q_ref[h], kj, (((1,), (1,)), ((), ())), preferred_element_type=F32)
            sp = _softplus(z)
            cost = jnp.where(causal, sp, 0.0) if masked else sp
            after = jnp.dot(cost.astype(BF16), minus_later, preferred_element_type=F32)
            w = jnp.exp((z - sp) + after + carries[h])
            if masked:
                w = jnp.where(causal, w, 0.0)
            new_accs.append(accs[h] + jnp.dot(w.astype(BF16), vj, preferred_element_type=F32))
            new_carries.append(carries[h] - jnp.sum(cost, axis=1, keepdims=True))
        return tuple(new_carries), tuple(new_accs)

    def live(carries):
        return functools.reduce(jnp.maximum, [jnp.max(c) for c in carries])

    zero = ((jnp.zeros((tq, 1), F32),) * hp, (jnp.zeros((tq, HEAD_DIM), F32),) * hp)

    def first(n):
        def run():
            state = block(i, *zero, True)
            for j in range(1, n):
                state = block(i - j, *state, False)
            return state
        return run

    done = jnp.minimum(i, SB_MERGED_BLOCKS - 1)
    carries, accs = lax.switch(done, [first(n) for n in range(1, SB_MERGED_BLOCKS + 1)])

    def cond(state):
        return (state[0] < i) & (state[1] > SB_DEAD_LOG_WEIGHT)

    def body(state):
        step, _, carries, accs = state
        carries, accs = block(i - 1 - step, carries, accs, False)
        return step + 1, live(carries), carries, accs

    _, _, _, accs = lax.while_loop(cond, body, (done, live(carries), carries, accs))
    for h in range(hp):
        o_ref[h] = accs[h].astype(o_ref.dtype)


def _sb_attention(q, k, v, tq, hp):
    b, h, s, dh = q.shape
    return pl.pallas_call(
        functools.partial(_sb_attn_kernel, tq=tq, hp=hp),
        grid=(b, h // hp, s // tq),
        in_specs=[pl.BlockSpec((None, hp, tq, dh), lambda bi, hi, i: (bi, hi, i, 0)),
                  pl.BlockSpec((None, hp, s, dh), lambda bi, hi, i: (bi, hi, 0, 0), pipeline_mode=pl.Buffered(1)),
                  pl.BlockSpec((None, hp, s, dh), lambda bi, hi, i: (bi, hi, 0, 0), pipeline_mode=pl.Buffered(1))],
        out_specs=pl.BlockSpec((None, hp, tq, dh), lambda bi, hi, i: (bi, hi, i, 0)),
        out_shape=jax.ShapeDtypeStruct((b, h, s, dh), BF16),
        compiler_params=_params("parallel", "parallel", "parallel"),
        name="sb_attention",
    )(q, k, v)


def _gelu_tanh(x):
    return 0.5 * x * (1.0 + jnp.tanh(math.sqrt(2.0 / math.pi) * (x + 0.044715 * (x * x * x))))


def _lru_kernel(xr_ref, gr_ref, cw_ref, cb_ref, wr_ref, br_ref, wi_ref, bi_ref, lam_ref, y_ref,
                xbuf, hprev, *, ts):
    @pl.when(pl.program_id(1) == 0)
    def _():
        xbuf[0:SUBLANES, :] = jnp.zeros((SUBLANES, LRU_WIDTH), F32)
        hprev[...] = jnp.zeros_like(hprev)

    xbuf[SUBLANES:SUBLANES + ts, :] = xr_ref[...]
    xc = cb_ref[...] + cw_ref[CONV_WIDTH - 1:CONV_WIDTH, :] * xbuf[SUBLANES:SUBLANES + ts, :]
    for kk in range(CONV_WIDTH - 1):
        off = SUBLANES - (CONV_WIDTH - 1) + kk
        xc = xc + cw_ref[kk:kk + 1, :] * xbuf[off:off + ts, :]
    xbuf[0:SUBLANES, :] = xbuf[ts:ts + SUBLANES, :]

    xcb = xc.astype(BF16)
    r = _sigmoid(jnp.dot(xcb, wr_ref[...], preferred_element_type=F32) + br_ref[...])
    gi = _sigmoid(jnp.dot(xcb, wi_ref[...], preferred_element_type=F32) + bi_ref[...])
    log_a = (-LRU_C) * r * _softplus(-lam_ref[...])
    a = jnp.exp(log_a)
    u = jnp.sqrt(1.0 - a * a) * (gi * xc)

    row = lax.broadcasted_iota(jnp.int32, (ts, LRU_WIDTH), 0)
    d = 1
    while d < ts:
        if d < SUBLANES:
            keep = row >= d
            a_sh = jnp.where(keep, pltpu.roll(a, d, axis=0), 1.0)
            u_sh = jnp.where(keep, pltpu.roll(u, d, axis=0), 0.0)
            u = a * u_sh + u
            a = a * a_sh
        else:
            u = jnp.concatenate([u[:d], a[d:] * u[:ts - d] + u[d:]], axis=0)
            a = jnp.concatenate([a[:d], a[d:] * a[:ts - d]], axis=0)
        d *= 2
    h = a * hprev[0:1, :] + u
    hprev[...] = jnp.broadcast_to(h[ts - 1:ts, :], hprev.shape)
    y_ref[...] = (_gelu_tanh(gr_ref[...]) * h).astype(y_ref.dtype)


def _lru(xr, gr, conv_w, conv_b, wr_bd, b_r, wi_bd, b_i, lam, ts):
    b, s, w = xr.shape
    seq_spec = pl.BlockSpec((None, ts, w), lambda bi, i: (bi, i, 0))

    def full(shape):
        return pl.BlockSpec(shape, lambda bi, i: (0,) * len(shape))

    return pl.pallas_call(
        functools.partial(_lru_kernel, ts=ts),
        grid=(b, s // ts),
        in_specs=[seq_spec, seq_spec, full((CONV_WIDTH, w)), full((1, w)), full((w, w)), full((1, w)),
                  full((w, w)), full((1, w)), full((1, w))],
        out_specs=seq_spec,
        out_shape=jax.ShapeDtypeStruct((b, s, w), BF16),
        scratch_shapes=[pltpu.VMEM((ts + 2 * SUBLANES, w), F32), pltpu.VMEM((SUBLANES, w), F32)],
        compiler_params=_params("parallel", "arbitrary"),
        name="rg_lru",
    )(xr, gr, conv_w, conv_b.reshape(1, w), wr_bd, b_r.reshape(1, w), wi_bd, b_i.reshape(1, w),
      lam.reshape(1, w))


def _block_diag(w):
    n, c, d = w.shape
    eye = jnp.eye(n, dtype=w.dtype)
    return (eye[:, None, :, None] * w[:, :, None, :]).reshape(n * c, n * d)


def _proj_rope_kernel(x_ref, pos_ref, freq_ref, w_ref, q_ref, k_ref, vt_ref):
    tm = x_ref.shape[0]
    xb = x_ref[...].astype(BF16)
    ang_t = freq_ref[...] * pos_ref[...].astype(F32)
    reps = LANES // (HEAD_DIM // 2)
    cos = jnp.concatenate([jnp.cos(ang_t)] * reps, axis=0).T
    sin = jnp.concatenate([jnp.sin(ang_t)] * reps, axis=0).T
    lane = lax.broadcasted_iota(jnp.int32, (tm, LANES), 1)
    first_half = (lane % HEAD_DIM) < (HEAD_DIM // 2)
    heads_per_slab = LANES // HEAD_DIM

    def rope(r):
        upper = pltpu.roll(r, LANES - HEAD_DIM // 2, axis=1)
        lower = pltpu.roll(r, HEAD_DIM // 2, axis=1)
        return r * cos + jnp.where(first_half, -upper, lower) * sin

    def emit(ref, n_heads, col0, rotary, scale):
        for slab in range(n_heads // heads_per_slab):
            c0 = col0 + slab * LANES
            r = jnp.dot(xb, w_ref[:, c0:c0 + LANES], preferred_element_type=F32)
            if rotary:
                r = rope(r)
            if scale is not None:
                r = r * scale
            for j in range(heads_per_slab):
                ref[slab * heads_per_slab + j] = r[:, j * HEAD_DIM:(j + 1) * HEAD_DIM].astype(BF16)

    emit(q_ref, SWA_HEADS, 0, True, Q_SCALE)
    emit(k_ref, SWA_KV_HEADS, SWA_HEADS * HEAD_DIM, True, None)
    v0 = (SWA_HEADS + SWA_KV_HEADS) * HEAD_DIM
    for slab in range(SWA_KV_HEADS // heads_per_slab):
        r = jnp.dot(xb, w_ref[:, v0 + slab * LANES:v0 + (slab + 1) * LANES], preferred_element_type=F32)
        rt = r.T
        for j in range(heads_per_slab):
            vt_ref[slab * heads_per_slab + j] = rt[j * HEAD_DIM:(j + 1) * HEAD_DIM, :].astype(BF16)


def _proj_rope(x, positions, w_bf16, tm):
    b, s, d = x.shape
    n = w_bf16.shape[1]
    half = HEAD_DIM // 2
    inv_freq = (ROPE_THETA ** (-jnp.arange(half, dtype=F32) / half)).reshape(half, 1)

    def heads(nh):
        return (jax.ShapeDtypeStruct((b, nh, s, HEAD_DIM), BF16),
                pl.BlockSpec((None, nh, tm, HEAD_DIM), lambda bi, i: (bi, 0, i, 0)))

    (qs, qspec), (ks, kspec) = heads(SWA_HEADS), heads(SWA_KV_HEADS)
    vs = jax.ShapeDtypeStruct((b, SWA_KV_HEADS, HEAD_DIM, s), BF16)
    vspec = pl.BlockSpec((None, SWA_KV_HEADS, HEAD_DIM, tm), lambda bi, i: (bi, 0, 0, i))
    return pl.pallas_call(
        _proj_rope_kernel,
        grid=(b, s // tm),
        in_specs=[pl.BlockSpec((None, tm, d), lambda bi, i: (bi, i, 0)),
                  pl.BlockSpec((None, 1, tm), lambda bi, i: (bi, 0, i)),
                  pl.BlockSpec((half, 1), lambda bi, i: (0, 0)),
                  pl.BlockSpec((d, n), lambda bi, i: (0, 0))],
        out_specs=[qspec, kspec, vspec],
        out_shape=[qs, ks, vs],
        compiler_params=_params("parallel", "parallel"),
        name="proj_rope",
    )(x, positions.reshape(b, 1, s), inv_freq, w_bf16)


def _reduce_rows(x, op):
    while x.shape[0] > SUBLANES:
        half = x.shape[0] // 2
        x = op(x[:half], x[half:])
    for shift in (4, 2, 1):
        x = op(x, pltpu.roll(x, shift, axis=0))
    return x[0:1]


SWA_BLOCKS_PER_STEP = 8


def _swa_kernel(q_ref, kp_ref, kc_ref, vtp_ref, vtc_ref, sink_ref, o_ref):
    i = pl.program_id(1)
    w = SWA_WINDOW
    key = lax.broadcasted_iota(jnp.int32, (2 * w, w), 0)
    qry = lax.broadcasted_iota(jnp.int32, (2 * w, w), 1)
    dist = qry + w - key
    band = (dist >= 0) & (dist < w)
    for blk in range(SWA_BLOCKS_PER_STEP):
        visible = band if blk else band & ((key >= w) | (i > 0))
        bias = jnp.concatenate([jnp.where(visible, 0.0, NEG_BIG)] * SWA_GROUP, axis=1)
        outs = []
        for kv in range(SWA_KV_HEADS):
            if blk:
                kk = kc_ref[kv, (blk - 1) * w:(blk + 1) * w, :]
                vvt = vtc_ref[kv, :, (blk - 1) * w:(blk + 1) * w]
            else:
                kk = jnp.concatenate([kp_ref[kv], kc_ref[kv, 0:w, :]], axis=0)
                vvt = jnp.concatenate([vtp_ref[kv], vtc_ref[kv, :, 0:w]], axis=1)
            qg = jnp.concatenate([q_ref[kv * SWA_GROUP + g, blk * w:(blk + 1) * w, :]
                                  for g in range(SWA_GROUP)], axis=0)
            st = lax.dot_general(kk, qg, (((1,), (1,)), ((), ())), preferred_element_type=F32) + bias
            sink = sink_ref[kv:kv + 1, :]
            m = jnp.maximum(_reduce_rows(st, jnp.maximum), sink)
            p = jnp.exp(st - m)
            denom = _reduce_rows(p, jnp.add) + jnp.exp(sink - m)
            ot = jnp.dot(vvt, p.astype(BF16), preferred_element_type=F32) / denom
            outs.extend(ot[:, g * w:(g + 1) * w] for g in range(SWA_GROUP))
        o_ref[blk * w:(blk + 1) * w, :] = jnp.concatenate(outs, axis=0).T.astype(o_ref.dtype)


def _swa(q, k, vt, sinks):
    b, nh, s, dh = q.shape
    nkv = k.shape[1]
    w = SWA_WINDOW
    n = SWA_BLOCKS_PER_STEP
    assert s % (n * w) == 0
    cur = pl.BlockSpec((None, nkv, n * w, dh), lambda bi, i: (bi, 0, i, 0))
    prev = pl.BlockSpec((None, nkv, w, dh), lambda bi, i: (bi, 0, jnp.maximum(n * i - 1, 0), 0))
    cur_t = pl.BlockSpec((None, nkv, dh, n * w), lambda bi, i: (bi, 0, 0, i))
    prev_t = pl.BlockSpec((None, nkv, dh, w), lambda bi, i: (bi, 0, 0, jnp.maximum(n * i - 1, 0)))
    sink_tile = jnp.repeat(sinks.astype(F32).reshape(nkv, nh // nkv), w, axis=1)
    return pl.pallas_call(
        _swa_kernel,
        grid=(b, s // (n * w)),
        in_specs=[pl.BlockSpec((None, nh, n * w, dh), lambda bi, i: (bi, 0, i, 0)),
                  prev, cur, prev_t, cur_t,
                  pl.BlockSpec(sink_tile.shape, lambda bi, i: (0, 0))],
        out_specs=pl.BlockSpec((None, n * w, nh * dh), lambda bi, i: (bi, i, 0)),
        out_shape=jax.ShapeDtypeStruct((b, s, nh * dh), BF16),
        compiler_params=_params("parallel", "parallel"),
        name="swa",
    )(q, k, k, vt, vt, sink_tile)


PAIRS_PER_GROUP = GROUP_SIZE * (GROUP_SIZE - 1) // 2
N_BUCKETS = N_GROUPS * PAIRS_PER_GROUP
ROUTE_ROWS = SUBLANES


def _route(logits_t):
    rows = [logits_t[e:e + 1, :] for e in range(N_EXPERTS)]
    mx = functools.reduce(jnp.maximum, rows)
    ex = [jnp.exp(r - mx) for r in rows]
    total = functools.reduce(lambda p, q: p + q, ex)
    probs = [e / total for e in ex]

    group_score = []
    for g in range(N_GROUPS):
        a, b, c, d = probs[g * GROUP_SIZE:(g + 1) * GROUP_SIZE]
        hi1, lo1 = jnp.maximum(a, b), jnp.minimum(a, b)
        hi2, lo2 = jnp.maximum(c, d), jnp.minimum(c, d)
        top1 = jnp.maximum(hi1, hi2)
        top2 = jnp.maximum(jnp.minimum(hi1, hi2), jnp.maximum(lo1, lo2))
        group_score.append(top1 + top2)
    best = functools.reduce(jnp.maximum, group_score)
    g_sel = jnp.full(best.shape, N_GROUPS - 1, jnp.int32)
    for g in range(N_GROUPS - 2, -1, -1):
        g_sel = jnp.where(group_score[g] == best, g, g_sel)

    in_group = []
    for j in range(GROUP_SIZE):
        val = probs[(N_GROUPS - 1) * GROUP_SIZE + j]
        for g in range(N_GROUPS - 2, -1, -1):
            val = jnp.where(g_sel == g, probs[g * GROUP_SIZE + j], val)
        in_group.append(val)

    def first_argmax(vals):
        m = functools.reduce(jnp.maximum, vals)
        idx = jnp.full(m.shape, GROUP_SIZE - 1, jnp.int32)
        for j in range(GROUP_SIZE - 2, -1, -1):
            idx = jnp.where(vals[j] == m, j, idx)
        return m, idx

    w1, i1 = first_argmax(in_group)
    rest = [jnp.where(i1 == j, -1.0, in_group[j]) for j in range(GROUP_SIZE)]
    w2, i2 = first_argmax(rest)
    norm = w1 + w2
    first_is_lo = i1 < i2
    i_lo = jnp.minimum(i1, i2)
    i_hi = jnp.maximum(i1, i2)
    pair = jnp.where(i_lo == 0, i_hi - 1, jnp.where(i_lo == 1, i_hi + 1, PAIRS_PER_GROUP - 1))
    bucket = (g_sel * PAIRS_PER_GROUP + pair).astype(F32)
    w_lo = jnp.where(first_is_lo, w1, w2) / norm
    w_hi = jnp.where(first_is_lo, w2, w1) / norm
    return bucket, w_lo, w_hi


def _mix_out_kernel(*refs, n_head_major, alpha):
    x_ref = refs[0]
    y_refs = refs[1:1 + len(n_head_major)]
    w_refs = refs[1 + len(n_head_major):1 + 2 * len(n_head_major)]
    g_ref, b_ref, wrh_ref, wrl_ref, brt_ref, x1e_ref, route_ref = refs[1 + 2 * len(n_head_major):]
    tm, d = x_ref.shape
    h = alpha * x_ref[...]
    for y_ref, w_ref, nh in zip(y_refs, w_refs, n_head_major):
        if nh:
            y = jnp.concatenate([y_ref[j] for j in range(nh)], axis=-1)
        else:
            y = y_ref[...]
        h = h + jnp.dot(y, w_ref[...], preferred_element_type=F32)
    x1 = _layer_norm(h, g_ref[...], b_ref[...])
    x_hi = x1.astype(BF16)
    x_lo = (x1 - x_hi.astype(F32)).astype(BF16)

    def nt_dot(w, xv):
        return lax.dot_general(w, xv, (((1,), (1,)), ((), ())), preferred_element_type=F32)

    logits_t = (nt_dot(wrh_ref[...], x_hi) + nt_dot(wrh_ref[...], x_lo) + nt_dot(wrl_ref[...], x_hi)
                + brt_ref[...])
    route = jnp.concatenate(list(_route(logits_t)) + [jnp.zeros((ROUTE_ROWS - 3, tm), F32)], axis=0)
    route_ref[...] = route
    x1e_ref[:, 0:d] = x1
    x1e_ref[:, d:d + LANES] = jnp.concatenate([route, jnp.zeros((LANES - ROUTE_ROWS, tm), F32)], axis=0).T


def _mix_out(x, ys, ws, ln_g, ln_b, w_router, b_router, alpha, tm):
    b, s, d = x.shape
    n_head_major = tuple(y.shape[1] if hm else 0 for y, hm in ys)
    y_specs = []
    for (y, hm) in ys:
        if hm:
            y_specs.append(pl.BlockSpec((None, y.shape[1], tm, y.shape[3]), lambda bi, i: (bi, 0, i, 0)))
        else:
            y_specs.append(pl.BlockSpec((None, tm, y.shape[2]), lambda bi, i: (bi, i, 0)))
    w_specs = [pl.BlockSpec(w.shape, lambda bi, i: (0, 0)) for w in ws]
    row = pl.BlockSpec((1, d), lambda bi, i: (0, 0))
    tok = pl.BlockSpec((None, tm, d), lambda bi, i: (bi, i, 0))
    wr_t = w_router.T.astype(F32)
    wr_hi = wr_t.astype(BF16)
    wr_lo = (wr_t - wr_hi.astype(F32)).astype(BF16)
    wr_spec = pl.BlockSpec((N_EXPERTS, d), lambda bi, i: (0, 0))
    return pl.pallas_call(
        functools.partial(_mix_out_kernel, n_head_major=n_head_major, alpha=alpha),
        grid=(b, s // tm),
        in_specs=[tok] + y_specs + w_specs + [row, row, wr_spec, wr_spec,
                  pl.BlockSpec((N_EXPERTS, 1), lambda bi, i: (0, 0))],
        out_specs=[pl.BlockSpec((None, tm, d + LANES), lambda bi, i: (bi, i, 0)),
                   pl.BlockSpec((None, ROUTE_ROWS, tm), lambda bi, i: (bi, 0, i))],
        out_shape=[jax.ShapeDtypeStruct((b, s, d + LANES), F32),
                   jax.ShapeDtypeStruct((b, ROUTE_ROWS, s), F32)],
        compiler_params=_params("parallel", "parallel"),
        name="mix_out_ln_router",
    )(x, *[y for y, _ in ys], *ws, ln_g.reshape(1, d), ln_b.reshape(1, d),
      wr_hi, wr_lo, b_router.astype(F32).reshape(N_EXPERTS, 1))


MOE_TILE = 256
PLAN_COLS = 256
META_ROWS = SUBLANES


def _num_moe_tiles(t):
    return t // MOE_TILE + N_BUCKETS


def _plan_kernel(bid_ref, pos_ref, meta_ref):
    r, c = bid_ref.shape
    bid = bid_ref[...]
    before = (lax.broadcasted_iota(jnp.int32, (c, c), 0)
              < lax.broadcasted_iota(jnp.int32, (c, c), 1)).astype(BF16)
    rows_before = (lax.broadcasted_iota(jnp.int32, (r, r), 1)
                   < lax.broadcasted_iota(jnp.int32, (r, r), 0)).astype(BF16)
    lane = lax.broadcasted_iota(jnp.int32, (1, LANES), 1)
    tile_start = lane.astype(F32) * MOE_TILE

    def body(b, state):
        base, pos, tile_bucket, last_tile = state
        ind = (bid == lax.convert_element_type(b, F32)).astype(F32)
        within = jnp.dot(ind.astype(BF16), before, preferred_element_type=F32)
        row_total = jnp.sum(ind, axis=1, keepdims=True)
        row_off = jnp.dot(rows_before, jnp.broadcast_to(row_total, (r, LANES)).astype(BF16),
                          preferred_element_type=F32)[:, 0:1]
        count = jnp.sum(row_total, axis=0, keepdims=True)
        padded = jnp.floor((count + (MOE_TILE - 1)) * (1.0 / MOE_TILE)) * MOE_TILE
        pos = pos + ind * (base + row_off + within)
        end = base + padded
        tile_bucket = tile_bucket + (tile_start >= end).astype(F32)
        last_row = jnp.where(count > 0.0, end - MOE_TILE, -1.0)
        last_tile = jnp.where(lane == b, last_row, last_tile)
        return end, pos, tile_bucket, last_tile

    total, pos, tile_bucket, last_tile = lax.fori_loop(
        0, N_BUCKETS, body, (jnp.zeros((1, 1), F32), jnp.zeros((r, c), F32), jnp.zeros((1, LANES), F32),
                             jnp.full((1, LANES), -1.0, F32)))
    pos_ref[...] = pos.astype(jnp.int32)

    tb = jnp.minimum(tile_bucket, N_BUCKETS - 1.0)
    group = sum((tb >= g * PAIRS_PER_GROUP).astype(F32) for g in range(1, N_GROUPS))
    pair = tb - group * PAIRS_PER_GROUP
    i_lo = jnp.where(pair < 3, 0.0, jnp.where(pair < 5, 1.0, 2.0))
    i_hi = jnp.where(pair < 3, pair + 1.0, jnp.where(pair < 5, pair - 1.0, 3.0))
    meta = jnp.concatenate([group * GROUP_SIZE + i_lo, group * GROUP_SIZE + i_hi,
                            (tile_start < total).astype(F32),
                            jnp.where(lane == N_BUCKETS, total, last_tile),
                            jnp.zeros((META_ROWS - 4, LANES), F32)], axis=0)
    meta_ref[...] = meta.astype(jnp.int32)


def _plan(bucket_ids):
    r, c = bucket_ids.shape
    return pl.pallas_call(
        _plan_kernel,
        out_shape=[jax.ShapeDtypeStruct((r, c), jnp.int32), jax.ShapeDtypeStruct((META_ROWS, LANES), jnp.int32)],
        compiler_params=pltpu.CompilerParams(vmem_limit_bytes=VMEM_LIMIT_BYTES),
        name="moe_plan",
    )(bucket_ids)


DMA_PRIORITIES = 2


def _row_copy(src_ref, src_row, dst_ref, dst_row, sem):
    return pltpu.make_async_copy(src_ref.at[pl.ds(src_row, 1)], dst_ref.at[pl.ds(dst_row, 1)], sem)


def _dispatch_kernel(pos_ref, last_ref, x_ref, o_ref, inv_ref, zeros, inv, fill, sem, zsem, isem, *,
                     first_spare, n_tokens):
    tm = x_ref.shape[0]
    i = pl.program_id(0)

    @pl.when(i == 0)
    def _():
        zeros[...] = jnp.zeros_like(zeros)
        fill[...] = jnp.full(fill.shape, n_tokens, jnp.int32)
        to_smem = pltpu.make_async_copy(fill, inv, isem)
        to_smem.start()
        rows_in_use = last_ref[N_BUCKETS]
        clears = [(last_ref[b] >= 0, last_ref[b]) for b in range(N_BUCKETS)]
        clears += [(k * MOE_TILE >= rows_in_use, k * MOE_TILE)
                   for k in range(first_spare, o_ref.shape[0] // MOE_TILE)]

        def clear(row):
            start = row if isinstance(row, int) else pl.multiple_of(row, MOE_TILE)
            return pltpu.make_async_copy(zeros, o_ref.at[pl.ds(start, MOE_TILE)], zsem)

        for needed, row in clears:
            @pl.when(needed)
            def _():
                clear(row).start()
        for needed, row in clears:
            @pl.when(needed)
            def _():
                clear(row).wait()
        to_smem.wait()

    base = i * tm
    for r in range(tm):
        dst = pos_ref[base + r]
        inv[dst] = base + r
        _row_copy(x_ref, r, o_ref, dst, sem).start(priority=r % DMA_PRIORITIES)
    pltpu.make_async_copy(x_ref, o_ref.at[pl.ds(0, tm)], sem).wait()

    @pl.when(i == pl.num_programs(0) - 1)
    def _():
        to_hbm = pltpu.make_async_copy(inv, inv_ref, isem)
        to_hbm.start()
        to_hbm.wait()


def _dispatch(pos, last_tile_rows, xt, n_rows, tm):
    t = xt.shape[0]
    tile = xt.shape[1:]
    dma_sem = pltpu.SemaphoreType.DMA(())
    return pl.pallas_call(
        functools.partial(_dispatch_kernel, first_spare=t // MOE_TILE, n_tokens=t),
        grid_spec=pltpu.PrefetchScalarGridSpec(
            num_scalar_prefetch=2,
            grid=(t // tm,),
            in_specs=[pl.BlockSpec((tm,) + tile, lambda i, pos_ref, last_ref: (i,) + (0,) * len(tile))],
            out_specs=[pl.BlockSpec(memory_space=pl.ANY), pl.BlockSpec(memory_space=pl.ANY)],
            scratch_shapes=[pltpu.VMEM((MOE_TILE,) + tile, xt.dtype), pltpu.SMEM((n_rows,), jnp.int32),
                            pltpu.VMEM((n_rows,), jnp.int32), dma_sem, dma_sem, dma_sem]),
        out_shape=[jax.ShapeDtypeStruct((n_rows,) + tile, xt.dtype),
                   jax.ShapeDtypeStruct((n_rows,), jnp.int32)],
        compiler_params=_params("arbitrary"),
        name="moe_dispatch",
    )(pos, last_tile_rows, xt)


def _moe_tile_kernel(elo_ref, ehi_ref, valid_ref, inv, x_ref, wgl_ref, wul_ref, wdl_ref, wgh_ref, wuh_ref,
                     wdh_ref, m_hbm, o_even, o_odd, ssem_even, ssem_odd, wgl_b, wul_b, wdl_b, wgh_b, wuh_b,
                     wdh_b, *, n_tokens):
    d = wgl_ref.shape[0]
    k = pl.program_id(0)
    last = pl.num_programs(0) - 1
    prev = jnp.maximum(k - 1, 0)
    valid = valid_ref[k] != 0

    def scatter(tile, r, buf, sem, to_spare):
        tok = inv[tile * MOE_TILE + r]
        dst = jnp.where(to_spare | (tok >= n_tokens), n_tokens + r, tok)
        return _row_copy(buf, r, m_hbm, dst, sem)

    def scatter_done(buf, sem):
        pltpu.make_async_copy(buf, m_hbm.at[pl.ds(0, MOE_TILE)], sem).wait()

    @pl.when(k == 0)
    def _():
        o_odd[...] = jnp.zeros_like(o_odd)

    def refresh(e_ref, srcs, dsts):
        @pl.when(valid & ((k == 0) | (e_ref[k] != e_ref[prev])))
        def _():
            for src, dst in zip(srcs, dsts):
                dst[...] = src[...].astype(BF16)

    refresh(elo_ref, (wgl_ref, wul_ref, wdl_ref), (wgl_b, wul_b, wdl_b))
    refresh(ehi_ref, (wgh_ref, wuh_ref, wdh_ref), (wgh_b, wuh_b, wdh_b))

    def step(o_cur, ssem_cur, o_prv, ssem_prv):
        @pl.when(k > 0)
        def _():
            scatter_done(o_cur, ssem_cur)

        first = k == 0

        @pl.when(valid)
        def _():
            for r in range(MOE_TILE):
                scatter(prev, r, o_prv, ssem_prv, first).start(priority=r % DMA_PRIORITIES)
            x = x_ref[:, 0:d].astype(BF16)
            acc = None
            for wg_b, wu_b, wd_b, lane in ((wgl_b, wul_b, wdl_b, d + 1), (wgh_b, wuh_b, wdh_b, d + 2)):
                weight = x_ref[:, lane:lane + 1]
                hg = jnp.dot(x, wg_b[...], preferred_element_type=F32)
                hu = jnp.dot(x, wu_b[...], preferred_element_type=F32)
                hidden = (hg * _sigmoid(hg)) * hu * weight
                y = jnp.dot(hidden.astype(BF16), wd_b[...], preferred_element_type=F32)
                acc = y if acc is None else acc + y
            o_cur[...] = acc

        @pl.when(jnp.logical_not(valid))
        def _():
            def issue(r, carry):
                scatter(prev, r, o_prv, ssem_prv, first).start()
                return carry

            lax.fori_loop(0, MOE_TILE, issue, 0, unroll=8)
            o_cur[...] = jnp.zeros_like(o_cur)

        @pl.when(k == last)
        def _():
            scatter_done(o_prv, ssem_prv)

            def issue(r, carry):
                scatter(k, r, o_cur, ssem_cur, False).start()
                return carry

            lax.fori_loop(0, MOE_TILE, issue, 0, unroll=8)
            scatter_done(o_cur, ssem_cur)

    @pl.when(k % 2 == 0)
    def _():
        step(o_even, ssem_even, o_odd, ssem_odd)

    @pl.when(k % 2 == 1)
    def _():
        step(o_odd, ssem_odd, o_even, ssem_even)


def _moe_tiles(e_lo, e_hi, valid, inv, xs, wg, wu, wd, layer, n_tokens):
    n_rows, dw = xs.shape
    _, _, d, f = wg.shape
    up_lo = pl.BlockSpec((None, None, d, f), lambda k, lo, hi, ok, iv: (layer, lo[k], 0, 0))
    up_hi = pl.BlockSpec((None, None, d, f), lambda k, lo, hi, ok, iv: (layer, hi[k], 0, 0))
    down_lo = pl.BlockSpec((None, None, f, d), lambda k, lo, hi, ok, iv: (layer, lo[k], 0, 0))
    down_hi = pl.BlockSpec((None, None, f, d), lambda k, lo, hi, ok, iv: (layer, hi[k], 0, 0))
    up_b = pltpu.VMEM((d, f), BF16)
    down_b = pltpu.VMEM((f, d), BF16)
    o_buf = pltpu.VMEM((MOE_TILE, d), F32)
    dma_sem = pltpu.SemaphoreType.DMA(())
    return pl.pallas_call(
        functools.partial(_moe_tile_kernel, n_tokens=n_tokens),
        grid_spec=pltpu.PrefetchScalarGridSpec(
            num_scalar_prefetch=4,
            grid=(n_rows // MOE_TILE,),
            in_specs=[pl.BlockSpec((MOE_TILE, dw), lambda k, lo, hi, ok, iv: (k, 0)),
                      up_lo, up_lo, down_lo, up_hi, up_hi, down_hi],
            out_specs=pl.BlockSpec(memory_space=pl.ANY),
            scratch_shapes=[o_buf, o_buf, dma_sem, dma_sem, up_b, up_b, down_b, up_b, up_b, down_b]),
        out_shape=jax.ShapeDtypeStruct((n_tokens + MOE_TILE, d), F32),
        compiler_params=_params("arbitrary"),
        name="moe_tiles",
    )(e_lo, e_hi, valid, inv, xs, wg, wu, wd, wg, wu, wd)


def _ln_ple_kernel(x1_ref, m_ref, p_ref, g_ref, b_ref, wg_ref, bg_ref, wp_ref, o_ref, *, alpha):
    x2 = _layer_norm(alpha * x1_ref[...] + m_ref[...], g_ref[...], b_ref[...])
    gate = _sigmoid(jnp.dot(x2.astype(BF16), wg_ref[...], preferred_element_type=F32) + bg_ref[...])
    emb = jnp.dot(p_ref[...].astype(BF16), wp_ref[...], preferred_element_type=F32)
    o_ref[...] = x2 + gate * emb


def _ln_ple(x1e, m, p, layer, ln_g, ln_b, wg, bg, wp, alpha, tm):
    t = x1e.shape[0]
    d = m.shape[1]
    pd = p.shape[2]
    tok = pl.BlockSpec((tm, d), lambda i: (i, 0))
    row = pl.BlockSpec((1, d), lambda i: (0, 0))
    return pl.pallas_call(
        functools.partial(_ln_ple_kernel, alpha=alpha),
        grid=(t // tm,),
        in_specs=[tok, tok, pl.BlockSpec((None, tm, pd), lambda i: (layer, i, 0)), row, row,
                  pl.BlockSpec((d, d), lambda i: (0, 0)), row, pl.BlockSpec((pd, d), lambda i: (0, 0))],
        out_specs=tok,
        out_shape=jax.ShapeDtypeStruct((t, d), F32),
        compiler_params=_params("parallel"),
        name="ln_ple",
    )(x1e, m, p, ln_g.reshape(1, d), ln_b.reshape(1, d), wg, bg.reshape(1, d), wp)


def _pick_tile(n, target):
    t = min(n, target)
    while n % t:
        t //= 2
    return t


def kernel(x, p, positions, w_in_ab, w_out_ab, conv_w, conv_b, lru_w_r, lru_b_r, lru_w_i, lru_b_i, lru_lambda, w_qkv_c, w_out_c, sinks_c, ln_mix_g, ln_mix_b, ln_ffn_g, ln_ffn_b, w_router, b_router, exp_w_gate, exp_w_up, exp_w_down, ple_w_proj, ple_w_gate, ple_b_gate):
    b, s, d = x.shape
    depth = p.shape[0]
    t = b * s
    alpha = (2 * depth) ** 0.25
    tm = _pick_tile(s, 1024)
    assert t % MOE_TILE == 0 and t % PLAN_COLS == 0
    n_tiles = _num_moe_tiles(t)
    assert n_tiles <= LANES
    for i in range(depth):
        j = i // 2
        if i % 2 == 0:
            q, k, v, xr, gr = _proj_ab(x, w_in_ab[j].astype(BF16), tm)
            y_sb = _sb_attention(q, k, v, _pick_tile(s, 256), SB_HEADS)
            y_lru = _lru(xr, gr, conv_w[j], conv_b[j], _block_diag(lru_w_r[j]).astype(BF16), lru_b_r[j],
                         _block_diag(lru_w_i[j]).astype(BF16), lru_b_i[j], lru_lambda[j],
                         _pick_tile(s, 256))
            w_out = w_out_ab[j].astype(BF16)
            ys = [(y_sb, True), (y_lru, False)]
            ws = [w_out[:SB_WIDTH], w_out[SB_WIDTH:]]
        else:
            q, k, v = _proj_rope(x, positions, w_qkv_c[j].astype(BF16), tm)
            y = _swa(q, k, v, sinks_c[j])
            ys = [(y, False)]
            ws = [w_out_c[j].astype(BF16)]
        x1e, route = _mix_out(x, ys, ws, ln_mix_g[i], ln_mix_b[i], w_router, b_router, alpha, tm)
        x1e = x1e.reshape(t, d + LANES)
        pos, meta = _plan(route[:, 0, :].reshape(t // PLAN_COLS, PLAN_COLS))
        pos = pos.reshape(t)
        xs, inv = _dispatch(pos, meta[3, :N_BUCKETS + 1], x1e, n_tiles * MOE_TILE, _pick_tile(t, 512))
        m = _moe_tiles(meta[0, :n_tiles], meta[1, :n_tiles], meta[2, :n_tiles], inv, xs,
                       exp_w_gate, exp_w_up, exp_w_down, i, t)
        x = _ln_ple(x1e, m, p.reshape(depth, t, -1), i, ln_ffn_g[i], ln_ffn_b[i],
                    ple_w_gate[i].astype(BF16), ple_b_gate[i], ple_w_proj[i].astype(BF16), alpha,
                    _pick_tile(t, 1024)).reshape(b, s, d)
    return x
```

```python
import functools
import math

import jax
import jax.numpy as jnp
from jax import lax
from jax.experimental import pallas as pl
from jax.experimental.pallas import tpu as pltpu

HEAD_DIM = 64
SB_HEADS = 8
SB_WIDTH = SB_HEADS * HEAD_DIM
LRU_WIDTH = 512
LRU_BLOCKS = 8
LRU_C = 8.0
CONV_WIDTH = 4
SWA_HEADS = 16
SWA_KV_HEADS = 4
SWA_GROUP = SWA_HEADS // SWA_KV_HEADS
SWA_WINDOW = 128
ROPE_THETA = 10000.0
N_EXPERTS = 16
N_GROUPS = 4
GROUP_SIZE = N_EXPERTS // N_GROUPS
LN_EPS = 1e-5
Q_SCALE = HEAD_DIM ** -0.5

LANES = 128
SUBLANES = 8
VMEM_LIMIT_BYTES = 48 * 1024 * 1024

NEG_BIG = -1e30

BF16 = jnp.bfloat16
F32 = jnp.float32


def _params(*semantics):
    return pltpu.CompilerParams(dimension_semantics=semantics, vmem_limit_bytes=VMEM_LIMIT_BYTES)


def _softplus(z):
    return jnp.maximum(z, 0.0) + jnp.log(1.0 + jnp.exp(-jnp.abs(z)))


def _sigmoid(z):
    return 1.0 / (1.0 + jnp.exp(-z))


def _layer_norm(y, g, b):
    mu = jnp.mean(y, axis=-1, keepdims=True)
    d = y - mu
    var = jnp.mean(d * d, axis=-1, keepdims=True)
    return d * lax.rsqrt(var + LN_EPS) * g + b


def _proj_ab_kernel(x_ref, w_ref, q_ref, k_ref, v_ref, xr_ref, gr_ref):
    xb = x_ref[...].astype(BF16)

    def chunk(c):
        return jnp.dot(xb, w_ref[:, c * SB_WIDTH:(c + 1) * SB_WIDTH], preferred_element_type=F32)

    for c, (ref, scale) in enumerate(((q_ref, Q_SCALE), (k_ref, None), (v_ref, None))):
        r = chunk(c)
        if scale is not None:
            r = r * scale
        for h in range(SB_HEADS):
            ref[h] = r[:, h * HEAD_DIM:(h + 1) * HEAD_DIM].astype(BF16)
    xr_ref[...] = chunk(3)
    gr_ref[...] = chunk(4)


def _proj_ab(x, w_bf16, tm):
    b, s, d = x.shape
    n = w_bf16.shape[1]
    heads = jax.ShapeDtypeStruct((b, SB_HEADS, s, HEAD_DIM), BF16)
    flat = jax.ShapeDtypeStruct((b, s, LRU_WIDTH), F32)
    head_spec = pl.BlockSpec((None, SB_HEADS, tm, HEAD_DIM), lambda bi, i: (bi, 0, i, 0))
    flat_spec = pl.BlockSpec((None, tm, LRU_WIDTH), lambda bi, i: (bi, i, 0))
    return pl.pallas_call(
        _proj_ab_kernel,
        grid=(b, s // tm),
        in_specs=[pl.BlockSpec((None, tm, d), lambda bi, i: (bi, i, 0)),
                  pl.BlockSpec((d, n), lambda bi, i: (0, 0))],
        out_specs=[head_spec, head_spec, head_spec, flat_spec, flat_spec],
        out_shape=[heads, heads, heads, flat, flat],
        compiler_params=_params("parallel", "parallel"),
        name="proj_ab",
    )(x, w_bf16)


SB_DEAD_LOG_WEIGHT = -105.0
SB_MERGED_BLOCKS = 2


def _sb_attn_kernel(q_ref, k_ref, v_ref, o_ref, *, tq, hp):
    i = pl.program_id(2)
    row = lax.broadcasted_iota(jnp.int32, (tq, tq), 0)
    col = lax.broadcasted_iota(jnp.int32, (tq, tq), 1)
    minus_later = jnp.where(row > col, -1.0, 0.0).astype(BF16)
    causal = col < row

    def block(jb, carries, accs, masked):
        start = pl.multiple_of(jb * tq, tq)
        new_carries, new_accs = [], []
        for h in range(hp):
            kj = k_ref[h, pl.ds(start, tq), :]
            vj = v_ref[h, pl.ds(start, tq), :]
            z = lax.dot_general(q_ref[h], kj, (((1,), (1,)), ((), ())), preferred_element_type=F32)
            sp = _softplus(z)
            cost = jnp.where(causal, sp, 0.0) if masked else sp
            after = jnp.dot(cost.astype(BF16), minus_later, preferred_element_type=F32)
            w = jnp.exp((z - sp) + after + carries[h])
            if masked:
                w = jnp.where(causal, w, 0.0)
            new_accs.append(accs[h] + jnp.dot(w.astype(BF16), vj, preferred_element_type=F32))
            new_carries.append(carries[h] - jnp.sum(cost, axis=1, keepdims=True))
        return tuple(new_carries), tuple(new_accs)

    def live(carries):
        return functools.reduce(jnp.maximum, [jnp.max(c) for c in carries])

    zero = ((jnp.zeros((tq, 1), F32),) * hp, (jnp.zeros((tq, HEAD_DIM), F32),) * hp)

    def first(n):
        def run():
            state = block(i, *zero, True)
            for j in range(1, n):
                state = block(i - j, *state, False)
            return state
        return run

    done = jnp.minimum(i, SB_MERGED_BLOCKS - 1)
    carries, accs = lax.switch(done, [first(n) for n in range(1, SB_MERGED_BLOCKS + 1)])

    def cond(state):
        return (state[0] < i) & (state[1] > SB_DEAD_LOG_WEIGHT)

    def body(state):
        step, _, carries, accs = state
        carries, accs = block(i - 1 - step, carries, accs, False)
        return step + 1, live(carries), carries, accs

    _, _, _, accs = lax.while_loop(cond, body, (done, live(carries), carries, accs))
    for h in range(hp):
        o_ref[h] = accs[h].astype(o_ref.dtype)


def _sb_attention(q, k, v, tq, hp):
    b, h, s, dh = q.shape
    return pl.pallas_call(
        functools.partial(_sb_attn_kernel, tq=tq, hp=hp),
        grid=(b, h // hp, s // tq),
        in_specs=[pl.BlockSpec((None, hp, tq, dh), lambda bi, hi, i: (bi, hi, i, 0)),
                  pl.BlockSpec((None, hp, s, dh), lambda bi, hi, i: (bi, hi, 0, 0), pipeline_mode=pl.Buffered(1)),
                  pl.BlockSpec((None, hp, s, dh), lambda bi, hi, i: (bi, hi, 0, 0), pipeline_mode=pl.Buffered(1))],
        out_specs=pl.BlockSpec((None, hp, tq, dh), lambda bi, hi, i: (bi, hi, i, 0)),
        out_shape=jax.ShapeDtypeStruct((b, h, s, dh), BF16),
        compiler_params=_params("parallel", "parallel", "parallel"),
        name="sb_attention",
    )(q, k, v)


def _gelu_tanh(x):
    return 0.5 * x * (1.0 + jnp.tanh(math.sqrt(2.0 / math.pi) * (x + 0.044715 * (x * x * x))))


def _lru_kernel(xr_ref, gr_ref, cw_ref, cb_ref, wr_ref, br_ref, wi_ref, bi_ref, lam_ref, y_ref,
                xbuf, hprev, *, ts):
    @pl.when(pl.program_id(1) == 0)
    def _():
        xbuf[0:SUBLANES, :] = jnp.zeros((SUBLANES, LRU_WIDTH), F32)
        hprev[...] = jnp.zeros_like(hprev)

    xbuf[SUBLANES:SUBLANES + ts, :] = xr_ref[...]
    xc = cb_ref[...] + cw_ref[CONV_WIDTH - 1:CONV_WIDTH, :] * xbuf[SUBLANES:SUBLANES + ts, :]
    for kk in range(CONV_WIDTH - 1):
        off = SUBLANES - (CONV_WIDTH - 1) + kk
        xc = xc + cw_ref[kk:kk + 1, :] * xbuf[off:off + ts, :]
    xbuf[0:SUBLANES, :] = xbuf[ts:ts + SUBLANES, :]

    xcb = xc.astype(BF16)
    r = _sigmoid(jnp.dot(xcb, wr_ref[...], preferred_element_type=F32) + br_ref[...])
    gi = _sigmoid(jnp.dot(xcb, wi_ref[...], preferred_element_type=F32) + bi_ref[...])
    log_a = (-LRU_C) * r * _softplus(-lam_ref[...])
    a = jnp.exp(log_a)
    u = jnp.sqrt(1.0 - a * a) * (gi * xc)

    row = lax.broadcasted_iota(jnp.int32, (ts, LRU_WIDTH), 0)
    d = 1
    while d < ts:
        if d < SUBLANES:
            keep = row >= d
            a_sh = jnp.where(keep, pltpu.roll(a, d, axis=0), 1.0)
            u_sh = jnp.where(keep, pltpu.roll(u, d, axis=0), 0.0)
            u = a * u_sh + u
            a = a * a_sh
        else:
            u = jnp.concatenate([u[:d], a[d:] * u[:ts - d] + u[d:]], axis=0)
            a = jnp.concatenate([a[:d], a[d:] * a[:ts - d]], axis=0)
        d *= 2
    h = a * hprev[0:1, :] + u
    hprev[...] = jnp.broadcast_to(h[ts - 1:ts, :], hprev.shape)
    y_ref[...] = (_gelu_tanh(gr_ref[...]) * h).astype(y_ref.dtype)


def _lru(xr, gr, conv_w, conv_b, wr_bd, b_r, wi_bd, b_i, lam, ts):
    b, s, w = xr.shape
    seq_spec = pl.BlockSpec((None, ts, w), lambda bi, i: (bi, i, 0))

    def full(shape):
        return pl.BlockSpec(shape, lambda bi, i: (0,) * len(shape))

    return pl.pallas_call(
        functools.partial(_lru_kernel, ts=ts),
        grid=(b, s // ts),
        in_specs=[seq_spec, seq_spec, full((CONV_WIDTH, w)), full((1, w)), full((w, w)), full((1, w)),
                  full((w, w)), full((1, w)), full((1, w))],
        out_specs=seq_spec,
        out_shape=jax.ShapeDtypeStruct((b, s, w), BF16),
        scratch_shapes=[pltpu.VMEM((ts + 2 * SUBLANES, w), F32), pltpu.VMEM((SUBLANES, w), F32)],
        compiler_params=_params("parallel", "arbitrary"),
        name="rg_lru",
    )(xr, gr, conv_w, conv_b.reshape(1, w), wr_bd, b_r.reshape(1, w), wi_bd, b_i.reshape(1, w),
      lam.reshape(1, w))


def _block_diag(w):
    n, c, d = w.shape
    eye = jnp.eye(n, dtype=w.dtype)
    return (eye[:, None, :, None] * w[:, :, None, :]).reshape(n * c, n * d)


def _proj_rope_kernel(x_ref, pos_ref, freq_ref, w_ref, q_ref, k_ref, vt_ref):
    tm = x_ref.shape[0]
    xb = x_ref[...].astype(BF16)
    ang_t = freq_ref[...] * pos_ref[...].astype(F32)
    reps = LANES // (HEAD_DIM // 2)
    cos = jnp.concatenate([jnp.cos(ang_t)] * reps, axis=0).T
    sin = jnp.concatenate([jnp.sin(ang_t)] * reps, axis=0).T
    lane = lax.broadcasted_iota(jnp.int32, (tm, LANES), 1)
    first_half = (lane % HEAD_DIM) < (HEAD_DIM // 2)
    heads_per_slab = LANES // HEAD_DIM

    def rope(r):
        upper = pltpu.roll(r, LANES - HEAD_DIM // 2, axis=1)
        lower = pltpu.roll(r, HEAD_DIM // 2, axis=1)
        return r * cos + jnp.where(first_half, -upper, lower) * sin

    def emit(ref, n_heads, col0, rotary, scale):
        for slab in range(n_heads // heads_per_slab):
            c0 = col0 + slab * LANES
            r = jnp.dot(xb, w_ref[:, c0:c0 + LANES], preferred_element_type=F32)
            if rotary:
                r = rope(r)
            if scale is not None:
                r = r * scale
            for j in range(heads_per_slab):
                ref[slab * heads_per_slab + j] = r[:, j * HEAD_DIM:(j + 1) * HEAD_DIM].astype(BF16)

    emit(q_ref, SWA_HEADS, 0, True, Q_SCALE)
    emit(k_ref, SWA_KV_HEADS, SWA_HEADS * HEAD_DIM, True, None)
    v0 = (SWA_HEADS + SWA_KV_HEADS) * HEAD_DIM
    for slab in range(SWA_KV_HEADS // heads_per_slab):
        r = jnp.dot(xb, w_ref[:, v0 + slab * LANES:v0 + (slab + 1) * LANES], preferred_element_type=F32)
        rt = r.T
        for j in range(heads_per_slab):
            vt_ref[slab * heads_per_slab + j] = rt[j * HEAD_DIM:(j + 1) * HEAD_DIM, :].astype(BF16)


def _proj_rope(x, positions, w_bf16, tm):
    b, s, d = x.shape
    n = w_bf16.shape[1]
    half = HEAD_DIM // 2
    inv_freq = (ROPE_THETA ** (-jnp.arange(half, dtype=F32) / half)).reshape(half, 1)

    def heads(nh):
        return (jax.ShapeDtypeStruct((b, nh, s, HEAD_DIM), BF16),
                pl.BlockSpec((None, nh, tm, HEAD_DIM), lambda bi, i: (bi, 0, i, 0)))

    (qs, qspec), (ks, kspec) = heads(SWA_HEADS), heads(SWA_KV_HEADS)
    vs = jax.ShapeDtypeStruct((b, SWA_KV_HEADS, HEAD_DIM, s), BF16)
    vspec = pl.BlockSpec((None, SWA_KV_HEADS, HEAD_DIM, tm), lambda bi, i: (bi, 0, 0, i))
    return pl.pallas_call(
        _proj_rope_kernel,
        grid=(b, s // tm),
        in_specs=[pl.BlockSpec((None, tm, d), lambda bi, i: (bi, i, 0)),
                  pl.BlockSpec((None, 1, tm), lambda bi, i: (bi, 0, i)),
                  pl.BlockSpec((half, 1), lambda bi, i: (0, 0)),
                  pl.BlockSpec((d, n), lambda bi, i: (0, 0))],
        out_specs=[qspec, kspec, vspec],
        out_shape=[qs, ks, vs],
        compiler_params=_params("parallel", "parallel"),
        name="proj_rope",
    )(x, positions.reshape(b, 1, s), inv_freq, w_bf16)


def _reduce_rows(x, op):
    while x.shape[0] > SUBLANES:
        half = x.shape[0] // 2
        x = op(x[:half], x[half:])
    for shift in (4, 2, 1):
        x = op(x, pltpu.roll(x, shift, axis=0))
    return x[0:1]


SWA_BLOCKS_PER_STEP = 8


def _swa_kernel(q_ref, kp_ref, kc_ref, vtp_ref, vtc_ref, sink_ref, o_ref):
    i = pl.program_id(1)
    w = SWA_WINDOW
    key = lax.broadcasted_iota(jnp.int32, (2 * w, w), 0)
    qry = lax.broadcasted_iota(jnp.int32, (2 * w, w), 1)
    dist = qry + w - key
    band = (dist >= 0) & (dist < w)
    for blk in range(SWA_BLOCKS_PER_STEP):
        visible = band if blk else band & ((key >= w) | (i > 0))
        bias = jnp.concatenate([jnp.where(visible, 0.0, NEG_BIG)] * SWA_GROUP, axis=1)
        outs = []
        for kv in range(SWA_KV_HEADS):
            if blk:
                kk = kc_ref[kv, (blk - 1) * w:(blk + 1) * w, :]
                vvt = vtc_ref[kv, :, (blk - 1) * w:(blk + 1) * w]
            else:
                kk = jnp.concatenate([kp_ref[kv], kc_ref[kv, 0:w, :]], axis=0)
                vvt = jnp.concatenate([vtp_ref[kv], vtc_ref[kv, :, 0:w]], axis=1)
            qg = jnp.concatenate([q_ref[kv * SWA_GROUP + g, blk * w:(blk + 1) * w, :]
                                  for g in range(SWA_GROUP)], axis=0)
            st = lax.dot_general(kk, qg, (((1,), (1,)), ((), ())), preferred_element_type=F32) + bias
            sink = sink_ref[kv:kv + 1, :]
            m = jnp.maximum(_reduce_rows(st, jnp.maximum), sink)
            p = jnp.exp(st - m)
            denom = _reduce_rows(p, jnp.add) + jnp.exp(sink - m)
            ot = jnp.dot(vvt, p.astype(BF16), preferred_element_type=F32) / denom
            outs.extend(ot[:, g * w:(g + 1) * w] for g in range(SWA_GROUP))
        o_ref[blk * w:(blk + 1) * w, :] = jnp.concatenate(outs, axis=0).T.astype(o_ref.dtype)


def _swa(q, k, vt, sinks):
    b, nh, s, dh = q.shape
    nkv = k.shape[1]
    w = SWA_WINDOW
    n = SWA_BLOCKS_PER_STEP
    assert s % (n * w) == 0
    cur = pl.BlockSpec((None, nkv, n * w, dh), lambda bi, i: (bi, 0, i, 0))
    prev = pl.BlockSpec((None, nkv, w, dh), lambda bi, i: (bi, 0, jnp.maximum(n * i - 1, 0), 0))
    cur_t = pl.BlockSpec((None, nkv, dh, n * w), lambda bi, i: (bi, 0, 0, i))
    prev_t = pl.BlockSpec((None, nkv, dh, w), lambda bi, i: (bi, 0, 0, jnp.maximum(n * i - 1, 0)))
    sink_tile = jnp.repeat(sinks.astype(F32).reshape(nkv, nh // nkv), w, axis=1)
    return pl.pallas_call(
        _swa_kernel,
        grid=(b, s // (n * w)),
        in_specs=[pl.BlockSpec((None, nh, n * w, dh), lambda bi, i: (bi, 0, i, 0)),
                  prev, cur, prev_t, cur_t,
                  pl.BlockSpec(sink_tile.shape, lambda bi, i: (0, 0))],
        out_specs=pl.BlockSpec((None, n * w, nh * dh), lambda bi, i: (bi, i, 0)),
        out_shape=jax.ShapeDtypeStruct((b, s, nh * dh), BF16),
        compiler_params=_params("parallel", "parallel"),
        name="swa",
    )(q, k, k, vt, vt, sink_tile)


PAIRS_PER_GROUP = GROUP_SIZE * (GROUP_SIZE - 1) // 2
N_BUCKETS = N_GROUPS * PAIRS_PER_GROUP
ROUTE_ROWS = SUBLANES


def _route(logits_t):
    rows = [logits_t[e:e + 1, :] for e in range(N_EXPERTS)]
    mx = functools.reduce(jnp.maximum, rows)
    ex = [jnp.exp(r - mx) for r in rows]
    total = functools.reduce(lambda p, q: p + q, ex)
    probs = [e / total for e in ex]

    group_score = []
    for g in range(N_GROUPS):
        a, b, c, d = probs[g * GROUP_SIZE:(g + 1) * GROUP_SIZE]
        hi1, lo1 = jnp.maximum(a, b), jnp.minimum(a, b)
        hi2, lo2 = jnp.maximum(c, d), jnp.minimum(c, d)
        top1 = jnp.maximum(hi1, hi2)
        top2 = jnp.maximum(jnp.minimum(hi1, hi2), jnp.maximum(lo1, lo2))
        group_score.append(top1 + top2)
    best = functools.reduce(jnp.maximum, group_score)
    g_sel = jnp.full(best.shape, N_GROUPS - 1, jnp.int32)
    for g in range(N_GROUPS - 2, -1, -1):
        g_sel = jnp.where(group_score[g] == best, g, g_sel)

    in_group = []
    for j in range(GROUP_SIZE):
        val = probs[(N_GROUPS - 1) * GROUP_SIZE + j]
        for g in range(N_GROUPS - 2, -1, -1):
            val = jnp.where(g_sel == g, probs[g * GROUP_SIZE + j], val)
        in_group.append(val)

    def first_argmax(vals):
        m = functools.reduce(jnp.maximum, vals)
        idx = jnp.full(m.shape, GROUP_SIZE - 1, jnp.int32)
        for j in range(GROUP_SIZE - 2, -1, -1):
            idx = jnp.where(vals[j] == m, j, idx)
        return m, idx

    w1, i1 = first_argmax(in_group)
    rest = [jnp.where(i1 == j, -1.0, in_group[j]) for j in range(GROUP_SIZE)]
    w2, i2 = first_argmax(rest)
    norm = w1 + w2
    first_is_lo = i1 < i2
    i_lo = jnp.minimum(i1, i2)
    i_hi = jnp.maximum(i1, i2)
    pair = jnp.where(i_lo == 0, i_hi - 1, jnp.where(i_lo == 1, i_hi + 1, PAIRS_PER_GROUP - 1))
    bucket = (g_sel * PAIRS_PER_GROUP + pair).astype(F32)
    w_lo = jnp.where(first_is_lo, w1, w2) / norm
    w_hi = jnp.where(first_is_lo, w2, w1) / norm
    return bucket, w_lo, w_hi


def _mix_out_kernel(*refs, n_head_major, alpha):
    x_ref = refs[0]
    y_refs = refs[1:1 + len(n_head_major)]
    w_refs = refs[1 + len(n_head_major):1 + 2 * len(n_head_major)]
    g_ref, b_ref, wrh_ref, wrl_ref, brt_ref, x1e_ref, route_ref = refs[1 + 2 * len(n_head_major):]
    tm, d = x_ref.shape
    h = alpha * x_ref[...]
    for y_ref, w_ref, nh in zip(y_refs, w_refs, n_head_major):
        if nh:
            y = jnp.concatenate([y_ref[j] for j in range(nh)], axis=-1)
        else:
            y = y_ref[...]
        h = h + jnp.dot(y, w_ref[...], preferred_element_type=F32)
    x1 = _layer_norm(h, g_ref[...], b_ref[...])
    x_hi = x1.astype(BF16)
    x_lo = (x1 - x_hi.astype(F32)).astype(BF16)

    def nt_dot(w, xv):
        return lax.dot_general(w, xv, (((1,), (1,)), ((), ())), preferred_element_type=F32)

    logits_t = (nt_dot(wrh_ref[...], x_hi) + nt_dot(wrh_ref[...], x_lo) + nt_dot(wrl_ref[...], x_hi)
                + brt_ref[...])
    route = jnp.concatenate(list(_route(logits_t)) + [jnp.zeros((ROUTE_ROWS - 3, tm), F32)], axis=0)
    route_ref[...] = route
    x1e_ref[:, 0:d] = x1
    x1e_ref[:, d:d + LANES] = jnp.concatenate([route, jnp.zeros((LANES - ROUTE_ROWS, tm), F32)], axis=0).T


def _mix_out(x, ys, ws, ln_g, ln_b, w_router, b_router, alpha, tm):
    b, s, d = x.shape
    n_head_major = tuple(y.shape[1] if hm else 0 for y, hm in ys)
    y_specs = []
    for (y, hm) in ys:
        if hm:
            y_specs.append(pl.BlockSpec((None, y.shape[1], tm, y.shape[3]), lambda bi, i: (bi, 0, i, 0)))
        else:
            y_specs.append(pl.BlockSpec((None, tm, y.shape[2]), lambda bi, i: (bi, i, 0)))
    w_specs = [pl.BlockSpec(w.shape, lambda bi, i: (0, 0)) for w in ws]
    row = pl.BlockSpec((1, d), lambda bi, i: (0, 0))
    tok = pl.BlockSpec((None, tm, d), lambda bi, i: (bi, i, 0))
    wr_t = w_router.T.astype(F32)
    wr_hi = wr_t.astype(BF16)
    wr_lo = (wr_t - wr_hi.astype(F32)).astype(BF16)
    wr_spec = pl.BlockSpec((N_EXPERTS, d), lambda bi, i: (0, 0))
    return pl.pallas_call(
        functools.partial(_mix_out_kernel, n_head_major=n_head_major, alpha=alpha),
        grid=(b, s // tm),
        in_specs=[tok] + y_specs + w_specs + [row, row, wr_spec, wr_spec,
                  pl.BlockSpec((N_EXPERTS, 1), lambda bi, i: (0, 0))],
        out_specs=[pl.BlockSpec((None, tm, d + LANES), lambda bi, i: (bi, i, 0)),
                   pl.BlockSpec((None, ROUTE_ROWS, tm), lambda bi, i: (bi, 0, i))],
        out_shape=[jax.ShapeDtypeStruct((b, s, d + LANES), F32),
                   jax.ShapeDtypeStruct((b, ROUTE_ROWS, s), F32)],
        compiler_params=_params("parallel", "parallel"),
        name="mix_out_ln_router",
    )(x, *[y for y, _ in ys], *ws, ln_g.reshape(1, d), ln_b.reshape(1, d),
      wr_hi, wr_lo, b_router.astype(F32).reshape(N_EXPERTS, 1))


MOE_TILE = 256
PLAN_COLS = 256
META_ROWS = SUBLANES


def _num_moe_tiles(t):
    return t // MOE_TILE + N_BUCKETS


def _plan_kernel(bid_ref, pos_ref, meta_ref):
    r, c = bid_ref.shape
    bid = bid_ref[...]
    before = (lax.broadcasted_iota(jnp.int32, (c, c), 0)
              < lax.broadcasted_iota(jnp.int32, (c, c), 1)).astype(BF16)
    rows_before = (lax.broadcasted_iota(jnp.int32, (r, r), 1)
                   < lax.broadcasted_iota(jnp.int32, (r, r), 0)).astype(BF16)
    lane = lax.broadcasted_iota(jnp.int32, (1, LANES), 1)
    tile_start = lane.astype(F32) * MOE_TILE

    def body(b, state):
        base, pos, tile_bucket, last_tile = state
        ind = (bid == lax.convert_element_type(b, F32)).astype(F32)
        within = jnp.dot(ind.astype(BF16), before, preferred_element_type=F32)
        row_total = jnp.sum(ind, axis=1, keepdims=True)
        row_off = jnp.dot(rows_before, jnp.broadcast_to(row_total, (r, LANES)).astype(BF16),
                          preferred_element_type=F32)[:, 0:1]
        count = jnp.sum(row_total, axis=0, keepdims=True)
        padded = jnp.floor((count + (MOE_TILE - 1)) * (1.0 / MOE_TILE)) * MOE_TILE
        pos = pos + ind * (base + row_off + within)
        end = base + padded
        tile_bucket = tile_bucket + (tile_start >= end).astype(F32)
        last_row = jnp.where(count > 0.0, end - MOE_TILE, -1.0)
        last_tile = jnp.where(lane == b, last_row, last_tile)
        return end, pos, tile_bucket, last_tile

    total, pos, tile_bucket, last_tile = lax.fori_loop(
        0, N_BUCKETS, body, (jnp.zeros((1, 1), F32), jnp.zeros((r, c), F32), jnp.zeros((1, LANES), F32),
                             jnp.full((1, LANES), -1.0, F32)))
    pos_ref[...] = pos.astype(jnp.int32)

    tb = jnp.minimum(tile_bucket, N_BUCKETS - 1.0)
    group = sum((tb >= g * PAIRS_PER_GROUP).astype(F32) for g in range(1, N_GROUPS))
    pair = tb - group * PAIRS_PER_GROUP
    i_lo = jnp.where(pair < 3, 0.0, jnp.where(pair < 5, 1.0, 2.0))
    i_hi = jnp.where(pair < 3, pair + 1.0, jnp.where(pair < 5, pair - 1.0, 3.0))
    meta = jnp.concatenate([group * GROUP_SIZE + i_lo, group * GROUP_SIZE + i_hi,
                            (tile_start < total).astype(F32),
                            jnp.where(lane == N_BUCKETS, total, last_tile),
                            jnp.zeros((META_ROWS - 4, LANES), F32)], axis=0)
    meta_ref[...] = meta.astype(jnp.int32)


def _plan(bucket_ids):
    r, c = bucket_ids.shape
    return pl.pallas_call(
        _plan_kernel,
        out_shape=[jax.ShapeDtypeStruct((r, c), jnp.int32), jax.ShapeDtypeStruct((META_ROWS, LANES), jnp.int32)],
        compiler_params=pltpu.CompilerParams(vmem_limit_bytes=VMEM_LIMIT_BYTES),
        name="moe_plan",
    )(bucket_ids)


DMA_PRIORITIES = 2


def _row_copy(src_ref, src_row, dst_ref, dst_row, sem):
    return pltpu.make_async_copy(src_ref.at[pl.ds(src_row, 1)], dst_ref.at[pl.ds(dst_row, 1)], sem)


def _dispatch_kernel(pos_ref, last_ref, x_ref, o_ref, inv_ref, zeros, inv, fill, sem, zsem, isem, *,
                     first_spare, n_tokens):
    tm = x_ref.shape[0]
    i = pl.program_id(0)

    @pl.when(i == 0)
    def _():
        zeros[...] = jnp.zeros_like(zeros)
        fill[...] = jnp.full(fill.shape, n_tokens, jnp.int32)
        to_smem = pltpu.make_async_copy(fill, inv, isem)
        to_smem.start()
        rows_in_use = last_ref[N_BUCKETS]
        clears = [(last_ref[b] >= 0, last_ref[b]) for b in range(N_BUCKETS)]
        clears += [(k * MOE_TILE >= rows_in_use, k * MOE_TILE)
                   for k in range(first_spare, o_ref.shape[0] // MOE_TILE)]

        def clear(row):
            start = row if isinstance(row, int) else pl.multiple_of(row, MOE_TILE)
            return pltpu.make_async_copy(zeros, o_ref.at[pl.ds(start, MOE_TILE)], zsem)

        for needed, row in clears:
            @pl.when(needed)
            def _():
                clear(row).start()
        for needed, row in clears:
            @pl.when(needed)
            def _():
                clear(row).wait()
        to_smem.wait()

    base = i * tm
    for r in range(tm):
        dst = pos_ref[base + r]
        inv[dst] = base + r
        _row_copy(x_ref, r, o_ref, dst, sem).start(priority=r % DMA_PRIORITIES)
    pltpu.make_async_copy(x_ref, o_ref.at[pl.ds(0, tm)], sem).wait()

    @pl.when(i == pl.num_programs(0) - 1)
    def _():
        to_hbm = pltpu.make_async_copy(inv, inv_ref, isem)
        to_hbm.start()
        to_hbm.wait()


def _dispatch(pos, last_tile_rows, xt, n_rows, tm):
    t = xt.shape[0]
    tile = xt.shape[1:]
    dma_sem = pltpu.SemaphoreType.DMA(())
    return pl.pallas_call(
        functools.partial(_dispatch_kernel, first_spare=t // MOE_TILE, n_tokens=t),
        grid_spec=pltpu.PrefetchScalarGridSpec(
            num_scalar_prefetch=2,
            grid=(t // tm,),
            in_specs=[pl.BlockSpec((tm,) + tile, lambda i, pos_ref, last_ref: (i,) + (0,) * len(tile))],
            out_specs=[pl.BlockSpec(memory_space=pl.ANY), pl.BlockSpec(memory_space=pl.ANY)],
            scratch_shapes=[pltpu.VMEM((MOE_TILE,) + tile, xt.dtype), pltpu.SMEM((n_rows,), jnp.int32),
                            pltpu.VMEM((n_rows,), jnp.int32), dma_sem, dma_sem, dma_sem]),
        out_shape=[jax.ShapeDtypeStruct((n_rows,) + tile, xt.dtype),
                   jax.ShapeDtypeStruct((n_rows,), jnp.int32)],
        compiler_params=_params("arbitrary"),
        name="moe_dispatch",
    )(pos, last_tile_rows, xt)


def _moe_tile_kernel(elo_ref, ehi_ref, valid_ref, inv, x_ref, wgl_ref, wul_ref, wdl_ref, wgh_ref, wuh_ref,
                     wdh_ref, m_hbm, o_even, o_odd, ssem_even, ssem_odd, wgl_b, wul_b, wdl_b, wgh_b, wuh_b,
                     wdh_b, *, n_tokens):
    d = wgl_ref.shape[0]
    k = pl.program_id(0)
    last = pl.num_programs(0) - 1
    prev = jnp.maximum(k - 1, 0)
    valid = valid_ref[k] != 0

    def scatter(tile, r, buf, sem, to_spare):
        tok = inv[tile * MOE_TILE + r]
        dst = jnp.where(to_spare | (tok >= n_tokens), n_tokens + r, tok)
        return _row_copy(buf, r, m_hbm, dst, sem)

    def scatter_done(buf, sem):
        pltpu.make_async_copy(buf, m_hbm.at[pl.ds(0, MOE_TILE)], sem).wait()

    @pl.when(k == 0)
    def _():
        o_odd[...] = jnp.zeros_like(o_odd)

    def refresh(e_ref, srcs, dsts):
        @pl.when(valid & ((k == 0) | (e_ref[k] != e_ref[prev])))
        def _():
            for src, dst in zip(srcs, dsts):
                dst[...] = src[...].astype(BF16)

    refresh(elo_ref, (wgl_ref, wul_ref, wdl_ref), (wgl_b, wul_b, wdl_b))
    refresh(ehi_ref, (wgh_ref, wuh_ref, wdh_ref), (wgh_b, wuh_b, wdh_b))

    def tile_valid(j):
        return (j >= 0) & (valid_ref[jnp.maximum(j, 0)] != 0)

    def scattered_in(j):
        return (j == 0) | tile_valid(j) | tile_valid(j - 1)

    def step(o_cur, ssem_cur, o_prv, ssem_prv):
        @pl.when((k > 0) & scattered_in(k - 1))
        def _():
            scatter_done(o_cur, ssem_cur)

        first = k == 0

        @pl.when(valid)
        def _():
            for r in range(MOE_TILE):
                scatter(prev, r, o_prv, ssem_prv, first).start(priority=r % DMA_PRIORITIES)
            x = x_ref[:, 0:d].astype(BF16)
            acc = None
            for wg_b, wu_b, wd_b, lane in ((wgl_b, wul_b, wdl_b, d + 1), (wgh_b, wuh_b, wdh_b, d + 2)):
                weight = x_ref[:, lane:lane + 1]
                hg = jnp.dot(x, wg_b[...], preferred_element_type=F32)
                hu = jnp.dot(x, wu_b[...], preferred_element_type=F32)
                hidden = (hg * _sigmoid(hg)) * hu * weight
                y = jnp.dot(hidden.astype(BF16), wd_b[...], preferred_element_type=F32)
                acc = y if acc is None else acc + y
            o_cur[...] = acc

        @pl.when(jnp.logical_not(valid) & tile_valid(k - 1))
        def _():
            def issue(r, carry):
                scatter(prev, r, o_prv, ssem_prv, first).start()
                return carry

            lax.fori_loop(0, MOE_TILE, issue, 0, unroll=8)

        @pl.when((k == last) & scattered_in(k))
        def _():
            scatter_done(o_prv, ssem_prv)

        @pl.when((k == last) & valid)
        def _():
            def issue(r, carry):
                scatter(k, r, o_cur, ssem_cur, False).start()
                return carry

            lax.fori_loop(0, MOE_TILE, issue, 0, unroll=8)
            scatter_done(o_cur, ssem_cur)

    @pl.when(k % 2 == 0)
    def _():
        step(o_even, ssem_even, o_odd, ssem_odd)

    @pl.when(k % 2 == 1)
    def _():
        step(o_odd, ssem_odd, o_even, ssem_even)


def _moe_tiles(e_lo, e_hi, valid, inv, xs, wg, wu, wd, layer, n_tokens):
    n_rows, dw = xs.shape
    _, _, d, f = wg.shape
    up_lo = pl.BlockSpec((None, None, d, f), lambda k, lo, hi, ok, iv: (layer, lo[k], 0, 0))
    up_hi = pl.BlockSpec((None, None, d, f), lambda k, lo, hi, ok, iv: (layer, hi[k], 0, 0))
    down_lo = pl.BlockSpec((None, None, f, d), lambda k, lo, hi, ok, iv: (layer, lo[k], 0, 0))
    down_hi = pl.BlockSpec((None, None, f, d), lambda k, lo, hi, ok, iv: (layer, hi[k], 0, 0))
    up_b = pltpu.VMEM((d, f), BF16)
    down_b = pltpu.VMEM((f, d), BF16)
    o_buf = pltpu.VMEM((MOE_TILE, d), F32)
    dma_sem = pltpu.SemaphoreType.DMA(())
    return pl.pallas_call(
        functools.partial(_moe_tile_kernel, n_tokens=n_tokens),
        grid_spec=pltpu.PrefetchScalarGridSpec(
            num_scalar_prefetch=4,
            grid=(n_rows // MOE_TILE,),
            in_specs=[pl.BlockSpec((MOE_TILE, dw), lambda k, lo, hi, ok, iv: (k, 0)),
                      up_lo, up_lo, down_lo, up_hi, up_hi, down_hi],
            out_specs=pl.BlockSpec(memory_space=pl.ANY),
            scratch_shapes=[o_buf, o_buf, dma_sem, dma_sem, up_b, up_b, down_b, up_b, up_b, down_b]),
        out_shape=jax.ShapeDtypeStruct((n_tokens + MOE_TILE, d), F32),
        compiler_params=_params("arbitrary"),
        name="moe_tiles",
    )(e_lo, e_hi, valid, inv, xs, wg, wu, wd, wg, wu, wd)


def _ln_ple_kernel(x1_ref, m_ref, p_ref, g_ref, b_ref, wg_ref, bg_ref, wp_ref, o_ref, *, alpha):
    x2 = _layer_norm(alpha * x1_ref[...] + m_ref[...], g_ref[...], b_ref[...])
    gate = _sigmoid(jnp.dot(x2.astype(BF16), wg_ref[...], preferred_element_type=F32) + bg_ref[...])
    emb = jnp.dot(p_ref[...].astype(BF16), wp_ref[...], preferred_element_type=F32)
    o_ref[...] = x2 + gate * emb


def _ln_ple(x1e, m, p, layer, ln_g, ln_b, wg, bg, wp, alpha, tm):
    t = x1e.shape[0]
    d = m.shape[1]
    pd = p.shape[2]
    tok = pl.BlockSpec((tm, d), lambda i: (i, 0))
    row = pl.BlockSpec((1, d), lambda i: (0, 0))
    return pl.pallas_call(
        functools.partial(_ln_ple_kernel, alpha=alpha),
        grid=(t // tm,),
        in_specs=[tok, tok, pl.BlockSpec((None, tm, pd), lambda i: (layer, i, 0)), row, row,
                  pl.BlockSpec((d, d), lambda i: (0, 0)), row, pl.BlockSpec((pd, d), lambda i: (0, 0))],
        out_specs=tok,
        out_shape=jax.ShapeDtypeStruct((t, d), F32),
        compiler_params=_params("parallel"),
        name="ln_ple",
    )(x1e, m, p, ln_g.reshape(1, d), ln_b.reshape(1, d), wg, bg.reshape(1, d), wp)


def _pick_tile(n, target):
    t = min(n, target)
    while n % t:
        t //= 2
    return t


def kernel(x, p, positions, w_in_ab, w_out_ab, conv_w, conv_b, lru_w_r, lru_b_r, lru_w_i, lru_b_i, lru_lambda, w_qkv_c, w_out_c, sinks_c, ln_mix_g, ln_mix_b, ln_ffn_g, ln_ffn_b, w_router, b_router, exp_w_gate, exp_w_up, exp_w_down, ple_w_proj, ple_w_gate, ple_b_gate):
    b, s, d = x.shape
    depth = p.shape[0]
    t = b * s
    alpha = (2 * depth) ** 0.25
    tm = _pick_tile(s, 1024)
    assert t % MOE_TILE == 0 and t % PLAN_COLS == 0
    n_tiles = _num_moe_tiles(t)
    assert n_tiles <= LANES
    for i in range(depth):
        j = i // 2
        if i % 2 == 0:
            q, k, v, xr, gr = _proj_ab(x, w_in_ab[j].astype(BF16), tm)
            y_sb = _sb_attention(q, k, v, _pick_tile(s, 256), SB_HEADS)
            y_lru = _lru(xr, gr, conv_w[j], conv_b[j], _block_diag(lru_w_r[j]).astype(BF16), lru_b_r[j],
                         _block_diag(lru_w_i[j]).astype(BF16), lru_b_i[j], lru_lambda[j],
                         _pick_tile(s, 256))
            w_out = w_out_ab[j].astype(BF16)
            ys = [(y_sb, True), (y_lru, False)]
            ws = [w_out[:SB_WIDTH], w_out[SB_WIDTH:]]
        else:
            q, k, v = _proj_rope(x, positions, w_qkv_c[j].astype(BF16), tm)
            y = _swa(q, k, v, sinks_c[j])
            ys = [(y, False)]
            ws = [w_out_c[j].astype(BF16)]
        x1e, route = _mix_out(x, ys, ws, ln_mix_g[i], ln_mix_b[i], w_router, b_router, alpha, tm)
        x1e = x1e.reshape(t, d + LANES)
        pos, meta = _plan(route[:, 0, :].reshape(t // PLAN_COLS, PLAN_COLS))
        pos = pos.reshape(t)
        xs, inv = _dispatch(pos, meta[3, :N_BUCKETS + 1], x1e, n_tiles * MOE_TILE, _pick_tile(t, 512))
        m = _moe_tiles(meta[0, :n_tiles], meta[1, :n_tiles], meta[2, :n_tiles], inv, xs,
                       exp_w_gate, exp_w_up, exp_w_down, i, t)
        x = _ln_ple(x1e, m, p.reshape(depth, t, -1), i, ln_ffn_g[i], ln_ffn_b[i],
                    ple_w_gate[i].astype(BF16), ple_b_gate[i], ple_w_proj[i].astype(BF16), alpha,
                    _pick_tile(t, 1024)).reshape(b, s, d)
    return x
```

```python
import functools
import math

import jax
import jax.numpy as jnp
from jax import lax
from jax.experimental import pallas as pl
from jax.experimental.pallas import tpu as pltpu

HEAD_DIM = 64
SB_HEADS = 8
SB_WIDTH = SB_HEADS * HEAD_DIM
LRU_WIDTH = 512
LRU_BLOCKS = 8
LRU_C = 8.0
CONV_WIDTH = 4
SWA_HEADS = 16
SWA_KV_HEADS = 4
SWA_GROUP = SWA_HEADS // SWA_KV_HEADS
SWA_WINDOW = 128
ROPE_THETA = 10000.0
N_EXPERTS = 16
N_GROUPS = 4
GROUP_SIZE = N_EXPERTS // N_GROUPS
LN_EPS = 1e-5
Q_SCALE = HEAD_DIM ** -0.5

LANES = 128
SUBLANES = 8
VMEM_LIMIT_BYTES = 48 * 1024 * 1024

NEG_BIG = -1e30

BF16 = jnp.bfloat16
F32 = jnp.float32


def _params(*semantics):
    return pltpu.CompilerParams(dimension_semantics=semantics, vmem_limit_bytes=VMEM_LIMIT_BYTES)


def _softplus(z):
    return jnp.maximum(z, 0.0) + jnp.log(1.0 + jnp.exp(-jnp.abs(z)))


def _sigmoid(z):
    return 1.0 / (1.0 + jnp.exp(-z))


def _layer_norm(y, g, b):
    mu = jnp.mean(y, axis=-1, keepdims=True)
    d = y - mu
    var = jnp.mean(d * d, axis=-1, keepdims=True)
    return d * lax.rsqrt(var + LN_EPS) * g + b


def _proj_ab_kernel(x_ref, w_ref, q_ref, k_ref, v_ref, xr_ref, gr_ref):
    xb = x_ref[...].astype(BF16)

    def chunk(c):
        return jnp.dot(xb, w_ref[:, c * SB_WIDTH:(c + 1) * SB_WIDTH], preferred_element_type=F32)

    for c, (ref, scale) in enumerate(((q_ref, Q_SCALE), (k_ref, None), (v_ref, None))):
        r = chunk(c)
        if scale is not None:
            r = r * scale
        for h in range(SB_HEADS):
            ref[h] = r[:, h * HEAD_DIM:(h + 1) * HEAD_DIM].astype(BF16)
    xr_ref[...] = chunk(3)
    gr_ref[...] = chunk(4)


def _proj_ab(x, w_bf16, tm):
    b, s, d = x.shape
    n = w_bf16.shape[1]
    heads = jax.ShapeDtypeStruct((b, SB_HEADS, s, HEAD_DIM), BF16)
    flat = jax.ShapeDtypeStruct((b, s, LRU_WIDTH), F32)
    head_spec = pl.BlockSpec((None, SB_HEADS, tm, HEAD_DIM), lambda bi, i: (bi, 0, i, 0))
    flat_spec = pl.BlockSpec((None, tm, LRU_WIDTH), lambda bi, i: (bi, i, 0))
    return pl.pallas_call(
        _proj_ab_kernel,
        grid=(b, s // tm),
        in_specs=[pl.BlockSpec((None, tm, d), lambda bi, i: (bi, i, 0)),
                  pl.BlockSpec((d, n), lambda bi, i: (0, 0))],
        out_specs=[head_spec, head_spec, head_spec, flat_spec, flat_spec],
        out_shape=[heads, heads, heads, flat, flat],
        compiler_params=_params("parallel", "parallel"),
        name="proj_ab",
    )(x, w_bf16)


SB_DEAD_LOG_WEIGHT = -105.0
SB_MERGED_BLOCKS = 2


def _sb_attn_kernel(q_ref, k_ref, v_ref, o_ref, *, tq, hp):
    i = pl.program_id(2)
    row = lax.broadcasted_iota(jnp.int32, (tq, tq), 0)
    col = lax.broadcasted_iota(jnp.int32, (tq, tq), 1)
    minus_later = jnp.where(row > col, -1.0, 0.0).astype(BF16)
    causal = col < row

    def block(jb, carries, accs, masked):
        start = pl.multiple_of(jb * tq, tq)
        new_carries, new_accs = [], []
        for h in range(hp):
            kj = k_ref[h, pl.ds(start, tq), :]
            vj = v_ref[h, pl.ds(start, tq), :]
            z = lax.dot_general(q_ref[h], kj, (((1,), (1,)), ((), ())), preferred_element_type=F32)
            sp = _softplus(z)
            cost = jnp.where(causal, sp, 0.0) if masked else sp
            after = jnp.dot(cost.astype(BF16), minus_later, preferred_element_type=F32)
            w = jnp.exp((z - sp) + after + carries[h])
            if masked:
                w = jnp.where(causal, w, 0.0)
            new_accs.append(accs[h] + jnp.dot(w.astype(BF16), vj, preferred_element_type=F32))
            new_carries.append(carries[h] - jnp.sum(cost, axis=1, keepdims=True))
        return tuple(new_carries), tuple(new_accs)

    def live(carries):
        return functools.reduce(jnp.maximum, [jnp.max(c) for c in carries])

    zero = ((jnp.zeros((tq, 1), F32),) * hp, (jnp.zeros((tq, HEAD_DIM), F32),) * hp)

    def first(n):
        def run():
            state = block(i, *zero, True)
            for j in range(1, n):
                state = block(i - j, *state, False)
            return state
        return run

    done = jnp.minimum(i, SB_MERGED_BLOCKS - 1)
    carries, accs = lax.switch(done, [first(n) for n in range(1, SB_MERGED_BLOCKS + 1)])

    def cond(state):
        return (state[0] < i) & (state[1] > SB_DEAD_LOG_WEIGHT)

    def body(state):
        step, _, carries, accs = state
        carries, accs = block(i - 1 - step, carries, accs, False)
        return step + 1, live(carries), carries, accs

    _, _, _, accs = lax.while_loop(cond, body, (done, live(carries), carries, accs))
    for h in range(hp):
        o_ref[h] = accs[h].astype(o_ref.dtype)


def _sb_attention(q, k, v, tq, hp):
    b, h, s, dh = q.shape
    return pl.pallas_call(
        functools.partial(_sb_attn_kernel, tq=tq, hp=hp),
        grid=(b, h // hp, s // tq),
        in_specs=[pl.BlockSpec((None, hp, tq, dh), lambda bi, hi, i: (bi, hi, i, 0)),
                  pl.BlockSpec((None, hp, s, dh), lambda bi, hi, i: (bi, hi, 0, 0), pipeline_mode=pl.Buffered(1)),
                  pl.BlockSpec((None, hp, s, dh), lambda bi, hi, i: (bi, hi, 0, 0), pipeline_mode=pl.Buffered(1))],
        out_specs=pl.BlockSpec((None, hp, tq, dh), lambda bi, hi, i: (bi, hi, i, 0)),
        out_shape=jax.ShapeDtypeStruct((b, h, s, dh), BF16),
        compiler_params=_params("parallel", "parallel", "parallel"),
        name="sb_attention",
    )(q, k, v)


def _gelu_tanh(x):
    return 0.5 * x * (1.0 + jnp.tanh(math.sqrt(2.0 / math.pi) * (x + 0.044715 * (x * x * x))))


def _lru_kernel(xr_ref, gr_ref, cw_ref, cb_ref, wr_ref, br_ref, wi_ref, bi_ref, lam_ref, y_ref,
                xbuf, hprev, *, ts):
    @pl.when(pl.program_id(1) == 0)
    def _():
        xbuf[0:SUBLANES, :] = jnp.zeros((SUBLANES, LRU_WIDTH), F32)
        hprev[...] = jnp.zeros_like(hprev)

    xbuf[SUBLANES:SUBLANES + ts, :] = xr_ref[...]
    xc = cb_ref[...] + cw_ref[CONV_WIDTH - 1:CONV_WIDTH, :] * xbuf[SUBLANES:SUBLANES + ts, :]
    for kk in range(CONV_WIDTH - 1):
        off = SUBLANES - (CONV_WIDTH - 1) + kk
        xc = xc + cw_ref[kk:kk + 1, :] * xbuf[off:off + ts, :]
    xbuf[0:SUBLANES, :] = xbuf[ts:ts + SUBLANES, :]

    xcb = xc.astype(BF16)
    r = _sigmoid(jnp.dot(xcb, wr_ref[...], preferred_element_type=F32) + br_ref[...])
    gi = _sigmoid(jnp.dot(xcb, wi_ref[...], preferred_element_type=F32) + bi_ref[...])
    log_a = (-LRU_C) * r * _softplus(-lam_ref[...])
    a = jnp.exp(log_a)
    u = jnp.sqrt(1.0 - a * a) * (gi * xc)

    row = lax.broadcasted_iota(jnp.int32, (ts, LRU_WIDTH), 0)
    d = 1
    while d < ts:
        if d < SUBLANES:
            keep = row >= d
            a_sh = jnp.where(keep, pltpu.roll(a, d, axis=0), 1.0)
            u_sh = jnp.where(keep, pltpu.roll(u, d, axis=0), 0.0)
            u = a * u_sh + u
            a = a * a_sh
        else:
            u = jnp.concatenate([u[:d], a[d:] * u[:ts - d] + u[d:]], axis=0)
            a = jnp.concatenate([a[:d], a[d:] * a[:ts - d]], axis=0)
        d *= 2
    h = a * hprev[0:1, :] + u
    hprev[...] = jnp.broadcast_to(h[ts - 1:ts, :], hprev.shape)
    y_ref[...] = (_gelu_tanh(gr_ref[...]) * h).astype(y_ref.dtype)


def _lru(xr, gr, conv_w, conv_b, wr_bd, b_r, wi_bd, b_i, lam, ts):
    b, s, w = xr.shape
    seq_spec = pl.BlockSpec((None, ts, w), lambda bi, i: (bi, i, 0))

    def full(shape):
        return pl.BlockSpec(shape, lambda bi, i: (0,) * len(shape))

    return pl.pallas_call(
        functools.partial(_lru_kernel, ts=ts),
        grid=(b, s // ts),
        in_specs=[seq_spec, seq_spec, full((CONV_WIDTH, w)), full((1, w)), full((w, w)), full((1, w)),
                  full((w, w)), full((1, w)), full((1, w))],
        out_specs=seq_spec,
        out_shape=jax.ShapeDtypeStruct((b, s, w), BF16),
        scratch_shapes=[pltpu.VMEM((ts + 2 * SUBLANES, w), F32), pltpu.VMEM((SUBLANES, w), F32)],
        compiler_params=_params("parallel", "arbitrary"),
        name="rg_lru",
    )(xr, gr, conv_w, conv_b.reshape(1, w), wr_bd, b_r.reshape(1, w), wi_bd, b_i.reshape(1, w),
      lam.reshape(1, w))


def _block_diag(w):
    n, c, d = w.shape
    eye = jnp.eye(n, dtype=w.dtype)
    return (eye[:, None, :, None] * w[:, :, None, :]).reshape(n * c, n * d)


def _rope_project(xb, pos_ref, freq_ref, w_ref, q_ref, k_ref, vt_ref):
    tm = xb.shape[0]
    ang_t = freq_ref[...] * pos_ref[...].astype(F32)
    reps = LANES // (HEAD_DIM // 2)
    cos = jnp.concatenate([jnp.cos(ang_t)] * reps, axis=0).T
    sin = jnp.concatenate([jnp.sin(ang_t)] * reps, axis=0).T
    lane = lax.broadcasted_iota(jnp.int32, (tm, LANES), 1)
    first_half = (lane % HEAD_DIM) < (HEAD_DIM // 2)
    heads_per_slab = LANES // HEAD_DIM

    def rope(r):
        upper = pltpu.roll(r, LANES - HEAD_DIM // 2, axis=1)
        lower = pltpu.roll(r, HEAD_DIM // 2, axis=1)
        return r * cos + jnp.where(first_half, -upper, lower) * sin

    def emit(ref, n_heads, col0, rotary, scale):
        for slab in range(n_heads // heads_per_slab):
            c0 = col0 + slab * LANES
            r = jnp.dot(xb, w_ref[:, c0:c0 + LANES], preferred_element_type=F32)
            if rotary:
                r = rope(r)
            if scale is not None:
                r = r * scale
            for j in range(heads_per_slab):
                ref[slab * heads_per_slab + j] = r[:, j * HEAD_DIM:(j + 1) * HEAD_DIM].astype(BF16)

    emit(q_ref, SWA_HEADS, 0, True, Q_SCALE)
    emit(k_ref, SWA_KV_HEADS, SWA_HEADS * HEAD_DIM, True, None)
    v0 = (SWA_HEADS + SWA_KV_HEADS) * HEAD_DIM
    for slab in range(SWA_KV_HEADS // heads_per_slab):
        r = jnp.dot(xb, w_ref[:, v0 + slab * LANES:v0 + (slab + 1) * LANES], preferred_element_type=F32)
        rt = r.T
        for j in range(heads_per_slab):
            vt_ref[slab * heads_per_slab + j] = rt[j * HEAD_DIM:(j + 1) * HEAD_DIM, :].astype(BF16)


def _proj_rope_kernel(x_ref, pos_ref, freq_ref, w_ref, q_ref, k_ref, vt_ref):
    _rope_project(x_ref[...].astype(BF16), pos_ref, freq_ref, w_ref, q_ref, k_ref, vt_ref)


def _rope_io(b, s, d, n, tm):
    half = HEAD_DIM // 2
    inv_freq = (ROPE_THETA ** (-jnp.arange(half, dtype=F32) / half)).reshape(half, 1)

    def heads(nh):
        return (jax.ShapeDtypeStruct((b, nh, s, HEAD_DIM), BF16),
                pl.BlockSpec((None, nh, tm, HEAD_DIM), lambda bi, i: (bi, 0, i, 0)))

    (qs, qspec), (ks, kspec) = heads(SWA_HEADS), heads(SWA_KV_HEADS)
    vs = jax.ShapeDtypeStruct((b, SWA_KV_HEADS, HEAD_DIM, s), BF16)
    vspec = pl.BlockSpec((None, SWA_KV_HEADS, HEAD_DIM, tm), lambda bi, i: (bi, 0, 0, i))
    in_specs = [pl.BlockSpec((None, 1, tm), lambda bi, i: (bi, 0, i)),
                pl.BlockSpec((half, 1), lambda bi, i: (0, 0)),
                pl.BlockSpec((d, n), lambda bi, i: (0, 0))]
    return inv_freq, in_specs, [qspec, kspec, vspec], [qs, ks, vs]


def _proj_rope(x, positions, w_bf16, tm):
    b, s, d = x.shape
    inv_freq, in_specs, out_specs, out_shape = _rope_io(b, s, d, w_bf16.shape[1], tm)
    return pl.pallas_call(
        _proj_rope_kernel,
        grid=(b, s // tm),
        in_specs=[pl.BlockSpec((None, tm, d), lambda bi, i: (bi, i, 0))] + in_specs,
        out_specs=out_specs,
        out_shape=out_shape,
        compiler_params=_params("parallel", "parallel"),
        name="proj_rope",
    )(x, positions.reshape(b, 1, s), inv_freq, w_bf16)


def _reduce_rows(x, op):
    while x.shape[0] > SUBLANES:
        half = x.shape[0] // 2
        x = op(x[:half], x[half:])
    for shift in (4, 2, 1):
        x = op(x, pltpu.roll(x, shift, axis=0))
    return x[0:1]


SWA_BLOCKS_PER_STEP = 8


def _swa_kernel(q_ref, kp_ref, kc_ref, vtp_ref, vtc_ref, sink_ref, o_ref):
    i = pl.program_id(1)
    w = SWA_WINDOW
    key = lax.broadcasted_iota(jnp.int32, (2 * w, w), 0)
    qry = lax.broadcasted_iota(jnp.int32, (2 * w, w), 1)
    dist = qry + w - key
    band = (dist >= 0) & (dist < w)
    for blk in range(SWA_BLOCKS_PER_STEP):
        visible = band if blk else band & ((key >= w) | (i > 0))
        bias = jnp.concatenate([jnp.where(visible, 0.0, NEG_BIG)] * SWA_GROUP, axis=1)
        outs = []
        for kv in range(SWA_KV_HEADS):
            if blk:
                kk = kc_ref[kv, (blk - 1) * w:(blk + 1) * w, :]
                vvt = vtc_ref[kv, :, (blk - 1) * w:(blk + 1) * w]
            else:
                kk = jnp.concatenate([kp_ref[kv], kc_ref[kv, 0:w, :]], axis=0)
                vvt = jnp.concatenate([vtp_ref[kv], vtc_ref[kv, :, 0:w]], axis=1)
            qg = jnp.concatenate([q_ref[kv * SWA_GROUP + g, blk * w:(blk + 1) * w, :]
                                  for g in range(SWA_GROUP)], axis=0)
            st = lax.dot_general(kk, qg, (((1,), (1,)), ((), ())), preferred_element_type=F32) + bias
            sink = sink_ref[kv:kv + 1, :]
            m = jnp.maximum(_reduce_rows(st, jnp.maximum), sink)
            p = jnp.exp(st - m)
            denom = _reduce_rows(p, jnp.add) + jnp.exp(sink - m)
            ot = jnp.dot(vvt, p.astype(BF16), preferred_element_type=F32) / denom
            outs.extend(ot[:, g * w:(g + 1) * w] for g in range(SWA_GROUP))
        o_ref[blk * w:(blk + 1) * w, :] = jnp.concatenate(outs, axis=0).T.astype(o_ref.dtype)


def _swa(q, k, vt, sinks):
    b, nh, s, dh = q.shape
    nkv = k.shape[1]
    w = SWA_WINDOW
    n = SWA_BLOCKS_PER_STEP
    assert s % (n * w) == 0
    cur = pl.BlockSpec((None, nkv, n * w, dh), lambda bi, i: (bi, 0, i, 0))
    prev = pl.BlockSpec((None, nkv, w, dh), lambda bi, i: (bi, 0, jnp.maximum(n * i - 1, 0), 0))
    cur_t = pl.BlockSpec((None, nkv, dh, n * w), lambda bi, i: (bi, 0, 0, i))
    prev_t = pl.BlockSpec((None, nkv, dh, w), lambda bi, i: (bi, 0, 0, jnp.maximum(n * i - 1, 0)))
    sink_tile = jnp.repeat(sinks.astype(F32).reshape(nkv, nh // nkv), w, axis=1)
    return pl.pallas_call(
        _swa_kernel,
        grid=(b, s // (n * w)),
        in_specs=[pl.BlockSpec((None, nh, n * w, dh), lambda bi, i: (bi, 0, i, 0)),
                  prev, cur, prev_t, cur_t,
                  pl.BlockSpec(sink_tile.shape, lambda bi, i: (0, 0))],
        out_specs=pl.BlockSpec((None, n * w, nh * dh), lambda bi, i: (bi, i, 0)),
        out_shape=jax.ShapeDtypeStruct((b, s, nh * dh), BF16),
        compiler_params=_params("parallel", "parallel"),
        name="swa",
    )(q, k, k, vt, vt, sink_tile)


PAIRS_PER_GROUP = GROUP_SIZE * (GROUP_SIZE - 1) // 2
N_BUCKETS = N_GROUPS * PAIRS_PER_GROUP
ROUTE_ROWS = SUBLANES


def _route(logits_t):
    rows = [logits_t[e:e + 1, :] for e in range(N_EXPERTS)]
    mx = functools.reduce(jnp.maximum, rows)
    ex = [jnp.exp(r - mx) for r in rows]
    total = functools.reduce(lambda p, q: p + q, ex)
    probs = [e / total for e in ex]

    group_score = []
    for g in range(N_GROUPS):
        a, b, c, d = probs[g * GROUP_SIZE:(g + 1) * GROUP_SIZE]
        hi1, lo1 = jnp.maximum(a, b), jnp.minimum(a, b)
        hi2, lo2 = jnp.maximum(c, d), jnp.minimum(c, d)
        top1 = jnp.maximum(hi1, hi2)
        top2 = jnp.maximum(jnp.minimum(hi1, hi2), jnp.maximum(lo1, lo2))
        group_score.append(top1 + top2)
    best = functools.reduce(jnp.maximum, group_score)
    g_sel = jnp.full(best.shape, N_GROUPS - 1, jnp.int32)
    for g in range(N_GROUPS - 2, -1, -1):
        g_sel = jnp.where(group_score[g] == best, g, g_sel)

    in_group = []
    for j in range(GROUP_SIZE):
        val = probs[(N_GROUPS - 1) * GROUP_SIZE + j]
        for g in range(N_GROUPS - 2, -1, -1):
            val = jnp.where(g_sel == g, probs[g * GROUP_SIZE + j], val)
        in_group.append(val)

    def first_argmax(vals):
        m = functools.reduce(jnp.maximum, vals)
        idx = jnp.full(m.shape, GROUP_SIZE - 1, jnp.int32)
        for j in range(GROUP_SIZE - 2, -1, -1):
            idx = jnp.where(vals[j] == m, j, idx)
        return m, idx

    w1, i1 = first_argmax(in_group)
    rest = [jnp.where(i1 == j, -1.0, in_group[j]) for j in range(GROUP_SIZE)]
    w2, i2 = first_argmax(rest)
    norm = w1 + w2
    first_is_lo = i1 < i2
    i_lo = jnp.minimum(i1, i2)
    i_hi = jnp.maximum(i1, i2)
    pair = jnp.where(i_lo == 0, i_hi - 1, jnp.where(i_lo == 1, i_hi + 1, PAIRS_PER_GROUP - 1))
    bucket = (g_sel * PAIRS_PER_GROUP + pair).astype(F32)
    w_lo = jnp.where(first_is_lo, w1, w2) / norm
    w_hi = jnp.where(first_is_lo, w2, w1) / norm
    return bucket, w_lo, w_hi


def _mix_out_kernel(*refs, n_head_major, alpha):
    x_ref = refs[0]
    y_refs = refs[1:1 + len(n_head_major)]
    w_refs = refs[1 + len(n_head_major):1 + 2 * len(n_head_major)]
    g_ref, b_ref, wrh_ref, wrl_ref, brt_ref, x1e_ref, route_ref = refs[1 + 2 * len(n_head_major):]
    tm, d = x_ref.shape
    h = alpha * x_ref[...]
    for y_ref, w_ref, nh in zip(y_refs, w_refs, n_head_major):
        if nh:
            y = jnp.concatenate([y_ref[j] for j in range(nh)], axis=-1)
        else:
            y = y_ref[...]
        h = h + jnp.dot(y, w_ref[...], preferred_element_type=F32)
    x1 = _layer_norm(h, g_ref[...], b_ref[...])
    x_hi = x1.astype(BF16)
    x_lo = (x1 - x_hi.astype(F32)).astype(BF16)

    def nt_dot(w, xv):
        return lax.dot_general(w, xv, (((1,), (1,)), ((), ())), preferred_element_type=F32)

    logits_t = (nt_dot(wrh_ref[...], x_hi) + nt_dot(wrh_ref[...], x_lo) + nt_dot(wrl_ref[...], x_hi)
                + brt_ref[...])
    route = jnp.concatenate(list(_route(logits_t)) + [jnp.zeros((ROUTE_ROWS - 3, tm), F32)], axis=0)
    route_ref[...] = route
    x1e_ref[:, 0:d] = x1
    x1e_ref[:, d:d + LANES] = jnp.concatenate([route, jnp.zeros((LANES - ROUTE_ROWS, tm), F32)], axis=0).T


def _mix_out(x, ys, ws, ln_g, ln_b, w_router, b_router, alpha, tm):
    b, s, d = x.shape
    n_head_major = tuple(y.shape[1] if hm else 0 for y, hm in ys)
    y_specs = []
    for (y, hm) in ys:
        if hm:
            y_specs.append(pl.BlockSpec((None, y.shape[1], tm, y.shape[3]), lambda bi, i: (bi, 0, i, 0)))
        else:
            y_specs.append(pl.BlockSpec((None, tm, y.shape[2]), lambda bi, i: (bi, i, 0)))
    w_specs = [pl.BlockSpec(w.shape, lambda bi, i: (0, 0)) for w in ws]
    row = pl.BlockSpec((1, d), lambda bi, i: (0, 0))
    tok = pl.BlockSpec((None, tm, d), lambda bi, i: (bi, i, 0))
    wr_t = w_router.T.astype(F32)
    wr_hi = wr_t.astype(BF16)
    wr_lo = (wr_t - wr_hi.astype(F32)).astype(BF16)
    wr_spec = pl.BlockSpec((N_EXPERTS, d), lambda bi, i: (0, 0))
    return pl.pallas_call(
        functools.partial(_mix_out_kernel, n_head_major=n_head_major, alpha=alpha),
        grid=(b, s // tm),
        in_specs=[tok] + y_specs + w_specs + [row, row, wr_spec, wr_spec,
                  pl.BlockSpec((N_EXPERTS, 1), lambda bi, i: (0, 0))],
        out_specs=[pl.BlockSpec((None, tm, d + LANES), lambda bi, i: (bi, i, 0)),
                   pl.BlockSpec((None, ROUTE_ROWS, tm), lambda bi, i: (bi, 0, i))],
        out_shape=[jax.ShapeDtypeStruct((b, s, d + LANES), F32),
                   jax.ShapeDtypeStruct((b, ROUTE_ROWS, s), F32)],
        compiler_params=_params("parallel", "parallel"),
        name="mix_out_ln_router",
    )(x, *[y for y, _ in ys], *ws, ln_g.reshape(1, d), ln_b.reshape(1, d),
      wr_hi, wr_lo, b_router.astype(F32).reshape(N_EXPERTS, 1))


MOE_TILE = 256
PLAN_COLS = 256
META_ROWS = SUBLANES


def _num_moe_tiles(t):
    return t // MOE_TILE + N_BUCKETS


def _plan_kernel(bid_ref, pos_ref, meta_ref):
    r, c = bid_ref.shape
    bid = bid_ref[...]
    before = (lax.broadcasted_iota(jnp.int32, (c, c), 0)
              < lax.broadcasted_iota(jnp.int32, (c, c), 1)).astype(BF16)
    rows_before = (lax.broadcasted_iota(jnp.int32, (r, r), 1)
                   < lax.broadcasted_iota(jnp.int32, (r, r), 0)).astype(BF16)
    lane = lax.broadcasted_iota(jnp.int32, (1, LANES), 1)
    tile_start = lane.astype(F32) * MOE_TILE

    def body(b, state):
        base, pos, tile_bucket, last_tile = state
        ind = (bid == lax.convert_element_type(b, F32)).astype(F32)
        within = jnp.dot(ind.astype(BF16), before, preferred_element_type=F32)
        row_total = jnp.sum(ind, axis=1, keepdims=True)
        row_off = jnp.dot(rows_before, jnp.broadcast_to(row_total, (r, LANES)).astype(BF16),
                          preferred_element_type=F32)[:, 0:1]
        count = jnp.sum(row_total, axis=0, keepdims=True)
        padded = jnp.floor((count + (MOE_TILE - 1)) * (1.0 / MOE_TILE)) * MOE_TILE
        pos = pos + ind * (base + row_off + within)
        end = base + padded
        tile_bucket = tile_bucket + (tile_start >= end).astype(F32)
        last_row = jnp.where(count > 0.0, end - MOE_TILE, -1.0)
        last_tile = jnp.where(lane == b, last_row, last_tile)
        return end, pos, tile_bucket, last_tile

    total, pos, tile_bucket, last_tile = lax.fori_loop(
        0, N_BUCKETS, body, (jnp.zeros((1, 1), F32), jnp.zeros((r, c), F32), jnp.zeros((1, LANES), F32),
                             jnp.full((1, LANES), -1.0, F32)))
    pos_ref[...] = pos.astype(jnp.int32)

    tb = jnp.minimum(tile_bucket, N_BUCKETS - 1.0)
    group = sum((tb >= g * PAIRS_PER_GROUP).astype(F32) for g in range(1, N_GROUPS))
    pair = tb - group * PAIRS_PER_GROUP
    i_lo = jnp.where(pair < 3, 0.0, jnp.where(pair < 5, 1.0, 2.0))
    i_hi = jnp.where(pair < 3, pair + 1.0, jnp.where(pair < 5, pair - 1.0, 3.0))
    meta = jnp.concatenate([group * GROUP_SIZE + i_lo, group * GROUP_SIZE + i_hi,
                            (tile_start < total).astype(F32),
                            jnp.where(lane == N_BUCKETS, total, last_tile),
                            jnp.zeros((META_ROWS - 4, LANES), F32)], axis=0)
    meta_ref[...] = meta.astype(jnp.int32)


def _plan(bucket_ids):
    r, c = bucket_ids.shape
    return pl.pallas_call(
        _plan_kernel,
        out_shape=[jax.ShapeDtypeStruct((r, c), jnp.int32), jax.ShapeDtypeStruct((META_ROWS, LANES), jnp.int32)],
        compiler_params=pltpu.CompilerParams(vmem_limit_bytes=VMEM_LIMIT_BYTES),
        name="moe_plan",
    )(bucket_ids)


DMA_PRIORITIES = 2


def _row_copy(src_ref, src_row, dst_ref, dst_row, sem):
    return pltpu.make_async_copy(src_ref.at[pl.ds(src_row, 1)], dst_ref.at[pl.ds(dst_row, 1)], sem)


def _dispatch_kernel(pos_ref, last_ref, x_ref, o_ref, inv_ref, zeros, inv, fill, sem, zsem, isem, *,
                     first_spare, n_tokens):
    tm = x_ref.shape[0]
    i = pl.program_id(0)

    @pl.when(i == 0)
    def _():
        zeros[...] = jnp.zeros_like(zeros)
        fill[...] = jnp.full(fill.shape, n_tokens, jnp.int32)
        to_smem = pltpu.make_async_copy(fill, inv, isem)
        to_smem.start()
        rows_in_use = last_ref[N_BUCKETS]
        clears = [(last_ref[b] >= 0, last_ref[b]) for b in range(N_BUCKETS)]
        clears += [(k * MOE_TILE >= rows_in_use, k * MOE_TILE)
                   for k in range(first_spare, o_ref.shape[0] // MOE_TILE)]

        def clear(row):
            start = row if isinstance(row, int) else pl.multiple_of(row, MOE_TILE)
            return pltpu.make_async_copy(zeros, o_ref.at[pl.ds(start, MOE_TILE)], zsem)

        for needed, row in clears:
            @pl.when(needed)
            def _():
                clear(row).start()
        for needed, row in clears:
            @pl.when(needed)
            def _():
                clear(row).wait()
        to_smem.wait()

    base = i * tm
    for r in range(tm):
        dst = pos_ref[base + r]
        inv[dst] = base + r
        _row_copy(x_ref, r, o_ref, dst, sem).start(priority=r % DMA_PRIORITIES)
    pltpu.make_async_copy(x_ref, o_ref.at[pl.ds(0, tm)], sem).wait()

    @pl.when(i == pl.num_programs(0) - 1)
    def _():
        to_hbm = pltpu.make_async_copy(inv, inv_ref, isem)
        to_hbm.start()
        to_hbm.wait()


def _dispatch(pos, last_tile_rows, xt, n_rows, tm):
    t = xt.shape[0]
    tile = xt.shape[1:]
    dma_sem = pltpu.SemaphoreType.DMA(())
    return pl.pallas_call(
        functools.partial(_dispatch_kernel, first_spare=t // MOE_TILE, n_tokens=t),
        grid_spec=pltpu.PrefetchScalarGridSpec(
            num_scalar_prefetch=2,
            grid=(t // tm,),
            in_specs=[pl.BlockSpec((tm,) + tile, lambda i, pos_ref, last_ref: (i,) + (0,) * len(tile))],
            out_specs=[pl.BlockSpec(memory_space=pl.ANY), pl.BlockSpec(memory_space=pl.ANY)],
            scratch_shapes=[pltpu.VMEM((MOE_TILE,) + tile, xt.dtype), pltpu.SMEM((n_rows,), jnp.int32),
                            pltpu.VMEM((n_rows,), jnp.int32), dma_sem, dma_sem, dma_sem]),
        out_shape=[jax.ShapeDtypeStruct((n_rows,) + tile, xt.dtype),
                   jax.ShapeDtypeStruct((n_rows,), jnp.int32)],
        compiler_params=_params("arbitrary"),
        name="moe_dispatch",
    )(pos, last_tile_rows, xt)


def _moe_tile_kernel(elo_ref, ehi_ref, valid_ref, inv, x_ref, wgl_ref, wul_ref, wdl_ref, wgh_ref, wuh_ref,
                     wdh_ref, m_hbm, o_even, o_odd, ssem_even, ssem_odd, wgl_b, wul_b, wdl_b, wgh_b, wuh_b,
                     wdh_b, *, n_tokens):
    d = wgl_ref.shape[0]
    k = pl.program_id(0)
    last = pl.num_programs(0) - 1
    prev = jnp.maximum(k - 1, 0)
    valid = valid_ref[k] != 0

    def scatter(tile, r, buf, sem, to_spare):
        tok = inv[tile * MOE_TILE + r]
        dst = jnp.where(to_spare | (tok >= n_tokens), n_tokens + r, tok)
        return _row_copy(buf, r, m_hbm, dst, sem)

    def scatter_done(buf, sem):
        pltpu.make_async_copy(buf, m_hbm.at[pl.ds(0, MOE_TILE)], sem).wait()

    @pl.when(k == 0)
    def _():
        o_odd[...] = jnp.zeros_like(o_odd)

    def refresh(e_ref, srcs, dsts):
        @pl.when(valid & ((k == 0) | (e_ref[k] != e_ref[prev])))
        def _():
            for src, dst in zip(srcs, dsts):
                dst[...] = src[...].astype(BF16)

    refresh(elo_ref, (wgl_ref, wul_ref, wdl_ref), (wgl_b, wul_b, wdl_b))
    refresh(ehi_ref, (wgh_ref, wuh_ref, wdh_ref), (wgh_b, wuh_b, wdh_b))

    def tile_valid(j):
        return (j >= 0) & (valid_ref[jnp.maximum(j, 0)] != 0)

    def scattered_in(j):
        return (j == 0) | tile_valid(j) | tile_valid(j - 1)

    def step(o_cur, ssem_cur, o_prv, ssem_prv):
        @pl.when((k > 0) & scattered_in(k - 1))
        def _():
            scatter_done(o_cur, ssem_cur)

        first = k == 0

        @pl.when(valid)
        def _():
            for r in range(MOE_TILE):
                scatter(prev, r, o_prv, ssem_prv, first).start(priority=r % DMA_PRIORITIES)
            x = x_ref[:, 0:d].astype(BF16)
            acc = None
            for wg_b, wu_b, wd_b, lane in ((wgl_b, wul_b, wdl_b, d + 1), (wgh_b, wuh_b, wdh_b, d + 2)):
                weight = x_ref[:, lane:lane + 1]
                hg = jnp.dot(x, wg_b[...], preferred_element_type=F32)
                hu = jnp.dot(x, wu_b[...], preferred_element_type=F32)
                hidden = (hg * _sigmoid(hg)) * hu * weight
                y = jnp.dot(hidden.astype(BF16), wd_b[...], preferred_element_type=F32)
                acc = y if acc is None else acc + y
            o_cur[...] = acc

        @pl.when(jnp.logical_not(valid) & tile_valid(k - 1))
        def _():
            def issue(r, carry):
                scatter(prev, r, o_prv, ssem_prv, first).start()
                return carry

            lax.fori_loop(0, MOE_TILE, issue, 0, unroll=8)

        @pl.when((k == last) & scattered_in(k))
        def _():
            scatter_done(o_prv, ssem_prv)

        @pl.when((k == last) & valid)
        def _():
            def issue(r, carry):
                scatter(k, r, o_cur, ssem_cur, False).start()
                return carry

            lax.fori_loop(0, MOE_TILE, issue, 0, unroll=8)
            scatter_done(o_cur, ssem_cur)

    @pl.when(k % 2 == 0)
    def _():
        step(o_even, ssem_even, o_odd, ssem_odd)

    @pl.when(k % 2 == 1)
    def _():
        step(o_odd, ssem_odd, o_even, ssem_even)


def _moe_tiles(e_lo, e_hi, valid, inv, xs, wg, wu, wd, layer, n_tokens):
    n_rows, dw = xs.shape
    _, _, d, f = wg.shape
    up_lo = pl.BlockSpec((None, None, d, f), lambda k, lo, hi, ok, iv: (layer, lo[k], 0, 0))
    up_hi = pl.BlockSpec((None, None, d, f), lambda k, lo, hi, ok, iv: (layer, hi[k], 0, 0))
    down_lo = pl.BlockSpec((None, None, f, d), lambda k, lo, hi, ok, iv: (layer, lo[k], 0, 0))
    down_hi = pl.BlockSpec((None, None, f, d), lambda k, lo, hi, ok, iv: (layer, hi[k], 0, 0))
    up_b = pltpu.VMEM((d, f), BF16)
    down_b = pltpu.VMEM((f, d), BF16)
    o_buf = pltpu.VMEM((MOE_TILE, d), F32)
    dma_sem = pltpu.SemaphoreType.DMA(())
    return pl.pallas_call(
        functools.partial(_moe_tile_kernel, n_tokens=n_tokens),
        grid_spec=pltpu.PrefetchScalarGridSpec(
            num_scalar_prefetch=4,
            grid=(n_rows // MOE_TILE,),
            in_specs=[pl.BlockSpec((MOE_TILE, dw), lambda k, lo, hi, ok, iv: (k, 0)),
                      up_lo, up_lo, down_lo, up_hi, up_hi, down_hi],
            out_specs=pl.BlockSpec(memory_space=pl.ANY),
            scratch_shapes=[o_buf, o_buf, dma_sem, dma_sem, up_b, up_b, down_b, up_b, up_b, down_b]),
        out_shape=jax.ShapeDtypeStruct((n_tokens + MOE_TILE, d), F32),
        compiler_params=_params("arbitrary"),
        name="moe_tiles",
    )(e_lo, e_hi, valid, inv, xs, wg, wu, wd, wg, wu, wd)


def _ln_ple_kernel(x1_ref, m_ref, p_ref, g_ref, b_ref, wg_ref, bg_ref, wp_ref, o_ref, *, alpha):
    x2 = _layer_norm(alpha * x1_ref[...] + m_ref[...], g_ref[...], b_ref[...])
    gate = _sigmoid(jnp.dot(x2.astype(BF16), wg_ref[...], preferred_element_type=F32) + bg_ref[...])
    emb = jnp.dot(p_ref[...].astype(BF16), wp_ref[...], preferred_element_type=F32)
    o_ref[...] = x2 + gate * emb


def _ln_ple(x1e, m, p, layer, ln_g, ln_b, wg, bg, wp, alpha, tm):
    t = x1e.shape[0]
    d = m.shape[1]
    pd = p.shape[2]
    tok = pl.BlockSpec((tm, d), lambda i: (i, 0))
    row = pl.BlockSpec((1, d), lambda i: (0, 0))
    return pl.pallas_call(
        functools.partial(_ln_ple_kernel, alpha=alpha),
        grid=(t // tm,),
        in_specs=[tok, tok, pl.BlockSpec((None, tm, pd), lambda i: (layer, i, 0)), row, row,
                  pl.BlockSpec((d, d), lambda i: (0, 0)), row, pl.BlockSpec((pd, d), lambda i: (0, 0))],
        out_specs=tok,
        out_shape=jax.ShapeDtypeStruct((t, d), F32),
        compiler_params=_params("parallel"),
        name="ln_ple",
    )(x1e, m, p, ln_g.reshape(1, d), ln_b.reshape(1, d), wg, bg.reshape(1, d), wp)


def _ln_ple_rope_kernel(x1_ref, m_ref, p_ref, g_ref, b_ref, wg_ref, bg_ref, wp_ref, pos_ref, freq_ref, w_ref,
                        o_ref, q_ref, k_ref, vt_ref, *, alpha):
    _ln_ple_kernel(x1_ref, m_ref, p_ref, g_ref, b_ref, wg_ref, bg_ref, wp_ref, o_ref, alpha=alpha)
    _rope_project(o_ref[...].astype(BF16), pos_ref, freq_ref, w_ref, q_ref, k_ref, vt_ref)


def _ln_ple_rope(x1e, m, p, layer, ln_g, ln_b, wg, bg, wp, alpha, positions, w_qkv_bf16, b, tm):
    t = x1e.shape[0]
    s = t // b
    d = m.shape[1]
    pd = p.shape[2]
    steps = s // tm
    inv_freq, rope_in, rope_out, rope_shape = _rope_io(b, s, d, w_qkv_bf16.shape[1], tm)
    tok = pl.BlockSpec((tm, d), lambda bi, i: (bi * steps + i, 0))
    row = pl.BlockSpec((1, d), lambda bi, i: (0, 0))
    return pl.pallas_call(
        functools.partial(_ln_ple_rope_kernel, alpha=alpha),
        grid=(b, steps),
        in_specs=[tok, tok, pl.BlockSpec((None, tm, pd), lambda bi, i: (layer, bi * steps + i, 0)), row, row,
                  pl.BlockSpec((d, d), lambda bi, i: (0, 0)), row,
                  pl.BlockSpec((pd, d), lambda bi, i: (0, 0))] + rope_in,
        out_specs=[pl.BlockSpec((None, tm, d), lambda bi, i: (bi, i, 0))] + rope_out,
        out_shape=[jax.ShapeDtypeStruct((b, s, d), F32)] + rope_shape,
        compiler_params=_params("parallel", "parallel"),
        name="ln_ple_proj_rope",
    )(x1e, m, p, ln_g.reshape(1, d), ln_b.reshape(1, d), wg, bg.reshape(1, d), wp,
      positions.reshape(b, 1, s), inv_freq, w_qkv_bf16)


def _pick_tile(n, target):
    t = min(n, target)
    while n % t:
        t //= 2
    return t


def kernel(x, p, positions, w_in_ab, w_out_ab, conv_w, conv_b, lru_w_r, lru_b_r, lru_w_i, lru_b_i, lru_lambda, w_qkv_c, w_out_c, sinks_c, ln_mix_g, ln_mix_b, ln_ffn_g, ln_ffn_b, w_router, b_router, exp_w_gate, exp_w_up, exp_w_down, ple_w_proj, ple_w_gate, ple_b_gate):
    b, s, d = x.shape
    depth = p.shape[0]
    t = b * s
    alpha = (2 * depth) ** 0.25
    tm = _pick_tile(s, 1024)
    assert t % MOE_TILE == 0 and t % PLAN_COLS == 0
    n_tiles = _num_moe_tiles(t)
    assert n_tiles <= LANES
    rope_qkv = None
    for i in range(depth):
        j = i // 2
        if i % 2 == 0:
            q, k, v, xr, gr = _proj_ab(x, w_in_ab[j].astype(BF16), tm)
            y_sb = _sb_attention(q, k, v, _pick_tile(s, 256), SB_HEADS)
            y_lru = _lru(xr, gr, conv_w[j], conv_b[j], _block_diag(lru_w_r[j]).astype(BF16), lru_b_r[j],
                         _block_diag(lru_w_i[j]).astype(BF16), lru_b_i[j], lru_lambda[j],
                         _pick_tile(s, 256))
            w_out = w_out_ab[j].astype(BF16)
            ys = [(y_sb, True), (y_lru, False)]
            ws = [w_out[:SB_WIDTH], w_out[SB_WIDTH:]]
        else:
            if rope_qkv is None:
                rope_qkv = _proj_rope(x, positions, w_qkv_c[j].astype(BF16), tm)
            y = _swa(*rope_qkv, sinks_c[j])
            rope_qkv = None
            ys = [(y, False)]
            ws = [w_out_c[j].astype(BF16)]
        x1e, route = _mix_out(x, ys, ws, ln_mix_g[i], ln_mix_b[i], w_router, b_router, alpha, tm)
        x1e = x1e.reshape(t, d + LANES)
        pos, meta = _plan(route[:, 0, :].reshape(t // PLAN_COLS, PLAN_COLS))
        pos = pos.reshape(t)
        xs, inv = _dispatch(pos, meta[3, :N_BUCKETS + 1], x1e, n_tiles * MOE_TILE, _pick_tile(t, 512))
        m = _moe_tiles(meta[0, :n_tiles], meta[1, :n_tiles], meta[2, :n_tiles], inv, xs,
                       exp_w_gate, exp_w_up, exp_w_down, i, t)
        ple = (x1e, m, p.reshape(depth, t, -1), i, ln_ffn_g[i], ln_ffn_b[i], ple_w_gate[i].astype(BF16),
               ple_b_gate[i], ple_w_proj[i].astype(BF16), alpha)
        if i + 1 < depth and (i + 1) % 2 == 1:
            x, *rope_qkv = _ln_ple_rope(*ple, positions, w_qkv_c[(i + 1) // 2].astype(BF16), b, tm)
        else:
            x = _ln_ple(*ple, _pick_tile(t, 1024)).reshape(b, s, d)
    return x
```

```python
import functools
import math

import jax
import jax.numpy as jnp
from jax import lax
from jax.experimental import pallas as pl
from jax.experimental.pallas import tpu as pltpu

HEAD_DIM = 64
SB_HEADS = 8
SB_WIDTH = SB_HEADS * HEAD_DIM
LRU_WIDTH = 512
LRU_BLOCKS = 8
LRU_C = 8.0
CONV_WIDTH = 4
SWA_HEADS = 16
SWA_KV_HEADS = 4
SWA_GROUP = SWA_HEADS // SWA_KV_HEADS
SWA_WINDOW = 128
ROPE_THETA = 10000.0
N_EXPERTS = 16
N_GROUPS = 4
GROUP_SIZE = N_EXPERTS // N_GROUPS
LN_EPS = 1e-5
Q_SCALE = HEAD_DIM ** -0.5

LANES = 128
SUBLANES = 8
VMEM_LIMIT_BYTES = 48 * 1024 * 1024

NEG_BIG = -1e30

BF16 = jnp.bfloat16
F32 = jnp.float32


def _params(*semantics):
    return pltpu.CompilerParams(dimension_semantics=semantics, vmem_limit_bytes=VMEM_LIMIT_BYTES)


def _softplus(z):
    return jnp.maximum(z, 0.0) + jnp.log(1.0 + jnp.exp(-jnp.abs(z)))


def _sigmoid(z):
    return 1.0 / (1.0 + jnp.exp(-z))


def _layer_norm(y, g, b):
    mu = jnp.mean(y, axis=-1, keepdims=True)
    d = y - mu
    var = jnp.mean(d * d, axis=-1, keepdims=True)
    return d * lax.rsqrt(var + LN_EPS) * g + b


def _proj_ab_kernel(x_ref, w_ref, q_ref, k_ref, v_ref, xr_ref, gr_ref):
    xb = x_ref[...].astype(BF16)

    def chunk(c):
        return jnp.dot(xb, w_ref[:, c * SB_WIDTH:(c + 1) * SB_WIDTH], preferred_element_type=F32)

    for c, (ref, scale) in enumerate(((q_ref, Q_SCALE), (k_ref, None), (v_ref, None))):
        r = chunk(c)
        if scale is not None:
            r = r * scale
        for h in range(SB_HEADS):
            ref[h] = r[:, h * HEAD_DIM:(h + 1) * HEAD_DIM].astype(BF16)
    xr_ref[...] = chunk(3)
    gr_ref[...] = chunk(4)


def _proj_ab(x, w_bf16, tm):
    b, s, d = x.shape
    n = w_bf16.shape[1]
    heads = jax.ShapeDtypeStruct((b, SB_HEADS, s, HEAD_DIM), BF16)
    flat = jax.ShapeDtypeStruct((b, s, LRU_WIDTH), F32)
    head_spec = pl.BlockSpec((None, SB_HEADS, tm, HEAD_DIM), lambda bi, i: (bi, 0, i, 0))
    flat_spec = pl.BlockSpec((None, tm, LRU_WIDTH), lambda bi, i: (bi, i, 0))
    return pl.pallas_call(
        _proj_ab_kernel,
        grid=(b, s // tm),
        in_specs=[pl.BlockSpec((None, tm, d), lambda bi, i: (bi, i, 0)),
                  pl.BlockSpec((d, n), lambda bi, i: (0, 0))],
        out_specs=[head_spec, head_spec, head_spec, flat_spec, flat_spec],
        out_shape=[heads, heads, heads, flat, flat],
        compiler_params=_params("parallel", "parallel"),
        name="proj_ab",
    )(x, w_bf16)


SB_DEAD_LOG_WEIGHT = -105.0
SB_MERGED_BLOCKS = 2


def _sb_attn_kernel(q_ref, k_ref, v_ref, o_ref, *, tq, hp):
    i = pl.program_id(2)
    row = lax.broadcasted_iota(jnp.int32, (tq, tq), 0)
    col = lax.broadcasted_iota(jnp.int32, (tq, tq), 1)
    minus_later = jnp.where(row > col, -1.0, 0.0).astype(BF16)
    causal = col < row

    def block(jb, carries, accs, masked):
        start = pl.multiple_of(jb * tq, tq)
        new_carries, new_accs = [], []
        for h in range(hp):
            kj = k_ref[h, pl.ds(start, tq), :]
            vj = v_ref[h, pl.ds(start, tq), :]
            z = lax.dot_general(q_ref[h], kj, (((1,), (1,)), ((), ())), preferred_element_type=F32)
            sp = _softplus(z)
            cost = jnp.where(causal, sp, 0.0) if masked else sp
            after = jnp.dot(cost.astype(BF16), minus_later, preferred_element_type=F32)
            w = jnp.exp((z - sp) + after + carries[h])
            if masked:
                w = jnp.where(causal, w, 0.0)
            new_accs.append(accs[h] + jnp.dot(w.astype(BF16), vj, preferred_element_type=F32))
            new_carries.append(carries[h] - jnp.sum(cost, axis=1, keepdims=True))
        return tuple(new_carries), tuple(new_accs)

    def live(carries):
        return jnp.max(functools.reduce(jnp.maximum, carries))

    zero = ((jnp.zeros((tq, 1), F32),) * hp, (jnp.zeros((tq, HEAD_DIM), F32),) * hp)

    def first(n):
        def run():
            state = block(i, *zero, True)
            for j in range(1, n):
                state = block(i - j, *state, False)
            return state
        return run

    done = jnp.minimum(i, SB_MERGED_BLOCKS - 1)
    carries, accs = lax.switch(done, [first(n) for n in range(1, SB_MERGED_BLOCKS + 1)])

    def cond(state):
        return (state[0] < i) & (state[1] > SB_DEAD_LOG_WEIGHT)

    def body(state):
        step, _, carries, accs = state
        carries, accs = block(i - 1 - step, carries, accs, False)
        return step + 1, live(carries), carries, accs

    _, _, _, accs = lax.while_loop(cond, body, (done, live(carries), carries, accs))
    for h in range(hp):
        o_ref[h] = accs[h].astype(o_ref.dtype)


def _sb_attention(q, k, v, tq, hp):
    b, h, s, dh = q.shape
    return pl.pallas_call(
        functools.partial(_sb_attn_kernel, tq=tq, hp=hp),
        grid=(b, h // hp, s // tq),
        in_specs=[pl.BlockSpec((None, hp, tq, dh), lambda bi, hi, i: (bi, hi, i, 0)),
                  pl.BlockSpec((None, hp, s, dh), lambda bi, hi, i: (bi, hi, 0, 0), pipeline_mode=pl.Buffered(1)),
                  pl.BlockSpec((None, hp, s, dh), lambda bi, hi, i: (bi, hi, 0, 0), pipeline_mode=pl.Buffered(1))],
        out_specs=pl.BlockSpec((None, hp, tq, dh), lambda bi, hi, i: (bi, hi, i, 0)),
        out_shape=jax.ShapeDtypeStruct((b, h, s, dh), BF16),
        compiler_params=_params("parallel", "parallel", "parallel"),
        name="sb_attention",
    )(q, k, v)


def _gelu_tanh(x):
    return 0.5 * x * (1.0 + jnp.tanh(math.sqrt(2.0 / math.pi) * (x + 0.044715 * (x * x * x))))


def _lru_kernel(xr_ref, gr_ref, cw_ref, cb_ref, wr_ref, br_ref, wi_ref, bi_ref, lam_ref, y_ref,
                xbuf, hprev, *, ts):
    @pl.when(pl.program_id(1) == 0)
    def _():
        xbuf[0:SUBLANES, :] = jnp.zeros((SUBLANES, LRU_WIDTH), F32)
        hprev[...] = jnp.zeros_like(hprev)

    xbuf[SUBLANES:SUBLANES + ts, :] = xr_ref[...]
    xc = cb_ref[...] + cw_ref[CONV_WIDTH - 1:CONV_WIDTH, :] * xbuf[SUBLANES:SUBLANES + ts, :]
    for kk in range(CONV_WIDTH - 1):
        off = SUBLANES - (CONV_WIDTH - 1) + kk
        xc = xc + cw_ref[kk:kk + 1, :] * xbuf[off:off + ts, :]
    xbuf[0:SUBLANES, :] = xbuf[ts:ts + SUBLANES, :]

    xcb = xc.astype(BF16)
    r = _sigmoid(jnp.dot(xcb, wr_ref[...], preferred_element_type=F32) + br_ref[...])
    gi = _sigmoid(jnp.dot(xcb, wi_ref[...], preferred_element_type=F32) + bi_ref[...])
    log_a = (-LRU_C) * r * _softplus(-lam_ref[...])
    a = jnp.exp(log_a)
    u = jnp.sqrt(1.0 - a * a) * (gi * xc)

    row = lax.broadcasted_iota(jnp.int32, (ts, LRU_WIDTH), 0)
    d = 1
    while d < ts:
        if d < SUBLANES:
            keep = row >= d
            a_sh = jnp.where(keep, pltpu.roll(a, d, axis=0), 1.0)
            u_sh = jnp.where(keep, pltpu.roll(u, d, axis=0), 0.0)
            u = a * u_sh + u
            a = a * a_sh
        else:
            u = jnp.concatenate([u[:d], a[d:] * u[:ts - d] + u[d:]], axis=0)
            a = jnp.concatenate([a[:d], a[d:] * a[:ts - d]], axis=0)
        d *= 2
    h = a * hprev[0:1, :] + u
    hprev[...] = jnp.broadcast_to(h[ts - 1:ts, :], hprev.shape)
    y_ref[...] = (_gelu_tanh(gr_ref[...]) * h).astype(y_ref.dtype)


def _lru(xr, gr, conv_w, conv_b, wr_bd, b_r, wi_bd, b_i, lam, ts):
    b, s, w = xr.shape
    seq_spec = pl.BlockSpec((None, ts, w), lambda bi, i: (bi, i, 0))

    def full(shape):
        return pl.BlockSpec(shape, lambda bi, i: (0,) * len(shape))

    return pl.pallas_call(
        functools.partial(_lru_kernel, ts=ts),
        grid=(b, s // ts),
        in_specs=[seq_spec, seq_spec, full((CONV_WIDTH, w)), full((1, w)), full((w, w)), full((1, w)),
                  full((w, w)), full((1, w)), full((1, w))],
        out_specs=seq_spec,
        out_shape=jax.ShapeDtypeStruct((b, s, w), BF16),
        scratch_shapes=[pltpu.VMEM((ts + 2 * SUBLANES, w), F32), pltpu.VMEM((SUBLANES, w), F32)],
        compiler_params=_params("parallel", "arbitrary"),
        name="rg_lru",
    )(xr, gr, conv_w, conv_b.reshape(1, w), wr_bd, b_r.reshape(1, w), wi_bd, b_i.reshape(1, w),
      lam.reshape(1, w))


def _block_diag(w):
    n, c, d = w.shape
    eye = jnp.eye(n, dtype=w.dtype)
    return (eye[:, None, :, None] * w[:, :, None, :]).reshape(n * c, n * d)


def _proj_rope_kernel(x_ref, pos_ref, freq_ref, w_ref, q_ref, k_ref, vt_ref):
    tm = x_ref.shape[0]
    xb = x_ref[...].astype(BF16)
    ang_t = freq_ref[...] * pos_ref[...].astype(F32)
    reps = LANES // (HEAD_DIM // 2)
    cos = jnp.concatenate([jnp.cos(ang_t)] * reps, axis=0).T
    sin = jnp.concatenate([jnp.sin(ang_t)] * reps, axis=0).T
    lane = lax.broadcasted_iota(jnp.int32, (tm, LANES), 1)
    first_half = (lane % HEAD_DIM) < (HEAD_DIM // 2)
    heads_per_slab = LANES // HEAD_DIM

    def rope(r):
        upper = pltpu.roll(r, LANES - HEAD_DIM // 2, axis=1)
        lower = pltpu.roll(r, HEAD_DIM // 2, axis=1)
        return r * cos + jnp.where(first_half, -upper, lower) * sin

    def emit(ref, n_heads, col0, rotary, scale):
        for slab in range(n_heads // heads_per_slab):
            c0 = col0 + slab * LANES
            r = jnp.dot(xb, w_ref[:, c0:c0 + LANES], preferred_element_type=F32)
            if rotary:
                r = rope(r)
            if scale is not None:
                r = r * scale
            for j in range(heads_per_slab):
                ref[slab * heads_per_slab + j] = r[:, j * HEAD_DIM:(j + 1) * HEAD_DIM].astype(BF16)

    emit(q_ref, SWA_HEADS, 0, True, Q_SCALE)
    emit(k_ref, SWA_KV_HEADS, SWA_HEADS * HEAD_DIM, True, None)
    v0 = (SWA_HEADS + SWA_KV_HEADS) * HEAD_DIM
    for slab in range(SWA_KV_HEADS // heads_per_slab):
        r = jnp.dot(xb, w_ref[:, v0 + slab * LANES:v0 + (slab + 1) * LANES], preferred_element_type=F32)
        rt = r.T
        for j in range(heads_per_slab):
            vt_ref[slab * heads_per_slab + j] = rt[j * HEAD_DIM:(j + 1) * HEAD_DIM, :].astype(BF16)


def _proj_rope(x, positions, w_bf16, tm):
    b, s, d = x.shape
    n = w_bf16.shape[1]
    half = HEAD_DIM // 2
    inv_freq = (ROPE_THETA ** (-jnp.arange(half, dtype=F32) / half)).reshape(half, 1)

    def heads(nh):
        return (jax.ShapeDtypeStruct((b, nh, s, HEAD_DIM), BF16),
                pl.BlockSpec((None, nh, tm, HEAD_DIM), lambda bi, i: (bi, 0, i, 0)))

    (qs, qspec), (ks, kspec) = heads(SWA_HEADS), heads(SWA_KV_HEADS)
    vs = jax.ShapeDtypeStruct((b, SWA_KV_HEADS, HEAD_DIM, s), BF16)
    vspec = pl.BlockSpec((None, SWA_KV_HEADS, HEAD_DIM, tm), lambda bi, i: (bi, 0, 0, i))
    return pl.pallas_call(
        _proj_rope_kernel,
        grid=(b, s // tm),
        in_specs=[pl.BlockSpec((None, tm, d), lambda bi, i: (bi, i, 0)),
                  pl.BlockSpec((None, 1, tm), lambda bi, i: (bi, 0, i)),
                  pl.BlockSpec((half, 1), lambda bi, i: (0, 0)),
                  pl.BlockSpec((d, n), lambda bi, i: (0, 0))],
        out_specs=[qspec, kspec, vspec],
        out_shape=[qs, ks, vs],
        compiler_params=_params("parallel", "parallel"),
        name="proj_rope",
    )(x, positions.reshape(b, 1, s), inv_freq, w_bf16)


def _reduce_rows(x, op):
    while x.shape[0] > SUBLANES:
        half = x.shape[0] // 2
        x = op(x[:half], x[half:])
    for shift in (4, 2, 1):
        x = op(x, pltpu.roll(x, shift, axis=0))
    return x[0:1]


SWA_BLOCKS_PER_STEP = 8


def _swa_kernel(q_ref, kp_ref, kc_ref, vtp_ref, vtc_ref, sink_ref, o_ref):
    i = pl.program_id(1)
    w = SWA_WINDOW
    key = lax.broadcasted_iota(jnp.int32, (2 * w, w), 0)
    qry = lax.broadcasted_iota(jnp.int32, (2 * w, w), 1)
    dist = qry + w - key
    band = (dist >= 0) & (dist < w)
    for blk in range(SWA_BLOCKS_PER_STEP):
        visible = band if blk else band & ((key >= w) | (i > 0))
        bias = jnp.concatenate([jnp.where(visible, 0.0, NEG_BIG)] * SWA_GROUP, axis=1)
        outs = []
        for kv in range(SWA_KV_HEADS):
            if blk:
                kk = kc_ref[kv, (blk - 1) * w:(blk + 1) * w, :]
                vvt = vtc_ref[kv, :, (blk - 1) * w:(blk + 1) * w]
            else:
                kk = jnp.concatenate([kp_ref[kv], kc_ref[kv, 0:w, :]], axis=0)
                vvt = jnp.concatenate([vtp_ref[kv], vtc_ref[kv, :, 0:w]], axis=1)
            qg = jnp.concatenate([q_ref[kv * SWA_GROUP + g, blk * w:(blk + 1) * w, :]
                                  for g in range(SWA_GROUP)], axis=0)
            st = lax.dot_general(kk, qg, (((1,), (1,)), ((), ())), preferred_element_type=F32) + bias
            sink = sink_ref[kv:kv + 1, :]
            m = jnp.maximum(_reduce_rows(st, jnp.maximum), sink)
            p = jnp.exp(st - m)
            denom = _reduce_rows(p, jnp.add) + jnp.exp(sink - m)
            ot = jnp.dot(vvt, p.astype(BF16), preferred_element_type=F32) / denom
            outs.extend(ot[:, g * w:(g + 1) * w] for g in range(SWA_GROUP))
        o_ref[blk * w:(blk + 1) * w, :] = jnp.concatenate(outs, axis=0).T.astype(o_ref.dtype)


def _swa(q, k, vt, sinks):
    b, nh, s, dh = q.shape
    nkv = k.shape[1]
    w = SWA_WINDOW
    n = SWA_BLOCKS_PER_STEP
    assert s % (n * w) == 0
    cur = pl.BlockSpec((None, nkv, n * w, dh), lambda bi, i: (bi, 0, i, 0))
    prev = pl.BlockSpec((None, nkv, w, dh), lambda bi, i: (bi, 0, jnp.maximum(n * i - 1, 0), 0))
    cur_t = pl.BlockSpec((None, nkv, dh, n * w), lambda bi, i: (bi, 0, 0, i))
    prev_t = pl.BlockSpec((None, nkv, dh, w), lambda bi, i: (bi, 0, 0, jnp.maximum(n * i - 1, 0)))
    sink_tile = jnp.repeat(sinks.astype(F32).reshape(nkv, nh // nkv), w, axis=1)
    return pl.pallas_call(
        _swa_kernel,
        grid=(b, s // (n * w)),
        in_specs=[pl.BlockSpec((None, nh, n * w, dh), lambda bi, i: (bi, 0, i, 0)),
                  prev, cur, prev_t, cur_t,
                  pl.BlockSpec(sink_tile.shape, lambda bi, i: (0, 0))],
        out_specs=pl.BlockSpec((None, n * w, nh * dh), lambda bi, i: (bi, i, 0)),
        out_shape=jax.ShapeDtypeStruct((b, s, nh * dh), BF16),
        compiler_params=_params("parallel", "parallel"),
        name="swa",
    )(q, k, k, vt, vt, sink_tile)


PAIRS_PER_GROUP = GROUP_SIZE * (GROUP_SIZE - 1) // 2
N_BUCKETS = N_GROUPS * PAIRS_PER_GROUP
ROUTE_ROWS = SUBLANES


def _route(logits_t):
    rows = [logits_t[e:e + 1, :] for e in range(N_EXPERTS)]
    mx = functools.reduce(jnp.maximum, rows)
    ex = [jnp.exp(r - mx) for r in rows]
    total = functools.reduce(lambda p, q: p + q, ex)
    probs = [e / total for e in ex]

    group_score = []
    for g in range(N_GROUPS):
        a, b, c, d = probs[g * GROUP_SIZE:(g + 1) * GROUP_SIZE]
        hi1, lo1 = jnp.maximum(a, b), jnp.minimum(a, b)
        hi2, lo2 = jnp.maximum(c, d), jnp.minimum(c, d)
        top1 = jnp.maximum(hi1, hi2)
        top2 = jnp.maximum(jnp.minimum(hi1, hi2), jnp.maximum(lo1, lo2))
        group_score.append(top1 + top2)
    best = functools.reduce(jnp.maximum, group_score)
    g_sel = jnp.full(best.shape, N_GROUPS - 1, jnp.int32)
    for g in range(N_GROUPS - 2, -1, -1):
        g_sel = jnp.where(group_score[g] == best, g, g_sel)

    in_group = []
    for j in range(GROUP_SIZE):
        val = probs[(N_GROUPS - 1) * GROUP_SIZE + j]
        for g in range(N_GROUPS - 2, -1, -1):
            val = jnp.where(g_sel == g, probs[g * GROUP_SIZE + j], val)
        in_group.append(val)

    def first_argmax(vals):
        m = functools.reduce(jnp.maximum, vals)
        idx = jnp.full(m.shape, GROUP_SIZE - 1, jnp.int32)
        for j in range(GROUP_SIZE - 2, -1, -1):
            idx = jnp.where(vals[j] == m, j, idx)
        return m, idx

    w1, i1 = first_argmax(in_group)
    rest = [jnp.where(i1 == j, -1.0, in_group[j]) for j in range(GROUP_SIZE)]
    w2, i2 = first_argmax(rest)
    norm = w1 + w2
    first_is_lo = i1 < i2
    i_lo = jnp.minimum(i1, i2)
    i_hi = jnp.maximum(i1, i2)
    pair = jnp.where(i_lo == 0, i_hi - 1, jnp.where(i_lo == 1, i_hi + 1, PAIRS_PER_GROUP - 1))
    bucket = (g_sel * PAIRS_PER_GROUP + pair).astype(F32)
    w_lo = jnp.where(first_is_lo, w1, w2) / norm
    w_hi = jnp.where(first_is_lo, w2, w1) / norm
    return bucket, w_lo, w_hi


def _mix_out_kernel(*refs, n_head_major, alpha):
    x_ref = refs[0]
    y_refs = refs[1:1 + len(n_head_major)]
    w_refs = refs[1 + len(n_head_major):1 + 2 * len(n_head_major)]
    g_ref, b_ref, wrh_ref, wrl_ref, brt_ref, x1e_ref, route_ref = refs[1 + 2 * len(n_head_major):]
    tm, d = x_ref.shape
    h = alpha * x_ref[...]
    for y_ref, w_ref, nh in zip(y_refs, w_refs, n_head_major):
        if nh:
            y = jnp.concatenate([y_ref[j] for j in range(nh)], axis=-1)
        else:
            y = y_ref[...]
        h = h + jnp.dot(y, w_ref[...], preferred_element_type=F32)
    x1 = _layer_norm(h, g_ref[...], b_ref[...])
    x_hi = x1.astype(BF16)
    x_lo = (x1 - x_hi.astype(F32)).astype(BF16)

    def nt_dot(w, xv):
        return lax.dot_general(w, xv, (((1,), (1,)), ((), ())), preferred_element_type=F32)

    logits_t = (nt_dot(wrh_ref[...], x_hi) + nt_dot(wrh_ref[...], x_lo) + nt_dot(wrl_ref[...], x_hi)
                + brt_ref[...])
    route = jnp.concatenate(list(_route(logits_t)) + [jnp.zeros((ROUTE_ROWS - 3, tm), F32)], axis=0)
    route_ref[...] = route
    x1e_ref[:, 0:d] = x1
    x1e_ref[:, d:d + LANES] = jnp.concatenate([route, jnp.zeros((LANES - ROUTE_ROWS, tm), F32)], axis=0).T


def _mix_out(x, ys, ws, ln_g, ln_b, w_router, b_router, alpha, tm):
    b, s, d = x.shape
    n_head_major = tuple(y.shape[1] if hm else 0 for y, hm in ys)
    y_specs = []
    for (y, hm) in ys:
        if hm:
            y_specs.append(pl.BlockSpec((None, y.shape[1], tm, y.shape[3]), lambda bi, i: (bi, 0, i, 0)))
        else:
            y_specs.append(pl.BlockSpec((None, tm, y.shape[2]), lambda bi, i: (bi, i, 0)))
    w_specs = [pl.BlockSpec(w.shape, lambda bi, i: (0, 0)) for w in ws]
    row = pl.BlockSpec((1, d), lambda bi, i: (0, 0))
    tok = pl.BlockSpec((None, tm, d), lambda bi, i: (bi, i, 0))
    wr_t = w_router.T.astype(F32)
    wr_hi = wr_t.astype(BF16)
    wr_lo = (wr_t - wr_hi.astype(F32)).astype(BF16)
    wr_spec = pl.BlockSpec((N_EXPERTS, d), lambda bi, i: (0, 0))
    return pl.pallas_call(
        functools.partial(_mix_out_kernel, n_head_major=n_head_major, alpha=alpha),
        grid=(b, s // tm),
        in_specs=[tok] + y_specs + w_specs + [row, row, wr_spec, wr_spec,
                  pl.BlockSpec((N_EXPERTS, 1), lambda bi, i: (0, 0))],
        out_specs=[pl.BlockSpec((None, tm, d + LANES), lambda bi, i: (bi, i, 0)),
                   pl.BlockSpec((None, ROUTE_ROWS, tm), lambda bi, i: (bi, 0, i))],
        out_shape=[jax.ShapeDtypeStruct((b, s, d + LANES), F32),
                   jax.ShapeDtypeStruct((b, ROUTE_ROWS, s), F32)],
        compiler_params=_params("parallel", "parallel"),
        name="mix_out_ln_router",
    )(x, *[y for y, _ in ys], *ws, ln_g.reshape(1, d), ln_b.reshape(1, d),
      wr_hi, wr_lo, b_router.astype(F32).reshape(N_EXPERTS, 1))


MOE_TILE = 256
PLAN_COLS = 256
META_ROWS = SUBLANES


def _num_moe_tiles(t):
    return t // MOE_TILE + N_BUCKETS


def _plan_kernel(bid_ref, pos_ref, meta_ref):
    r, c = bid_ref.shape
    bid = bid_ref[...]
    before = (lax.broadcasted_iota(jnp.int32, (c, c), 0)
              < lax.broadcasted_iota(jnp.int32, (c, c), 1)).astype(BF16)
    rows_before = (lax.broadcasted_iota(jnp.int32, (r, r), 1)
                   < lax.broadcasted_iota(jnp.int32, (r, r), 0)).astype(BF16)
    lane = lax.broadcasted_iota(jnp.int32, (1, LANES), 1)
    tile_start = lane.astype(F32) * MOE_TILE

    def body(b, state):
        base, pos, tile_bucket, last_tile = state
        ind = (bid == lax.convert_element_type(b, F32)).astype(F32)
        within = jnp.dot(ind.astype(BF16), before, preferred_element_type=F32)
        row_total = jnp.sum(ind, axis=1, keepdims=True)
        row_off = jnp.dot(rows_before, jnp.broadcast_to(row_total, (r, LANES)).astype(BF16),
                          preferred_element_type=F32)[:, 0:1]
        count = jnp.sum(row_total, axis=0, keepdims=True)
        padded = jnp.floor((count + (MOE_TILE - 1)) * (1.0 / MOE_TILE)) * MOE_TILE
        pos = pos + ind * (base + row_off + within)
        end = base + padded
        tile_bucket = tile_bucket + (tile_start >= end).astype(F32)
        last_row = jnp.where(count > 0.0, end - MOE_TILE, -1.0)
        last_tile = jnp.where(lane == b, last_row, last_tile)
        return end, pos, tile_bucket, last_tile

    total, pos, tile_bucket, last_tile = lax.fori_loop(
        0, N_BUCKETS, body, (jnp.zeros((1, 1), F32), jnp.zeros((r, c), F32), jnp.zeros((1, LANES), F32),
                             jnp.full((1, LANES), -1.0, F32)))
    pos_ref[...] = pos.astype(jnp.int32)

    tb = jnp.minimum(tile_bucket, N_BUCKETS - 1.0)
    group = sum((tb >= g * PAIRS_PER_GROUP).astype(F32) for g in range(1, N_GROUPS))
    pair = tb - group * PAIRS_PER_GROUP
    i_lo = jnp.where(pair < 3, 0.0, jnp.where(pair < 5, 1.0, 2.0))
    i_hi = jnp.where(pair < 3, pair + 1.0, jnp.where(pair < 5, pair - 1.0, 3.0))
    meta = jnp.concatenate([group * GROUP_SIZE + i_lo, group * GROUP_SIZE + i_hi,
                            (tile_start < total).astype(F32),
                            jnp.where(lane == N_BUCKETS, total, last_tile),
                            jnp.zeros((META_ROWS - 4, LANES), F32)], axis=0)
    meta_ref[...] = meta.astype(jnp.int32)


def _plan(bucket_ids):
    r, c = bucket_ids.shape
    return pl.pallas_call(
        _plan_kernel,
        out_shape=[jax.ShapeDtypeStruct((r, c), jnp.int32), jax.ShapeDtypeStruct((META_ROWS, LANES), jnp.int32)],
        compiler_params=pltpu.CompilerParams(vmem_limit_bytes=VMEM_LIMIT_BYTES),
        name="moe_plan",
    )(bucket_ids)


DMA_PRIORITIES = 2


def _row_copy(src_ref, src_row, dst_ref, dst_row, sem):
    return pltpu.make_async_copy(src_ref.at[pl.ds(src_row, 1)], dst_ref.at[pl.ds(dst_row, 1)], sem)


def _dispatch_kernel(pos_ref, last_ref, x_ref, o_ref, inv_ref, zeros, inv, fill, sem, zsem, isem, *,
                     first_spare, n_tokens):
    tm = x_ref.shape[0]
    i = pl.program_id(0)

    @pl.when(i == 0)
    def _():
        zeros[...] = jnp.zeros_like(zeros)
        fill[...] = jnp.full(fill.shape, n_tokens, jnp.int32)
        to_smem = pltpu.make_async_copy(fill, inv, isem)
        to_smem.start()
        rows_in_use = last_ref[N_BUCKETS]
        clears = [(last_ref[b] >= 0, last_ref[b]) for b in range(N_BUCKETS)]
        clears += [(k * MOE_TILE >= rows_in_use, k * MOE_TILE)
                   for k in range(first_spare, o_ref.shape[0] // MOE_TILE)]

        def clear(row):
            start = row if isinstance(row, int) else pl.multiple_of(row, MOE_TILE)
            return pltpu.make_async_copy(zeros, o_ref.at[pl.ds(start, MOE_TILE)], zsem)

        for needed, row in clears:
            @pl.when(needed)
            def _():
                clear(row).start()
        for needed, row in clears:
            @pl.when(needed)
            def _():
                clear(row).wait()
        to_smem.wait()

    base = i * tm
    for r in range(tm):
        dst = pos_ref[base + r]
        inv[dst] = base + r
        _row_copy(x_ref, r, o_ref, dst, sem).start(priority=r % DMA_PRIORITIES)
    pltpu.make_async_copy(x_ref, o_ref.at[pl.ds(0, tm)], sem).wait()

    @pl.when(i == pl.num_programs(0) - 1)
    def _():
        to_hbm = pltpu.make_async_copy(inv, inv_ref, isem)
        to_hbm.start()
        to_hbm.wait()


def _dispatch(pos, last_tile_rows, xt, n_rows, tm):
    t = xt.shape[0]
    tile = xt.shape[1:]
    dma_sem = pltpu.SemaphoreType.DMA(())
    return pl.pallas_call(
        functools.partial(_dispatch_kernel, first_spare=t // MOE_TILE, n_tokens=t),
        grid_spec=pltpu.PrefetchScalarGridSpec(
            num_scalar_prefetch=2,
            grid=(t // tm,),
            in_specs=[pl.BlockSpec((tm,) + tile, lambda i, pos_ref, last_ref: (i,) + (0,) * len(tile))],
            out_specs=[pl.BlockSpec(memory_space=pl.ANY), pl.BlockSpec(memory_space=pl.ANY)],
            scratch_shapes=[pltpu.VMEM((MOE_TILE,) + tile, xt.dtype), pltpu.SMEM((n_rows,), jnp.int32),
                            pltpu.VMEM((n_rows,), jnp.int32), dma_sem, dma_sem, dma_sem]),
        out_shape=[jax.ShapeDtypeStruct((n_rows,) + tile, xt.dtype),
                   jax.ShapeDtypeStruct((n_rows,), jnp.int32)],
        compiler_params=_params("arbitrary"),
        name="moe_dispatch",
    )(pos, last_tile_rows, xt)


def _moe_tile_kernel(elo_ref, ehi_ref, valid_ref, inv, x_ref, wgl_ref, wul_ref, wdl_ref, wgh_ref, wuh_ref,
                     wdh_ref, m_hbm, o_even, o_odd, ssem_even, ssem_odd, wgl_b, wul_b, wdl_b, wgh_b, wuh_b,
                     wdh_b, *, n_tokens):
    d = wgl_ref.shape[0]
    k = pl.program_id(0)
    last = pl.num_programs(0) - 1
    prev = jnp.maximum(k - 1, 0)
    valid = valid_ref[k] != 0

    def scatter(tile, r, buf, sem, to_spare):
        tok = inv[tile * MOE_TILE + r]
        dst = jnp.where(to_spare | (tok >= n_tokens), n_tokens + r, tok)
        return _row_copy(buf, r, m_hbm, dst, sem)

    def scatter_done(buf, sem):
        pltpu.make_async_copy(buf, m_hbm.at[pl.ds(0, MOE_TILE)], sem).wait()

    @pl.when(k == 0)
    def _():
        o_odd[...] = jnp.zeros_like(o_odd)

    def refresh(e_ref, srcs, dsts):
        @pl.when(valid & ((k == 0) | (e_ref[k] != e_ref[prev])))
        def _():
            for src, dst in zip(srcs, dsts):
                dst[...] = src[...].astype(BF16)

    refresh(elo_ref, (wgl_ref, wul_ref, wdl_ref), (wgl_b, wul_b, wdl_b))
    refresh(ehi_ref, (wgh_ref, wuh_ref, wdh_ref), (wgh_b, wuh_b, wdh_b))

    def tile_valid(j):
        return (j >= 0) & (valid_ref[jnp.maximum(j, 0)] != 0)

    def scattered_in(j):
        return (j == 0) | tile_valid(j) | tile_valid(j - 1)

    def step(o_cur, ssem_cur, o_prv, ssem_prv):
        @pl.when((k > 0) & scattered_in(k - 1))
        def _():
            scatter_done(o_cur, ssem_cur)

        first = k == 0

        @pl.when(valid)
        def _():
            for r in range(MOE_TILE):
                scatter(prev, r, o_prv, ssem_prv, first).start(priority=r % DMA_PRIORITIES)
            x = x_ref[:, 0:d].astype(BF16)
            acc = None
            for wg_b, wu_b, wd_b, lane in ((wgl_b, wul_b, wdl_b, d + 1), (wgh_b, wuh_b, wdh_b, d + 2)):
                weight = x_ref[:, lane:lane + 1]
                hg = jnp.dot(x, wg_b[...], preferred_element_type=F32)
                hu = jnp.dot(x, wu_b[...], preferred_element_type=F32)
                hidden = (hg * _sigmoid(hg)) * hu * weight
                y = jnp.dot(hidden.astype(BF16), wd_b[...], preferred_element_type=F32)
                acc = y if acc is None else acc + y
            o_cur[...] = acc

        @pl.when(jnp.logical_not(valid) & tile_valid(k - 1))
        def _():
            def issue(r, carry):
                scatter(prev, r, o_prv, ssem_prv, first).start()
                return carry

            lax.fori_loop(0, MOE_TILE, issue, 0, unroll=8)

        @pl.when((k == last) & scattered_in(k))
        def _():
            scatter_done(o_prv, ssem_prv)

        @pl.when((k == last) & valid)
        def _():
            def issue(r, carry):
                scatter(k, r, o_cur, ssem_cur, False).start()
                return carry

            lax.fori_loop(0, MOE_TILE, issue, 0, unroll=8)
            scatter_done(o_cur, ssem_cur)

    @pl.when(k % 2 == 0)
    def _():
        step(o_even, ssem_even, o_odd, ssem_odd)

    @pl.when(k % 2 == 1)
    def _():
        step(o_odd, ssem_odd, o_even, ssem_even)


def _moe_tiles(e_lo, e_hi, valid, inv, xs, wg, wu, wd, layer, n_tokens):
    n_rows, dw = xs.shape
    _, _, d, f = wg.shape
    up_lo = pl.BlockSpec((None, None, d, f), lambda k, lo, hi, ok, iv: (layer, lo[k], 0, 0))
    up_hi = pl.BlockSpec((None, None, d, f), lambda k, lo, hi, ok, iv: (layer, hi[k], 0, 0))
    down_lo = pl.BlockSpec((None, None, f, d), lambda k, lo, hi, ok, iv: (layer, lo[k], 0, 0))
    down_hi = pl.BlockSpec((None, None, f, d), lambda k, lo, hi, ok, iv: (layer, hi[k], 0, 0))
    up_b = pltpu.VMEM((d, f), BF16)
    down_b = pltpu.VMEM((f, d), BF16)
    o_buf = pltpu.VMEM((MOE_TILE, d), F32)
    dma_sem = pltpu.SemaphoreType.DMA(())
    return pl.pallas_call(
        functools.partial(_moe_tile_kernel, n_tokens=n_tokens),
        grid_spec=pltpu.PrefetchScalarGridSpec(
            num_scalar_prefetch=4,
            grid=(n_rows // MOE_TILE,),
            in_specs=[pl.BlockSpec((MOE_TILE, dw), lambda k, lo, hi, ok, iv: (k, 0)),
                      up_lo, up_lo, down_lo, up_hi, up_hi, down_hi],
            out_specs=pl.BlockSpec(memory_space=pl.ANY),
            scratch_shapes=[o_buf, o_buf, dma_sem, dma_sem, up_b, up_b, down_b, up_b, up_b, down_b]),
        out_shape=jax.ShapeDtypeStruct((n_tokens + MOE_TILE, d), F32),
        compiler_params=_params("arbitrary"),
        name="moe_tiles",
    )(e_lo, e_hi, valid, inv, xs, wg, wu, wd, wg, wu, wd)


def _ln_ple_kernel(x1_ref, m_ref, p_ref, g_ref, b_ref, wg_ref, bg_ref, wp_ref, o_ref, *, alpha):
    x2 = _layer_norm(alpha * x1_ref[...] + m_ref[...], g_ref[...], b_ref[...])
    gate = _sigmoid(jnp.dot(x2.astype(BF16), wg_ref[...], preferred_element_type=F32) + bg_ref[...])
    emb = jnp.dot(p_ref[...].astype(BF16), wp_ref[...], preferred_element_type=F32)
    o_ref[...] = x2 + gate * emb


def _ln_ple(x1e, m, p, layer, ln_g, ln_b, wg, bg, wp, alpha, tm):
    t = x1e.shape[0]
    d = m.shape[1]
    pd = p.shape[2]
    tok = pl.BlockSpec((tm, d), lambda i: (i, 0))
    row = pl.BlockSpec((1, d), lambda i: (0, 0))
    return pl.pallas_call(
        functools.partial(_ln_ple_kernel, alpha=alpha),
        grid=(t // tm,),
        in_specs=[tok, tok, pl.BlockSpec((None, tm, pd), lambda i: (layer, i, 0)), row, row,
                  pl.BlockSpec((d, d), lambda i: (0, 0)), row, pl.BlockSpec((pd, d), lambda i: (0, 0))],
        out_specs=tok,
        out_shape=jax.ShapeDtypeStruct((t, d), F32),
        compiler_params=_params("parallel"),
        name="ln_ple",
    )(x1e, m, p, ln_g.reshape(1, d), ln_b.reshape(1, d), wg, bg.reshape(1, d), wp)


def _pick_tile(n, target):
    t = min(n, target)
    while n % t:
        t //= 2
    return t


def kernel(x, p, positions, w_in_ab, w_out_ab, conv_w, conv_b, lru_w_r, lru_b_r, lru_w_i, lru_b_i, lru_lambda, w_qkv_c, w_out_c, sinks_c, ln_mix_g, ln_mix_b, ln_ffn_g, ln_ffn_b, w_router, b_router, exp_w_gate, exp_w_up, exp_w_down, ple_w_proj, ple_w_gate, ple_b_gate):
    b, s, d = x.shape
    depth = p.shape[0]
    t = b * s
    alpha = (2 * depth) ** 0.25
    tm = _pick_tile(s, 1024)
    assert t % MOE_TILE == 0 and t % PLAN_COLS == 0
    n_tiles = _num_moe_tiles(t)
    assert n_tiles <= LANES
    for i in range(depth):
        j = i // 2
        if i % 2 == 0:
            q, k, v, xr, gr = _proj_ab(x, w_in_ab[j].astype(BF16), tm)
            y_sb = _sb_attention(q, k, v, _pick_tile(s, 256), SB_HEADS)
            y_lru = _lru(xr, gr, conv_w[j], conv_b[j], _block_diag(lru_w_r[j]).astype(BF16), lru_b_r[j],
                         _block_diag(lru_w_i[j]).astype(BF16), lru_b_i[j], lru_lambda[j],
                         _pick_tile(s, 256))
            w_out = w_out_ab[j].astype(BF16)
            ys = [(y_sb, True), (y_lru, False)]
            ws = [w_out[:SB_WIDTH], w_out[SB_WIDTH:]]
        else:
            q, k, v = _proj_rope(x, positions, w_qkv_c[j].astype(BF16), tm)
            y = _swa(q, k, v, sinks_c[j])
            ys = [(y, False)]
            ws = [w_out_c[j].astype(BF16)]
        x1e, route = _mix_out(x, ys, ws, ln_mix_g[i], ln_mix_b[i], w_router, b_router, alpha, tm)
        x1e = x1e.reshape(t, d + LANES)
        pos, meta = _plan(route[:, 0, :].reshape(t // PLAN_COLS, PLAN_COLS))
        pos = pos.reshape(t)
        xs, inv = _dispatch(pos, meta[3, :N_BUCKETS + 1], x1e, n_tiles * MOE_TILE, _pick_tile(t, 512))
        m = _moe_tiles(meta[0, :n_tiles], meta[1, :n_tiles], meta[2, :n_tiles], inv, xs,
                       exp_w_gate, exp_w_up, exp_w_down, i, t)
        x = _ln_ple(x1e, m, p.reshape(depth, t, -1), i, ln_ffn_g[i], ln_ffn_b[i],
                    ple_w_gate[i].astype(BF16), ple_b_gate[i], ple_w_proj[i].astype(BF16), alpha,
                    _pick_tile(t, 1024)).reshape(b, s, d)
    return x
```

```python
import functools
import math

import jax
import jax.numpy as jnp
from jax import lax
from jax.experimental import pallas as pl
from jax.experimental.pallas import tpu as pltpu

HEAD_DIM = 64
SB_HEADS = 8
SB_WIDTH = SB_HEADS * HEAD_DIM
LRU_WIDTH = 512
LRU_BLOCKS = 8
LRU_C = 8.0
CONV_WIDTH = 4
SWA_HEADS = 16
SWA_KV_HEADS = 4
SWA_GROUP = SWA_HEADS // SWA_KV_HEADS
SWA_WINDOW = 128
ROPE_THETA = 10000.0
N_EXPERTS = 16
N_GROUPS = 4
GROUP_SIZE = N_EXPERTS // N_GROUPS
LN_EPS = 1e-5
Q_SCALE = HEAD_DIM ** -0.5

LANES = 128
SUBLANES = 8
VMEM_LIMIT_BYTES = 48 * 1024 * 1024

NEG_BIG = -1e30

BF16 = jnp.bfloat16
F32 = jnp.float32


def _params(*semantics):
    return pltpu.CompilerParams(dimension_semantics=semantics, vmem_limit_bytes=VMEM_LIMIT_BYTES)


def _softplus(z):
    return jnp.maximum(z, 0.0) + jnp.log(1.0 + jnp.exp(-jnp.abs(z)))


def _sigmoid(z):
    return 1.0 / (1.0 + jnp.exp(-z))


def _layer_norm(y, g, b):
    mu = jnp.mean(y, axis=-1, keepdims=True)
    d = y - mu
    var = jnp.mean(d * d, axis=-1, keepdims=True)
    return d * lax.rsqrt(var + LN_EPS) * g + b


def _proj_ab_kernel(x_ref, w_ref, q_ref, k_ref, v_ref, xr_ref, gr_ref):
    xb = x_ref[...].astype(BF16)

    def chunk(c):
        return jnp.dot(xb, w_ref[:, c * SB_WIDTH:(c + 1) * SB_WIDTH], preferred_element_type=F32)

    for c, (ref, scale) in enumerate(((q_ref, Q_SCALE), (k_ref, None), (v_ref, None))):
        r = chunk(c)
        if scale is not None:
            r = r * scale
        for h in range(SB_HEADS):
            ref[h] = r[:, h * HEAD_DIM:(h + 1) * HEAD_DIM].astype(BF16)
    xr_ref[...] = chunk(3)
    gr_ref[...] = chunk(4)


def _proj_ab(x, w_bf16, tm):
    b, s, d = x.shape
    n = w_bf16.shape[1]
    heads = jax.ShapeDtypeStruct((b, SB_HEADS, s, HEAD_DIM), BF16)
    flat = jax.ShapeDtypeStruct((b, s, LRU_WIDTH), F32)
    head_spec = pl.BlockSpec((None, SB_HEADS, tm, HEAD_DIM), lambda bi, i: (bi, 0, i, 0))
    flat_spec = pl.BlockSpec((None, tm, LRU_WIDTH), lambda bi, i: (bi, i, 0))
    return pl.pallas_call(
        _proj_ab_kernel,
        grid=(b, s // tm),
        in_specs=[pl.BlockSpec((None, tm, d), lambda bi, i: (bi, i, 0)),
                  pl.BlockSpec((d, n), lambda bi, i: (0, 0))],
        out_specs=[head_spec, head_spec, head_spec, flat_spec, flat_spec],
        out_shape=[heads, heads, heads, flat, flat],
        compiler_params=_params("parallel", "parallel"),
        name="proj_ab",
    )(x, w_bf16)


SB_DEAD_LOG_WEIGHT = -105.0
SB_MERGED_BLOCKS = 2


def _sb_attn_kernel(q_ref, k_ref, v_ref, o_ref, *, tq, hp):
    i = pl.program_id(2)
    row = lax.broadcasted_iota(jnp.int32, (tq, tq), 0)
    col = lax.broadcasted_iota(jnp.int32, (tq, tq), 1)
    minus_later = jnp.where(row > col, -1.0, 0.0).astype(BF16)
    causal = col < row

    def block(jb, carries, accs, masked):
        start = pl.multiple_of(jb * tq, tq)
        new_carries, new_accs = [], []
        for h in range(hp):
            kj = k_ref[h, pl.ds(start, tq), :]
            vj = v_ref[h, pl.ds(start, tq), :]
            z = lax.dot_general(q_ref[h], kj, (((1,), (1,)), ((), ())), preferred_element_type=F32)
            sp = _softplus(z)
            cost = jnp.where(causal, sp, 0.0) if masked else sp
            after = jnp.dot(cost.astype(BF16), minus_later, preferred_element_type=F32)
            w = jnp.exp((z - sp) + after + carries[h])
            if masked:
                w = jnp.where(causal, w, 0.0)
            new_accs.append(accs[h] + jnp.dot(w.astype(BF16), vj, preferred_element_type=F32))
            new_carries.append(carries[h] - jnp.sum(cost, axis=1, keepdims=True))
        return tuple(new_carries), tuple(new_accs)

    def live(carries):
        return functools.reduce(jnp.maximum, [jnp.max(c) for c in carries])

    zero = ((jnp.zeros((tq, 1), F32),) * hp, (jnp.zeros((tq, HEAD_DIM), F32),) * hp)

    def first(n):
        def run():
            state = block(i, *zero, True)
            for j in range(1, n):
                state = block(i - j, *state, False)
            return state
        return run

    done = jnp.minimum(i, SB_MERGED_BLOCKS - 1)
    carries, accs = lax.switch(done, [first(n) for n in range(1, SB_MERGED_BLOCKS + 1)])

    def cond(state):
        return (state[0] < i) & (state[1] > SB_DEAD_LOG_WEIGHT)

    def body(state):
        step, _, carries, accs = state
        carries, accs = block(i - 1 - step, carries, accs, False)
        return step + 1, live(carries), carries, accs

    _, _, _, accs = lax.while_loop(cond, body, (done, live(carries), carries, accs))
    for h in range(hp):
        o_ref[h] = accs[h].astype(o_ref.dtype)


def _sb_attention(q, k, v, tq, hp):
    b, h, s, dh = q.shape
    return pl.pallas_call(
        functools.partial(_sb_attn_kernel, tq=tq, hp=hp),
        grid=(b, h // hp, s // tq),
        in_specs=[pl.BlockSpec((None, hp, tq, dh), lambda bi, hi, i: (bi, hi, i, 0)),
                  pl.BlockSpec((None, hp, s, dh), lambda bi, hi, i: (bi, hi, 0, 0), pipeline_mode=pl.Buffered(1)),
                  pl.BlockSpec((None, hp, s, dh), lambda bi, hi, i: (bi, hi, 0, 0), pipeline_mode=pl.Buffered(1))],
        out_specs=pl.BlockSpec((None, hp, tq, dh), lambda bi, hi, i: (bi, hi, i, 0)),
        out_shape=jax.ShapeDtypeStruct((b, h, s, dh), BF16),
        compiler_params=_params("parallel", "parallel", "parallel"),
        name="sb_attention",
    )(q, k, v)


def _gelu_tanh(x):
    return 0.5 * x * (1.0 + jnp.tanh(math.sqrt(2.0 / math.pi) * (x + 0.044715 * (x * x * x))))


def _lru_kernel(xr_ref, gr_ref, cw_ref, cb_ref, wr_ref, br_ref, wi_ref, bi_ref, lam_ref, y_ref,
                xbuf, hprev, *, ts):
    @pl.when(pl.program_id(1) == 0)
    def _():
        xbuf[0:SUBLANES, :] = jnp.zeros((SUBLANES, LRU_WIDTH), F32)
        hprev[...] = jnp.zeros_like(hprev)

    xbuf[SUBLANES:SUBLANES + ts, :] = xr_ref[...]
    xc = cb_ref[...] + cw_ref[CONV_WIDTH - 1:CONV_WIDTH, :] * xbuf[SUBLANES:SUBLANES + ts, :]
    for kk in range(CONV_WIDTH - 1):
        off = SUBLANES - (CONV_WIDTH - 1) + kk
        xc = xc + cw_ref[kk:kk + 1, :] * xbuf[off:off + ts, :]
    xbuf[0:SUBLANES, :] = xbuf[ts:ts + SUBLANES, :]

    xcb = xc.astype(BF16)
    r = _sigmoid(jnp.dot(xcb, wr_ref[...], preferred_element_type=F32) + br_ref[...])
    gi = _sigmoid(jnp.dot(xcb, wi_ref[...], preferred_element_type=F32) + bi_ref[...])
    log_a = (-LRU_C) * r * _softplus(-lam_ref[...])
    a = jnp.exp(log_a)
    u = jnp.sqrt(1.0 - a * a) * (gi * xc)

    row = lax.broadcasted_iota(jnp.int32, (ts, LRU_WIDTH), 0)
    d = 1
    while d < ts:
        if d < SUBLANES:
            keep = row >= d
            a_sh = jnp.where(keep, pltpu.roll(a, d, axis=0), 1.0)
            u_sh = jnp.where(keep, pltpu.roll(u, d, axis=0), 0.0)
            u = a * u_sh + u
            a = a * a_sh
        else:
            u = jnp.concatenate([u[:d], a[d:] * u[:ts - d] + u[d:]], axis=0)
            a = jnp.concatenate([a[:d], a[d:] * a[:ts - d]], axis=0)
        d *= 2
    h = a * hprev[0:1, :] + u
    hprev[...] = jnp.broadcast_to(h[ts - 1:ts, :], hprev.shape)
    y_ref[...] = (_gelu_tanh(gr_ref[...]) * h).astype(y_ref.dtype)


def _lru(xr, gr, conv_w, conv_b, wr_bd, b_r, wi_bd, b_i, lam, ts):
    b, s, w = xr.shape
    seq_spec = pl.BlockSpec((None, ts, w), lambda bi, i: (bi, i, 0))

    def full(shape):
        return pl.BlockSpec(shape, lambda bi, i: (0,) * len(shape))

    return pl.pallas_call(
        functools.partial(_lru_kernel, ts=ts),
        grid=(b, s // ts),
        in_specs=[seq_spec, seq_spec, full((CONV_WIDTH, w)), full((1, w)), full((w, w)), full((1, w)),
                  full((w, w)), full((1, w)), full((1, w))],
        out_specs=seq_spec,
        out_shape=jax.ShapeDtypeStruct((b, s, w), BF16),
        scratch_shapes=[pltpu.VMEM((ts + 2 * SUBLANES, w), F32), pltpu.VMEM((SUBLANES, w), F32)],
        compiler_params=_params("parallel", "arbitrary"),
        name="rg_lru",
    )(xr, gr, conv_w, conv_b.reshape(1, w), wr_bd, b_r.reshape(1, w), wi_bd, b_i.reshape(1, w),
      lam.reshape(1, w))


def _block_diag(w):
    n, c, d = w.shape
    eye = jnp.eye(n, dtype=w.dtype)
    return (eye[:, None, :, None] * w[:, :, None, :]).reshape(n * c, n * d)


def _proj_rope_kernel(x_ref, pos_ref, freq_ref, w_ref, q_ref, k_ref, vt_ref):
    tm = x_ref.shape[0]
    xb = x_ref[...].astype(BF16)
    ang_t = freq_ref[...] * pos_ref[...].astype(F32)
    reps = LANES // (HEAD_DIM // 2)
    cos = jnp.concatenate([jnp.cos(ang_t)] * reps, axis=0).T
    sin = jnp.concatenate([jnp.sin(ang_t)] * reps, axis=0).T
    lane = lax.broadcasted_iota(jnp.int32, (tm, LANES), 1)
    first_half = (lane % HEAD_DIM) < (HEAD_DIM // 2)
    heads_per_slab = LANES // HEAD_DIM

    def rope(r):
        upper = pltpu.roll(r, LANES - HEAD_DIM // 2, axis=1)
        lower = pltpu.roll(r, HEAD_DIM // 2, axis=1)
        return r * cos + jnp.where(first_half, -upper, lower) * sin

    def emit(ref, n_heads, col0, rotary, scale):
        for slab in range(n_heads // heads_per_slab):
            c0 = col0 + slab * LANES
            r = jnp.dot(xb, w_ref[:, c0:c0 + LANES], preferred_element_type=F32)
            if rotary:
                r = rope(r)
            if scale is not None:
                r = r * scale
            for j in range(heads_per_slab):
                ref[slab * heads_per_slab + j] = r[:, j * HEAD_DIM:(j + 1) * HEAD_DIM].astype(BF16)

    emit(q_ref, SWA_HEADS, 0, True, Q_SCALE)
    emit(k_ref, SWA_KV_HEADS, SWA_HEADS * HEAD_DIM, True, None)
    v0 = (SWA_HEADS + SWA_KV_HEADS) * HEAD_DIM
    for slab in range(SWA_KV_HEADS // heads_per_slab):
        r = jnp.dot(xb, w_ref[:, v0 + slab * LANES:v0 + (slab + 1) * LANES], preferred_element_type=F32)
        rt = r.T
        for j in range(heads_per_slab):
            vt_ref[slab * heads_per_slab + j] = rt[j * HEAD_DIM:(j + 1) * HEAD_DIM, :].astype(BF16)


def _proj_rope(x, positions, w_bf16, tm):
    b, s, d = x.shape
    n = w_bf16.shape[1]
    half = HEAD_DIM // 2
    inv_freq = (ROPE_THETA ** (-jnp.arange(half, dtype=F32) / half)).reshape(half, 1)

    def heads(nh):
        return (jax.ShapeDtypeStruct((b, nh, s, HEAD_DIM), BF16),
                pl.BlockSpec((None, nh, tm, HEAD_DIM), lambda bi, i: (bi, 0, i, 0)))

    (qs, qspec), (ks, kspec) = heads(SWA_HEADS), heads(SWA_KV_HEADS)
    vs = jax.ShapeDtypeStruct((b, SWA_KV_HEADS, HEAD_DIM, s), BF16)
    vspec = pl.BlockSpec((None, SWA_KV_HEADS, HEAD_DIM, tm), lambda bi, i: (bi, 0, 0, i))
    return pl.pallas_call(
        _proj_rope_kernel,
        grid=(b, s // tm),
        in_specs=[pl.BlockSpec((None, tm, d), lambda bi, i: (bi, i, 0)),
                  pl.BlockSpec((None, 1, tm), lambda bi, i: (bi, 0, i)),
                  pl.BlockSpec((half, 1), lambda bi, i: (0, 0)),
                  pl.BlockSpec((d, n), lambda bi, i: (0, 0))],
        out_specs=[qspec, kspec, vspec],
        out_shape=[qs, ks, vs],
        compiler_params=_params("parallel", "parallel"),
        name="proj_rope",
    )(x, positions.reshape(b, 1, s), inv_freq, w_bf16)


def _reduce_rows(x, op):
    while x.shape[0] > SUBLANES:
        half = x.shape[0] // 2
        x = op(x[:half], x[half:])
    for shift in (4, 2, 1):
        x = op(x, pltpu.roll(x, shift, axis=0))
    return x[0:1]


SWA_BLOCKS_PER_STEP = 8


def _swa_kernel(q_ref, kp_ref, kc_ref, vtp_ref, vtc_ref, sink_ref, o_ref):
    i = pl.program_id(1)
    w = SWA_WINDOW
    key = lax.broadcasted_iota(jnp.int32, (2 * w, w), 0)
    qry = lax.broadcasted_iota(jnp.int32, (2 * w, w), 1)
    dist = qry + w - key
    band = (dist >= 0) & (dist < w)
    for blk in range(SWA_BLOCKS_PER_STEP):
        visible = band if blk else band & ((key >= w) | (i > 0))
        bias = jnp.concatenate([jnp.where(visible, 0.0, NEG_BIG)] * SWA_GROUP, axis=1)
        outs = []
        for kv in range(SWA_KV_HEADS):
            if blk:
                kk = kc_ref[kv, (blk - 1) * w:(blk + 1) * w, :]
                vvt = vtc_ref[kv, :, (blk - 1) * w:(blk + 1) * w]
            else:
                kk = jnp.concatenate([kp_ref[kv], kc_ref[kv, 0:w, :]], axis=0)
                vvt = jnp.concatenate([vtp_ref[kv], vtc_ref[kv, :, 0:w]], axis=1)
            qg = jnp.concatenate([q_ref[kv * SWA_GROUP + g, blk * w:(blk + 1) * w, :]
                                  for g in range(SWA_GROUP)], axis=0)
            st = lax.dot_general(kk, qg, (((1,), (1,)), ((), ())), preferred_element_type=F32) + bias
            sink = sink_ref[kv:kv + 1, :]
            m = jnp.maximum(_reduce_rows(st, jnp.maximum), sink)
            p = jnp.exp(st - m)
            denom = _reduce_rows(p, jnp.add) + jnp.exp(sink - m)
            ot = jnp.dot(vvt, p.astype(BF16), preferred_element_type=F32) / denom
            outs.extend(ot[:, g * w:(g + 1) * w] for g in range(SWA_GROUP))
        o_ref[blk * w:(blk + 1) * w, :] = jnp.concatenate(outs, axis=0).T.astype(o_ref.dtype)


def _swa(q, k, vt, sinks):
    b, nh, s, dh = q.shape
    nkv = k.shape[1]
    w = SWA_WINDOW
    n = SWA_BLOCKS_PER_STEP
    assert s % (n * w) == 0
    cur = pl.BlockSpec((None, nkv, n * w, dh), lambda bi, i: (bi, 0, i, 0))
    prev = pl.BlockSpec((None, nkv, w, dh), lambda bi, i: (bi, 0, jnp.maximum(n * i - 1, 0), 0))
    cur_t = pl.BlockSpec((None, nkv, dh, n * w), lambda bi, i: (bi, 0, 0, i))
    prev_t = pl.BlockSpec((None, nkv, dh, w), lambda bi, i: (bi, 0, 0, jnp.maximum(n * i - 1, 0)))
    sink_tile = jnp.repeat(sinks.astype(F32).reshape(nkv, nh // nkv), w, axis=1)
    return pl.pallas_call(
        _swa_kernel,
        grid=(b, s // (n * w)),
        in_specs=[pl.BlockSpec((None, nh, n * w, dh), lambda bi, i: (bi, 0, i, 0)),
                  prev, cur, prev_t, cur_t,
                  pl.BlockSpec(sink_tile.shape, lambda bi, i: (0, 0))],
        out_specs=pl.BlockSpec((None, n * w, nh * dh), lambda bi, i: (bi, i, 0)),
        out_shape=jax.ShapeDtypeStruct((b, s, nh * dh), BF16),
        compiler_params=_params("parallel", "parallel"),
        name="swa",
    )(q, k, k, vt, vt, sink_tile)


PAIRS_PER_GROUP = GROUP_SIZE * (GROUP_SIZE - 1) // 2
N_BUCKETS = N_GROUPS * PAIRS_PER_GROUP
ROUTE_ROWS = SUBLANES


def _route(logits_t):
    rows = [logits_t[e:e + 1, :] for e in range(N_EXPERTS)]
    mx = functools.reduce(jnp.maximum, rows)
    ex = [jnp.exp(r - mx) for r in rows]
    total = functools.reduce(lambda p, q: p + q, ex)
    probs = [e / total for e in ex]

    group_score = []
    for g in range(N_GROUPS):
        a, b, c, d = probs[g * GROUP_SIZE:(g + 1) * GROUP_SIZE]
        hi1, lo1 = jnp.maximum(a, b), jnp.minimum(a, b)
        hi2, lo2 = jnp.maximum(c, d), jnp.minimum(c, d)
        top1 = jnp.maximum(hi1, hi2)
        top2 = jnp.maximum(jnp.minimum(hi1, hi2), jnp.maximum(lo1, lo2))
        group_score.append(top1 + top2)
    best = functools.reduce(jnp.maximum, group_score)
    g_sel = jnp.full(best.shape, N_GROUPS - 1, jnp.int32)
    for g in range(N_GROUPS - 2, -1, -1):
        g_sel = jnp.where(group_score[g] == best, g, g_sel)

    in_group = []
    for j in range(GROUP_SIZE):
        val = probs[(N_GROUPS - 1) * GROUP_SIZE + j]
        for g in range(N_GROUPS - 2, -1, -1):
            val = jnp.where(g_sel == g, probs[g * GROUP_SIZE + j], val)
        in_group.append(val)

    def first_argmax(vals):
        m = functools.reduce(jnp.maximum, vals)
        idx = jnp.full(m.shape, GROUP_SIZE - 1, jnp.int32)
        for j in range(GROUP_SIZE - 2, -1, -1):
            idx = jnp.where(vals[j] == m, j, idx)
        return m, idx

    w1, i1 = first_argmax(in_group)
    rest = [jnp.where(i1 == j, -1.0, in_group[j]) for j in range(GROUP_SIZE)]
    w2, i2 = first_argmax(rest)
    norm = w1 + w2
    first_is_lo = i1 < i2
    i_lo = jnp.minimum(i1, i2)
    i_hi = jnp.maximum(i1, i2)
    pair = jnp.where(i_lo == 0, i_hi - 1, jnp.where(i_lo == 1, i_hi + 1, PAIRS_PER_GROUP - 1))
    bucket = (g_sel * PAIRS_PER_GROUP + pair).astype(F32)
    w_lo = jnp.where(first_is_lo, w1, w2) / norm
    w_hi = jnp.where(first_is_lo, w2, w1) / norm
    return bucket, w_lo, w_hi


def _mix_out_kernel(*refs, n_head_major, alpha):
    x_ref = refs[0]
    y_refs = refs[1:1 + len(n_head_major)]
    w_refs = refs[1 + len(n_head_major):1 + 2 * len(n_head_major)]
    g_ref, b_ref, wrh_ref, wrl_ref, brt_ref, x1e_ref, route_ref = refs[1 + 2 * len(n_head_major):]
    tm, d = x_ref.shape
    h = alpha * x_ref[...]
    for y_ref, w_ref, nh in zip(y_refs, w_refs, n_head_major):
        if nh:
            y = jnp.concatenate([y_ref[j] for j in range(nh)], axis=-1)
        else:
            y = y_ref[...]
        h = h + jnp.dot(y, w_ref[...], preferred_element_type=F32)
    x1 = _layer_norm(h, g_ref[...], b_ref[...])
    x_hi = x1.astype(BF16)
    x_lo = (x1 - x_hi.astype(F32)).astype(BF16)

    def nt_dot(w, xv):
        return lax.dot_general(w, xv, (((1,), (1,)), ((), ())), preferred_element_type=F32)

    logits_t = (nt_dot(wrh_ref[...], x_hi) + nt_dot(wrh_ref[...], x_lo) + nt_dot(wrl_ref[...], x_hi)
                + brt_ref[...])
    route = jnp.concatenate(list(_route(logits_t)) + [jnp.zeros((ROUTE_ROWS - 3, tm), F32)], axis=0)
    route_ref[...] = route
    x1e_ref[:, 0:d] = x1
    x1e_ref[:, d:d + LANES] = jnp.concatenate([route, jnp.zeros((LANES - ROUTE_ROWS, tm), F32)], axis=0).T


def _mix_out(x, ys, ws, ln_g, ln_b, w_router, b_router, alpha, tm):
    b, s, d = x.shape
    n_head_major = tuple(y.shape[1] if hm else 0 for y, hm in ys)
    y_specs = []
    for (y, hm) in ys:
        if hm:
            y_specs.append(pl.BlockSpec((None, y.shape[1], tm, y.shape[3]), lambda bi, i: (bi, 0, i, 0)))
        else:
            y_specs.append(pl.BlockSpec((None, tm, y.shape[2]), lambda bi, i: (bi, i, 0)))
    w_specs = [pl.BlockSpec(w.shape, lambda bi, i: (0, 0)) for w in ws]
    row = pl.BlockSpec((1, d), lambda bi, i: (0, 0))
    tok = pl.BlockSpec((None, tm, d), lambda bi, i: (bi, i, 0))
    wr_t = w_router.T.astype(F32)
    wr_hi = wr_t.astype(BF16)
    wr_lo = (wr_t - wr_hi.astype(F32)).astype(BF16)
    wr_spec = pl.BlockSpec((N_EXPERTS, d), lambda bi, i: (0, 0))
    return pl.pallas_call(
        functools.partial(_mix_out_kernel, n_head_major=n_head_major, alpha=alpha),
        grid=(b, s // tm),
        in_specs=[tok] + y_specs + w_specs + [row, row, wr_spec, wr_spec,
                  pl.BlockSpec((N_EXPERTS, 1), lambda bi, i: (0, 0))],
        out_specs=[pl.BlockSpec((None, tm, d + LANES), lambda bi, i: (bi, i, 0)),
                   pl.BlockSpec((None, ROUTE_ROWS, tm), lambda bi, i: (bi, 0, i))],
        out_shape=[jax.ShapeDtypeStruct((b, s, d + LANES), F32),
                   jax.ShapeDtypeStruct((b, ROUTE_ROWS, s), F32)],
        compiler_params=_params("parallel", "parallel"),
        name="mix_out_ln_router",
    )(x, *[y for y, _ in ys], *ws, ln_g.reshape(1, d), ln_b.reshape(1, d),
      wr_hi, wr_lo, b_router.astype(F32).reshape(N_EXPERTS, 1))


MOE_TILE = 512
PLAN_COLS = 256
META_ROWS = SUBLANES


def _num_moe_tiles(t):
    return t // MOE_TILE + N_BUCKETS


def _plan_kernel(bid_ref, pos_ref, meta_ref):
    r, c = bid_ref.shape
    bid = bid_ref[...]
    before = (lax.broadcasted_iota(jnp.int32, (c, c), 0)
              < lax.broadcasted_iota(jnp.int32, (c, c), 1)).astype(BF16)
    rows_before = (lax.broadcasted_iota(jnp.int32, (r, r), 1)
                   < lax.broadcasted_iota(jnp.int32, (r, r), 0)).astype(BF16)
    lane = lax.broadcasted_iota(jnp.int32, (1, LANES), 1)
    tile_start = lane.astype(F32) * MOE_TILE

    def body(b, state):
        base, pos, tile_bucket, last_tile = state
        ind = (bid == lax.convert_element_type(b, F32)).astype(F32)
        within = jnp.dot(ind.astype(BF16), before, preferred_element_type=F32)
        row_total = jnp.sum(ind, axis=1, keepdims=True)
        row_off = jnp.dot(rows_before, jnp.broadcast_to(row_total, (r, LANES)).astype(BF16),
                          preferred_element_type=F32)[:, 0:1]
        count = jnp.sum(row_total, axis=0, keepdims=True)
        padded = jnp.floor((count + (MOE_TILE - 1)) * (1.0 / MOE_TILE)) * MOE_TILE
        pos = pos + ind * (base + row_off + within)
        end = base + padded
        tile_bucket = tile_bucket + (tile_start >= end).astype(F32)
        last_row = jnp.where(count > 0.0, end - MOE_TILE, -1.0)
        last_tile = jnp.where(lane == b, last_row, last_tile)
        return end, pos, tile_bucket, last_tile

    total, pos, tile_bucket, last_tile = lax.fori_loop(
        0, N_BUCKETS, body, (jnp.zeros((1, 1), F32), jnp.zeros((r, c), F32), jnp.zeros((1, LANES), F32),
                             jnp.full((1, LANES), -1.0, F32)))
    pos_ref[...] = pos.astype(jnp.int32)

    tb = jnp.minimum(tile_bucket, N_BUCKETS - 1.0)
    group = sum((tb >= g * PAIRS_PER_GROUP).astype(F32) for g in range(1, N_GROUPS))
    pair = tb - group * PAIRS_PER_GROUP
    i_lo = jnp.where(pair < 3, 0.0, jnp.where(pair < 5, 1.0, 2.0))
    i_hi = jnp.where(pair < 3, pair + 1.0, jnp.where(pair < 5, pair - 1.0, 3.0))
    meta = jnp.concatenate([group * GROUP_SIZE + i_lo, group * GROUP_SIZE + i_hi,
                            (tile_start < total).astype(F32),
                            jnp.where(lane == N_BUCKETS, total, last_tile),
                            jnp.zeros((META_ROWS - 4, LANES), F32)], axis=0)
    meta_ref[...] = meta.astype(jnp.int32)


def _plan(bucket_ids):
    r, c = bucket_ids.shape
    return pl.pallas_call(
        _plan_kernel,
        out_shape=[jax.ShapeDtypeStruct((r, c), jnp.int32), jax.ShapeDtypeStruct((META_ROWS, LANES), jnp.int32)],
        compiler_params=pltpu.CompilerParams(vmem_limit_bytes=VMEM_LIMIT_BYTES),
        name="moe_plan",
    )(bucket_ids)


DMA_PRIORITIES = 2


def _row_copy(src_ref, src_row, dst_ref, dst_row, sem):
    return pltpu.make_async_copy(src_ref.at[pl.ds(src_row, 1)], dst_ref.at[pl.ds(dst_row, 1)], sem)


def _dispatch_kernel(pos_ref, last_ref, x_ref, o_ref, inv_ref, zeros, inv, fill, sem, zsem, isem, *,
                     first_spare, n_tokens):
    tm = x_ref.shape[0]
    i = pl.program_id(0)

    @pl.when(i == 0)
    def _():
        zeros[...] = jnp.zeros_like(zeros)
        fill[...] = jnp.full(fill.shape, n_tokens, jnp.int32)
        to_smem = pltpu.make_async_copy(fill, inv, isem)
        to_smem.start()
        rows_in_use = last_ref[N_BUCKETS]
        clears = [(last_ref[b] >= 0, last_ref[b]) for b in range(N_BUCKETS)]
        clears += [(k * MOE_TILE >= rows_in_use, k * MOE_TILE)
                   for k in range(first_spare, o_ref.shape[0] // MOE_TILE)]

        def clear(row):
            start = row if isinstance(row, int) else pl.multiple_of(row, MOE_TILE)
            return pltpu.make_async_copy(zeros, o_ref.at[pl.ds(start, MOE_TILE)], zsem)

        for needed, row in clears:
            @pl.when(needed)
            def _():
                clear(row).start()
        for needed, row in clears:
            @pl.when(needed)
            def _():
                clear(row).wait()
        to_smem.wait()

    base = i * tm
    for r in range(tm):
        dst = pos_ref[base + r]
        inv[dst] = base + r
        _row_copy(x_ref, r, o_ref, dst, sem).start(priority=r % DMA_PRIORITIES)
    pltpu.make_async_copy(x_ref, o_ref.at[pl.ds(0, tm)], sem).wait()

    @pl.when(i == pl.num_programs(0) - 1)
    def _():
        to_hbm = pltpu.make_async_copy(inv, inv_ref, isem)
        to_hbm.start()
        to_hbm.wait()


def _dispatch(pos, last_tile_rows, xt, n_rows, tm):
    t = xt.shape[0]
    tile = xt.shape[1:]
    dma_sem = pltpu.SemaphoreType.DMA(())
    return pl.pallas_call(
        functools.partial(_dispatch_kernel, first_spare=t // MOE_TILE, n_tokens=t),
        grid_spec=pltpu.PrefetchScalarGridSpec(
            num_scalar_prefetch=2,
            grid=(t // tm,),
            in_specs=[pl.BlockSpec((tm,) + tile, lambda i, pos_ref, last_ref: (i,) + (0,) * len(tile))],
            out_specs=[pl.BlockSpec(memory_space=pl.ANY), pl.BlockSpec(memory_space=pl.ANY)],
            scratch_shapes=[pltpu.VMEM((MOE_TILE,) + tile, xt.dtype), pltpu.SMEM((n_rows,), jnp.int32),
                            pltpu.VMEM((n_rows,), jnp.int32), dma_sem, dma_sem, dma_sem]),
        out_shape=[jax.ShapeDtypeStruct((n_rows,) + tile, xt.dtype),
                   jax.ShapeDtypeStruct((n_rows,), jnp.int32)],
        compiler_params=_params("arbitrary"),
        name="moe_dispatch",
    )(pos, last_tile_rows, xt)


def _moe_tile_kernel(elo_ref, ehi_ref, valid_ref, inv, x_ref, wgl_ref, wul_ref, wdl_ref, wgh_ref, wuh_ref,
                     wdh_ref, m_hbm, o_even, o_odd, ssem_even, ssem_odd, wgl_b, wul_b, wdl_b, wgh_b, wuh_b,
                     wdh_b, *, n_tokens):
    d = wgl_ref.shape[0]
    k = pl.program_id(0)
    last = pl.num_programs(0) - 1
    prev = jnp.maximum(k - 1, 0)
    valid = valid_ref[k] != 0

    def scatter(tile, r, buf, sem, to_spare):
        tok = inv[tile * MOE_TILE + r]
        dst = jnp.where(to_spare | (tok >= n_tokens), n_tokens + r, tok)
        return _row_copy(buf, r, m_hbm, dst, sem)

    def scatter_done(buf, sem):
        pltpu.make_async_copy(buf, m_hbm.at[pl.ds(0, MOE_TILE)], sem).wait()

    @pl.when(k == 0)
    def _():
        o_odd[...] = jnp.zeros_like(o_odd)

    def refresh(e_ref, srcs, dsts):
        @pl.when(valid & ((k == 0) | (e_ref[k] != e_ref[prev])))
        def _():
            for src, dst in zip(srcs, dsts):
                dst[...] = src[...].astype(BF16)

    refresh(elo_ref, (wgl_ref, wul_ref, wdl_ref), (wgl_b, wul_b, wdl_b))
    refresh(ehi_ref, (wgh_ref, wuh_ref, wdh_ref), (wgh_b, wuh_b, wdh_b))

    def tile_valid(j):
        return (j >= 0) & (valid_ref[jnp.maximum(j, 0)] != 0)

    def scattered_in(j):
        return (j == 0) | tile_valid(j) | tile_valid(j - 1)

    def step(o_cur, ssem_cur, o_prv, ssem_prv):
        @pl.when((k > 0) & scattered_in(k - 1))
        def _():
            scatter_done(o_cur, ssem_cur)

        first = k == 0

        @pl.when(valid)
        def _():
            for r in range(MOE_TILE):
                scatter(prev, r, o_prv, ssem_prv, first).start(priority=r % DMA_PRIORITIES)
            x = x_ref[:, 0:d].astype(BF16)
            acc = None
            for wg_b, wu_b, wd_b, lane in ((wgl_b, wul_b, wdl_b, d + 1), (wgh_b, wuh_b, wdh_b, d + 2)):
                weight = x_ref[:, lane:lane + 1]
                hg = jnp.dot(x, wg_b[...], preferred_element_type=F32)
                hu = jnp.dot(x, wu_b[...], preferred_element_type=F32)
                hidden = (hg * _sigmoid(hg)) * hu * weight
                y = jnp.dot(hidden.astype(BF16), wd_b[...], preferred_element_type=F32)
                acc = y if acc is None else acc + y
            o_cur[...] = acc

        @pl.when(jnp.logical_not(valid) & tile_valid(k - 1))
        def _():
            def issue(r, carry):
                scatter(prev, r, o_prv, ssem_prv, first).start()
                return carry

            lax.fori_loop(0, MOE_TILE, issue, 0, unroll=8)

        @pl.when((k == last) & scattered_in(k))
        def _():
            scatter_done(o_prv, ssem_prv)

        @pl.when((k == last) & valid)
        def _():
            def issue(r, carry):
                scatter(k, r, o_cur, ssem_cur, False).start()
                return carry

            lax.fori_loop(0, MOE_TILE, issue, 0, unroll=8)
            scatter_done(o_cur, ssem_cur)

    @pl.when(k % 2 == 0)
    def _():
        step(o_even, ssem_even, o_odd, ssem_odd)

    @pl.when(k % 2 == 1)
    def _():
        step(o_odd, ssem_odd, o_even, ssem_even)


def _moe_tiles(e_lo, e_hi, valid, inv, xs, wg, wu, wd, layer, n_tokens):
    n_rows, dw = xs.shape
    _, _, d, f = wg.shape
    up_lo = pl.BlockSpec((None, None, d, f), lambda k, lo, hi, ok, iv: (layer, lo[k], 0, 0))
    up_hi = pl.BlockSpec((None, None, d, f), lambda k, lo, hi, ok, iv: (layer, hi[k], 0, 0))
    down_lo = pl.BlockSpec((None, None, f, d), lambda k, lo, hi, ok, iv: (layer, lo[k], 0, 0))
    down_hi = pl.BlockSpec((None, None, f, d), lambda k, lo, hi, ok, iv: (layer, hi[k], 0, 0))
    up_b = pltpu.VMEM((d, f), BF16)
    down_b = pltpu.VMEM((f, d), BF16)
    o_buf = pltpu.VMEM((MOE_TILE, d), F32)
    dma_sem = pltpu.SemaphoreType.DMA(())
    return pl.pallas_call(
        functools.partial(_moe_tile_kernel, n_tokens=n_tokens),
        grid_spec=pltpu.PrefetchScalarGridSpec(
            num_scalar_prefetch=4,
            grid=(n_rows // MOE_TILE,),
            in_specs=[pl.BlockSpec((MOE_TILE, dw), lambda k, lo, hi, ok, iv: (k, 0)),
                      up_lo, up_lo, down_lo, up_hi, up_hi, down_hi],
            out_specs=pl.BlockSpec(memory_space=pl.ANY),
            scratch_shapes=[o_buf, o_buf, dma_sem, dma_sem, up_b, up_b, down_b, up_b, up_b, down_b]),
        out_shape=jax.ShapeDtypeStruct((n_tokens + MOE_TILE, d), F32),
        compiler_params=_params("arbitrary"),
        name="moe_tiles",
    )(e_lo, e_hi, valid, inv, xs, wg, wu, wd, wg, wu, wd)


def _ln_ple_kernel(x1_ref, m_ref, p_ref, g_ref, b_ref, wg_ref, bg_ref, wp_ref, o_ref, *, alpha):
    x2 = _layer_norm(alpha * x1_ref[...] + m_ref[...], g_ref[...], b_ref[...])
    gate = _sigmoid(jnp.dot(x2.astype(BF16), wg_ref[...], preferred_element_type=F32) + bg_ref[...])
    emb = jnp.dot(p_ref[...].astype(BF16), wp_ref[...], preferred_element_type=F32)
    o_ref[...] = x2 + gate * emb


def _ln_ple(x1e, m, p, layer, ln_g, ln_b, wg, bg, wp, alpha, tm):
    t = x1e.shape[0]
    d = m.shape[1]
    pd = p.shape[2]
    tok = pl.BlockSpec((tm, d), lambda i: (i, 0))
    row = pl.BlockSpec((1, d), lambda i: (0, 0))
    return pl.pallas_call(
        functools.partial(_ln_ple_kernel, alpha=alpha),
        grid=(t // tm,),
        in_specs=[tok, tok, pl.BlockSpec((None, tm, pd), lambda i: (layer, i, 0)), row, row,
                  pl.BlockSpec((d, d), lambda i: (0, 0)), row, pl.BlockSpec((pd, d), lambda i: (0, 0))],
        out_specs=tok,
        out_shape=jax.ShapeDtypeStruct((t, d), F32),
        compiler_params=_params("parallel"),
        name="ln_ple",
    )(x1e, m, p, ln_g.reshape(1, d), ln_b.reshape(1, d), wg, bg.reshape(1, d), wp)


def _pick_tile(n, target):
    t = min(n, target)
    while n % t:
        t //= 2
    return t


def kernel(x, p, positions, w_in_ab, w_out_ab, conv_w, conv_b, lru_w_r, lru_b_r, lru_w_i, lru_b_i, lru_lambda, w_qkv_c, w_out_c, sinks_c, ln_mix_g, ln_mix_b, ln_ffn_g, ln_ffn_b, w_router, b_router, exp_w_gate, exp_w_up, exp_w_down, ple_w_proj, ple_w_gate, ple_b_gate):
    b, s, d = x.shape
    depth = p.shape[0]
    t = b * s
    alpha = (2 * depth) ** 0.25
    tm = _pick_tile(s, 1024)
    assert t % MOE_TILE == 0 and t % PLAN_COLS == 0
    n_tiles = _num_moe_tiles(t)
    assert n_tiles <= LANES
    for i in range(depth):
        j = i // 2
        if i % 2 == 0:
            q, k, v, xr, gr = _proj_ab(x, w_in_ab[j].astype(BF16), tm)
            y_sb = _sb_attention(q, k, v, _pick_tile(s, 256), SB_HEADS)
            y_lru = _lru(xr, gr, conv_w[j], conv_b[j], _block_diag(lru_w_r[j]).astype(BF16), lru_b_r[j],
                         _block_diag(lru_w_i[j]).astype(BF16), lru_b_i[j], lru_lambda[j],
                         _pick_tile(s, 256))
            w_out = w_out_ab[j].astype(BF16)
            ys = [(y_sb, True), (y_lru, False)]
            ws = [w_out[:SB_WIDTH], w_out[SB_WIDTH:]]
        else:
            q, k, v = _proj_rope(x, positions, w_qkv_c[j].astype(BF16), tm)
            y = _swa(q, k, v, sinks_c[j])
            ys = [(y, False)]
            ws = [w_out_c[j].astype(BF16)]
        x1e, route = _mix_out(x, ys, ws, ln_mix_g[i], ln_mix_b[i], w_router, b_router, alpha, tm)
        x1e = x1e.reshape(t, d + LANES)
        pos, meta = _plan(route[:, 0, :].reshape(t // PLAN_COLS, PLAN_COLS))
        pos = pos.reshape(t)
        xs, inv = _dispatch(pos, meta[3, :N_BUCKETS + 1], x1e, n_tiles * MOE_TILE, _pick_tile(t, 512))
        m = _moe_tiles(meta[0, :n_tiles], meta[1, :n_tiles], meta[2, :n_tiles], inv, xs,
                       exp_w_gate, exp_w_up, exp_w_down, i, t)
        x = _ln_ple(x1e, m, p.reshape(depth, t, -1), i, ln_ffn_g[i], ln_ffn_b[i],
                    ple_w_gate[i].astype(BF16), ple_b_gate[i], ple_w_proj[i].astype(BF16), alpha,
                    _pick_tile(t, 1024)).reshape(b, s, d)
    return x
```

```python
import functools
import math

import jax
import jax.numpy as jnp
from jax import lax
from jax.experimental import pallas as pl
from jax.experimental.pallas import tpu as pltpu

HEAD_DIM = 64
SB_HEADS = 8
SB_WIDTH = SB_HEADS * HEAD_DIM
LRU_WIDTH = 512
LRU_BLOCKS = 8
LRU_C = 8.0
CONV_WIDTH = 4
SWA_HEADS = 16
SWA_KV_HEADS = 4
SWA_GROUP = SWA_HEADS // SWA_KV_HEADS
SWA_WINDOW = 128
ROPE_THETA = 10000.0
N_EXPERTS = 16
N_GROUPS = 4
GROUP_SIZE = N_EXPERTS // N_GROUPS
LN_EPS = 1e-5
Q_SCALE = HEAD_DIM ** -0.5

LANES = 128
SUBLANES = 8
VMEM_LIMIT_BYTES = 48 * 1024 * 1024

NEG_BIG = -1e30

BF16 = jnp.bfloat16
F32 = jnp.float32


def _params(*semantics):
    return pltpu.CompilerParams(dimension_semantics=semantics, vmem_limit_bytes=VMEM_LIMIT_BYTES)


def _softplus(z):
    return jnp.maximum(z, 0.0) + jnp.log(1.0 + jnp.exp(-jnp.abs(z)))


def _sigmoid(z):
    return 1.0 / (1.0 + jnp.exp(-z))


def _layer_norm(y, g, b):
    mu = jnp.mean(y, axis=-1, keepdims=True)
    d = y - mu
    var = jnp.mean(d * d, axis=-1, keepdims=True)
    return d * lax.rsqrt(var + LN_EPS) * g + b


def _proj_ab_kernel(x_ref, w_ref, q_ref, k_ref, v_ref, xr_ref, gr_ref):
    xb = x_ref[...].astype(BF16)

    def chunk(c):
        return jnp.dot(xb, w_ref[:, c * SB_WIDTH:(c + 1) * SB_WIDTH], preferred_element_type=F32)

    for c, (ref, scale) in enumerate(((q_ref, Q_SCALE), (k_ref, None), (v_ref, None))):
        r = chunk(c)
        if scale is not None:
            r = r * scale
        for h in range(SB_HEADS):
            ref[h] = r[:, h * HEAD_DIM:(h + 1) * HEAD_DIM].astype(BF16)
    xr_ref[...] = chunk(3)
    gr_ref[...] = chunk(4)


def _proj_ab(x, w_bf16, tm):
    b, s, d = x.shape
    n = w_bf16.shape[1]
    heads = jax.ShapeDtypeStruct((b, SB_HEADS, s, HEAD_DIM), BF16)
    flat = jax.ShapeDtypeStruct((b, s, LRU_WIDTH), F32)
    head_spec = pl.BlockSpec((None, SB_HEADS, tm, HEAD_DIM), lambda bi, i: (bi, 0, i, 0))
    flat_spec = pl.BlockSpec((None, tm, LRU_WIDTH), lambda bi, i: (bi, i, 0))
    return pl.pallas_call(
        _proj_ab_kernel,
        grid=(b, s // tm),
        in_specs=[pl.BlockSpec((None, tm, d), lambda bi, i: (bi, i, 0)),
                  pl.BlockSpec((d, n), lambda bi, i: (0, 0))],
        out_specs=[head_spec, head_spec, head_spec, flat_spec, flat_spec],
        out_shape=[heads, heads, heads, flat, flat],
        compiler_params=_params("parallel", "parallel"),
        name="proj_ab",
    )(x, w_bf16)


SB_DEAD_LOG_WEIGHT = -105.0
SB_MERGED_BLOCKS = 2


def _sb_attn_kernel(q_ref, k_ref, v_ref, o_ref, *, tq, hp):
    i = pl.program_id(2)
    row = lax.broadcasted_iota(jnp.int32, (tq, tq), 0)
    col = lax.broadcasted_iota(jnp.int32, (tq, tq), 1)
    minus_later = jnp.where(row > col, -1.0, 0.0).astype(BF16)
    causal = col < row

    def block(jb, carries, accs, masked):
        start = pl.multiple_of(jb * tq, tq)
        new_carries, new_accs = [], []
        for h in range(hp):
            kj = k_ref[h, pl.ds(start, tq), :]
            vj = v_ref[h, pl.ds(start, tq), :]
            z = lax.dot_general(q_ref[h], kj, (((1,), (1,)), ((), ())), preferred_element_type=F32)
            sp = _softplus(z)
            cost = jnp.where(causal, sp, 0.0) if masked else sp
            after = jnp.dot(cost.astype(BF16), minus_later, preferred_element_type=F32)
            w = jnp.exp((z - sp) + after + carries[h])
            if masked:
                w = jnp.where(causal, w, 0.0)
            new_accs.append(accs[h] + jnp.dot(w.astype(BF16), vj, preferred_element_type=F32))
            new_carries.append(carries[h] - jnp.sum(cost, axis=1, keepdims=True))
        return tuple(new_carries), tuple(new_accs)

    def live(carries):
        return functools.reduce(jnp.maximum, [jnp.max(c) for c in carries])

    zero = ((jnp.zeros((tq, 1), F32),) * hp, (jnp.zeros((tq, HEAD_DIM), F32),) * hp)

    def first(n):
        def run():
            state = block(i, *zero, True)
            for j in range(1, n):
                state = block(i - j, *state, False)
            return state
        return run

    done = jnp.minimum(i, SB_MERGED_BLOCKS - 1)
    carries, accs = lax.switch(done, [first(n) for n in range(1, SB_MERGED_BLOCKS + 1)])

    def cond(state):
        return (state[0] < i) & (state[1] > SB_DEAD_LOG_WEIGHT)

    def body(state):
        step, _, carries, accs = state
        carries, accs = block(i - 1 - step, carries, accs, False)
        return step + 1, live(carries), carries, accs

    _, _, _, accs = lax.while_loop(cond, body, (done, live(carries), carries, accs))
    for h in range(hp):
        o_ref[h] = accs[h].astype(o_ref.dtype)


def _sb_attention(q, k, v, tq, hp):
    b, h, s, dh = q.shape
    return pl.pallas_call(
        functools.partial(_sb_attn_kernel, tq=tq, hp=hp),
        grid=(b, h // hp, s // tq),
        in_specs=[pl.BlockSpec((None, hp, tq, dh), lambda bi, hi, i: (bi, hi, i, 0)),
                  pl.BlockSpec((None, hp, s, dh), lambda bi, hi, i: (bi, hi, 0, 0), pipeline_mode=pl.Buffered(1)),
                  pl.BlockSpec((None, hp, s, dh), lambda bi, hi, i: (bi, hi, 0, 0), pipeline_mode=pl.Buffered(1))],
        out_specs=pl.BlockSpec((None, hp, tq, dh), lambda bi, hi, i: (bi, hi, i, 0)),
        out_shape=jax.ShapeDtypeStruct((b, h, s, dh), BF16),
        compiler_params=_params("parallel", "parallel", "parallel"),
        name="sb_attention",
    )(q, k, v)


def _gelu_tanh(x):
    return 0.5 * x * (1.0 + jnp.tanh(math.sqrt(2.0 / math.pi) * (x + 0.044715 * (x * x * x))))


def _lru_kernel(xr_ref, gr_ref, cw_ref, cb_ref, wr_ref, br_ref, wi_ref, bi_ref, lam_ref, y_ref,
                xbuf, hprev, *, ts):
    @pl.when(pl.program_id(1) == 0)
    def _():
        xbuf[0:SUBLANES, :] = jnp.zeros((SUBLANES, LRU_WIDTH), F32)
        hprev[...] = jnp.zeros_like(hprev)

    xbuf[SUBLANES:SUBLANES + ts, :] = xr_ref[...]
    xc = cb_ref[...] + cw_ref[CONV_WIDTH - 1:CONV_WIDTH, :] * xbuf[SUBLANES:SUBLANES + ts, :]
    for kk in range(CONV_WIDTH - 1):
        off = SUBLANES - (CONV_WIDTH - 1) + kk
        xc = xc + cw_ref[kk:kk + 1, :] * xbuf[off:off + ts, :]
    xbuf[0:SUBLANES, :] = xbuf[ts:ts + SUBLANES, :]

    xcb = xc.astype(BF16)
    r = _sigmoid(jnp.dot(xcb, wr_ref[...], preferred_element_type=F32) + br_ref[...])
    gi = _sigmoid(jnp.dot(xcb, wi_ref[...], preferred_element_type=F32) + bi_ref[...])
    log_a = (-LRU_C) * r * _softplus(-lam_ref[...])
    a = jnp.exp(log_a)
    u = jnp.sqrt(1.0 - a * a) * (gi * xc)

    row = lax.broadcasted_iota(jnp.int32, (ts, LRU_WIDTH), 0)
    d = 1
    while d < ts:
        if d < SUBLANES:
            keep = row >= d
            a_sh = jnp.where(keep, pltpu.roll(a, d, axis=0), 1.0)
            u_sh = jnp.where(keep, pltpu.roll(u, d, axis=0), 0.0)
            u = a * u_sh + u
            a = a * a_sh
        else:
            u = jnp.concatenate([u[:d], a[d:] * u[:ts - d] + u[d:]], axis=0)
            a = jnp.concatenate([a[:d], a[d:] * a[:ts - d]], axis=0)
        d *= 2
    h = a * hprev[0:1, :] + u
    hprev[...] = jnp.broadcast_to(h[ts - 1:ts, :], hprev.shape)
    y_ref[...] = (_gelu_tanh(gr_ref[...]) * h).astype(y_ref.dtype)


def _lru(xr, gr, conv_w, conv_b, wr_bd, b_r, wi_bd, b_i, lam, ts):
    b, s, w = xr.shape
    seq_spec = pl.BlockSpec((None, ts, w), lambda bi, i: (bi, i, 0))

    def full(shape):
        return pl.BlockSpec(shape, lambda bi, i: (0,) * len(shape))

    return pl.pallas_call(
        functools.partial(_lru_kernel, ts=ts),
        grid=(b, s // ts),
        in_specs=[seq_spec, seq_spec, full((CONV_WIDTH, w)), full((1, w)), full((w, w)), full((1, w)),
                  full((w, w)), full((1, w)), full((1, w))],
        out_specs=seq_spec,
        out_shape=jax.ShapeDtypeStruct((b, s, w), BF16),
        scratch_shapes=[pltpu.VMEM((ts + 2 * SUBLANES, w), F32), pltpu.VMEM((SUBLANES, w), F32)],
        compiler_params=_params("parallel", "arbitrary"),
        name="rg_lru",
    )(xr, gr, conv_w, conv_b.reshape(1, w), wr_bd, b_r.reshape(1, w), wi_bd, b_i.reshape(1, w),
      lam.reshape(1, w))


def _block_diag(w):
    n, c, d = w.shape
    eye = jnp.eye(n, dtype=w.dtype)
    return (eye[:, None, :, None] * w[:, :, None, :]).reshape(n * c, n * d)


def _proj_rope_kernel(x_ref, pos_ref, freq_ref, w_ref, q_ref, k_ref, vt_ref):
    tm = x_ref.shape[0]
    xb = x_ref[...].astype(BF16)
    ang_t = freq_ref[...] * pos_ref[...].astype(F32)
    reps = LANES // (HEAD_DIM // 2)
    cos = jnp.concatenate([jnp.cos(ang_t)] * reps, axis=0).T
    sin = jnp.concatenate([jnp.sin(ang_t)] * reps, axis=0).T
    lane = lax.broadcasted_iota(jnp.int32, (tm, LANES), 1)
    first_half = (lane % HEAD_DIM) < (HEAD_DIM // 2)
    heads_per_slab = LANES // HEAD_DIM

    def rope(r):
        upper = pltpu.roll(r, LANES - HEAD_DIM // 2, axis=1)
        lower = pltpu.roll(r, HEAD_DIM // 2, axis=1)
        return r * cos + jnp.where(first_half, -upper, lower) * sin

    def emit(ref, n_heads, col0, rotary, scale):
        for slab in range(n_heads // heads_per_slab):
            c0 = col0 + slab * LANES
            r = jnp.dot(xb, w_ref[:, c0:c0 + LANES], preferred_element_type=F32)
            if rotary:
                r = rope(r)
            if scale is not None:
                r = r * scale
            for j in range(heads_per_slab):
                ref[slab * heads_per_slab + j] = r[:, j * HEAD_DIM:(j + 1) * HEAD_DIM].astype(BF16)

    emit(q_ref, SWA_HEADS, 0, True, Q_SCALE)
    emit(k_ref, SWA_KV_HEADS, SWA_HEADS * HEAD_DIM, True, None)
    v0 = (SWA_HEADS + SWA_KV_HEADS) * HEAD_DIM
    for slab in range(SWA_KV_HEADS // heads_per_slab):
        r = jnp.dot(xb, w_ref[:, v0 + slab * LANES:v0 + (slab + 1) * LANES], preferred_element_type=F32)
        rt = r.T
        for j in range(heads_per_slab):
            vt_ref[slab * heads_per_slab + j] = rt[j * HEAD_DIM:(j + 1) * HEAD_DIM, :].astype(BF16)


def _proj_rope(x, positions, w_bf16, tm):
    b, s, d = x.shape
    n = w_bf16.shape[1]
    half = HEAD_DIM // 2
    inv_freq = (ROPE_THETA ** (-jnp.arange(half, dtype=F32) / half)).reshape(half, 1)

    def heads(nh):
        return (jax.ShapeDtypeStruct((b, nh, s, HEAD_DIM), BF16),
                pl.BlockSpec((None, nh, tm, HEAD_DIM), lambda bi, i: (bi, 0, i, 0)))

    (qs, qspec), (ks, kspec) = heads(SWA_HEADS), heads(SWA_KV_HEADS)
    vs = jax.ShapeDtypeStruct((b, SWA_KV_HEADS, HEAD_DIM, s), BF16)
    vspec = pl.BlockSpec((None, SWA_KV_HEADS, HEAD_DIM, tm), lambda bi, i: (bi, 0, 0, i))
    return pl.pallas_call(
        _proj_rope_kernel,
        grid=(b, s // tm),
        in_specs=[pl.BlockSpec((None, tm, d), lambda bi, i: (bi, i, 0)),
                  pl.BlockSpec((None, 1, tm), lambda bi, i: (bi, 0, i)),
                  pl.BlockSpec((half, 1), lambda bi, i: (0, 0)),
                  pl.BlockSpec((d, n), lambda bi, i: (0, 0))],
        out_specs=[qspec, kspec, vspec],
        out_shape=[qs, ks, vs],
        compiler_params=_params("parallel", "parallel"),
        name="proj_rope",
    )(x, positions.reshape(b, 1, s), inv_freq, w_bf16)


def _reduce_rows(x, op):
    while x.shape[0] > SUBLANES:
        half = x.shape[0] // 2
        x = op(x[:half], x[half:])
    for shift in (4, 2, 1):
        x = op(x, pltpu.roll(x, shift, axis=0))
    return x[0:1]


SWA_BLOCKS_PER_STEP = 8


def _swa_kernel(q_ref, kp_ref, kc_ref, vtp_ref, vtc_ref, sink_ref, o_ref):
    i = pl.program_id(1)
    w = SWA_WINDOW
    key = lax.broadcasted_iota(jnp.int32, (2 * w, w), 0)
    qry = lax.broadcasted_iota(jnp.int32, (2 * w, w), 1)
    dist = qry + w - key
    band = (dist >= 0) & (dist < w)
    for blk in range(SWA_BLOCKS_PER_STEP):
        visible = band if blk else band & ((key >= w) | (i > 0))
        bias = jnp.concatenate([jnp.where(visible, 0.0, NEG_BIG)] * SWA_GROUP, axis=1)
        outs = []
        for kv in range(SWA_KV_HEADS):
            if blk:
                kk = kc_ref[kv, (blk - 1) * w:(blk + 1) * w, :]
                vvt = vtc_ref[kv, :, (blk - 1) * w:(blk + 1) * w]
            else:
                kk = jnp.concatenate([kp_ref[kv], kc_ref[kv, 0:w, :]], axis=0)
                vvt = jnp.concatenate([vtp_ref[kv], vtc_ref[kv, :, 0:w]], axis=1)
            qg = jnp.concatenate([q_ref[kv * SWA_GROUP + g, blk * w:(blk + 1) * w, :]
                                  for g in range(SWA_GROUP)], axis=0)
            st = lax.dot_general(kk, qg, (((1,), (1,)), ((), ())), preferred_element_type=F32) + bias
            sink = sink_ref[kv:kv + 1, :]
            m = jnp.maximum(_reduce_rows(st, jnp.maximum), sink)
            p = jnp.exp(st - m)
            denom = _reduce_rows(p, jnp.add) + jnp.exp(sink - m)
            ot = jnp.dot(vvt, p.astype(BF16), preferred_element_type=F32) / denom
            outs.extend(ot[:, g * w:(g + 1) * w] for g in range(SWA_GROUP))
        o_ref[blk * w:(blk + 1) * w, :] = jnp.concatenate(outs, axis=0).T.astype(o_ref.dtype)


def _swa(q, k, vt, sinks):
    b, nh, s, dh = q.shape
    nkv = k.shape[1]
    w = SWA_WINDOW
    n = SWA_BLOCKS_PER_STEP
    assert s % (n * w) == 0
    cur = pl.BlockSpec((None, nkv, n * w, dh), lambda bi, i: (bi, 0, i, 0))
    prev = pl.BlockSpec((None, nkv, w, dh), lambda bi, i: (bi, 0, jnp.maximum(n * i - 1, 0), 0))
    cur_t = pl.BlockSpec((None, nkv, dh, n * w), lambda bi, i: (bi, 0, 0, i))
    prev_t = pl.BlockSpec((None, nkv, dh, w), lambda bi, i: (bi, 0, 0, jnp.maximum(n * i - 1, 0)))
    sink_tile = jnp.repeat(sinks.astype(F32).reshape(nkv, nh // nkv), w, axis=1)
    return pl.pallas_call(
        _swa_kernel,
        grid=(b, s // (n * w)),
        in_specs=[pl.BlockSpec((None, nh, n * w, dh), lambda bi, i: (bi, 0, i, 0)),
                  prev, cur, prev_t, cur_t,
                  pl.BlockSpec(sink_tile.shape, lambda bi, i: (0, 0))],
        out_specs=pl.BlockSpec((None, n * w, nh * dh), lambda bi, i: (bi, i, 0)),
        out_shape=jax.ShapeDtypeStruct((b, s, nh * dh), BF16),
        compiler_params=_params("parallel", "parallel"),
        name="swa",
    )(q, k, k, vt, vt, sink_tile)


PAIRS_PER_GROUP = GROUP_SIZE * (GROUP_SIZE - 1) // 2
N_BUCKETS = N_GROUPS * PAIRS_PER_GROUP
ROUTE_ROWS = SUBLANES


def _route(logits_t):
    rows = [logits_t[e:e + 1, :] for e in range(N_EXPERTS)]
    mx = functools.reduce(jnp.maximum, rows)
    ex = [jnp.exp(r - mx) for r in rows]
    total = functools.reduce(lambda p, q: p + q, ex)
    probs = [e / total for e in ex]

    group_score = []
    for g in range(N_GROUPS):
        a, b, c, d = probs[g * GROUP_SIZE:(g + 1) * GROUP_SIZE]
        hi1, lo1 = jnp.maximum(a, b), jnp.minimum(a, b)
        hi2, lo2 = jnp.maximum(c, d), jnp.minimum(c, d)
        top1 = jnp.maximum(hi1, hi2)
        top2 = jnp.maximum(jnp.minimum(hi1, hi2), jnp.maximum(lo1, lo2))
        group_score.append(top1 + top2)
    best = functools.reduce(jnp.maximum, group_score)
    g_sel = jnp.full(best.shape, N_GROUPS - 1, jnp.int32)
    for g in range(N_GROUPS - 2, -1, -1):
        g_sel = jnp.where(group_score[g] == best, g, g_sel)

    in_group = []
    for j in range(GROUP_SIZE):
        val = probs[(N_GROUPS - 1) * GROUP_SIZE + j]
        for g in range(N_GROUPS - 2, -1, -1):
            val = jnp.where(g_sel == g, probs[g * GROUP_SIZE + j], val)
        in_group.append(val)

    def first_argmax(vals):
        m = functools.reduce(jnp.maximum, vals)
        idx = jnp.full(m.shape, GROUP_SIZE - 1, jnp.int32)
        for j in range(GROUP_SIZE - 2, -1, -1):
            idx = jnp.where(vals[j] == m, j, idx)
        return m, idx

    w1, i1 = first_argmax(in_group)
    rest = [jnp.where(i1 == j, -1.0, in_group[j]) for j in range(GROUP_SIZE)]
    w2, i2 = first_argmax(rest)
    norm = w1 + w2
    first_is_lo = i1 < i2
    i_lo = jnp.minimum(i1, i2)
    i_hi = jnp.maximum(i1, i2)
    pair = jnp.where(i_lo == 0, i_hi - 1, jnp.where(i_lo == 2, 3, jnp.where(i_hi == 3, 4, 5)))
    bucket = (g_sel * PAIRS_PER_GROUP + pair).astype(F32)
    w_lo = jnp.where(first_is_lo, w1, w2) / norm
    w_hi = jnp.where(first_is_lo, w2, w1) / norm
    return bucket, w_lo, w_hi


def _mix_out_kernel(*refs, n_head_major, alpha):
    x_ref = refs[0]
    y_refs = refs[1:1 + len(n_head_major)]
    w_refs = refs[1 + len(n_head_major):1 + 2 * len(n_head_major)]
    g_ref, b_ref, wrh_ref, wrl_ref, brt_ref, x1e_ref, route_ref = refs[1 + 2 * len(n_head_major):]
    tm, d = x_ref.shape
    h = alpha * x_ref[...]
    for y_ref, w_ref, nh in zip(y_refs, w_refs, n_head_major):
        if nh:
            y = jnp.concatenate([y_ref[j] for j in range(nh)], axis=-1)
        else:
            y = y_ref[...]
        h = h + jnp.dot(y, w_ref[...], preferred_element_type=F32)
    x1 = _layer_norm(h, g_ref[...], b_ref[...])
    x_hi = x1.astype(BF16)
    x_lo = (x1 - x_hi.astype(F32)).astype(BF16)

    def nt_dot(w, xv):
        return lax.dot_general(w, xv, (((1,), (1,)), ((), ())), preferred_element_type=F32)

    logits_t = (nt_dot(wrh_ref[...], x_hi) + nt_dot(wrh_ref[...], x_lo) + nt_dot(wrl_ref[...], x_hi)
                + brt_ref[...])
    route = jnp.concatenate(list(_route(logits_t)) + [jnp.zeros((ROUTE_ROWS - 3, tm), F32)], axis=0)
    route_ref[...] = route
    x1e_ref[:, 0:d] = x1
    x1e_ref[:, d:d + LANES] = jnp.concatenate([route, jnp.zeros((LANES - ROUTE_ROWS, tm), F32)], axis=0).T


def _mix_out(x, ys, ws, ln_g, ln_b, w_router, b_router, alpha, tm):
    b, s, d = x.shape
    n_head_major = tuple(y.shape[1] if hm else 0 for y, hm in ys)
    y_specs = []
    for (y, hm) in ys:
        if hm:
            y_specs.append(pl.BlockSpec((None, y.shape[1], tm, y.shape[3]), lambda bi, i: (bi, 0, i, 0)))
        else:
            y_specs.append(pl.BlockSpec((None, tm, y.shape[2]), lambda bi, i: (bi, i, 0)))
    w_specs = [pl.BlockSpec(w.shape, lambda bi, i: (0, 0)) for w in ws]
    row = pl.BlockSpec((1, d), lambda bi, i: (0, 0))
    tok = pl.BlockSpec((None, tm, d), lambda bi, i: (bi, i, 0))
    wr_t = w_router.T.astype(F32)
    wr_hi = wr_t.astype(BF16)
    wr_lo = (wr_t - wr_hi.astype(F32)).astype(BF16)
    wr_spec = pl.BlockSpec((N_EXPERTS, d), lambda bi, i: (0, 0))
    return pl.pallas_call(
        functools.partial(_mix_out_kernel, n_head_major=n_head_major, alpha=alpha),
        grid=(b, s // tm),
        in_specs=[tok] + y_specs + w_specs + [row, row, wr_spec, wr_spec,
                  pl.BlockSpec((N_EXPERTS, 1), lambda bi, i: (0, 0))],
        out_specs=[pl.BlockSpec((None, tm, d + LANES), lambda bi, i: (bi, i, 0)),
                   pl.BlockSpec((None, ROUTE_ROWS, tm), lambda bi, i: (bi, 0, i))],
        out_shape=[jax.ShapeDtypeStruct((b, s, d + LANES), F32),
                   jax.ShapeDtypeStruct((b, ROUTE_ROWS, s), F32)],
        compiler_params=_params("parallel", "parallel"),
        name="mix_out_ln_router",
    )(x, *[y for y, _ in ys], *ws, ln_g.reshape(1, d), ln_b.reshape(1, d),
      wr_hi, wr_lo, b_router.astype(F32).reshape(N_EXPERTS, 1))


MOE_TILE = 256
PLAN_COLS = 256
META_ROWS = SUBLANES


def _num_moe_tiles(t):
    return t // MOE_TILE + N_BUCKETS


def _plan_kernel(bid_ref, pos_ref, meta_ref):
    r, c = bid_ref.shape
    bid = bid_ref[...]
    before = (lax.broadcasted_iota(jnp.int32, (c, c), 0)
              < lax.broadcasted_iota(jnp.int32, (c, c), 1)).astype(BF16)
    rows_before = (lax.broadcasted_iota(jnp.int32, (r, r), 1)
                   < lax.broadcasted_iota(jnp.int32, (r, r), 0)).astype(BF16)
    lane = lax.broadcasted_iota(jnp.int32, (1, LANES), 1)
    tile_start = lane.astype(F32) * MOE_TILE

    def body(b, state):
        base, pos, tile_bucket, last_tile = state
        ind = (bid == lax.convert_element_type(b, F32)).astype(F32)
        within = jnp.dot(ind.astype(BF16), before, preferred_element_type=F32)
        row_total = jnp.sum(ind, axis=1, keepdims=True)
        row_off = jnp.dot(rows_before, jnp.broadcast_to(row_total, (r, LANES)).astype(BF16),
                          preferred_element_type=F32)[:, 0:1]
        count = jnp.sum(row_total, axis=0, keepdims=True)
        padded = jnp.floor((count + (MOE_TILE - 1)) * (1.0 / MOE_TILE)) * MOE_TILE
        pos = pos + ind * (base + row_off + within)
        end = base + padded
        tile_bucket = tile_bucket + (tile_start >= end).astype(F32)
        last_row = jnp.where(count > 0.0, end - MOE_TILE, -1.0)
        last_tile = jnp.where(lane == b, last_row, last_tile)
        return end, pos, tile_bucket, last_tile

    total, pos, tile_bucket, last_tile = lax.fori_loop(
        0, N_BUCKETS, body, (jnp.zeros((1, 1), F32), jnp.zeros((r, c), F32), jnp.zeros((1, LANES), F32),
                             jnp.full((1, LANES), -1.0, F32)))
    pos_ref[...] = pos.astype(jnp.int32)

    tb = jnp.minimum(tile_bucket, N_BUCKETS - 1.0)
    group = sum((tb >= g * PAIRS_PER_GROUP).astype(F32) for g in range(1, N_GROUPS))
    pair = tb - group * PAIRS_PER_GROUP
    i_lo = jnp.where(pair < 3, 0.0, jnp.where(pair == 3, 2.0, 1.0))
    i_hi = jnp.where(pair < 3, pair + 1.0, jnp.where(pair == 5, 2.0, 3.0))
    meta = jnp.concatenate([group * GROUP_SIZE + i_lo, group * GROUP_SIZE + i_hi,
                            (tile_start < total).astype(F32),
                            jnp.where(lane == N_BUCKETS, total, last_tile),
                            jnp.zeros((META_ROWS - 4, LANES), F32)], axis=0)
    meta_ref[...] = meta.astype(jnp.int32)


def _plan(bucket_ids):
    r, c = bucket_ids.shape
    return pl.pallas_call(
        _plan_kernel,
        out_shape=[jax.ShapeDtypeStruct((r, c), jnp.int32), jax.ShapeDtypeStruct((META_ROWS, LANES), jnp.int32)],
        compiler_params=pltpu.CompilerParams(vmem_limit_bytes=VMEM_LIMIT_BYTES),
        name="moe_plan",
    )(bucket_ids)


DMA_PRIORITIES = 2


def _row_copy(src_ref, src_row, dst_ref, dst_row, sem):
    return pltpu.make_async_copy(src_ref.at[pl.ds(src_row, 1)], dst_ref.at[pl.ds(dst_row, 1)], sem)


def _dispatch_kernel(pos_ref, last_ref, x_ref, o_ref, inv_ref, zeros, inv, fill, sem, zsem, isem, *,
                     first_spare, n_tokens):
    tm = x_ref.shape[0]
    i = pl.program_id(0)

    @pl.when(i == 0)
    def _():
        zeros[...] = jnp.zeros_like(zeros)
        fill[...] = jnp.full(fill.shape, n_tokens, jnp.int32)
        to_smem = pltpu.make_async_copy(fill, inv, isem)
        to_smem.start()
        rows_in_use = last_ref[N_BUCKETS]
        clears = [(last_ref[b] >= 0, last_ref[b]) for b in range(N_BUCKETS)]
        clears += [(k * MOE_TILE >= rows_in_use, k * MOE_TILE)
                   for k in range(first_spare, o_ref.shape[0] // MOE_TILE)]

        def clear(row):
            start = row if isinstance(row, int) else pl.multiple_of(row, MOE_TILE)
            return pltpu.make_async_copy(zeros, o_ref.at[pl.ds(start, MOE_TILE)], zsem)

        for needed, row in clears:
            @pl.when(needed)
            def _():
                clear(row).start()
        for needed, row in clears:
            @pl.when(needed)
            def _():
                clear(row).wait()
        to_smem.wait()

    base = i * tm
    for r in range(tm):
        dst = pos_ref[base + r]
        inv[dst] = base + r
        _row_copy(x_ref, r, o_ref, dst, sem).start(priority=r % DMA_PRIORITIES)
    pltpu.make_async_copy(x_ref, o_ref.at[pl.ds(0, tm)], sem).wait()

    @pl.when(i == pl.num_programs(0) - 1)
    def _():
        to_hbm = pltpu.make_async_copy(inv, inv_ref, isem)
        to_hbm.start()
        to_hbm.wait()


def _dispatch(pos, last_tile_rows, xt, n_rows, tm):
    t = xt.shape[0]
    tile = xt.shape[1:]
    dma_sem = pltpu.SemaphoreType.DMA(())
    return pl.pallas_call(
        functools.partial(_dispatch_kernel, first_spare=t // MOE_TILE, n_tokens=t),
        grid_spec=pltpu.PrefetchScalarGridSpec(
            num_scalar_prefetch=2,
            grid=(t // tm,),
            in_specs=[pl.BlockSpec((tm,) + tile, lambda i, pos_ref, last_ref: (i,) + (0,) * len(tile))],
            out_specs=[pl.BlockSpec(memory_space=pl.ANY), pl.BlockSpec(memory_space=pl.ANY)],
            scratch_shapes=[pltpu.VMEM((MOE_TILE,) + tile, xt.dtype), pltpu.SMEM((n_rows,), jnp.int32),
                            pltpu.VMEM((n_rows,), jnp.int32), dma_sem, dma_sem, dma_sem]),
        out_shape=[jax.ShapeDtypeStruct((n_rows,) + tile, xt.dtype),
                   jax.ShapeDtypeStruct((n_rows,), jnp.int32)],
        compiler_params=_params("arbitrary"),
        name="moe_dispatch",
    )(pos, last_tile_rows, xt)


def _moe_tile_kernel(elo_ref, ehi_ref, valid_ref, inv, x_ref, wgl_ref, wul_ref, wdl_ref, wgh_ref, wuh_ref,
                     wdh_ref, m_hbm, o_even, o_odd, ssem_even, ssem_odd, wgl_b, wul_b, wdl_b, wgh_b, wuh_b,
                     wdh_b, *, n_tokens):
    d = wgl_ref.shape[0]
    k = pl.program_id(0)
    last = pl.num_programs(0) - 1
    prev = jnp.maximum(k - 1, 0)
    valid = valid_ref[k] != 0

    def scatter(tile, r, buf, sem, to_spare):
        tok = inv[tile * MOE_TILE + r]
        dst = jnp.where(to_spare | (tok >= n_tokens), n_tokens + r, tok)
        return _row_copy(buf, r, m_hbm, dst, sem)

    def scatter_done(buf, sem):
        pltpu.make_async_copy(buf, m_hbm.at[pl.ds(0, MOE_TILE)], sem).wait()

    @pl.when(k == 0)
    def _():
        o_odd[...] = jnp.zeros_like(o_odd)

    def refresh(e_ref, srcs, dsts):
        @pl.when(valid & ((k == 0) | (e_ref[k] != e_ref[prev])))
        def _():
            for src, dst in zip(srcs, dsts):
                dst[...] = src[...].astype(BF16)

    refresh(elo_ref, (wgl_ref, wul_ref, wdl_ref), (wgl_b, wul_b, wdl_b))
    refresh(ehi_ref, (wgh_ref, wuh_ref, wdh_ref), (wgh_b, wuh_b, wdh_b))

    def tile_valid(j):
        return (j >= 0) & (valid_ref[jnp.maximum(j, 0)] != 0)

    def scattered_in(j):
        return (j == 0) | tile_valid(j) | tile_valid(j - 1)

    def step(o_cur, ssem_cur, o_prv, ssem_prv):
        @pl.when((k > 0) & scattered_in(k - 1))
        def _():
            scatter_done(o_cur, ssem_cur)

        first = k == 0

        @pl.when(valid)
        def _():
            for r in range(MOE_TILE):
                scatter(prev, r, o_prv, ssem_prv, first).start(priority=r % DMA_PRIORITIES)
            x = x_ref[:, 0:d].astype(BF16)
            acc = None
            for wg_b, wu_b, wd_b, lane in ((wgl_b, wul_b, wdl_b, d + 1), (wgh_b, wuh_b, wdh_b, d + 2)):
                weight = x_ref[:, lane:lane + 1]
                hg = jnp.dot(x, wg_b[...], preferred_element_type=F32)
                hu = jnp.dot(x, wu_b[...], preferred_element_type=F32)
                hidden = (hg * _sigmoid(hg)) * hu * weight
                y = jnp.dot(hidden.astype(BF16), wd_b[...], preferred_element_type=F32)
                acc = y if acc is None else acc + y
            o_cur[...] = acc

        @pl.when(jnp.logical_not(valid) & tile_valid(k - 1))
        def _():
            def issue(r, carry):
                scatter(prev, r, o_prv, ssem_prv, first).start()
                return carry

            lax.fori_loop(0, MOE_TILE, issue, 0, unroll=8)

        @pl.when((k == last) & scattered_in(k))
        def _():
            scatter_done(o_prv, ssem_prv)

        @pl.when((k == last) & valid)
        def _():
            def issue(r, carry):
                scatter(k, r, o_cur, ssem_cur, False).start()
                return carry

            lax.fori_loop(0, MOE_TILE, issue, 0, unroll=8)
            scatter_done(o_cur, ssem_cur)

    @pl.when(k % 2 == 0)
    def _():
        step(o_even, ssem_even, o_odd, ssem_odd)

    @pl.when(k % 2 == 1)
    def _():
        step(o_odd, ssem_odd, o_even, ssem_even)


def _moe_tiles(e_lo, e_hi, valid, inv, xs, wg, wu, wd, layer, n_tokens):
    n_rows, dw = xs.shape
    _, _, d, f = wg.shape
    up_lo = pl.BlockSpec((None, None, d, f), lambda k, lo, hi, ok, iv: (layer, lo[k], 0, 0))
    up_hi = pl.BlockSpec((None, None, d, f), lambda k, lo, hi, ok, iv: (layer, hi[k], 0, 0))
    down_lo = pl.BlockSpec((None, None, f, d), lambda k, lo, hi, ok, iv: (layer, lo[k], 0, 0))
    down_hi = pl.BlockSpec((None, None, f, d), lambda k, lo, hi, ok, iv: (layer, hi[k], 0, 0))
    up_b = pltpu.VMEM((d, f), BF16)
    down_b = pltpu.VMEM((f, d), BF16)
    o_buf = pltpu.VMEM((MOE_TILE, d), F32)
    dma_sem = pltpu.SemaphoreType.DMA(())
    return pl.pallas_call(
        functools.partial(_moe_tile_kernel, n_tokens=n_tokens),
        grid_spec=pltpu.PrefetchScalarGridSpec(
            num_scalar_prefetch=4,
            grid=(n_rows // MOE_TILE,),
            in_specs=[pl.BlockSpec((MOE_TILE, dw), lambda k, lo, hi, ok, iv: (k, 0)),
                      up_lo, up_lo, down_lo, up_hi, up_hi, down_hi],
            out_specs=pl.BlockSpec(memory_space=pl.ANY),
            scratch_shapes=[o_buf, o_buf, dma_sem, dma_sem, up_b, up_b, down_b, up_b, up_b, down_b]),
        out_shape=jax.ShapeDtypeStruct((n_tokens + MOE_TILE, d), F32),
        compiler_params=_params("arbitrary"),
        name="moe_tiles",
    )(e_lo, e_hi, valid, inv, xs, wg, wu, wd, wg, wu, wd)


def _ln_ple_kernel(x1_ref, m_ref, p_ref, g_ref, b_ref, wg_ref, bg_ref, wp_ref, o_ref, *, alpha):
    x2 = _layer_norm(alpha * x1_ref[...] + m_ref[...], g_ref[...], b_ref[...])
    gate = _sigmoid(jnp.dot(x2.astype(BF16), wg_ref[...], preferred_element_type=F32) + bg_ref[...])
    emb = jnp.dot(p_ref[...].astype(BF16), wp_ref[...], preferred_element_type=F32)
    o_ref[...] = x2 + gate * emb


def _ln_ple(x1e, m, p, layer, ln_g, ln_b, wg, bg, wp, alpha, tm):
    t = x1e.shape[0]
    d = m.shape[1]
    pd = p.shape[2]
    tok = pl.BlockSpec((tm, d), lambda i: (i, 0))
    row = pl.BlockSpec((1, d), lambda i: (0, 0))
    return pl.pallas_call(
        functools.partial(_ln_ple_kernel, alpha=alpha),
        grid=(t // tm,),
        in_specs=[tok, tok, pl.BlockSpec((None, tm, pd), lambda i: (layer, i, 0)), row, row,
                  pl.BlockSpec((d, d), lambda i: (0, 0)), row, pl.BlockSpec((pd, d), lambda i: (0, 0))],
        out_specs=tok,
        out_shape=jax.ShapeDtypeStruct((t, d), F32),
        compiler_params=_params("parallel"),
        name="ln_ple",
    )(x1e, m, p, ln_g.reshape(1, d), ln_b.reshape(1, d), wg, bg.reshape(1, d), wp)


def _pick_tile(n, target):
    t = min(n, target)
    while n % t:
        t //= 2
    return t


def kernel(x, p, positions, w_in_ab, w_out_ab, conv_w, conv_b, lru_w_r, lru_b_r, lru_w_i, lru_b_i, lru_lambda, w_qkv_c, w_out_c, sinks_c, ln_mix_g, ln_mix_b, ln_ffn_g, ln_ffn_b, w_router, b_router, exp_w_gate, exp_w_up, exp_w_down, ple_w_proj, ple_w_gate, ple_b_gate):
    b, s, d = x.shape
    depth = p.shape[0]
    t = b * s
    alpha = (2 * depth) ** 0.25
    tm = _pick_tile(s, 1024)
    assert t % MOE_TILE == 0 and t % PLAN_COLS == 0
    n_tiles = _num_moe_tiles(t)
    assert n_tiles <= LANES
    for i in range(depth):
        j = i // 2
        if i % 2 == 0:
            q, k, v, xr, gr = _proj_ab(x, w_in_ab[j].astype(BF16), tm)
            y_sb = _sb_attention(q, k, v, _pick_tile(s, 256), SB_HEADS)
            y_lru = _lru(xr, gr, conv_w[j], conv_b[j], _block_diag(lru_w_r[j]).astype(BF16), lru_b_r[j],
                         _block_diag(lru_w_i[j]).astype(BF16), lru_b_i[j], lru_lambda[j],
                         _pick_tile(s, 256))
            w_out = w_out_ab[j].astype(BF16)
            ys = [(y_sb, True), (y_lru, False)]
            ws = [w_out[:SB_WIDTH], w_out[SB_WIDTH:]]
        else:
            q, k, v = _proj_rope(x, positions, w_qkv_c[j].astype(BF16), tm)
            y = _swa(q, k, v, sinks_c[j])
            ys = [(y, False)]
            ws = [w_out_c[j].astype(BF16)]
        x1e, route = _mix_out(x, ys, ws, ln_mix_g[i], ln_mix_b[i], w_router, b_router, alpha, tm)
        x1e = x1e.reshape(t, d + LANES)
        pos, meta = _plan(route[:, 0, :].reshape(t // PLAN_COLS, PLAN_COLS))
        pos = pos.reshape(t)
        xs, inv = _dispatch(pos, meta[3, :N_BUCKETS + 1], x1e, n_tiles * MOE_TILE, _pick_tile(t, 512))
        m = _moe_tiles(meta[0, :n_tiles], meta[1, :n_tiles], meta[2, :n_tiles], inv, xs,
                       exp_w_gate, exp_w_up, exp_w_down, i, t)
        x = _ln_ple(x1e, m, p.reshape(depth, t, -1), i, ln_ffn_g[i], ln_ffn_b[i],
                    ple_w_gate[i].astype(BF16), ple_b_gate[i], ple_w_proj[i].astype(BF16), alpha,
                    _pick_tile(t, 1024)).reshape(b, s, d)
    return x
```

```python
import functools
import math

import jax
import jax.numpy as jnp
from jax import lax
from jax.experimental import pallas as pl
from jax.experimental.pallas import tpu as pltpu

HEAD_DIM = 64
SB_HEADS = 8
SB_WIDTH = SB_HEADS * HEAD_DIM
LRU_WIDTH = 512
LRU_BLOCKS = 8
LRU_C = 8.0
CONV_WIDTH = 4
SWA_HEADS = 16
SWA_KV_HEADS = 4
SWA_GROUP = SWA_HEADS // SWA_KV_HEADS
SWA_WINDOW = 128
ROPE_THETA = 10000.0
N_EXPERTS = 16
N_GROUPS = 4
GROUP_SIZE = N_EXPERTS // N_GROUPS
LN_EPS = 1e-5
Q_SCALE = HEAD_DIM ** -0.5

LANES = 128
SUBLANES = 8
VMEM_LIMIT_BYTES = 48 * 1024 * 1024

NEG_BIG = -1e30

BF16 = jnp.bfloat16
F32 = jnp.float32


def _params(*semantics):
    return pltpu.CompilerParams(dimension_semantics=semantics, vmem_limit_bytes=VMEM_LIMIT_BYTES)


def _softplus(z):
    return jnp.maximum(z, 0.0) + jnp.log(1.0 + jnp.exp(-jnp.abs(z)))


def _sigmoid(z):
    return 1.0 / (1.0 + jnp.exp(-z))


def _layer_norm(y, g, b):
    mu = jnp.mean(y, axis=-1, keepdims=True)
    d = y - mu
    var = jnp.mean(d * d, axis=-1, keepdims=True)
    return d * lax.rsqrt(var + LN_EPS) * g + b


def _proj_ab_kernel(x_ref, w_ref, q_ref, k_ref, v_ref, xr_ref, gr_ref):
    xb = x_ref[...].astype(BF16)

    def chunk(c):
        return jnp.dot(xb, w_ref[:, c * SB_WIDTH:(c + 1) * SB_WIDTH], preferred_element_type=F32)

    for c, (ref, scale) in enumerate(((q_ref, Q_SCALE), (k_ref, None), (v_ref, None))):
        r = chunk(c)
        if scale is not None:
            r = r * scale
        for h in range(SB_HEADS):
            ref[h] = r[:, h * HEAD_DIM:(h + 1) * HEAD_DIM].astype(BF16)
    xr_ref[...] = chunk(3)
    gr_ref[...] = chunk(4)


def _proj_ab(x, w_bf16, tm):
    b, s, d = x.shape
    n = w_bf16.shape[1]
    heads = jax.ShapeDtypeStruct((b, SB_HEADS, s, HEAD_DIM), BF16)
    flat = jax.ShapeDtypeStruct((b, s, LRU_WIDTH), F32)
    head_spec = pl.BlockSpec((None, SB_HEADS, tm, HEAD_DIM), lambda bi, i: (bi, 0, i, 0))
    flat_spec = pl.BlockSpec((None, tm, LRU_WIDTH), lambda bi, i: (bi, i, 0))
    return pl.pallas_call(
        _proj_ab_kernel,
        grid=(b, s // tm),
        in_specs=[pl.BlockSpec((None, tm, d), lambda bi, i: (bi, i, 0)),
                  pl.BlockSpec((d, n), lambda bi, i: (0, 0))],
        out_specs=[head_spec, head_spec, head_spec, flat_spec, flat_spec],
        out_shape=[heads, heads, heads, flat, flat],
        compiler_params=_params("parallel", "parallel"),
        name="proj_ab",
    )(x, w_bf16)


SB_DEAD_LOG_WEIGHT = -105.0
SB_MERGED_BLOCKS = 2


def _sb_attn_kernel(q_ref, k_ref, v_ref, o_ref, *, tq, hp):
    i = pl.program_id(2)
    row = lax.broadcasted_iota(jnp.int32, (tq, tq), 0)
    col = lax.broadcasted_iota(jnp.int32, (tq, tq), 1)
    minus_later = jnp.where(row > col, -1.0, 0.0).astype(BF16)
    causal = col < row

    def block(jb, carries, accs, masked):
        start = pl.multiple_of(jb * tq, tq)
        new_carries, new_accs = [], []
        for h in range(hp):
            kj = k_ref[h, pl.ds(start, tq), :]
            vj = v_ref[h, pl.ds(start, tq), :]
            z = lax.dot_general(q_ref[h], kj, (((1,), (1,)), ((), ())), preferred_element_type=F32)
            sp = _softplus(z)
            cost = jnp.where(causal, sp, 0.0) if masked else sp
            after = jnp.dot(cost.astype(BF16), minus_later, preferred_element_type=F32)
            w = jnp.exp((z - sp) + after + carries[h])
            if masked:
                w = jnp.where(causal, w, 0.0)
            new_accs.append(accs[h] + jnp.dot(w.astype(BF16), vj, preferred_element_type=F32))
            new_carries.append(carries[h] - jnp.sum(cost, axis=1, keepdims=True))
        return tuple(new_carries), tuple(new_accs)

    def live(carries):
        return functools.reduce(jnp.maximum, [jnp.max(c) for c in carries])

    zero = ((jnp.zeros((tq, 1), F32),) * hp, (jnp.zeros((tq, HEAD_DIM), F32),) * hp)

    def first(n):
        def run():
            state = block(i, *zero, True)
            for j in range(1, n):
                state = block(i - j, *state, False)
            return state
        return run

    done = jnp.minimum(i, SB_MERGED_BLOCKS - 1)
    carries, accs = lax.switch(done, [first(n) for n in range(1, SB_MERGED_BLOCKS + 1)])

    def cond(state):
        return (state[0] < i) & (state[1] > SB_DEAD_LOG_WEIGHT)

    def body(state):
        step, _, carries, accs = state
        carries, accs = block(i - 1 - step, carries, accs, False)
        return step + 1, live(carries), carries, accs

    _, _, _, accs = lax.while_loop(cond, body, (done, live(carries), carries, accs))
    for h in range(hp):
        o_ref[h] = accs[h].astype(o_ref.dtype)


def _sb_attention(q, k, v, tq, hp):
    b, h, s, dh = q.shape
    return pl.pallas_call(
        functools.partial(_sb_attn_kernel, tq=tq, hp=hp),
        grid=(b, h // hp, s // tq),
        in_specs=[pl.BlockSpec((None, hp, tq, dh), lambda bi, hi, i: (bi, hi, i, 0)),
                  pl.BlockSpec((None, hp, s, dh), lambda bi, hi, i: (bi, hi, 0, 0), pipeline_mode=pl.Buffered(1)),
                  pl.BlockSpec((None, hp, s, dh), lambda bi, hi, i: (bi, hi, 0, 0), pipeline_mode=pl.Buffered(1))],
        out_specs=pl.BlockSpec((None, hp, tq, dh), lambda bi, hi, i: (bi, hi, i, 0)),
        out_shape=jax.ShapeDtypeStruct((b, h, s, dh), BF16),
        compiler_params=_params("parallel", "parallel", "parallel"),
        name="sb_attention",
    )(q, k, v)


def _gelu_tanh(x):
    return 0.5 * x * (1.0 + jnp.tanh(math.sqrt(2.0 / math.pi) * (x + 0.044715 * (x * x * x))))


def _lru_kernel(xr_ref, gr_ref, cw_ref, cb_ref, wr_ref, br_ref, wi_ref, bi_ref, lam_ref, y_ref,
                xbuf, hprev, *, ts):
    @pl.when(pl.program_id(1) == 0)
    def _():
        xbuf[0:SUBLANES, :] = jnp.zeros((SUBLANES, LRU_WIDTH), F32)
        hprev[...] = jnp.zeros_like(hprev)

    xbuf[SUBLANES:SUBLANES + ts, :] = xr_ref[...]
    xc = cb_ref[...] + cw_ref[CONV_WIDTH - 1:CONV_WIDTH, :] * xbuf[SUBLANES:SUBLANES + ts, :]
    for kk in range(CONV_WIDTH - 1):
        off = SUBLANES - (CONV_WIDTH - 1) + kk
        xc = xc + cw_ref[kk:kk + 1, :] * xbuf[off:off + ts, :]
    xbuf[0:SUBLANES, :] = xbuf[ts:ts + SUBLANES, :]

    xcb = xc.astype(BF16)
    r = _sigmoid(jnp.dot(xcb, wr_ref[...], preferred_element_type=F32) + br_ref[...])
    gi = _sigmoid(jnp.dot(xcb, wi_ref[...], preferred_element_type=F32) + bi_ref[...])
    log_a = (-LRU_C) * r * _softplus(-lam_ref[...])
    a = jnp.exp(log_a)
    u = jnp.sqrt(1.0 - a * a) * (gi * xc)

    row = lax.broadcasted_iota(jnp.int32, (ts, LRU_WIDTH), 0)
    d = 1
    while d < ts:
        if d < SUBLANES:
            keep = row >= d
            a_sh = jnp.where(keep, pltpu.roll(a, d, axis=0), 1.0)
            u_sh = jnp.where(keep, pltpu.roll(u, d, axis=0), 0.0)
            u = a * u_sh + u
            a = a * a_sh
        else:
            u = jnp.concatenate([u[:d], a[d:] * u[:ts - d] + u[d:]], axis=0)
            a = jnp.concatenate([a[:d], a[d:] * a[:ts - d]], axis=0)
        d *= 2
    h = a * hprev[0:1, :] + u
    hprev[...] = jnp.broadcast_to(h[ts - 1:ts, :], hprev.shape)
    y_ref[...] = (_gelu_tanh(gr_ref[...]) * h).astype(y_ref.dtype)


def _lru(xr, gr, conv_w, conv_b, wr_bd, b_r, wi_bd, b_i, lam, ts):
    b, s, w = xr.shape
    seq_spec = pl.BlockSpec((None, ts, w), lambda bi, i: (bi, i, 0))

    def full(shape):
        return pl.BlockSpec(shape, lambda bi, i: (0,) * len(shape))

    return pl.pallas_call(
        functools.partial(_lru_kernel, ts=ts),
        grid=(b, s // ts),
        in_specs=[seq_spec, seq_spec, full((CONV_WIDTH, w)), full((1, w)), full((w, w)), full((1, w)),
                  full((w, w)), full((1, w)), full((1, w))],
        out_specs=seq_spec,
        out_shape=jax.ShapeDtypeStruct((b, s, w), BF16),
        scratch_shapes=[pltpu.VMEM((ts + 2 * SUBLANES, w), F32), pltpu.VMEM((SUBLANES, w), F32)],
        compiler_params=_params("parallel", "arbitrary"),
        name="rg_lru",
    )(xr, gr, conv_w, conv_b.reshape(1, w), wr_bd, b_r.reshape(1, w), wi_bd, b_i.reshape(1, w),
      lam.reshape(1, w))


def _block_diag(w):
    n, c, d = w.shape
    eye = jnp.eye(n, dtype=w.dtype)
    return (eye[:, None, :, None] * w[:, :, None, :]).reshape(n * c, n * d)


def _proj_rope_kernel(x_ref, pos_ref, freq_ref, w_ref, q_ref, k_ref, vt_ref):
    tm = x_ref.shape[0]
    xb = x_ref[...].astype(BF16)
    ang_t = freq_ref[...] * pos_ref[...].astype(F32)
    reps = LANES // (HEAD_DIM // 2)
    cos = jnp.concatenate([jnp.cos(ang_t)] * reps, axis=0).T
    sin = jnp.concatenate([jnp.sin(ang_t)] * reps, axis=0).T
    lane = lax.broadcasted_iota(jnp.int32, (tm, LANES), 1)
    first_half = (lane % HEAD_DIM) < (HEAD_DIM // 2)
    heads_per_slab = LANES // HEAD_DIM

    def rope(r):
        upper = pltpu.roll(r, LANES - HEAD_DIM // 2, axis=1)
        lower = pltpu.roll(r, HEAD_DIM // 2, axis=1)
        return r * cos + jnp.where(first_half, -upper, lower) * sin

    def emit(ref, n_heads, col0, rotary, scale):
        for slab in range(n_heads // heads_per_slab):
            c0 = col0 + slab * LANES
            r = jnp.dot(xb, w_ref[:, c0:c0 + LANES], preferred_element_type=F32)
            if rotary:
                r = rope(r)
            if scale is not None:
                r = r * scale
            for j in range(heads_per_slab):
                ref[slab * heads_per_slab + j] = r[:, j * HEAD_DIM:(j + 1) * HEAD_DIM].astype(BF16)

    emit(q_ref, SWA_HEADS, 0, True, Q_SCALE)
    emit(k_ref, SWA_KV_HEADS, SWA_HEADS * HEAD_DIM, True, None)
    v0 = (SWA_HEADS + SWA_KV_HEADS) * HEAD_DIM
    for slab in range(SWA_KV_HEADS // heads_per_slab):
        r = jnp.dot(xb, w_ref[:, v0 + slab * LANES:v0 + (slab + 1) * LANES], preferred_element_type=F32)
        rt = r.T
        for j in range(heads_per_slab):
            vt_ref[slab * heads_per_slab + j] = rt[j * HEAD_DIM:(j + 1) * HEAD_DIM, :].astype(BF16)


def _proj_rope(x, positions, w_bf16, tm):
    b, s, d = x.shape
    n = w_bf16.shape[1]
    half = HEAD_DIM // 2
    inv_freq = (ROPE_THETA ** (-jnp.arange(half, dtype=F32) / half)).reshape(half, 1)

    def heads(nh):
        return (jax.ShapeDtypeStruct((b, nh, s, HEAD_DIM), BF16),
                pl.BlockSpec((None, nh, tm, HEAD_DIM), lambda bi, i: (bi, 0, i, 0)))

    (qs, qspec), (ks, kspec) = heads(SWA_HEADS), heads(SWA_KV_HEADS)
    vs = jax.ShapeDtypeStruct((b, SWA_KV_HEADS, HEAD_DIM, s), BF16)
    vspec = pl.BlockSpec((None, SWA_KV_HEADS, HEAD_DIM, tm), lambda bi, i: (bi, 0, 0, i))
    return pl.pallas_call(
        _proj_rope_kernel,
        grid=(b, s // tm),
        in_specs=[pl.BlockSpec((None, tm, d), lambda bi, i: (bi, i, 0)),
                  pl.BlockSpec((None, 1, tm), lambda bi, i: (bi, 0, i)),
                  pl.BlockSpec((half, 1), lambda bi, i: (0, 0)),
                  pl.BlockSpec((d, n), lambda bi, i: (0, 0))],
        out_specs=[qspec, kspec, vspec],
        out_shape=[qs, ks, vs],
        compiler_params=_params("parallel", "parallel"),
        name="proj_rope",
    )(x, positions.reshape(b, 1, s), inv_freq, w_bf16)


def _reduce_rows(x, op):
    while x.shape[0] > SUBLANES:
        half = x.shape[0] // 2
        x = op(x[:half], x[half:])
    for shift in (4, 2, 1):
        x = op(x, pltpu.roll(x, shift, axis=0))
    return x[0:1]


SWA_BLOCKS_PER_STEP = 8


def _swa_kernel(q_ref, kp_ref, kc_ref, vtp_ref, vtc_ref, sink_ref, o_ref):
    i = pl.program_id(1)
    w = SWA_WINDOW
    key = lax.broadcasted_iota(jnp.int32, (2 * w, w), 0)
    qry = lax.broadcasted_iota(jnp.int32, (2 * w, w), 1)
    dist = qry + w - key
    band = (dist >= 0) & (dist < w)
    for blk in range(SWA_BLOCKS_PER_STEP):
        visible = band if blk else band & ((key >= w) | (i > 0))
        bias = jnp.concatenate([jnp.where(visible, 0.0, NEG_BIG)] * SWA_GROUP, axis=1)
        outs = []
        for kv in range(SWA_KV_HEADS):
            if blk:
                kk = kc_ref[kv, (blk - 1) * w:(blk + 1) * w, :]
                vvt = vtc_ref[kv, :, (blk - 1) * w:(blk + 1) * w]
            else:
                kk = jnp.concatenate([kp_ref[kv], kc_ref[kv, 0:w, :]], axis=0)
                vvt = jnp.concatenate([vtp_ref[kv], vtc_ref[kv, :, 0:w]], axis=1)
            qg = jnp.concatenate([q_ref[kv * SWA_GROUP + g, blk * w:(blk + 1) * w, :]
                                  for g in range(SWA_GROUP)], axis=0)
            st = lax.dot_general(kk, qg, (((1,), (1,)), ((), ())), preferred_element_type=F32) + bias
            sink = sink_ref[kv:kv + 1, :]
            m = jnp.maximum(_reduce_rows(st, jnp.maximum), sink)
            p = jnp.exp(st - m)
            denom = _reduce_rows(p, jnp.add) + jnp.exp(sink - m)
            ot = jnp.dot(vvt, p.astype(BF16), preferred_element_type=F32) / denom
            outs.extend(ot[:, g * w:(g + 1) * w] for g in range(SWA_GROUP))
        o_ref[blk * w:(blk + 1) * w, :] = jnp.concatenate(outs, axis=0).T.astype(o_ref.dtype)


def _swa(q, k, vt, sinks):
    b, nh, s, dh = q.shape
    nkv = k.shape[1]
    w = SWA_WINDOW
    n = SWA_BLOCKS_PER_STEP
    assert s % (n * w) == 0
    cur = pl.BlockSpec((None, nkv, n * w, dh), lambda bi, i: (bi, 0, i, 0))
    prev = pl.BlockSpec((None, nkv, w, dh), lambda bi, i: (bi, 0, jnp.maximum(n * i - 1, 0), 0))
    cur_t = pl.BlockSpec((None, nkv, dh, n * w), lambda bi, i: (bi, 0, 0, i))
    prev_t = pl.BlockSpec((None, nkv, dh, w), lambda bi, i: (bi, 0, 0, jnp.maximum(n * i - 1, 0)))
    sink_tile = jnp.repeat(sinks.astype(F32).reshape(nkv, nh // nkv), w, axis=1)
    return pl.pallas_call(
        _swa_kernel,
        grid=(b, s // (n * w)),
        in_specs=[pl.BlockSpec((None, nh, n * w, dh), lambda bi, i: (bi, 0, i, 0)),
                  prev, cur, prev_t, cur_t,
                  pl.BlockSpec(sink_tile.shape, lambda bi, i: (0, 0))],
        out_specs=pl.BlockSpec((None, n * w, nh * dh), lambda bi, i: (bi, i, 0)),
        out_shape=jax.ShapeDtypeStruct((b, s, nh * dh), BF16),
        compiler_params=_params("parallel", "parallel"),
        name="swa",
    )(q, k, k, vt, vt, sink_tile)


PAIRS_PER_GROUP = GROUP_SIZE * (GROUP_SIZE - 1) // 2
N_BUCKETS = N_GROUPS * PAIRS_PER_GROUP
ROUTE_ROWS = SUBLANES


def _route(logits_t):
    rows = [logits_t[e:e + 1, :] for e in range(N_EXPERTS)]
    mx = functools.reduce(jnp.maximum, rows)
    ex = [jnp.exp(r - mx) for r in rows]
    total = functools.reduce(lambda p, q: p + q, ex)
    probs = [e / total for e in ex]

    group_score = []
    for g in range(N_GROUPS):
        a, b, c, d = probs[g * GROUP_SIZE:(g + 1) * GROUP_SIZE]
        hi1, lo1 = jnp.maximum(a, b), jnp.minimum(a, b)
        hi2, lo2 = jnp.maximum(c, d), jnp.minimum(c, d)
        top1 = jnp.maximum(hi1, hi2)
        top2 = jnp.maximum(jnp.minimum(hi1, hi2), jnp.maximum(lo1, lo2))
        group_score.append(top1 + top2)
    best = functools.reduce(jnp.maximum, group_score)
    g_sel = jnp.full(best.shape, N_GROUPS - 1, jnp.int32)
    for g in range(N_GROUPS - 2, -1, -1):
        g_sel = jnp.where(group_score[g] == best, g, g_sel)

    in_group = []
    for j in range(GROUP_SIZE):
        val = probs[(N_GROUPS - 1) * GROUP_SIZE + j]
        for g in range(N_GROUPS - 2, -1, -1):
            val = jnp.where(g_sel == g, probs[g * GROUP_SIZE + j], val)
        in_group.append(val)

    def first_argmax(vals):
        m = functools.reduce(jnp.maximum, vals)
        idx = jnp.full(m.shape, GROUP_SIZE - 1, jnp.int32)
        for j in range(GROUP_SIZE - 2, -1, -1):
            idx = jnp.where(vals[j] == m, j, idx)
        return m, idx

    w1, i1 = first_argmax(in_group)
    rest = [jnp.where(i1 == j, -1.0, in_group[j]) for j in range(GROUP_SIZE)]
    w2, i2 = first_argmax(rest)
    norm = w1 + w2
    first_is_lo = i1 < i2
    i_lo = jnp.minimum(i1, i2)
    i_hi = jnp.maximum(i1, i2)
    pair = jnp.where(i_lo == 0, i_hi - 1, jnp.where(i_lo == 2, 3, jnp.where(i_hi == 3, 4, 5)))
    bucket = (g_sel * PAIRS_PER_GROUP + pair).astype(F32)
    w_lo = jnp.where(first_is_lo, w1, w2) / norm
    w_hi = jnp.where(first_is_lo, w2, w1) / norm
    return bucket, w_lo, w_hi


def _mix_out_kernel(*refs, n_head_major, alpha):
    x_ref = refs[0]
    y_refs = refs[1:1 + len(n_head_major)]
    w_refs = refs[1 + len(n_head_major):1 + 2 * len(n_head_major)]
    g_ref, b_ref, wrh_ref, wrl_ref, brt_ref, x1e_ref, route_ref = refs[1 + 2 * len(n_head_major):]
    tm, d = x_ref.shape
    h = alpha * x_ref[...]
    for y_ref, w_ref, nh in zip(y_refs, w_refs, n_head_major):
        if nh:
            y = jnp.concatenate([y_ref[j] for j in range(nh)], axis=-1)
        else:
            y = y_ref[...]
        h = h + jnp.dot(y, w_ref[...], preferred_element_type=F32)
    x1 = _layer_norm(h, g_ref[...], b_ref[...])
    x_hi = x1.astype(BF16)
    x_lo = (x1 - x_hi.astype(F32)).astype(BF16)

    def nt_dot(w, xv):
        return lax.dot_general(w, xv, (((1,), (1,)), ((), ())), preferred_element_type=F32)

    logits_t = (nt_dot(wrh_ref[...], x_hi) + nt_dot(wrh_ref[...], x_lo) + nt_dot(wrl_ref[...], x_hi)
                + brt_ref[...])
    route = jnp.concatenate(list(_route(logits_t)) + [jnp.zeros((ROUTE_ROWS - 3, tm), F32)], axis=0)
    route_ref[...] = route
    x1e_ref[:, 0:d] = x1
    x1e_ref[:, d:d + LANES] = jnp.concatenate([route, jnp.zeros((LANES - ROUTE_ROWS, tm), F32)], axis=0).T


def _mix_out(x, ys, ws, ln_g, ln_b, w_router, b_router, alpha, tm):
    b, s, d = x.shape
    n_head_major = tuple(y.shape[1] if hm else 0 for y, hm in ys)
    y_specs = []
    for (y, hm) in ys:
        if hm:
            y_specs.append(pl.BlockSpec((None, y.shape[1], tm, y.shape[3]), lambda bi, i: (bi, 0, i, 0)))
        else:
            y_specs.append(pl.BlockSpec((None, tm, y.shape[2]), lambda bi, i: (bi, i, 0)))
    w_specs = [pl.BlockSpec(w.shape, lambda bi, i: (0, 0)) for w in ws]
    row = pl.BlockSpec((1, d), lambda bi, i: (0, 0))
    tok = pl.BlockSpec((None, tm, d), lambda bi, i: (bi, i, 0))
    wr_t = w_router.T.astype(F32)
    wr_hi = wr_t.astype(BF16)
    wr_lo = (wr_t - wr_hi.astype(F32)).astype(BF16)
    wr_spec = pl.BlockSpec((N_EXPERTS, d), lambda bi, i: (0, 0))
    return pl.pallas_call(
        functools.partial(_mix_out_kernel, n_head_major=n_head_major, alpha=alpha),
        grid=(b, s // tm),
        in_specs=[tok] + y_specs + w_specs + [row, row, wr_spec, wr_spec,
                  pl.BlockSpec((N_EXPERTS, 1), lambda bi, i: (0, 0))],
        out_specs=[pl.BlockSpec((None, tm, d + LANES), lambda bi, i: (bi, i, 0)),
                   pl.BlockSpec((None, ROUTE_ROWS, tm), lambda bi, i: (bi, 0, i))],
        out_shape=[jax.ShapeDtypeStruct((b, s, d + LANES), F32),
                   jax.ShapeDtypeStruct((b, ROUTE_ROWS, s), F32)],
        compiler_params=_params("parallel", "parallel"),
        name="mix_out_ln_router",
    )(x, *[y for y, _ in ys], *ws, ln_g.reshape(1, d), ln_b.reshape(1, d),
      wr_hi, wr_lo, b_router.astype(F32).reshape(N_EXPERTS, 1))


MOE_TILE = 256
PLAN_COLS = 256
META_ROWS = SUBLANES


def _num_moe_tiles(t):
    return t // MOE_TILE + N_BUCKETS


def _plan_kernel(bid_ref, pos_ref, meta_ref):
    r, c = bid_ref.shape
    bid = bid_ref[...]
    before = (lax.broadcasted_iota(jnp.int32, (c, c), 0)
              < lax.broadcasted_iota(jnp.int32, (c, c), 1)).astype(BF16)
    rows_before = (lax.broadcasted_iota(jnp.int32, (r, r), 1)
                   < lax.broadcasted_iota(jnp.int32, (r, r), 0)).astype(BF16)
    lane = lax.broadcasted_iota(jnp.int32, (1, LANES), 1)
    tile_start = lane.astype(F32) * MOE_TILE

    def body(b, state):
        base, pos, tile_bucket, last_tile = state
        ind = (bid == lax.convert_element_type(b, F32)).astype(F32)
        within = jnp.dot(ind.astype(BF16), before, preferred_element_type=F32)
        row_total = jnp.sum(ind, axis=1, keepdims=True)
        row_off = jnp.dot(rows_before, jnp.broadcast_to(row_total, (r, LANES)).astype(BF16),
                          preferred_element_type=F32)[:, 0:1]
        count = jnp.sum(row_total, axis=0, keepdims=True)
        padded = jnp.floor((count + (MOE_TILE - 1)) * (1.0 / MOE_TILE)) * MOE_TILE
        pos = pos + ind * (base + row_off + within)
        end = base + padded
        tile_bucket = tile_bucket + (tile_start >= end).astype(F32)
        last_row = jnp.where(count > 0.0, end - MOE_TILE, -1.0)
        last_tile = jnp.where(lane == b, last_row, last_tile)
        return end, pos, tile_bucket, last_tile

    total, pos, tile_bucket, last_tile = lax.fori_loop(
        0, N_BUCKETS, body, (jnp.zeros((1, 1), F32), jnp.zeros((r, c), F32), jnp.zeros((1, LANES), F32),
                             jnp.full((1, LANES), -1.0, F32)))
    pos_ref[...] = pos.astype(jnp.int32)

    tb = jnp.minimum(tile_bucket, N_BUCKETS - 1.0)
    group = sum((tb >= g * PAIRS_PER_GROUP).astype(F32) for g in range(1, N_GROUPS))
    pair = tb - group * PAIRS_PER_GROUP
    i_lo = jnp.where(pair < 3, 0.0, jnp.where(pair == 3, 2.0, 1.0))
    i_hi = jnp.where(pair < 3, pair + 1.0, jnp.where(pair == 5, 2.0, 3.0))
    meta = jnp.concatenate([group * GROUP_SIZE + i_lo, group * GROUP_SIZE + i_hi,
                            (tile_start < total).astype(F32),
                            jnp.where(lane == N_BUCKETS, total, last_tile),
                            jnp.zeros((META_ROWS - 4, LANES), F32)], axis=0)
    meta_ref[...] = meta.astype(jnp.int32)


def _plan(bucket_ids):
    r, c = bucket_ids.shape
    return pl.pallas_call(
        _plan_kernel,
        out_shape=[jax.ShapeDtypeStruct((r, c), jnp.int32), jax.ShapeDtypeStruct((META_ROWS, LANES), jnp.int32)],
        compiler_params=pltpu.CompilerParams(vmem_limit_bytes=VMEM_LIMIT_BYTES),
        name="moe_plan",
    )(bucket_ids)


DMA_PRIORITIES = 2


def _row_copy(src_ref, src_row, dst_ref, dst_row, sem):
    return pltpu.make_async_copy(src_ref.at[pl.ds(src_row, 1)], dst_ref.at[pl.ds(dst_row, 1)], sem)


def _dispatch_kernel(pos_ref, last_ref, x_ref, o_ref, inv_ref, zeros, inv, fill, sem, zsem, isem, *,
                     first_spare, n_tokens):
    tm = x_ref.shape[0]
    i = pl.program_id(0)

    @pl.when(i == 0)
    def _():
        zeros[...] = jnp.zeros_like(zeros)
        fill[...] = jnp.full(fill.shape, n_tokens, jnp.int32)
        to_smem = pltpu.make_async_copy(fill, inv, isem)
        to_smem.start()
        rows_in_use = last_ref[N_BUCKETS]
        clears = [(last_ref[b] >= 0, last_ref[b]) for b in range(N_BUCKETS)]
        clears += [(k * MOE_TILE >= rows_in_use, k * MOE_TILE)
                   for k in range(first_spare, o_ref.shape[0] // MOE_TILE)]

        def clear(row):
            start = row if isinstance(row, int) else pl.multiple_of(row, MOE_TILE)
            return pltpu.make_async_copy(zeros, o_ref.at[pl.ds(start, MOE_TILE)], zsem)

        for needed, row in clears:
            @pl.when(needed)
            def _():
                clear(row).start()
        for needed, row in clears:
            @pl.when(needed)
            def _():
                clear(row).wait()
        to_smem.wait()

    base = i * tm
    for r in range(tm):
        dst = pos_ref[base + r]
        inv[dst] = base + r
        _row_copy(x_ref, r, o_ref, dst, sem).start(priority=r % DMA_PRIORITIES)
    pltpu.make_async_copy(x_ref, o_ref.at[pl.ds(0, tm)], sem).wait()

    @pl.when(i == pl.num_programs(0) - 1)
    def _():
        to_hbm = pltpu.make_async_copy(inv, inv_ref, isem)
        to_hbm.start()
        to_hbm.wait()


def _dispatch(pos, last_tile_rows, xt, n_rows, tm):
    t = xt.shape[0]
    tile = xt.shape[1:]
    dma_sem = pltpu.SemaphoreType.DMA(())
    return pl.pallas_call(
        functools.partial(_dispatch_kernel, first_spare=t // MOE_TILE, n_tokens=t),
        grid_spec=pltpu.PrefetchScalarGridSpec(
            num_scalar_prefetch=2,
            grid=(t // tm,),
            in_specs=[pl.BlockSpec((tm,) + tile, lambda i, pos_ref, last_ref: (i,) + (0,) * len(tile))],
            out_specs=[pl.BlockSpec(memory_space=pl.ANY), pl.BlockSpec(memory_space=pl.ANY)],
            scratch_shapes=[pltpu.VMEM((MOE_TILE,) + tile, xt.dtype), pltpu.SMEM((n_rows,), jnp.int32),
                            pltpu.VMEM((n_rows,), jnp.int32), dma_sem, dma_sem, dma_sem]),
        out_shape=[jax.ShapeDtypeStruct((n_rows,) + tile, xt.dtype),
                   jax.ShapeDtypeStruct((n_rows,), jnp.int32)],
        compiler_params=_params("arbitrary"),
        name="moe_dispatch",
    )(pos, last_tile_rows, xt)


def _moe_tile_kernel(elo_ref, ehi_ref, valid_ref, inv, x_ref, wgl_ref, wul_ref, wdl_ref, wgh_ref, wuh_ref,
                     wdh_ref, m_hbm, o_even, o_odd, ssem_even, ssem_odd, wgl_b, wul_b, wdl_b, wgh_b, wuh_b,
                     wdh_b, *, n_tokens):
    d = wgl_ref.shape[0]
    k = pl.program_id(0)
    last = pl.num_programs(0) - 1
    prev = jnp.maximum(k - 1, 0)
    valid = valid_ref[k] != 0

    def scatter(tile, r, buf, sem, to_spare):
        tok = inv[tile * MOE_TILE + r]
        dst = jnp.where(to_spare | (tok >= n_tokens), n_tokens + r, tok)
        return _row_copy(buf, r, m_hbm, dst, sem)

    def scatter_done(buf, sem):
        pltpu.make_async_copy(buf, m_hbm.at[pl.ds(0, MOE_TILE)], sem).wait()

    @pl.when(k == 0)
    def _():
        o_odd[...] = jnp.zeros_like(o_odd)

    new_lo = (k == 0) | (elo_ref[k] != elo_ref[prev])
    new_hi = (k == 0) | (ehi_ref[k] != ehi_ref[prev])
    experts = (((wgl_ref, wul_ref, wdl_ref), (wgl_b, wul_b, wdl_b), d + 1),
               ((wgh_ref, wuh_ref, wdh_ref), (wgh_b, wuh_b, wdh_b), d + 2))

    def tile_valid(j):
        return (j >= 0) & (valid_ref[jnp.maximum(j, 0)] != 0)

    def scattered_in(j):
        return (j == 0) | tile_valid(j) | tile_valid(j - 1)

    def step(o_cur, ssem_cur, o_prv, ssem_prv):
        @pl.when((k > 0) & scattered_in(k - 1))
        def _():
            scatter_done(o_cur, ssem_cur)

        first = k == 0

        def run_tile(refresh):
            for r in range(MOE_TILE):
                scatter(prev, r, o_prv, ssem_prv, first).start(priority=r % DMA_PRIORITIES)
            x = x_ref[:, 0:d].astype(BF16)
            acc = None
            for (srcs, bufs, lane), fresh in zip(experts, refresh):
                if fresh:
                    wg, wu, wd = (src[...].astype(BF16) for src in srcs)
                    for buf, w in zip(bufs, (wg, wu, wd)):
                        buf[...] = w
                else:
                    wg, wu, wd = (buf[...] for buf in bufs)
                weight = x_ref[:, lane:lane + 1]
                hg = jnp.dot(x, wg, preferred_element_type=F32)
                hu = jnp.dot(x, wu, preferred_element_type=F32)
                hidden = (hg * _sigmoid(hg)) * hu * weight
                y = jnp.dot(hidden.astype(BF16), wd, preferred_element_type=F32)
                acc = y if acc is None else acc + y
            o_cur[...] = acc

        for fresh_lo in (False, True):
            for fresh_hi in (False, True):
                @pl.when(valid & (new_lo == fresh_lo) & (new_hi == fresh_hi))
                def _():
                    run_tile((fresh_lo, fresh_hi))

        @pl.when(jnp.logical_not(valid) & tile_valid(k - 1))
        def _():
            def issue(r, carry):
                scatter(prev, r, o_prv, ssem_prv, first).start()
                return carry

            lax.fori_loop(0, MOE_TILE, issue, 0, unroll=8)

        @pl.when((k == last) & scattered_in(k))
        def _():
            scatter_done(o_prv, ssem_prv)

        @pl.when((k == last) & valid)
        def _():
            def issue(r, carry):
                scatter(k, r, o_cur, ssem_cur, False).start()
                return carry

            lax.fori_loop(0, MOE_TILE, issue, 0, unroll=8)
            scatter_done(o_cur, ssem_cur)

    @pl.when(k % 2 == 0)
    def _():
        step(o_even, ssem_even, o_odd, ssem_odd)

    @pl.when(k % 2 == 1)
    def _():
        step(o_odd, ssem_odd, o_even, ssem_even)


def _moe_tiles(e_lo, e_hi, valid, inv, xs, wg, wu, wd, layer, n_tokens):
    n_rows, dw = xs.shape
    _, _, d, f = wg.shape
    up_lo = pl.BlockSpec((None, None, d, f), lambda k, lo, hi, ok, iv: (layer, lo[k], 0, 0))
    up_hi = pl.BlockSpec((None, None, d, f), lambda k, lo, hi, ok, iv: (layer, hi[k], 0, 0))
    down_lo = pl.BlockSpec((None, None, f, d), lambda k, lo, hi, ok, iv: (layer, lo[k], 0, 0))
    down_hi = pl.BlockSpec((None, None, f, d), lambda k, lo, hi, ok, iv: (layer, hi[k], 0, 0))
    up_b = pltpu.VMEM((d, f), BF16)
    down_b = pltpu.VMEM((f, d), BF16)
    o_buf = pltpu.VMEM((MOE_TILE, d), F32)
    dma_sem = pltpu.SemaphoreType.DMA(())
    return pl.pallas_call(
        functools.partial(_moe_tile_kernel, n_tokens=n_tokens),
        grid_spec=pltpu.PrefetchScalarGridSpec(
            num_scalar_prefetch=4,
            grid=(n_rows // MOE_TILE,),
            in_specs=[pl.BlockSpec((MOE_TILE, dw), lambda k, lo, hi, ok, iv: (k, 0)),
                      up_lo, up_lo, down_lo, up_hi, up_hi, down_hi],
            out_specs=pl.BlockSpec(memory_space=pl.ANY),
            scratch_shapes=[o_buf, o_buf, dma_sem, dma_sem, up_b, up_b, down_b, up_b, up_b, down_b]),
        out_shape=jax.ShapeDtypeStruct((n_tokens + MOE_TILE, d), F32),
        compiler_params=_params("arbitrary"),
        name="moe_tiles",
    )(e_lo, e_hi, valid, inv, xs, wg, wu, wd, wg, wu, wd)


def _ln_ple_kernel(x1_ref, m_ref, p_ref, g_ref, b_ref, wg_ref, bg_ref, wp_ref, o_ref, *, alpha):
    x2 = _layer_norm(alpha * x1_ref[...] + m_ref[...], g_ref[...], b_ref[...])
    gate = _sigmoid(jnp.dot(x2.astype(BF16), wg_ref[...], preferred_element_type=F32) + bg_ref[...])
    emb = jnp.dot(p_ref[...].astype(BF16), wp_ref[...], preferred_element_type=F32)
    o_ref[...] = x2 + gate * emb


def _ln_ple(x1e, m, p, layer, ln_g, ln_b, wg, bg, wp, alpha, tm):
    t = x1e.shape[0]
    d = m.shape[1]
    pd = p.shape[2]
    tok = pl.BlockSpec((tm, d), lambda i: (i, 0))
    row = pl.BlockSpec((1, d), lambda i: (0, 0))
    return pl.pallas_call(
        functools.partial(_ln_ple_kernel, alpha=alpha),
        grid=(t // tm,),
        in_specs=[tok, tok, pl.BlockSpec((None, tm, pd), lambda i: (layer, i, 0)), row, row,
                  pl.BlockSpec((d, d), lambda i: (0, 0)), row, pl.BlockSpec((pd, d), lambda i: (0, 0))],
        out_specs=tok,
        out_shape=jax.ShapeDtypeStruct((t, d), F32),
        compiler_params=_params("parallel"),
        name="ln_ple",
    )(x1e, m, p, ln_g.reshape(1, d), ln_b.reshape(1, d), wg, bg.reshape(1, d), wp)


def _pick_tile(n, target):
    t = min(n, target)
    while n % t:
        t //= 2
    return t


def kernel(x, p, positions, w_in_ab, w_out_ab, conv_w, conv_b, lru_w_r, lru_b_r, lru_w_i, lru_b_i, lru_lambda, w_qkv_c, w_out_c, sinks_c, ln_mix_g, ln_mix_b, ln_ffn_g, ln_ffn_b, w_router, b_router, exp_w_gate, exp_w_up, exp_w_down, ple_w_proj, ple_w_gate, ple_b_gate):
    b, s, d = x.shape
    depth = p.shape[0]
    t = b * s
    alpha = (2 * depth) ** 0.25
    tm = _pick_tile(s, 1024)
    assert t % MOE_TILE == 0 and t % PLAN_COLS == 0
    n_tiles = _num_moe_tiles(t)
    assert n_tiles <= LANES
    for i in range(depth):
        j = i // 2
        if i % 2 == 0:
            q, k, v, xr, gr = _proj_ab(x, w_in_ab[j].astype(BF16), tm)
            y_sb = _sb_attention(q, k, v, _pick_tile(s, 256), SB_HEADS)
            y_lru = _lru(xr, gr, conv_w[j], conv_b[j], _block_diag(lru_w_r[j]).astype(BF16), lru_b_r[j],
                         _block_diag(lru_w_i[j]).astype(BF16), lru_b_i[j], lru_lambda[j],
                         _pick_tile(s, 256))
            w_out = w_out_ab[j].astype(BF16)
            ys = [(y_sb, True), (y_lru, False)]
            ws = [w_out[:SB_WIDTH], w_out[SB_WIDTH:]]
        else:
            q, k, v = _proj_rope(x, positions, w_qkv_c[j].astype(BF16), tm)
            y = _swa(q, k, v, sinks_c[j])
            ys = [(y, False)]
            ws = [w_out_c[j].astype(BF16)]
        x1e, route = _mix_out(x, ys, ws, ln_mix_g[i], ln_mix_b[i], w_router, b_router, alpha, tm)
        x1e = x1e.reshape(t, d + LANES)
        pos, meta = _plan(route[:, 0, :].reshape(t // PLAN_COLS, PLAN_COLS))
        pos = pos.reshape(t)
        xs, inv = _dispatch(pos, meta[3, :N_BUCKETS + 1], x1e, n_tiles * MOE_TILE, _pick_tile(t, 512))
        m = _moe_tiles(meta[0, :n_tiles], meta[1, :n_tiles], meta[2, :n_tiles], inv, xs,
                       exp_w_gate, exp_w_up, exp_w_down, i, t)
        x = _ln_ple(x1e, m, p.reshape(depth, t, -1), i, ln_ffn_g[i], ln_ffn_b[i],
                    ple_w_gate[i].astype(BF16), ple_b_gate[i], ple_w_proj[i].astype(BF16), alpha,
                    _pick_tile(t, 1024)).reshape(b, s, d)
    return x
```

```python
import functools
import math

import jax
import jax.numpy as jnp
from jax import lax
from jax.experimental import pallas as pl
from jax.experimental.pallas import tpu as pltpu

HEAD_DIM = 64
SB_HEADS = 8
SB_WIDTH = SB_HEADS * HEAD_DIM
LRU_WIDTH = 512
LRU_BLOCKS = 8
LRU_C = 8.0
CONV_WIDTH = 4
SWA_HEADS = 16
SWA_KV_HEADS = 4
SWA_GROUP = SWA_HEADS // SWA_KV_HEADS
SWA_WINDOW = 128
ROPE_THETA = 10000.0
N_EXPERTS = 16
N_GROUPS = 4
GROUP_SIZE = N_EXPERTS // N_GROUPS
LN_EPS = 1e-5
Q_SCALE = HEAD_DIM ** -0.5

LANES = 128
SUBLANES = 8
VMEM_LIMIT_BYTES = 48 * 1024 * 1024

NEG_BIG = -1e30

BF16 = jnp.bfloat16
F32 = jnp.float32


def _params(*semantics):
    return pltpu.CompilerParams(dimension_semantics=semantics, vmem_limit_bytes=VMEM_LIMIT_BYTES)


def _softplus(z):
    return jnp.maximum(z, 0.0) + jnp.log(1.0 + jnp.exp(-jnp.abs(z)))


def _sigmoid(z):
    return 1.0 / (1.0 + jnp.exp(-z))


def _layer_norm(y, g, b):
    mu = jnp.mean(y, axis=-1, keepdims=True)
    d = y - mu
    var = jnp.mean(d * d, axis=-1, keepdims=True)
    return d * lax.rsqrt(var + LN_EPS) * g + b


def _proj_ab_kernel(x_ref, w_ref, q_ref, k_ref, v_ref, xr_ref, gr_ref):
    xb = x_ref[...].astype(BF16)

    def chunk(c):
        return jnp.dot(xb, w_ref[:, c * SB_WIDTH:(c + 1) * SB_WIDTH], preferred_element_type=F32)

    for c, (ref, scale) in enumerate(((q_ref, Q_SCALE), (k_ref, None), (v_ref, None))):
        r = chunk(c)
        if scale is not None:
            r = r * scale
        for h in range(SB_HEADS):
            ref[h] = r[:, h * HEAD_DIM:(h + 1) * HEAD_DIM].astype(BF16)
    xr_ref[...] = chunk(3)
    gr_ref[...] = chunk(4)


def _proj_ab(x, w_bf16, tm):
    b, s, d = x.shape
    n = w_bf16.shape[1]
    heads = jax.ShapeDtypeStruct((b, SB_HEADS, s, HEAD_DIM), BF16)
    flat = jax.ShapeDtypeStruct((b, s, LRU_WIDTH), F32)
    head_spec = pl.BlockSpec((None, SB_HEADS, tm, HEAD_DIM), lambda bi, i: (bi, 0, i, 0))
    flat_spec = pl.BlockSpec((None, tm, LRU_WIDTH), lambda bi, i: (bi, i, 0))
    return pl.pallas_call(
        _proj_ab_kernel,
        grid=(b, s // tm),
        in_specs=[pl.BlockSpec((None, tm, d), lambda bi, i: (bi, i, 0)),
                  pl.BlockSpec((d, n), lambda bi, i: (0, 0))],
        out_specs=[head_spec, head_spec, head_spec, flat_spec, flat_spec],
        out_shape=[heads, heads, heads, flat, flat],
        compiler_params=_params("parallel", "parallel"),
        name="proj_ab",
    )(x, w_bf16)


SB_DEAD_LOG_WEIGHT = -105.0
SB_MERGED_BLOCKS = 2


def _sb_attn_kernel(q_ref, k_ref, v_ref, o_ref, *, tq, hp):
    i = pl.program_id(2)
    row = lax.broadcasted_iota(jnp.int32, (tq, tq), 0)
    col = lax.broadcasted_iota(jnp.int32, (tq, tq), 1)
    minus_later = jnp.where(row > col, -1.0, 0.0).astype(BF16)
    causal = col < row

    def block(jb, carries, accs, masked):
        start = pl.multiple_of(jb * tq, tq)
        new_carries, new_accs = [], []
        for h in range(hp):
            kj = k_ref[h, pl.ds(start, tq), :]
            vj = v_ref[h, pl.ds(start, tq), :]
            z = lax.dot_general(q_ref[h], kj, (((1,), (1,)), ((), ())), preferred_element_type=F32)
            sp = _softplus(z)
            cost = jnp.where(causal, sp, 0.0) if masked else sp
            after = jnp.dot(cost.astype(BF16), minus_later, preferred_element_type=F32)
            w = jnp.exp((z - sp) + after + carries[h])
            if masked:
                w = jnp.where(causal, w, 0.0)
            new_accs.append(accs[h] + jnp.dot(w.astype(BF16), vj, preferred_element_type=F32))
            new_carries.append(carries[h] - jnp.sum(cost, axis=1, keepdims=True))
        return tuple(new_carries), tuple(new_accs)

    def live(carries):
        return functools.reduce(jnp.maximum, [jnp.max(c) for c in carries])

    zero = ((jnp.zeros((tq, 1), F32),) * hp, (jnp.zeros((tq, HEAD_DIM), F32),) * hp)

    def first(n):
        def run():
            state = block(i, *zero, True)
            for j in range(1, n):
                state = block(i - j, *state, False)
            return state
        return run

    done = jnp.minimum(i, SB_MERGED_BLOCKS - 1)
    carries, accs = lax.switch(done, [first(n) for n in range(1, SB_MERGED_BLOCKS + 1)])

    def cond(state):
        return (state[0] < i) & (state[1] > SB_DEAD_LOG_WEIGHT)

    def body(state):
        step, _, carries, accs = state
        carries, accs = block(i - 1 - step, carries, accs, False)
        return step + 1, live(carries), carries, accs

    _, _, _, accs = lax.while_loop(cond, body, (done, live(carries), carries, accs))
    for h in range(hp):
        o_ref[h] = accs[h].astype(o_ref.dtype)


def _sb_attention(q, k, v, tq, hp):
    b, h, s, dh = q.shape
    return pl.pallas_call(
        functools.partial(_sb_attn_kernel, tq=tq, hp=hp),
        grid=(b, h // hp, s // tq),
        in_specs=[pl.BlockSpec((None, hp, tq, dh), lambda bi, hi, i: (bi, hi, i, 0)),
                  pl.BlockSpec((None, hp, s, dh), lambda bi, hi, i: (bi, hi, 0, 0), pipeline_mode=pl.Buffered(1)),
                  pl.BlockSpec((None, hp, s, dh), lambda bi, hi, i: (bi, hi, 0, 0), pipeline_mode=pl.Buffered(1))],
        out_specs=pl.BlockSpec((None, hp, tq, dh), lambda bi, hi, i: (bi, hi, i, 0)),
        out_shape=jax.ShapeDtypeStruct((b, h, s, dh), BF16),
        compiler_params=_params("parallel", "parallel", "parallel"),
        name="sb_attention",
    )(q, k, v)


def _gelu_tanh(x):
    return 0.5 * x * (1.0 + jnp.tanh(math.sqrt(2.0 / math.pi) * (x + 0.044715 * (x * x * x))))


def _lru_kernel(xr_ref, gr_ref, cw_ref, cb_ref, wr_ref, br_ref, wi_ref, bi_ref, lam_ref, y_ref,
                xbuf, hprev, *, ts):
    @pl.when(pl.program_id(1) == 0)
    def _():
        xbuf[0:SUBLANES, :] = jnp.zeros((SUBLANES, LRU_WIDTH), F32)
        hprev[...] = jnp.zeros_like(hprev)

    xbuf[SUBLANES:SUBLANES + ts, :] = xr_ref[...]
    xc = cb_ref[...] + cw_ref[CONV_WIDTH - 1:CONV_WIDTH, :] * xbuf[SUBLANES:SUBLANES + ts, :]
    for kk in range(CONV_WIDTH - 1):
        off = SUBLANES - (CONV_WIDTH - 1) + kk
        xc = xc + cw_ref[kk:kk + 1, :] * xbuf[off:off + ts, :]
    xbuf[0:SUBLANES, :] = xbuf[ts:ts + SUBLANES, :]

    xcb = xc.astype(BF16)
    r = _sigmoid(jnp.dot(xcb, wr_ref[...], preferred_element_type=F32) + br_ref[...])
    gi = _sigmoid(jnp.dot(xcb, wi_ref[...], preferred_element_type=F32) + bi_ref[...])
    log_a = (-LRU_C) * r * _softplus(-lam_ref[...])
    a = jnp.exp(log_a)
    u = jnp.sqrt(1.0 - a * a) * (gi * xc)

    row = lax.broadcasted_iota(jnp.int32, (ts, LRU_WIDTH), 0)
    d = 1
    while d < ts:
        if d < SUBLANES:
            keep = row >= d
            a_sh = jnp.where(keep, pltpu.roll(a, d, axis=0), 1.0)
            u_sh = jnp.where(keep, pltpu.roll(u, d, axis=0), 0.0)
            u = a * u_sh + u
            a = a * a_sh
        else:
            u = jnp.concatenate([u[:d], a[d:] * u[:ts - d] + u[d:]], axis=0)
            a = jnp.concatenate([a[:d], a[d:] * a[:ts - d]], axis=0)
        d *= 2
    h = a * hprev[0:1, :] + u
    hprev[...] = jnp.broadcast_to(h[ts - 1:ts, :], hprev.shape)
    y_ref[...] = (_gelu_tanh(gr_ref[...]) * h).astype(y_ref.dtype)


def _lru(xr, gr, conv_w, conv_b, wr_bd, b_r, wi_bd, b_i, lam, ts):
    b, s, w = xr.shape
    seq_spec = pl.BlockSpec((None, ts, w), lambda bi, i: (bi, i, 0))

    def full(shape):
        return pl.BlockSpec(shape, lambda bi, i: (0,) * len(shape))

    return pl.pallas_call(
        functools.partial(_lru_kernel, ts=ts),
        grid=(b, s // ts),
        in_specs=[seq_spec, seq_spec, full((CONV_WIDTH, w)), full((1, w)), full((w, w)), full((1, w)),
                  full((w, w)), full((1, w)), full((1, w))],
        out_specs=seq_spec,
        out_shape=jax.ShapeDtypeStruct((b, s, w), BF16),
        scratch_shapes=[pltpu.VMEM((ts + 2 * SUBLANES, w), F32), pltpu.VMEM((SUBLANES, w), F32)],
        compiler_params=_params("parallel", "arbitrary"),
        name="rg_lru",
    )(xr, gr, conv_w, conv_b.reshape(1, w), wr_bd, b_r.reshape(1, w), wi_bd, b_i.reshape(1, w),
      lam.reshape(1, w))


def _block_diag(w):
    n, c, d = w.shape
    eye = jnp.eye(n, dtype=w.dtype)
    return (eye[:, None, :, None] * w[:, :, None, :]).reshape(n * c, n * d)


def _proj_rope_kernel(x_ref, pos_ref, freq_ref, w_ref, q_ref, k_ref, vt_ref):
    tm = x_ref.shape[0]
    xb = x_ref[...].astype(BF16)
    ang_t = freq_ref[...] * pos_ref[...].astype(F32)
    reps = LANES // (HEAD_DIM // 2)
    cos = jnp.concatenate([jnp.cos(ang_t)] * reps, axis=0).T
    sin = jnp.concatenate([jnp.sin(ang_t)] * reps, axis=0).T
    lane = lax.broadcasted_iota(jnp.int32, (tm, LANES), 1)
    first_half = (lane % HEAD_DIM) < (HEAD_DIM // 2)
    heads_per_slab = LANES // HEAD_DIM

    def rope(r):
        upper = pltpu.roll(r, LANES - HEAD_DIM // 2, axis=1)
        lower = pltpu.roll(r, HEAD_DIM // 2, axis=1)
        return r * cos + jnp.where(first_half, -upper, lower) * sin

    def emit(ref, n_heads, col0, rotary, scale):
        for slab in range(n_heads // heads_per_slab):
            c0 = col0 + slab * LANES
            r = jnp.dot(xb, w_ref[:, c0:c0 + LANES], preferred_element_type=F32)
            if rotary:
                r = rope(r)
            if scale is not None:
                r = r * scale
            for j in range(heads_per_slab):
                ref[slab * heads_per_slab + j] = r[:, j * HEAD_DIM:(j + 1) * HEAD_DIM].astype(BF16)

    emit(q_ref, SWA_HEADS, 0, True, Q_SCALE)
    emit(k_ref, SWA_KV_HEADS, SWA_HEADS * HEAD_DIM, True, None)
    v0 = (SWA_HEADS + SWA_KV_HEADS) * HEAD_DIM
    for slab in range(SWA_KV_HEADS // heads_per_slab):
        r = jnp.dot(xb, w_ref[:, v0 + slab * LANES:v0 + (slab + 1) * LANES], preferred_element_type=F32)
        rt = r.T
        for j in range(heads_per_slab):
            vt_ref[slab * heads_per_slab + j] = rt[j * HEAD_DIM:(j + 1) * HEAD_DIM, :].astype(BF16)


def _proj_rope(x, positions, w_bf16, tm):
    b, s, d = x.shape
    n = w_bf16.shape[1]
    half = HEAD_DIM // 2
    inv_freq = (ROPE_THETA ** (-jnp.arange(half, dtype=F32) / half)).reshape(half, 1)

    def heads(nh):
        return (jax.ShapeDtypeStruct((b, nh, s, HEAD_DIM), BF16),
                pl.BlockSpec((None, nh, tm, HEAD_DIM), lambda bi, i: (bi, 0, i, 0)))

    (qs, qspec), (ks, kspec) = heads(SWA_HEADS), heads(SWA_KV_HEADS)
    vs = jax.ShapeDtypeStruct((b, SWA_KV_HEADS, HEAD_DIM, s), BF16)
    vspec = pl.BlockSpec((None, SWA_KV_HEADS, HEAD_DIM, tm), lambda bi, i: (bi, 0, 0, i))
    return pl.pallas_call(
        _proj_rope_kernel,
        grid=(b, s // tm),
        in_specs=[pl.BlockSpec((None, tm, d), lambda bi, i: (bi, i, 0)),
                  pl.BlockSpec((None, 1, tm), lambda bi, i: (bi, 0, i)),
                  pl.BlockSpec((half, 1), lambda bi, i: (0, 0)),
                  pl.BlockSpec((d, n), lambda bi, i: (0, 0))],
        out_specs=[qspec, kspec, vspec],
        out_shape=[qs, ks, vs],
        compiler_params=_params("parallel", "parallel"),
        name="proj_rope",
    )(x, positions.reshape(b, 1, s), inv_freq, w_bf16)


def _reduce_rows(x, op):
    while x.shape[0] > SUBLANES:
        half = x.shape[0] // 2
        x = op(x[:half], x[half:])
    for shift in (4, 2, 1):
        x = op(x, pltpu.roll(x, shift, axis=0))
    return x[0:1]


SWA_BLOCKS_PER_STEP = 8


def _swa_kernel(q_ref, kp_ref, kc_ref, vtp_ref, vtc_ref, sink_ref, o_ref):
    i = pl.program_id(1)
    w = SWA_WINDOW
    key = lax.broadcasted_iota(jnp.int32, (2 * w, w), 0)
    qry = lax.broadcasted_iota(jnp.int32, (2 * w, w), 1)
    dist = qry + w - key
    band = (dist >= 0) & (dist < w)
    for blk in range(SWA_BLOCKS_PER_STEP):
        visible = band if blk else band & ((key >= w) | (i > 0))
        bias = jnp.concatenate([jnp.where(visible, 0.0, NEG_BIG)] * SWA_GROUP, axis=1)
        outs = []
        for kv in range(SWA_KV_HEADS):
            if blk:
                kk = kc_ref[kv, (blk - 1) * w:(blk + 1) * w, :]
                vvt = vtc_ref[kv, :, (blk - 1) * w:(blk + 1) * w]
            else:
                kk = jnp.concatenate([kp_ref[kv], kc_ref[kv, 0:w, :]], axis=0)
                vvt = jnp.concatenate([vtp_ref[kv], vtc_ref[kv, :, 0:w]], axis=1)
            qg = jnp.concatenate([q_ref[kv * SWA_GROUP + g, blk * w:(blk + 1) * w, :]
                                  for g in range(SWA_GROUP)], axis=0)
            st = lax.dot_general(kk, qg, (((1,), (1,)), ((), ())), preferred_element_type=F32) + bias
            sink = sink_ref[kv:kv + 1, :]
            m = jnp.maximum(_reduce_rows(st, jnp.maximum), sink)
            p = jnp.exp(st - m)
            denom = _reduce_rows(p, jnp.add) + jnp.exp(sink - m)
            ot = jnp.dot(vvt, p.astype(BF16), preferred_element_type=F32) / denom
            outs.extend(ot[:, g * w:(g + 1) * w] for g in range(SWA_GROUP))
        o_ref[blk * w:(blk + 1) * w, :] = jnp.concatenate(outs, axis=0).T.astype(o_ref.dtype)


def _swa(q, k, vt, sinks):
    b, nh, s, dh = q.shape
    nkv = k.shape[1]
    w = SWA_WINDOW
    n = SWA_BLOCKS_PER_STEP
    assert s % (n * w) == 0
    cur = pl.BlockSpec((None, nkv, n * w, dh), lambda bi, i: (bi, 0, i, 0))
    prev = pl.BlockSpec((None, nkv, w, dh), lambda bi, i: (bi, 0, jnp.maximum(n * i - 1, 0), 0))
    cur_t = pl.BlockSpec((None, nkv, dh, n * w), lambda bi, i: (bi, 0, 0, i))
    prev_t = pl.BlockSpec((None, nkv, dh, w), lambda bi, i: (bi, 0, 0, jnp.maximum(n * i - 1, 0)))
    sink_tile = jnp.repeat(sinks.astype(F32).reshape(nkv, nh // nkv), w, axis=1)
    return pl.pallas_call(
        _swa_kernel,
        grid=(b, s // (n * w)),
        in_specs=[pl.BlockSpec((None, nh, n * w, dh), lambda bi, i: (bi, 0, i, 0)),
                  prev, cur, prev_t, cur_t,
                  pl.BlockSpec(sink_tile.shape, lambda bi, i: (0, 0))],
        out_specs=pl.BlockSpec((None, n * w, nh * dh), lambda bi, i: (bi, i, 0)),
        out_shape=jax.ShapeDtypeStruct((b, s, nh * dh), BF16),
        compiler_params=_params("parallel", "parallel"),
        name="swa",
    )(q, k, k, vt, vt, sink_tile)


PAIRS_PER_GROUP = GROUP_SIZE * (GROUP_SIZE - 1) // 2
N_BUCKETS = N_GROUPS * PAIRS_PER_GROUP
ROUTE_ROWS = SUBLANES


def _route(logits_t):
    rows = [logits_t[e:e + 1, :] for e in range(N_EXPERTS)]
    mx = functools.reduce(jnp.maximum, rows)
    ex = [jnp.exp(r - mx) for r in rows]
    total = functools.reduce(lambda p, q: p + q, ex)
    probs = [e / total for e in ex]

    group_score = []
    for g in range(N_GROUPS):
        a, b, c, d = probs[g * GROUP_SIZE:(g + 1) * GROUP_SIZE]
        hi1, lo1 = jnp.maximum(a, b), jnp.minimum(a, b)
        hi2, lo2 = jnp.maximum(c, d), jnp.minimum(c, d)
        top1 = jnp.maximum(hi1, hi2)
        top2 = jnp.maximum(jnp.minimum(hi1, hi2), jnp.maximum(lo1, lo2))
        group_score.append(top1 + top2)
    best = functools.reduce(jnp.maximum, group_score)
    g_sel = jnp.full(best.shape, N_GROUPS - 1, jnp.int32)
    for g in range(N_GROUPS - 2, -1, -1):
        g_sel = jnp.where(group_score[g] == best, g, g_sel)

    in_group = []
    for j in range(GROUP_SIZE):
        val = probs[(N_GROUPS - 1) * GROUP_SIZE + j]
        for g in range(N_GROUPS - 2, -1, -1):
            val = jnp.where(g_sel == g, probs[g * GROUP_SIZE + j], val)
        in_group.append(val)

    def first_argmax(vals):
        m = functools.reduce(jnp.maximum, vals)
        idx = jnp.full(m.shape, GROUP_SIZE - 1, jnp.int32)
        for j in range(GROUP_SIZE - 2, -1, -1):
            idx = jnp.where(vals[j] == m, j, idx)
        return m, idx

    w1, i1 = first_argmax(in_group)
    rest = [jnp.where(i1 == j, -1.0, in_group[j]) for j in range(GROUP_SIZE)]
    w2, i2 = first_argmax(rest)
    norm = w1 + w2
    first_is_lo = i1 < i2
    i_lo = jnp.minimum(i1, i2)
    i_hi = jnp.maximum(i1, i2)
    pair = jnp.where(i_lo == 0, i_hi - 1, jnp.where(i_lo == 2, 3, jnp.where(i_hi == 3, 4, 5)))
    bucket = (g_sel * PAIRS_PER_GROUP + pair).astype(F32)
    w_lo = jnp.where(first_is_lo, w1, w2) / norm
    w_hi = jnp.where(first_is_lo, w2, w1) / norm
    return bucket, w_lo, w_hi


def _mix_out_kernel(*refs, n_head_major, alpha):
    x_ref = refs[0]
    y_refs = refs[1:1 + len(n_head_major)]
    w_refs = refs[1 + len(n_head_major):1 + 2 * len(n_head_major)]
    g_ref, b_ref, wrh_ref, wrl_ref, brt_ref, x1e_ref, route_ref = refs[1 + 2 * len(n_head_major):]
    tm, d = x_ref.shape
    h = alpha * x_ref[...]
    for y_ref, w_ref, nh in zip(y_refs, w_refs, n_head_major):
        if nh:
            y = jnp.concatenate([y_ref[j] for j in range(nh)], axis=-1)
        else:
            y = y_ref[...]
        h = h + jnp.dot(y, w_ref[...], preferred_element_type=F32)
    x1 = _layer_norm(h, g_ref[...], b_ref[...])
    x_hi = x1.astype(BF16)
    x_lo = (x1 - x_hi.astype(F32)).astype(BF16)

    def nt_dot(w, xv):
        return lax.dot_general(w, xv, (((1,), (1,)), ((), ())), preferred_element_type=F32)

    logits_t = (nt_dot(wrh_ref[...], x_hi) + nt_dot(wrh_ref[...], x_lo) + nt_dot(wrl_ref[...], x_hi)
                + brt_ref[...])
    route = jnp.concatenate(list(_route(logits_t)) + [jnp.zeros((ROUTE_ROWS - 3, tm), F32)], axis=0)
    route_ref[...] = route
    x1e_ref[:, 0:d] = x1
    x1e_ref[:, d:d + LANES] = jnp.concatenate([route, jnp.zeros((LANES - ROUTE_ROWS, tm), F32)], axis=0).T


def _mix_out(x, ys, ws, ln_g, ln_b, w_router, b_router, alpha, tm):
    b, s, d = x.shape
    n_head_major = tuple(y.shape[1] if hm else 0 for y, hm in ys)
    y_specs = []
    for (y, hm) in ys:
        if hm:
            y_specs.append(pl.BlockSpec((None, y.shape[1], tm, y.shape[3]), lambda bi, i: (bi, 0, i, 0)))
        else:
            y_specs.append(pl.BlockSpec((None, tm, y.shape[2]), lambda bi, i: (bi, i, 0)))
    w_specs = [pl.BlockSpec(w.shape, lambda bi, i: (0, 0)) for w in ws]
    row = pl.BlockSpec((1, d), lambda bi, i: (0, 0))
    tok = pl.BlockSpec((None, tm, d), lambda bi, i: (bi, i, 0))
    wr_t = w_router.T.astype(F32)
    wr_hi = wr_t.astype(BF16)
    wr_lo = (wr_t - wr_hi.astype(F32)).astype(BF16)
    wr_spec = pl.BlockSpec((N_EXPERTS, d), lambda bi, i: (0, 0))
    return pl.pallas_call(
        functools.partial(_mix_out_kernel, n_head_major=n_head_major, alpha=alpha),
        grid=(b, s // tm),
        in_specs=[tok] + y_specs + w_specs + [row, row, wr_spec, wr_spec,
                  pl.BlockSpec((N_EXPERTS, 1), lambda bi, i: (0, 0))],
        out_specs=[pl.BlockSpec((None, tm, d + LANES), lambda bi, i: (bi, i, 0)),
                   pl.BlockSpec((None, ROUTE_ROWS, tm), lambda bi, i: (bi, 0, i))],
        out_shape=[jax.ShapeDtypeStruct((b, s, d + LANES), F32),
                   jax.ShapeDtypeStruct((b, ROUTE_ROWS, s), F32)],
        compiler_params=_params("parallel", "parallel"),
        name="mix_out_ln_router",
    )(x, *[y for y, _ in ys], *ws, ln_g.reshape(1, d), ln_b.reshape(1, d),
      wr_hi, wr_lo, b_router.astype(F32).reshape(N_EXPERTS, 1))


MOE_TILE = 256
PLAN_COLS = 256
META_ROWS = SUBLANES


def _num_moe_tiles(t):
    return t // MOE_TILE + N_BUCKETS


def _plan_kernel(bid_ref, pos_ref, meta_ref):
    r, c = bid_ref.shape
    bid = bid_ref[...]
    before = (lax.broadcasted_iota(jnp.int32, (c, c), 0)
              < lax.broadcasted_iota(jnp.int32, (c, c), 1)).astype(BF16)
    rows_before = (lax.broadcasted_iota(jnp.int32, (r, r), 1)
                   < lax.broadcasted_iota(jnp.int32, (r, r), 0)).astype(BF16)
    lane = lax.broadcasted_iota(jnp.int32, (1, LANES), 1)
    tile_start = lane.astype(F32) * MOE_TILE

    def body(b, state):
        base, pos, tile_bucket, last_tile = state
        ind = (bid == lax.convert_element_type(b, F32)).astype(F32)
        within = jnp.dot(ind.astype(BF16), before, preferred_element_type=F32)
        row_total = jnp.sum(ind, axis=1, keepdims=True)
        row_off = jnp.dot(rows_before, jnp.broadcast_to(row_total, (r, LANES)).astype(BF16),
                          preferred_element_type=F32)[:, 0:1]
        count = jnp.sum(row_total, axis=0, keepdims=True)
        padded = jnp.floor((count + (MOE_TILE - 1)) * (1.0 / MOE_TILE)) * MOE_TILE
        pos = pos + ind * (base + row_off + within)
        end = base + padded
        tile_bucket = tile_bucket + (tile_start >= end).astype(F32)
        last_row = jnp.where(count > 0.0, end - MOE_TILE, -1.0)
        last_tile = jnp.where(lane == b, last_row, last_tile)
        return end, pos, tile_bucket, last_tile

    total, pos, tile_bucket, last_tile = lax.fori_loop(
        0, N_BUCKETS, body, (jnp.zeros((1, 1), F32), jnp.zeros((r, c), F32), jnp.zeros((1, LANES), F32),
                             jnp.full((1, LANES), -1.0, F32)))
    pos_ref[...] = pos.astype(jnp.int32)

    tb = jnp.minimum(tile_bucket, N_BUCKETS - 1.0)
    group = sum((tb >= g * PAIRS_PER_GROUP).astype(F32) for g in range(1, N_GROUPS))
    pair = tb - group * PAIRS_PER_GROUP
    i_lo = jnp.where(pair < 3, 0.0, jnp.where(pair == 3, 2.0, 1.0))
    i_hi = jnp.where(pair < 3, pair + 1.0, jnp.where(pair == 5, 2.0, 3.0))
    meta = jnp.concatenate([group * GROUP_SIZE + i_lo, group * GROUP_SIZE + i_hi,
                            (tile_start < total).astype(F32),
                            jnp.where(lane == N_BUCKETS, total, last_tile),
                            jnp.zeros((META_ROWS - 4, LANES), F32)], axis=0)
    meta_ref[...] = meta.astype(jnp.int32)


def _plan(bucket_ids):
    r, c = bucket_ids.shape
    return pl.pallas_call(
        _plan_kernel,
        out_shape=[jax.ShapeDtypeStruct((r, c), jnp.int32), jax.ShapeDtypeStruct((META_ROWS, LANES), jnp.int32)],
        compiler_params=pltpu.CompilerParams(vmem_limit_bytes=VMEM_LIMIT_BYTES),
        name="moe_plan",
    )(bucket_ids)


DMA_PRIORITIES = 2


def _row_copy(src_ref, src_row, dst_ref, dst_row, sem):
    return pltpu.make_async_copy(src_ref.at[pl.ds(src_row, 1)], dst_ref.at[pl.ds(dst_row, 1)], sem)


def _dispatch_kernel(pos_ref, last_ref, x_ref, o_ref, inv_ref, zeros, inv, fill, sem, zsem, isem, *,
                     first_spare, n_tokens):
    tm = x_ref.shape[0]
    i = pl.program_id(0)

    @pl.when(i == 0)
    def _():
        zeros[...] = jnp.zeros_like(zeros)
        fill[...] = jnp.full(fill.shape, n_tokens, jnp.int32)
        to_smem = pltpu.make_async_copy(fill, inv, isem)
        to_smem.start()
        rows_in_use = last_ref[N_BUCKETS]
        clears = [(last_ref[b] >= 0, last_ref[b]) for b in range(N_BUCKETS)]
        clears += [(k * MOE_TILE >= rows_in_use, k * MOE_TILE)
                   for k in range(first_spare, o_ref.shape[0] // MOE_TILE)]

        def clear(row):
            start = row if isinstance(row, int) else pl.multiple_of(row, MOE_TILE)
            return pltpu.make_async_copy(zeros, o_ref.at[pl.ds(start, MOE_TILE)], zsem)

        for needed, row in clears:
            @pl.when(needed)
            def _():
                clear(row).start()
        for needed, row in clears:
            @pl.when(needed)
            def _():
                clear(row).wait()
        to_smem.wait()

    base = i * tm
    for r in range(tm):
        dst = pos_ref[base + r]
        inv[dst] = base + r
        _row_copy(x_ref, r, o_ref, dst, sem).start(priority=r % DMA_PRIORITIES)
    pltpu.make_async_copy(x_ref, o_ref.at[pl.ds(0, tm)], sem).wait()

    @pl.when(i == pl.num_programs(0) - 1)
    def _():
        to_hbm = pltpu.make_async_copy(inv, inv_ref, isem)
        to_hbm.start()
        to_hbm.wait()


def _dispatch(pos, last_tile_rows, xt, n_rows, tm):
    t = xt.shape[0]
    tile = xt.shape[1:]
    dma_sem = pltpu.SemaphoreType.DMA(())
    return pl.pallas_call(
        functools.partial(_dispatch_kernel, first_spare=t // MOE_TILE, n_tokens=t),
        grid_spec=pltpu.PrefetchScalarGridSpec(
            num_scalar_prefetch=2,
            grid=(t // tm,),
            in_specs=[pl.BlockSpec((tm,) + tile, lambda i, pos_ref, last_ref: (i,) + (0,) * len(tile))],
            out_specs=[pl.BlockSpec(memory_space=pl.ANY), pl.BlockSpec(memory_space=pl.ANY)],
            scratch_shapes=[pltpu.VMEM((MOE_TILE,) + tile, xt.dtype), pltpu.SMEM((n_rows,), jnp.int32),
                            pltpu.VMEM((n_rows,), jnp.int32), dma_sem, dma_sem, dma_sem]),
        out_shape=[jax.ShapeDtypeStruct((n_rows,) + tile, xt.dtype),
                   jax.ShapeDtypeStruct((n_rows,), jnp.int32)],
        compiler_params=_params("arbitrary"),
        name="moe_dispatch",
    )(pos, last_tile_rows, xt)


def _moe_tile_kernel(elo_ref, ehi_ref, valid_ref, inv, x_ref, wgl_ref, wul_ref, wdl_ref, wgh_ref, wuh_ref,
                     wdh_ref, m_hbm, o_even, o_odd, ssem_even, ssem_odd, wgl_b, wul_b, wdl_b, wgh_b, wuh_b,
                     wdh_b, *, n_tokens):
    d = wgl_ref.shape[0]
    k = pl.program_id(0)
    last = pl.num_programs(0) - 1
    prev = jnp.maximum(k - 1, 0)
    valid = valid_ref[k] != 0

    def scatter(tile, r, buf, sem, to_spare):
        tok = inv[tile * MOE_TILE + r]
        dst = jnp.where(to_spare | (tok >= n_tokens), n_tokens + r, tok)
        return _row_copy(buf, r, m_hbm, dst, sem)

    def scatter_done(buf, sem):
        pltpu.make_async_copy(buf, m_hbm.at[pl.ds(0, MOE_TILE)], sem).wait()

    @pl.when(k == 0)
    def _():
        o_odd[...] = jnp.zeros_like(o_odd)

    def refresh(e_ref, srcs, dsts):
        @pl.when(valid & ((k == 0) | (e_ref[k] != e_ref[prev])))
        def _():
            for src, dst in zip(srcs, dsts):
                dst[...] = src[...].astype(BF16)

    refresh(elo_ref, (wgl_ref, wul_ref, wdl_ref), (wgl_b, wul_b, wdl_b))
    refresh(ehi_ref, (wgh_ref, wuh_ref, wdh_ref), (wgh_b, wuh_b, wdh_b))

    def tile_valid(j):
        return (j >= 0) & (valid_ref[jnp.maximum(j, 0)] != 0)

    def scattered_in(j):
        return (j == 0) | tile_valid(j) | tile_valid(j - 1)

    def step(o_cur, ssem_cur, o_prv, ssem_prv):
        @pl.when((k > 0) & scattered_in(k - 1))
        def _():
            scatter_done(o_cur, ssem_cur)

        first = k == 0

        @pl.when(valid)
        def _():
            for r in range(MOE_TILE):
                scatter(prev, r, o_prv, ssem_prv, first).start(priority=r % DMA_PRIORITIES)
            x = x_ref[:, 0:d].astype(BF16)
            acc = None
            for wg_b, wu_b, wd_b, lane in ((wgl_b, wul_b, wdl_b, d + 1), (wgh_b, wuh_b, wdh_b, d + 2)):
                weight = x_ref[:, lane:lane + 1]
                hg = jnp.dot(x, wg_b[...], preferred_element_type=F32)
                hu = jnp.dot(x, wu_b[...], preferred_element_type=F32)
                hidden = (hg * _sigmoid(hg)) * hu * weight
                y = jnp.dot(hidden.astype(BF16), wd_b[...], preferred_element_type=F32)
                acc = y if acc is None else acc + y
            o_cur[...] = acc

        @pl.when(jnp.logical_not(valid) & tile_valid(k - 1))
        def _():
            def issue(r, carry):
                scatter(prev, r, o_prv, ssem_prv, first).start()
                return carry

            lax.fori_loop(0, MOE_TILE, issue, 0, unroll=8)

        @pl.when((k == last) & scattered_in(k))
        def _():
            scatter_done(o_prv, ssem_prv)

        @pl.when((k == last) & valid)
        def _():
            def issue(r, carry):
                scatter(k, r, o_cur, ssem_cur, False).start()
                return carry

            lax.fori_loop(0, MOE_TILE, issue, 0, unroll=8)
            scatter_done(o_cur, ssem_cur)

    @pl.when(k % 2 == 0)
    def _():
        step(o_even, ssem_even, o_odd, ssem_odd)

    @pl.when(k % 2 == 1)
    def _():
        step(o_odd, ssem_odd, o_even, ssem_even)


def _moe_tiles(e_lo, e_hi, valid, inv, xs, wg, wu, wd, layer, n_tokens):
    n_rows, dw = xs.shape
    _, _, d, f = wg.shape
    up_lo = pl.BlockSpec((None, None, d, f), lambda k, lo, hi, ok, iv: (layer, lo[k], 0, 0))
    up_hi = pl.BlockSpec((None, None, d, f), lambda k, lo, hi, ok, iv: (layer, hi[k], 0, 0))
    down_lo = pl.BlockSpec((None, None, f, d), lambda k, lo, hi, ok, iv: (layer, lo[k], 0, 0))
    down_hi = pl.BlockSpec((None, None, f, d), lambda k, lo, hi, ok, iv: (layer, hi[k], 0, 0))
    up_b = pltpu.VMEM((d, f), BF16)
    down_b = pltpu.VMEM((f, d), BF16)
    o_buf = pltpu.VMEM((MOE_TILE, d), F32)
    dma_sem = pltpu.SemaphoreType.DMA(())
    return pl.pallas_call(
        functools.partial(_moe_tile_kernel, n_tokens=n_tokens),
        grid_spec=pltpu.PrefetchScalarGridSpec(
            num_scalar_prefetch=4,
            grid=(n_rows // MOE_TILE,),
            in_specs=[pl.BlockSpec((MOE_TILE, dw), lambda k, lo, hi, ok, iv: (k, 0)),
                      up_lo, up_lo, down_lo, up_hi, up_hi, down_hi],
            out_specs=pl.BlockSpec(memory_space=pl.ANY),
            scratch_shapes=[o_buf, o_buf, dma_sem, dma_sem, up_b, up_b, down_b, up_b, up_b, down_b]),
        out_shape=jax.ShapeDtypeStruct((n_tokens + MOE_TILE, d), F32),
        compiler_params=_params("arbitrary"),
        name="moe_tiles",
    )(e_lo, e_hi, valid, inv, xs, wg, wu, wd, wg, wu, wd)


def _ln_ple_kernel(x1_ref, m_ref, p_ref, g_ref, b_ref, wg_ref, bg_ref, wp_ref, o_ref, *, alpha):
    x2 = _layer_norm(alpha * x1_ref[...] + m_ref[...], g_ref[...], b_ref[...])
    gate = _sigmoid(jnp.dot(x2.astype(BF16), wg_ref[...], preferred_element_type=F32) + bg_ref[...])
    emb = jnp.dot(p_ref[...].astype(BF16), wp_ref[...], preferred_element_type=F32)
    o_ref[...] = x2 + gate * emb


def _ln_ple(x1e, m, p, layer, ln_g, ln_b, wg, bg, wp, alpha, tm):
    t = x1e.shape[0]
    d = m.shape[1]
    pd = p.shape[2]
    tok = pl.BlockSpec((tm, d), lambda i: (i, 0))
    row = pl.BlockSpec((1, d), lambda i: (0, 0))
    return pl.pallas_call(
        functools.partial(_ln_ple_kernel, alpha=alpha),
        grid=(t // tm,),
        in_specs=[tok, tok, pl.BlockSpec((None, tm, pd), lambda i: (layer, i, 0)), row, row,
                  pl.BlockSpec((d, d), lambda i: (0, 0)), row, pl.BlockSpec((pd, d), lambda i: (0, 0))],
        out_specs=tok,
        out_shape=jax.ShapeDtypeStruct((t, d), F32),
        compiler_params=_params("parallel"),
        name="ln_ple",
    )(x1e, m, p, ln_g.reshape(1, d), ln_b.reshape(1, d), wg, bg.reshape(1, d), wp)


def _pick_tile(n, target):
    t = min(n, target)
    while n % t:
        t //= 2
    return t


def kernel(x, p, positions, w_in_ab, w_out_ab, conv_w, conv_b, lru_w_r, lru_b_r, lru_w_i, lru_b_i, lru_lambda, w_qkv_c, w_out_c, sinks_c, ln_mix_g, ln_mix_b, ln_ffn_g, ln_ffn_b, w_router, b_router, exp_w_gate, exp_w_up, exp_w_down, ple_w_proj, ple_w_gate, ple_b_gate):
    b, s, d = x.shape
    depth = p.shape[0]
    t = b * s
    alpha = (2 * depth) ** 0.25
    tm = _pick_tile(s, 1024)
    assert t % MOE_TILE == 0 and t % PLAN_COLS == 0
    n_tiles = _num_moe_tiles(t)
    assert n_tiles <= LANES
    for i in range(depth):
        j = i // 2
        if i % 2 == 0:
            q, k, v, xr, gr = _proj_ab(x, w_in_ab[j].astype(BF16), tm)
            y_sb = _sb_attention(q, k, v, _pick_tile(s, 256), SB_HEADS)
            y_lru = _lru(xr, gr, conv_w[j], conv_b[j], _block_diag(lru_w_r[j]).astype(BF16), lru_b_r[j],
                         _block_diag(lru_w_i[j]).astype(BF16), lru_b_i[j], lru_lambda[j],
                         _pick_tile(s, 256))
            w_out = w_out_ab[j].astype(BF16)
            ys = [(y_sb, True), (y_lru, False)]
            ws = [w_out[:SB_WIDTH], w_out[SB_WIDTH:]]
        else:
            q, k, v = _proj_rope(x, positions, w_qkv_c[j].astype(BF16), tm)
            y = _swa(q, k, v, sinks_c[j])
            ys = [(y, False)]
            ws = [w_out_c[j].astype(BF16)]
        x1e, route = _mix_out(x, ys, ws, ln_mix_g[i], ln_mix_b[i], w_router, b_router, alpha, tm)
        x1e = x1e.reshape(t, d + LANES)
        pos, meta = _plan(route[:, 0, :].reshape(t // PLAN_COLS, PLAN_COLS))
        pos = pos.reshape(t)
        xs, inv = _dispatch(pos, meta[3, :N_BUCKETS + 1], x1e, n_tiles * MOE_TILE, _pick_tile(t, 1024))
        m = _moe_tiles(meta[0, :n_tiles], meta[1, :n_tiles], meta[2, :n_tiles], inv, xs,
                       exp_w_gate, exp_w_up, exp_w_down, i, t)
        x = _ln_ple(x1e, m, p.reshape(depth, t, -1), i, ln_ffn_g[i], ln_ffn_b[i],
                    ple_w_gate[i].astype(BF16), ple_b_gate[i], ple_w_proj[i].astype(BF16), alpha,
                    _pick_tile(t, 1024)).reshape(b, s, d)
    return x
```

```python
import functools
import math

import jax
import jax.numpy as jnp
from jax import lax
from jax.experimental import pallas as pl
from jax.experimental.pallas import tpu as pltpu

HEAD_DIM = 64
SB_HEADS = 8
SB_WIDTH = SB_HEADS * HEAD_DIM
LRU_WIDTH = 512
LRU_BLOCKS = 8
LRU_C = 8.0
CONV_WIDTH = 4
SWA_HEADS = 16
SWA_KV_HEADS = 4
SWA_GROUP = SWA_HEADS // SWA_KV_HEADS
SWA_WINDOW = 128
ROPE_THETA = 10000.0
N_EXPERTS = 16
N_GROUPS = 4
GROUP_SIZE = N_EXPERTS // N_GROUPS
LN_EPS = 1e-5
Q_SCALE = HEAD_DIM ** -0.5

LANES = 128
SUBLANES = 8
VMEM_LIMIT_BYTES = 48 * 1024 * 1024

NEG_BIG = -1e30

BF16 = jnp.bfloat16
F32 = jnp.float32


def _params(*semantics):
    return pltpu.CompilerParams(dimension_semantics=semantics, vmem_limit_bytes=VMEM_LIMIT_BYTES)


def _softplus(z):
    return jnp.maximum(z, 0.0) + jnp.log(1.0 + jnp.exp(-jnp.abs(z)))


def _sigmoid(z):
    return 1.0 / (1.0 + jnp.exp(-z))


def _layer_norm(y, g, b):
    mu = jnp.mean(y, axis=-1, keepdims=True)
    d = y - mu
    var = jnp.mean(d * d, axis=-1, keepdims=True)
    return d * lax.rsqrt(var + LN_EPS) * g + b


def _proj_ab_kernel(x_ref, w_ref, q_ref, k_ref, v_ref, xr_ref, gr_ref):
    xb = x_ref[...].astype(BF16)

    def chunk(c):
        return jnp.dot(xb, w_ref[:, c * SB_WIDTH:(c + 1) * SB_WIDTH], preferred_element_type=F32)

    for c, (ref, scale) in enumerate(((q_ref, Q_SCALE), (k_ref, None), (v_ref, None))):
        r = chunk(c)
        if scale is not None:
            r = r * scale
        for h in range(SB_HEADS):
            ref[h] = r[:, h * HEAD_DIM:(h + 1) * HEAD_DIM].astype(BF16)
    xr_ref[...] = chunk(3)
    gr_ref[...] = chunk(4)


def _proj_ab(x, w_bf16, tm):
    b, s, d = x.shape
    n = w_bf16.shape[1]
    heads = jax.ShapeDtypeStruct((b, SB_HEADS, s, HEAD_DIM), BF16)
    flat = jax.ShapeDtypeStruct((b, s, LRU_WIDTH), F32)
    head_spec = pl.BlockSpec((None, SB_HEADS, tm, HEAD_DIM), lambda bi, i: (bi, 0, i, 0))
    flat_spec = pl.BlockSpec((None, tm, LRU_WIDTH), lambda bi, i: (bi, i, 0))
    return pl.pallas_call(
        _proj_ab_kernel,
        grid=(b, s // tm),
        in_specs=[pl.BlockSpec((None, tm, d), lambda bi, i: (bi, i, 0)),
                  pl.BlockSpec((d, n), lambda bi, i: (0, 0))],
        out_specs=[head_spec, head_spec, head_spec, flat_spec, flat_spec],
        out_shape=[heads, heads, heads, flat, flat],
        compiler_params=_params("parallel", "parallel"),
        name="proj_ab",
    )(x, w_bf16)


SB_DEAD_LOG_WEIGHT = -105.0
SB_MERGED_BLOCKS = 2


def _sb_attn_kernel(q_ref, k_ref, v_ref, o_ref, *, tq, hp):
    i = pl.program_id(2)
    row = lax.broadcasted_iota(jnp.int32, (tq, tq), 0)
    col = lax.broadcasted_iota(jnp.int32, (tq, tq), 1)
    minus_later = jnp.where(row > col, -1.0, 0.0).astype(BF16)
    causal = col < row

    def block(jb, carries, accs, masked):
        start = pl.multiple_of(jb * tq, tq)
        new_carries, new_accs = [], []
        for h in range(hp):
            kj = k_ref[h, pl.ds(start, tq), :]
            vj = v_ref[h, pl.ds(start, tq), :]
            z = lax.dot_general(q_ref[h], kj, (((1,), (1,)), ((), ())), preferred_element_type=F32)
            sp = _softplus(z)
            cost = jnp.where(causal, sp, 0.0) if masked else sp
            after = jnp.dot(cost.astype(BF16), minus_later, preferred_element_type=F32)
            w = jnp.exp((z - sp) + after + carries[h])
            if masked:
                w = jnp.where(causal, w, 0.0)
            new_accs.append(accs[h] + jnp.dot(w.astype(BF16), vj, preferred_element_type=F32))
            new_carries.append(carries[h] - jnp.sum(cost, axis=1, keepdims=True))
        return tuple(new_carries), tuple(new_accs)

    def live(carries):
        return functools.reduce(jnp.maximum, [jnp.max(c) for c in carries])

    zero = ((jnp.zeros((tq, 1), F32),) * hp, (jnp.zeros((tq, HEAD_DIM), F32),) * hp)

    def first(n):
        def run():
            state = block(i, *zero, True)
            for j in range(1, n):
                state = block(i - j, *state, False)
            return state
        return run

    done = jnp.minimum(i, SB_MERGED_BLOCKS - 1)
    carries, accs = lax.switch(done, [first(n) for n in range(1, SB_MERGED_BLOCKS + 1)])

    def cond(state):
        return (state[0] < i) & (state[1] > SB_DEAD_LOG_WEIGHT)

    def body(state):
        step, _, carries, accs = state
        carries, accs = block(i - 1 - step, carries, accs, False)
        return step + 1, live(carries), carries, accs

    _, _, _, accs = lax.while_loop(cond, body, (done, live(carries), carries, accs))
    for h in range(hp):
        o_ref[h] = accs[h].astype(o_ref.dtype)


def _sb_attention(q, k, v, tq, hp):
    b, h, s, dh = q.shape
    return pl.pallas_call(
        functools.partial(_sb_attn_kernel, tq=tq, hp=hp),
        grid=(b, h // hp, s // tq),
        in_specs=[pl.BlockSpec((None, hp, tq, dh), lambda bi, hi, i: (bi, hi, i, 0)),
                  pl.BlockSpec((None, hp, s, dh), lambda bi, hi, i: (bi, hi, 0, 0), pipeline_mode=pl.Buffered(1)),
                  pl.BlockSpec((None, hp, s, dh), lambda bi, hi, i: (bi, hi, 0, 0), pipeline_mode=pl.Buffered(1))],
        out_specs=pl.BlockSpec((None, hp, tq, dh), lambda bi, hi, i: (bi, hi, i, 0)),
        out_shape=jax.ShapeDtypeStruct((b, h, s, dh), BF16),
        compiler_params=_params("parallel", "parallel", "parallel"),
        name="sb_attention",
    )(q, k, v)


def _gelu_tanh(x):
    return 0.5 * x * (1.0 + jnp.tanh(math.sqrt(2.0 / math.pi) * (x + 0.044715 * (x * x * x))))


def _lru_kernel(xr_ref, gr_ref, cw_ref, cb_ref, wr_ref, br_ref, wi_ref, bi_ref, lam_ref, y_ref,
                xbuf, hprev, *, ts):
    @pl.when(pl.program_id(1) == 0)
    def _():
        xbuf[0:SUBLANES, :] = jnp.zeros((SUBLANES, LRU_WIDTH), F32)
        hprev[...] = jnp.zeros_like(hprev)

    xbuf[SUBLANES:SUBLANES + ts, :] = xr_ref[...]
    xc = cb_ref[...] + cw_ref[CONV_WIDTH - 1:CONV_WIDTH, :] * xbuf[SUBLANES:SUBLANES + ts, :]
    for kk in range(CONV_WIDTH - 1):
        off = SUBLANES - (CONV_WIDTH - 1) + kk
        xc = xc + cw_ref[kk:kk + 1, :] * xbuf[off:off + ts, :]
    xbuf[0:SUBLANES, :] = xbuf[ts:ts + SUBLANES, :]

    xcb = xc.astype(BF16)
    r = _sigmoid(jnp.dot(xcb, wr_ref[...], preferred_element_type=F32) + br_ref[...])
    gi = _sigmoid(jnp.dot(xcb, wi_ref[...], preferred_element_type=F32) + bi_ref[...])
    log_a = (-LRU_C) * r * _softplus(-lam_ref[...])
    a = jnp.exp(log_a)
    u = jnp.sqrt(1.0 - a * a) * (gi * xc)

    row = lax.broadcasted_iota(jnp.int32, (ts, LRU_WIDTH), 0)
    d = 1
    while d < ts:
        if d < SUBLANES:
            keep = row >= d
            a_sh = jnp.where(keep, pltpu.roll(a, d, axis=0), 1.0)
            u_sh = jnp.where(keep, pltpu.roll(u, d, axis=0), 0.0)
            u = a * u_sh + u
            a = a * a_sh
        else:
            u = jnp.concatenate([u[:d], a[d:] * u[:ts - d] + u[d:]], axis=0)
            a = jnp.concatenate([a[:d], a[d:] * a[:ts - d]], axis=0)
        d *= 2
    h = a * hprev[0:1, :] + u
    hprev[...] = jnp.broadcast_to(h[ts - 1:ts, :], hprev.shape)
    y_ref[...] = (_gelu_tanh(gr_ref[...]) * h).astype(y_ref.dtype)


def _lru(xr, gr, conv_w, conv_b, wr_bd, b_r, wi_bd, b_i, lam, ts):
    b, s, w = xr.shape
    seq_spec = pl.BlockSpec((None, ts, w), lambda bi, i: (bi, i, 0))

    def full(shape):
        return pl.BlockSpec(shape, lambda bi, i: (0,) * len(shape))

    return pl.pallas_call(
        functools.partial(_lru_kernel, ts=ts),
        grid=(b, s // ts),
        in_specs=[seq_spec, seq_spec, full((CONV_WIDTH, w)), full((1, w)), full((w, w)), full((1, w)),
                  full((w, w)), full((1, w)), full((1, w))],
        out_specs=seq_spec,
        out_shape=jax.ShapeDtypeStruct((b, s, w), BF16),
        scratch_shapes=[pltpu.VMEM((ts + 2 * SUBLANES, w), F32), pltpu.VMEM((SUBLANES, w), F32)],
        compiler_params=_params("parallel", "arbitrary"),
        name="rg_lru",
    )(xr, gr, conv_w, conv_b.reshape(1, w), wr_bd, b_r.reshape(1, w), wi_bd, b_i.reshape(1, w),
      lam.reshape(1, w))


def _block_diag(w):
    n, c, d = w.shape
    eye = jnp.eye(n, dtype=w.dtype)
    return (eye[:, None, :, None] * w[:, :, None, :]).reshape(n * c, n * d)


def _proj_rope_kernel(x_ref, pos_ref, freq_ref, w_ref, q_ref, k_ref, vt_ref):
    tm = x_ref.shape[0]
    xb = x_ref[...].astype(BF16)
    ang_t = freq_ref[...] * pos_ref[...].astype(F32)
    reps = LANES // (HEAD_DIM // 2)
    cos = jnp.concatenate([jnp.cos(ang_t)] * reps, axis=0).T
    sin = jnp.concatenate([jnp.sin(ang_t)] * reps, axis=0).T
    lane = lax.broadcasted_iota(jnp.int32, (tm, LANES), 1)
    first_half = (lane % HEAD_DIM) < (HEAD_DIM // 2)
    heads_per_slab = LANES // HEAD_DIM

    def rope(r):
        upper = pltpu.roll(r, LANES - HEAD_DIM // 2, axis=1)
        lower = pltpu.roll(r, HEAD_DIM // 2, axis=1)
        return r * cos + jnp.where(first_half, -upper, lower) * sin

    def emit(ref, n_heads, col0, rotary, scale):
        for slab in range(n_heads // heads_per_slab):
            c0 = col0 + slab * LANES
            r = jnp.dot(xb, w_ref[:, c0:c0 + LANES], preferred_element_type=F32)
            if rotary:
                r = rope(r)
            if scale is not None:
                r = r * scale
            for j in range(heads_per_slab):
                ref[slab * heads_per_slab + j] = r[:, j * HEAD_DIM:(j + 1) * HEAD_DIM].astype(BF16)

    emit(q_ref, SWA_HEADS, 0, True, Q_SCALE)
    emit(k_ref, SWA_KV_HEADS, SWA_HEADS * HEAD_DIM, True, None)
    v0 = (SWA_HEADS + SWA_KV_HEADS) * HEAD_DIM
    for slab in range(SWA_KV_HEADS // heads_per_slab):
        r = jnp.dot(xb, w_ref[:, v0 + slab * LANES:v0 + (slab + 1) * LANES], preferred_element_type=F32)
        rt = r.T
        for j in range(heads_per_slab):
            vt_ref[slab * heads_per_slab + j] = rt[j * HEAD_DIM:(j + 1) * HEAD_DIM, :].astype(BF16)


def _proj_rope(x, positions, w_bf16, tm):
    b, s, d = x.shape
    n = w_bf16.shape[1]
    half = HEAD_DIM // 2
    inv_freq = (ROPE_THETA ** (-jnp.arange(half, dtype=F32) / half)).reshape(half, 1)

    def heads(nh):
        return (jax.ShapeDtypeStruct((b, nh, s, HEAD_DIM), BF16),
                pl.BlockSpec((None, nh, tm, HEAD_DIM), lambda bi, i: (bi, 0, i, 0)))

    (qs, qspec), (ks, kspec) = heads(SWA_HEADS), heads(SWA_KV_HEADS)
    vs = jax.ShapeDtypeStruct((b, SWA_KV_HEADS, HEAD_DIM, s), BF16)
    vspec = pl.BlockSpec((None, SWA_KV_HEADS, HEAD_DIM, tm), lambda bi, i: (bi, 0, 0, i))
    return pl.pallas_call(
        _proj_rope_kernel,
        grid=(b, s // tm),
        in_specs=[pl.BlockSpec((None, tm, d), lambda bi, i: (bi, i, 0)),
                  pl.BlockSpec((None, 1, tm), lambda bi, i: (bi, 0, i)),
                  pl.BlockSpec((half, 1), lambda bi, i: (0, 0)),
                  pl.BlockSpec((d, n), lambda bi, i: (0, 0))],
        out_specs=[qspec, kspec, vspec],
        out_shape=[qs, ks, vs],
        compiler_params=_params("parallel", "parallel"),
        name="proj_rope",
    )(x, positions.reshape(b, 1, s), inv_freq, w_bf16)


def _reduce_rows(x, op):
    while x.shape[0] > SUBLANES:
        half = x.shape[0] // 2
        x = op(x[:half], x[half:])
    for shift in (4, 2, 1):
        x = op(x, pltpu.roll(x, shift, axis=0))
    return x[0:1]


SWA_BLOCKS_PER_STEP = 8


def _swa_kernel(q_ref, kp_ref, kc_ref, vtp_ref, vtc_ref, sink_ref, o_ref):
    i = pl.program_id(1)
    w = SWA_WINDOW
    key = lax.broadcasted_iota(jnp.int32, (2 * w, w), 0)
    qry = lax.broadcasted_iota(jnp.int32, (2 * w, w), 1)
    dist = qry + w - key
    band = (dist >= 0) & (dist < w)
    for blk in range(SWA_BLOCKS_PER_STEP):
        visible = band if blk else band & ((key >= w) | (i > 0))
        bias = jnp.concatenate([jnp.where(visible, 0.0, NEG_BIG)] * SWA_GROUP, axis=1)
        outs = []
        for kv in range(SWA_KV_HEADS):
            if blk:
                kk = kc_ref[kv, (blk - 1) * w:(blk + 1) * w, :]
                vvt = vtc_ref[kv, :, (blk - 1) * w:(blk + 1) * w]
            else:
                kk = jnp.concatenate([kp_ref[kv], kc_ref[kv, 0:w, :]], axis=0)
                vvt = jnp.concatenate([vtp_ref[kv], vtc_ref[kv, :, 0:w]], axis=1)
            qg = jnp.concatenate([q_ref[kv * SWA_GROUP + g, blk * w:(blk + 1) * w, :]
                                  for g in range(SWA_GROUP)], axis=0)
            st = lax.dot_general(kk, qg, (((1,), (1,)), ((), ())), preferred_element_type=F32) + bias
            sink = sink_ref[kv:kv + 1, :]
            m = jnp.maximum(_reduce_rows(st, jnp.maximum), sink)
            p = jnp.exp(st - m)
            denom = _reduce_rows(p, jnp.add) + jnp.exp(sink - m)
            ot = jnp.dot(vvt, p.astype(BF16), preferred_element_type=F32) / denom
            outs.extend(ot[:, g * w:(g + 1) * w] for g in range(SWA_GROUP))
        o_ref[blk * w:(blk + 1) * w, :] = jnp.concatenate(outs, axis=0).T.astype(o_ref.dtype)


def _swa(q, k, vt, sinks):
    b, nh, s, dh = q.shape
    nkv = k.shape[1]
    w = SWA_WINDOW
    n = SWA_BLOCKS_PER_STEP
    assert s % (n * w) == 0
    cur = pl.BlockSpec((None, nkv, n * w, dh), lambda bi, i: (bi, 0, i, 0))
    prev = pl.BlockSpec((None, nkv, w, dh), lambda bi, i: (bi, 0, jnp.maximum(n * i - 1, 0), 0))
    cur_t = pl.BlockSpec((None, nkv, dh, n * w), lambda bi, i: (bi, 0, 0, i))
    prev_t = pl.BlockSpec((None, nkv, dh, w), lambda bi, i: (bi, 0, 0, jnp.maximum(n * i - 1, 0)))
    sink_tile = jnp.repeat(sinks.astype(F32).reshape(nkv, nh // nkv), w, axis=1)
    return pl.pallas_call(
        _swa_kernel,
        grid=(b, s // (n * w)),
        in_specs=[pl.BlockSpec((None, nh, n * w, dh), lambda bi, i: (bi, 0, i, 0)),
                  prev, cur, prev_t, cur_t,
                  pl.BlockSpec(sink_tile.shape, lambda bi, i: (0, 0))],
        out_specs=pl.BlockSpec((None, n * w, nh * dh), lambda bi, i: (bi, i, 0)),
        out_shape=jax.ShapeDtypeStruct((b, s, nh * dh), BF16),
        compiler_params=_params("parallel", "parallel"),
        name="swa",
    )(q, k, k, vt, vt, sink_tile)


PAIRS_PER_GROUP = GROUP_SIZE * (GROUP_SIZE - 1) // 2
N_BUCKETS = N_GROUPS * PAIRS_PER_GROUP
ROUTE_ROWS = SUBLANES


def _route(logits_t):
    rows = [logits_t[e:e + 1, :] for e in range(N_EXPERTS)]
    mx = functools.reduce(jnp.maximum, rows)
    ex = [jnp.exp(r - mx) for r in rows]
    total = functools.reduce(lambda p, q: p + q, ex)
    probs = [e / total for e in ex]

    group_score = []
    for g in range(N_GROUPS):
        a, b, c, d = probs[g * GROUP_SIZE:(g + 1) * GROUP_SIZE]
        hi1, lo1 = jnp.maximum(a, b), jnp.minimum(a, b)
        hi2, lo2 = jnp.maximum(c, d), jnp.minimum(c, d)
        top1 = jnp.maximum(hi1, hi2)
        top2 = jnp.maximum(jnp.minimum(hi1, hi2), jnp.maximum(lo1, lo2))
        group_score.append(top1 + top2)
    best = functools.reduce(jnp.maximum, group_score)
    g_sel = jnp.full(best.shape, N_GROUPS - 1, jnp.int32)
    for g in range(N_GROUPS - 2, -1, -1):
        g_sel = jnp.where(group_score[g] == best, g, g_sel)

    in_group = []
    for j in range(GROUP_SIZE):
        val = probs[(N_GROUPS - 1) * GROUP_SIZE + j]
        for g in range(N_GROUPS - 2, -1, -1):
            val = jnp.where(g_sel == g, probs[g * GROUP_SIZE + j], val)
        in_group.append(val)

    def first_argmax(vals):
        m = functools.reduce(jnp.maximum, vals)
        idx = jnp.full(m.shape, GROUP_SIZE - 1, jnp.int32)
        for j in range(GROUP_SIZE - 2, -1, -1):
            idx = jnp.where(vals[j] == m, j, idx)
        return m, idx

    w1, i1 = first_argmax(in_group)
    rest = [jnp.where(i1 == j, -1.0, in_group[j]) for j in range(GROUP_SIZE)]
    w2, i2 = first_argmax(rest)
    norm = w1 + w2
    first_is_lo = i1 < i2
    i_lo = jnp.minimum(i1, i2)
    i_hi = jnp.maximum(i1, i2)
    pair = jnp.where(i_lo == 0, i_hi - 1, jnp.where(i_lo == 2, 3, jnp.where(i_hi == 3, 4, 5)))
    bucket = (g_sel * PAIRS_PER_GROUP + pair).astype(F32)
    w_lo = jnp.where(first_is_lo, w1, w2) / norm
    w_hi = jnp.where(first_is_lo, w2, w1) / norm
    return bucket, w_lo, w_hi


def _mix_out_kernel(*refs, n_head_major, alpha):
    x_ref = refs[0]
    y_refs = refs[1:1 + len(n_head_major)]
    w_refs = refs[1 + len(n_head_major):1 + 2 * len(n_head_major)]
    g_ref, b_ref, wrh_ref, wrl_ref, brt_ref, x1e_ref, route_ref = refs[1 + 2 * len(n_head_major):]
    tm, d = x_ref.shape
    h = alpha * x_ref[...]
    for y_ref, w_ref, nh in zip(y_refs, w_refs, n_head_major):
        if nh:
            y = jnp.concatenate([y_ref[j] for j in range(nh)], axis=-1)
        else:
            y = y_ref[...]
        h = h + jnp.dot(y, w_ref[...], preferred_element_type=F32)
    x1 = _layer_norm(h, g_ref[...], b_ref[...])
    x_hi = x1.astype(BF16)
    x_lo = (x1 - x_hi.astype(F32)).astype(BF16)

    def nt_dot(w, xv):
        return lax.dot_general(w, xv, (((1,), (1,)), ((), ())), preferred_element_type=F32)

    logits_t = (nt_dot(wrh_ref[...], x_hi) + nt_dot(wrh_ref[...], x_lo) + nt_dot(wrl_ref[...], x_hi)
                + brt_ref[...])
    route = jnp.concatenate(list(_route(logits_t)) + [jnp.zeros((ROUTE_ROWS - 3, tm), F32)], axis=0)
    route_ref[...] = route
    x1e_ref[:, 0:d] = x1
    x1e_ref[:, d:d + LANES] = jnp.concatenate([route, jnp.zeros((LANES - ROUTE_ROWS, tm), F32)], axis=0).T


def _mix_out(x, ys, ws, ln_g, ln_b, w_router, b_router, alpha, tm):
    b, s, d = x.shape
    n_head_major = tuple(y.shape[1] if hm else 0 for y, hm in ys)
    y_specs = []
    for (y, hm) in ys:
        if hm:
            y_specs.append(pl.BlockSpec((None, y.shape[1], tm, y.shape[3]), lambda bi, i: (bi, 0, i, 0)))
        else:
            y_specs.append(pl.BlockSpec((None, tm, y.shape[2]), lambda bi, i: (bi, i, 0)))
    w_specs = [pl.BlockSpec(w.shape, lambda bi, i: (0, 0)) for w in ws]
    row = pl.BlockSpec((1, d), lambda bi, i: (0, 0))
    tok = pl.BlockSpec((None, tm, d), lambda bi, i: (bi, i, 0))
    wr_t = w_router.T.astype(F32)
    wr_hi = wr_t.astype(BF16)
    wr_lo = (wr_t - wr_hi.astype(F32)).astype(BF16)
    wr_spec = pl.BlockSpec((N_EXPERTS, d), lambda bi, i: (0, 0))
    return pl.pallas_call(
        functools.partial(_mix_out_kernel, n_head_major=n_head_major, alpha=alpha),
        grid=(b, s // tm),
        in_specs=[tok] + y_specs + w_specs + [row, row, wr_spec, wr_spec,
                  pl.BlockSpec((N_EXPERTS, 1), lambda bi, i: (0, 0))],
        out_specs=[pl.BlockSpec((None, tm, d + LANES), lambda bi, i: (bi, i, 0)),
                   pl.BlockSpec((None, ROUTE_ROWS, tm), lambda bi, i: (bi, 0, i))],
        out_shape=[jax.ShapeDtypeStruct((b, s, d + LANES), F32),
                   jax.ShapeDtypeStruct((b, ROUTE_ROWS, s), F32)],
        compiler_params=_params("parallel", "parallel"),
        name="mix_out_ln_router",
    )(x, *[y for y, _ in ys], *ws, ln_g.reshape(1, d), ln_b.reshape(1, d),
      wr_hi, wr_lo, b_router.astype(F32).reshape(N_EXPERTS, 1))


MOE_TILE = 256
PLAN_COLS = 256
META_ROWS = SUBLANES


def _num_moe_tiles(t):
    return t // MOE_TILE + N_BUCKETS


def _plan_kernel(bid_ref, pos_ref, meta_ref):
    r, c = bid_ref.shape
    bid = bid_ref[...]
    before = (lax.broadcasted_iota(jnp.int32, (c, c), 0)
              < lax.broadcasted_iota(jnp.int32, (c, c), 1)).astype(BF16)
    rows_before = (lax.broadcasted_iota(jnp.int32, (r, r), 1)
                   < lax.broadcasted_iota(jnp.int32, (r, r), 0)).astype(BF16)
    lane = lax.broadcasted_iota(jnp.int32, (1, LANES), 1)
    tile_start = lane.astype(F32) * MOE_TILE

    def body(b, state):
        base, pos, tile_bucket, last_tile = state
        ind = (bid == lax.convert_element_type(b, F32)).astype(F32)
        within = jnp.dot(ind.astype(BF16), before, preferred_element_type=F32)
        row_total = jnp.sum(ind, axis=1, keepdims=True)
        row_off = jnp.dot(rows_before, jnp.broadcast_to(row_total, (r, LANES)).astype(BF16),
                          preferred_element_type=F32)[:, 0:1]
        count = jnp.sum(row_total, axis=0, keepdims=True)
        padded = jnp.floor((count + (MOE_TILE - 1)) * (1.0 / MOE_TILE)) * MOE_TILE
        pos = pos + ind * (base + row_off + within)
        end = base + padded
        tile_bucket = tile_bucket + (tile_start >= end).astype(F32)
        last_row = jnp.where(count > 0.0, end - MOE_TILE, -1.0)
        last_tile = jnp.where(lane == b, last_row, last_tile)
        return end, pos, tile_bucket, last_tile

    total, pos, tile_bucket, last_tile = lax.fori_loop(
        0, N_BUCKETS, body, (jnp.zeros((1, 1), F32), jnp.zeros((r, c), F32), jnp.zeros((1, LANES), F32),
                             jnp.full((1, LANES), -1.0, F32)))
    pos_ref[...] = pos.astype(jnp.int32)

    tb = jnp.minimum(tile_bucket, N_BUCKETS - 1.0)
    group = sum((tb >= g * PAIRS_PER_GROUP).astype(F32) for g in range(1, N_GROUPS))
    pair = tb - group * PAIRS_PER_GROUP
    i_lo = jnp.where(pair < 3, 0.0, jnp.where(pair == 3, 2.0, 1.0))
    i_hi = jnp.where(pair < 3, pair + 1.0, jnp.where(pair == 5, 2.0, 3.0))
    meta = jnp.concatenate([group * GROUP_SIZE + i_lo, group * GROUP_SIZE + i_hi,
                            (tile_start < total).astype(F32),
                            jnp.where(lane == N_BUCKETS, total, last_tile),
                            jnp.zeros((META_ROWS - 4, LANES), F32)], axis=0)
    meta_ref[...] = meta.astype(jnp.int32)


def _plan(bucket_ids):
    r, c = bucket_ids.shape
    return pl.pallas_call(
        _plan_kernel,
        out_shape=[jax.ShapeDtypeStruct((r, c), jnp.int32), jax.ShapeDtypeStruct((META_ROWS, LANES), jnp.int32)],
        compiler_params=pltpu.CompilerParams(vmem_limit_bytes=VMEM_LIMIT_BYTES),
        name="moe_plan",
    )(bucket_ids)


DMA_PRIORITIES = 2


def _row_copy(src_ref, src_row, dst_ref, dst_row, sem):
    return pltpu.make_async_copy(src_ref.at[pl.ds(src_row, 1)], dst_ref.at[pl.ds(dst_row, 1)], sem)


def _dispatch_kernel(pos_ref, last_ref, x_ref, o_ref, inv_ref, zeros, inv, fill, sem, zsem, isem, *,
                     first_spare, n_tokens):
    tm = x_ref.shape[0]
    i = pl.program_id(0)

    @pl.when(i == 0)
    def _():
        zeros[...] = jnp.zeros_like(zeros)
        fill[...] = jnp.full(fill.shape, n_tokens, jnp.int32)
        to_smem = pltpu.make_async_copy(fill, inv, isem)
        to_smem.start()
        rows_in_use = last_ref[N_BUCKETS]
        clears = [(last_ref[b] >= 0, last_ref[b]) for b in range(N_BUCKETS)]
        clears += [(k * MOE_TILE >= rows_in_use, k * MOE_TILE)
                   for k in range(first_spare, o_ref.shape[0] // MOE_TILE)]

        def clear(row):
            start = row if isinstance(row, int) else pl.multiple_of(row, MOE_TILE)
            return pltpu.make_async_copy(zeros, o_ref.at[pl.ds(start, MOE_TILE)], zsem)

        for needed, row in clears:
            @pl.when(needed)
            def _():
                clear(row).start()
        for needed, row in clears:
            @pl.when(needed)
            def _():
                clear(row).wait()
        to_smem.wait()

    base = i * tm
    for r in range(tm):
        dst = pos_ref[base + r]
        inv[dst] = base + r
        _row_copy(x_ref, r, o_ref, dst, sem).start(priority=r % DMA_PRIORITIES)
    pltpu.make_async_copy(x_ref, o_ref.at[pl.ds(0, tm)], sem).wait()

    @pl.when(i == pl.num_programs(0) - 1)
    def _():
        to_hbm = pltpu.make_async_copy(inv, inv_ref, isem)
        to_hbm.start()
        to_hbm.wait()


def _dispatch(pos, last_tile_rows, xt, n_rows, tm):
    t = xt.shape[0]
    tile = xt.shape[1:]
    dma_sem = pltpu.SemaphoreType.DMA(())
    return pl.pallas_call(
        functools.partial(_dispatch_kernel, first_spare=t // MOE_TILE, n_tokens=t),
        grid_spec=pltpu.PrefetchScalarGridSpec(
            num_scalar_prefetch=2,
            grid=(t // tm,),
            in_specs=[pl.BlockSpec((tm,) + tile, lambda i, pos_ref, last_ref: (i,) + (0,) * len(tile))],
            out_specs=[pl.BlockSpec(memory_space=pl.ANY), pl.BlockSpec(memory_space=pl.ANY)],
            scratch_shapes=[pltpu.VMEM((MOE_TILE,) + tile, xt.dtype), pltpu.SMEM((n_rows,), jnp.int32),
                            pltpu.VMEM((n_rows,), jnp.int32), dma_sem, dma_sem, dma_sem]),
        out_shape=[jax.ShapeDtypeStruct((n_rows,) + tile, xt.dtype),
                   jax.ShapeDtypeStruct((n_rows,), jnp.int32)],
        compiler_params=_params("arbitrary"),
        name="moe_dispatch",
    )(pos, last_tile_rows, xt)


def _moe_tile_kernel(elo_ref, ehi_ref, valid_ref, inv, x_ref, wgl_ref, wul_ref, wdl_ref, wgh_ref, wuh_ref,
                     wdh_ref, m_hbm, o_even, o_odd, ssem_even, ssem_odd, wgl_b, wul_b, wdl_b, wgh_b, wuh_b,
                     wdh_b, *, n_tokens):
    d = wgl_ref.shape[0]
    k = pl.program_id(0)
    last = pl.num_programs(0) - 1
    prev = jnp.maximum(k - 1, 0)
    valid = valid_ref[k] != 0

    def scatter(tile, r, buf, sem, to_spare):
        tok = inv[tile * MOE_TILE + r]
        dst = jnp.where(to_spare | (tok >= n_tokens), n_tokens + r, tok)
        return _row_copy(buf, r, m_hbm, dst, sem)

    def scatter_done(buf, sem):
        pltpu.make_async_copy(buf, m_hbm.at[pl.ds(0, MOE_TILE)], sem).wait()

    @pl.when(k == 0)
    def _():
        o_odd[...] = jnp.zeros_like(o_odd)

    def refresh(e_ref, srcs, dsts):
        @pl.when(valid & ((k == 0) | (e_ref[k] != e_ref[prev])))
        def _():
            for src, dst in zip(srcs, dsts):
                dst[...] = src[...].astype(BF16)

    refresh(elo_ref, (wgl_ref, wul_ref, wdl_ref), (wgl_b, wul_b, wdl_b))
    refresh(ehi_ref, (wgh_ref, wuh_ref, wdh_ref), (wgh_b, wuh_b, wdh_b))

    def tile_valid(j):
        return (j >= 0) & (valid_ref[jnp.maximum(j, 0)] != 0)

    def scattered_in(j):
        return (j == 0) | tile_valid(j) | tile_valid(j - 1)

    def step(o_cur, ssem_cur, o_prv, ssem_prv):
        @pl.when((k > 0) & scattered_in(k - 1))
        def _():
            scatter_done(o_cur, ssem_cur)

        first = k == 0

        @pl.when(valid)
        def _():
            for r in range(MOE_TILE):
                scatter(prev, r, o_prv, ssem_prv, first).start(priority=r % DMA_PRIORITIES)
            x = x_ref[:, 0:d].astype(BF16)
            acc = None
            for wg_b, wu_b, wd_b, lane in ((wgl_b, wul_b, wdl_b, d + 1), (wgh_b, wuh_b, wdh_b, d + 2)):
                weight = x_ref[:, lane:lane + 1]
                hg = jnp.dot(x, wg_b[...], preferred_element_type=F32)
                hu = jnp.dot(x, wu_b[...], preferred_element_type=F32)
                hidden = (hg * _sigmoid(hg)) * hu * weight
                y = jnp.dot(hidden.astype(BF16), wd_b[...], preferred_element_type=F32)
                acc = y if acc is None else acc + y
            o_cur[...] = acc

        @pl.when(jnp.logical_not(valid) & tile_valid(k - 1))
        def _():
            def issue(r, carry):
                scatter(prev, r, o_prv, ssem_prv, first).start()
                return carry

            lax.fori_loop(0, MOE_TILE, issue, 0, unroll=8)

        @pl.when((k == last) & scattered_in(k))
        def _():
            scatter_done(o_prv, ssem_prv)

        @pl.when((k == last) & valid)
        def _():
            def issue(r, carry):
                scatter(k, r, o_cur, ssem_cur, False).start()
                return carry

            lax.fori_loop(0, MOE_TILE, issue, 0, unroll=8)
            scatter_done(o_cur, ssem_cur)

    @pl.when(k % 2 == 0)
    def _():
        step(o_even, ssem_even, o_odd, ssem_odd)

    @pl.when(k % 2 == 1)
    def _():
        step(o_odd, ssem_odd, o_even, ssem_even)


def _moe_tiles(e_lo, e_hi, valid, inv, xs, wg, wu, wd, layer, n_tokens):
    n_rows, dw = xs.shape
    _, _, d, f = wg.shape
    up_lo = pl.BlockSpec((None, None, d, f), lambda k, lo, hi, ok, iv: (layer, lo[k], 0, 0))
    up_hi = pl.BlockSpec((None, None, d, f), lambda k, lo, hi, ok, iv: (layer, hi[k], 0, 0))
    down_lo = pl.BlockSpec((None, None, f, d), lambda k, lo, hi, ok, iv: (layer, lo[k], 0, 0))
    down_hi = pl.BlockSpec((None, None, f, d), lambda k, lo, hi, ok, iv: (layer, hi[k], 0, 0))
    up_b = pltpu.VMEM((d, f), BF16)
    down_b = pltpu.VMEM((f, d), BF16)
    o_buf = pltpu.VMEM((MOE_TILE, d), F32)
    dma_sem = pltpu.SemaphoreType.DMA(())
    return pl.pallas_call(
        functools.partial(_moe_tile_kernel, n_tokens=n_tokens),
        grid_spec=pltpu.PrefetchScalarGridSpec(
            num_scalar_prefetch=4,
            grid=(n_rows // MOE_TILE,),
            in_specs=[pl.BlockSpec((MOE_TILE, dw), lambda k, lo, hi, ok, iv: (k, 0)),
                      up_lo, up_lo, down_lo, up_hi, up_hi, down_hi],
            out_specs=pl.BlockSpec(memory_space=pl.ANY),
            scratch_shapes=[o_buf, o_buf, dma_sem, dma_sem, up_b, up_b, down_b, up_b, up_b, down_b]),
        out_shape=jax.ShapeDtypeStruct((n_tokens + MOE_TILE, d), F32),
        compiler_params=_params("arbitrary"),
        name="moe_tiles",
    )(e_lo, e_hi, valid, inv, xs, wg, wu, wd, wg, wu, wd)


def _ln_ple_kernel(x1_ref, m_ref, p_ref, g_ref, b_ref, wg_ref, bg_ref, wp_ref, o_ref, *, alpha):
    x2 = _layer_norm(alpha * x1_ref[...] + m_ref[...], g_ref[...], b_ref[...])
    gate = _sigmoid(jnp.dot(x2.astype(BF16), wg_ref[...], preferred_element_type=F32) + bg_ref[...])
    emb = jnp.dot(p_ref[...].astype(BF16), wp_ref[...], preferred_element_type=F32)
    o_ref[...] = x2 + gate * emb


def _ln_ple(x1e, m, p, layer, ln_g, ln_b, wg, bg, wp, alpha, tm):
    t = x1e.shape[0]
    d = m.shape[1]
    pd = p.shape[2]
    tok = pl.BlockSpec((tm, d), lambda i: (i, 0))
    row = pl.BlockSpec((1, d), lambda i: (0, 0))
    return pl.pallas_call(
        functools.partial(_ln_ple_kernel, alpha=alpha),
        grid=(t // tm,),
        in_specs=[tok, tok, pl.BlockSpec((None, tm, pd), lambda i: (layer, i, 0)), row, row,
                  pl.BlockSpec((d, d), lambda i: (0, 0)), row, pl.BlockSpec((pd, d), lambda i: (0, 0))],
        out_specs=tok,
        out_shape=jax.ShapeDtypeStruct((t, d), F32),
        compiler_params=_params("parallel"),
        name="ln_ple",
    )(x1e, m, p, ln_g.reshape(1, d), ln_b.reshape(1, d), wg, bg.reshape(1, d), wp)


def _pick_tile(n, target):
    t = min(n, target)
    while n % t:
        t //= 2
    return t


def kernel(x, p, positions, w_in_ab, w_out_ab, conv_w, conv_b, lru_w_r, lru_b_r, lru_w_i, lru_b_i, lru_lambda, w_qkv_c, w_out_c, sinks_c, ln_mix_g, ln_mix_b, ln_ffn_g, ln_ffn_b, w_router, b_router, exp_w_gate, exp_w_up, exp_w_down, ple_w_proj, ple_w_gate, ple_b_gate):
    b, s, d = x.shape
    depth = p.shape[0]
    t = b * s
    alpha = (2 * depth) ** 0.25
    tm = _pick_tile(s, 1024)
    assert t % MOE_TILE == 0 and t % PLAN_COLS == 0
    n_tiles = _num_moe_tiles(t)
    assert n_tiles <= LANES
    for i in range(depth):
        j = i // 2
        if i % 2 == 0:
            q, k, v, xr, gr = _proj_ab(x, w_in_ab[j].astype(BF16), tm)
            y_sb = _sb_attention(q, k, v, _pick_tile(s, 256), SB_HEADS)
            y_lru = _lru(xr, gr, conv_w[j], conv_b[j], _block_diag(lru_w_r[j]).astype(BF16), lru_b_r[j],
                         _block_diag(lru_w_i[j]).astype(BF16), lru_b_i[j], lru_lambda[j],
                         _pick_tile(s, 256))
            w_out = w_out_ab[j].astype(BF16)
            ys = [(y_sb, True), (y_lru, False)]
            ws = [w_out[:SB_WIDTH], w_out[SB_WIDTH:]]
        else:
            q, k, v = _proj_rope(x, positions, w_qkv_c[j].astype(BF16), tm)
            y = _swa(q, k, v, sinks_c[j])
            ys = [(y, False)]
            ws = [w_out_c[j].astype(BF16)]
        x1e, route = _mix_out(x, ys, ws, ln_mix_g[i], ln_mix_b[i], w_router, b_router, alpha, tm)
        x1e = x1e.reshape(t, d + LANES)
        pos, meta = _plan(route[:, 0, :].reshape(t // PLAN_COLS, PLAN_COLS))
        pos = pos.reshape(t)
        xs, inv = _dispatch(pos, meta[3, :N_BUCKETS + 1], x1e, n_tiles * MOE_TILE, _pick_tile(t, 2048))
        m = _moe_tiles(meta[0, :n_tiles], meta[1, :n_tiles], meta[2, :n_tiles], inv, xs,
                       exp_w_gate, exp_w_up, exp_w_down, i, t)
        x = _ln_ple(x1e, m, p.reshape(depth, t, -1), i, ln_ffn_g[i], ln_ffn_b[i],
                    ple_w_gate[i].astype(BF16), ple_b_gate[i], ple_w_proj[i].astype(BF16), alpha,
                    _pick_tile(t, 1024)).reshape(b, s, d)
    return x
```
